```python
import jax, jax.numpy as jnp
from jax import lax
import numpy as np

D_MODEL = 4096
BATCH = 8
SEQ = 4096
DEPTH = 1

CHUNK = 64
N_LEFT_CHUNKS = 8
BAND = (N_LEFT_CHUNKS + 1) * CHUNK
ATTN_WIDTH = D_MODEL // 2
ATTN_HEAD_DIM = 128
ATTN_HEADS = ATTN_WIDTH // ATTN_HEAD_DIM
MAX_REL = 128
N_REL = 2 * MAX_REL + 1
POOL_WIDTH = D_MODEL // 2
POOL_WINDOWS = (2, 4, 8, 16)
N_POOL_GROUPS = len(POOL_WINDOWS)
POOL_GROUP_DIM = POOL_WIDTH // N_POOL_GROUPS
N_BRANCHES = 2
IN_COLS = 4 * ATTN_WIDTH + 2 * POOL_WIDTH + N_BRANCHES * D_MODEL
EPS = 1e-6

kernel_name = "hybrid_chunk_attn_pool_gated_block"


def _rmsnorm(x, g):
    xf = x.astype(jnp.float32)
    y = xf * lax.rsqrt(jnp.mean(xf * xf, axis=-1, keepdims=True) + EPS)
    return (y * g.astype(jnp.float32)).astype(x.dtype)


def _chunk_band_attention(q, k, v, rel_bias):
    B, S, H, Dh = q.shape
    nc = S // CHUNK
    pad = N_LEFT_CHUNKS * CHUNK
    kp = jnp.pad(k, ((0, 0), (pad, 0), (0, 0), (0, 0)))
    vp = jnp.pad(v, ((0, 0), (pad, 0), (0, 0), (0, 0)))
    qc = q.reshape(B, nc, CHUNK, H, Dh).transpose(1, 0, 2, 3, 4)
    rel = pad + jnp.arange(CHUNK)[:, None] - jnp.arange(BAND)[None, :]
    bias = rel_bias[:, jnp.clip(rel, -MAX_REL, MAX_REL) + MAX_REL].astype(jnp.float32)
    scale = Dh ** -0.5
    band_offsets = jnp.arange(BAND) - pad

    def one_chunk(args):
        c, qb = args
        start = c * CHUNK
        kb = lax.dynamic_slice_in_dim(kp, start, BAND, axis=1)
        vb = lax.dynamic_slice_in_dim(vp, start, BAND, axis=1)
        s = jnp.einsum('bqhd,bkhd->bhqk', qb, kb,
                       preferred_element_type=jnp.float32) * scale + bias
        valid = (start + band_offsets) >= 0
        s = jnp.where(valid[None, None, None, :], s, -jnp.inf)
        p = jax.nn.softmax(s, axis=-1)
        return jnp.einsum('bhqk,bkhd->bqhd', p.astype(vb.dtype), vb)

    out = lax.map(one_chunk, (jnp.arange(nc), qc))
    return out.transpose(1, 0, 2, 3, 4).reshape(B, S, H * Dh)


def _multiscale_pool(u, pool_w, pool_scale):
    B, S, W = u.shape
    uf = u.astype(jnp.float32).reshape(B, S, N_POOL_GROUPS, POOL_GROUP_DIM)
    cs = jnp.pad(jnp.cumsum(uf, axis=1), ((0, 0), (1, 0), (0, 0), (0, 0)))
    t = jnp.arange(S)
    means = []
    for g, w in enumerate(POOL_WINDOWS):
        csg = cs[:, :, g]
        lo = jnp.pad(csg[:, :S + 1 - w], ((0, 0), (w - 1, 0), (0, 0)))
        cnt = jnp.minimum(t + 1, w).astype(jnp.float32)[None, :, None]
        means.append((csg[:, 1:] - lo) / cnt)
    mean = jnp.stack(means, axis=2)
    d = (mean - uf).astype(u.dtype)
    y = jnp.einsum('bsgc,gcd->bsgd', d, pool_w)
    return y.reshape(B, S, W) * pool_scale


def _fwd_setup_inputs(seed: int = 0) -> dict:
    key = jax.random.key(seed)
    ks = jax.random.split(key, 12)
    f = jnp.float32
    x = jax.random.normal(ks[0], (BATCH, SEQ, D_MODEL), f)
    norm_gain = 1.0 + 0.1 * jax.random.normal(ks[1], (D_MODEL,), f)
    w_in = jax.random.normal(ks[2], (D_MODEL, IN_COLS), f) * D_MODEL ** -0.5
    rel_bias = 0.5 * jax.random.normal(ks[3], (ATTN_HEADS, N_REL), f)
    pool_w = jax.random.normal(ks[4], (N_POOL_GROUPS, POOL_GROUP_DIM, POOL_GROUP_DIM), f) * POOL_GROUP_DIM ** -0.5
    pool_scale = 1.0 + 0.1 * jax.random.normal(ks[5], (POOL_WIDTH,), f)
    w_out_attn = jax.random.normal(ks[6], (ATTN_WIDTH, D_MODEL), f) * ATTN_WIDTH ** -0.5
    w_out_pool = jax.random.normal(ks[7], (POOL_WIDTH, D_MODEL), f) * POOL_WIDTH ** -0.5
    gate_bias = 0.1 * jax.random.normal(ks[8], (N_BRANCHES, D_MODEL), f)
    w_out = jax.random.normal(ks[9], (D_MODEL, D_MODEL), f) * D_MODEL ** -0.5
    final_gain = 1.0 + 0.1 * jax.random.normal(ks[10], (D_MODEL,), f)
    return {"x": x, "norm_gain": norm_gain, "w_in": w_in, "rel_bias": rel_bias,
            "pool_w": pool_w, "pool_scale": pool_scale, "w_out_attn": w_out_attn,
            "w_out_pool": w_out_pool, "gate_bias": gate_bias, "w_out": w_out,
            "final_gain": final_gain}


def _fwd_reference(x, norm_gain, w_in, rel_bias, pool_w, pool_scale, w_out_attn,
              w_out_pool, gate_bias, w_out, final_gain):
    B, S, D = x.shape
    A, P = ATTN_WIDTH, POOL_WIDTH
    for _ in range(DEPTH):
        h = _rmsnorm(x, norm_gain)
        proj = jnp.einsum('bsd,dn->bsn', h, w_in)
        o = 0
        q = proj[..., o:o + A]; o += A
        k = proj[..., o:o + A]; o += A
        v = proj[..., o:o + A]; o += A
        z_attn = proj[..., o:o + A]; o += A
        u_pool = proj[..., o:o + P]; o += P
        z_pool = proj[..., o:o + P]; o += P
        g_attn = proj[..., o:o + D]; o += D
        g_pool = proj[..., o:o + D]

        hs = (B, S, ATTN_HEADS, ATTN_HEAD_DIM)
        y_attn = _chunk_band_attention(q.reshape(hs), k.reshape(hs), v.reshape(hs), rel_bias)
        y_attn = y_attn * jax.nn.silu(z_attn)
        y_pool = _multiscale_pool(u_pool, pool_w, pool_scale) * jax.nn.silu(z_pool)

        m = (jax.nn.sigmoid(g_attn + gate_bias[0]) * jnp.einsum('bsa,ad->bsd', y_attn, w_out_attn)
             + jax.nn.sigmoid(g_pool + gate_bias[1]) * jnp.einsum('bsp,pd->bsd', y_pool, w_out_pool))
        x = x + jnp.einsum('bsd,de->bse', m, w_out)
    return _rmsnorm(x, final_gain)


import jax as _jax
import jax.numpy as _jnp

TWIN_FORMAT = 'train_step'
FWD_PARAMS = ['x', 'norm_gain', 'w_in', 'rel_bias', 'pool_w', 'pool_scale', 'w_out_attn', 'w_out_pool', 'gate_bias', 'w_out', 'final_gain']
TWIN_WEIGHTS = ['norm_gain', 'w_in', 'rel_bias', 'pool_w', 'pool_scale', 'w_out_attn', 'w_out_pool', 'gate_bias', 'w_out', 'final_gain']
TWIN_DIFF_INPUT = 'x'
TWIN_INPUTS = ['x', 'norm_gain', 'w_in', 'rel_bias', 'pool_w', 'pool_scale', 'w_out_attn', 'w_out_pool', 'gate_bias', 'w_out', 'final_gain', 'loss_target', 'm_norm_gain', 'm_w_in', 'm_rel_bias', 'm_pool_w', 'm_pool_scale', 'm_w_out_attn', 'm_w_out_pool', 'm_gate_bias', 'm_w_out', 'm_final_gain', 'v_norm_gain', 'v_w_in', 'v_rel_bias', 'v_pool_w', 'v_pool_scale', 'v_w_out_attn', 'v_w_out_pool', 'v_gate_bias', 'v_w_out', 'v_final_gain']
TWIN_OUTPUTS = ['loss', 'grad_x', 'grad_norm_gain', 'grad_w_in', 'grad_rel_bias', 'grad_pool_w', 'grad_pool_scale', 'grad_w_out_attn', 'grad_w_out_pool', 'grad_gate_bias', 'grad_w_out', 'grad_final_gain', 'delta_norm_gain', 'delta_w_in', 'delta_rel_bias', 'delta_pool_w', 'delta_pool_scale', 'delta_w_out_attn', 'delta_w_out_pool', 'delta_gate_bias', 'delta_w_out', 'delta_final_gain', 'new_m_norm_gain', 'new_m_w_in', 'new_m_rel_bias', 'new_m_pool_w', 'new_m_pool_scale', 'new_m_w_out_attn', 'new_m_w_out_pool', 'new_m_gate_bias', 'new_m_w_out', 'new_m_final_gain', 'new_v_norm_gain', 'new_v_w_in', 'new_v_rel_bias', 'new_v_pool_w', 'new_v_pool_scale', 'new_v_w_out_attn', 'new_v_w_out_pool', 'new_v_gate_bias', 'new_v_w_out', 'new_v_final_gain']
TWIN_LEAF_KINDS = {'loss': 'loss', 'grad_x': 'grad_x', 'grad_norm_gain': 'grad_w', 'grad_w_in': 'grad_w', 'grad_rel_bias': 'grad_w', 'grad_pool_w': 'grad_w', 'grad_pool_scale': 'grad_w', 'grad_w_out_attn': 'grad_w', 'grad_w_out_pool': 'grad_w', 'grad_gate_bias': 'grad_w', 'grad_w_out': 'grad_w', 'grad_final_gain': 'grad_w', 'delta_norm_gain': 'delta_w', 'delta_w_in': 'delta_w', 'delta_rel_bias': 'delta_w', 'delta_pool_w': 'delta_w', 'delta_pool_scale': 'delta_w', 'delta_w_out_attn': 'delta_w', 'delta_w_out_pool': 'delta_w', 'delta_gate_bias': 'delta_w', 'delta_w_out': 'delta_w', 'delta_final_gain': 'delta_w', 'new_m_norm_gain': 'new_m', 'new_m_w_in': 'new_m', 'new_m_rel_bias': 'new_m', 'new_m_pool_w': 'new_m', 'new_m_pool_scale': 'new_m', 'new_m_w_out_attn': 'new_m', 'new_m_w_out_pool': 'new_m', 'new_m_gate_bias': 'new_m', 'new_m_w_out': 'new_m', 'new_m_final_gain': 'new_m', 'new_v_norm_gain': 'new_v', 'new_v_w_in': 'new_v', 'new_v_rel_bias': 'new_v', 'new_v_pool_w': 'new_v', 'new_v_pool_scale': 'new_v', 'new_v_w_out_attn': 'new_v', 'new_v_w_out_pool': 'new_v', 'new_v_gate_bias': 'new_v', 'new_v_w_out': 'new_v', 'new_v_final_gain': 'new_v'}


def _forward(args):
    return _fwd_reference(*[args[k] for k in FWD_PARAMS])


def _output_shape():
    out = _jax.eval_shape(lambda: _forward(_fwd_setup_inputs(0)))
    return out.shape, out.dtype

N_MICROBATCH = 1
ADAM_LR = 0.001
ADAM_B1 = 0.9
ADAM_B2 = 0.999
ADAM_EPS = 1e-08
ADAM_WD = 0.01
ADAM_STEP = 10
PER_EXAMPLE_BATCH_AXIS = {'x': 0, 'loss_target': 0}
SHARED_INPUTS = []
_WEIGHT_DTYPES = {'norm_gain': _jnp.float32, 'w_in': _jnp.float32, 'rel_bias': _jnp.float32, 'pool_w': _jnp.float32, 'pool_scale': _jnp.float32, 'w_out_attn': _jnp.float32, 'w_out_pool': _jnp.float32, 'gate_bias': _jnp.float32, 'w_out': _jnp.float32, 'final_gain': _jnp.float32}
MOMENT_SCALE = {'norm_gain': 1.935349e-02, 'w_in': 8.540839e-03, 'rel_bias': 2.127280e-03, 'pool_w': 1.791851e-02, 'pool_scale': 1.872287e-02, 'w_out_attn': 2.313329e-03, 'w_out_pool': 1.265627e-02, 'gate_bias': 3.572739e-03, 'w_out': 1.264218e-02, 'final_gain': 8.015111e+00}


def _to_microbatches(a, axis):
    t = _jnp.moveaxis(a, axis, 0)
    t = t.reshape((N_MICROBATCH, t.shape[0] // N_MICROBATCH) + t.shape[1:])
    return _jnp.moveaxis(t, 1, axis + 1)


def setup_inputs(seed: int = 0) -> dict:
    inp = _fwd_setup_inputs(seed)
    key = _jax.random.fold_in(_jax.random.key(seed), 7919)
    shape, _ = _output_shape()
    out = dict(inp)
    out["loss_target"] = _jax.random.normal(_jax.random.fold_in(key, 0), shape, _jnp.float32)
    for i, name in enumerate(TWIN_WEIGHTS):
        w = inp[name].astype(_jnp.float32)
        if MOMENT_SCALE is None:
            s = _jnp.sqrt(_jnp.mean(_jnp.square(w)) + 1e-30)
        else:
            s = MOMENT_SCALE[name]
        km, kv = _jax.random.split(_jax.random.fold_in(key, i + 1))
        out[name] = w
        out["m_" + name] = s * _jax.random.normal(km, w.shape, _jnp.float32)
        out["v_" + name] = (s * s) * _jax.random.uniform(kv, w.shape, _jnp.float32, 0.5, 1.5)
    if N_MICROBATCH > 1:
        for name, axis in PER_EXAMPLE_BATCH_AXIS.items():
            out[name] = _to_microbatches(out[name], axis)
    return {'x': out['x'], 'norm_gain': out['norm_gain'], 'w_in': out['w_in'], 'rel_bias': out['rel_bias'], 'pool_w': out['pool_w'], 'pool_scale': out['pool_scale'], 'w_out_attn': out['w_out_attn'], 'w_out_pool': out['w_out_pool'], 'gate_bias': out['gate_bias'], 'w_out': out['w_out'], 'final_gain': out['final_gain'], 'loss_target': out['loss_target'], 'm_norm_gain': out['m_norm_gain'], 'm_w_in': out['m_w_in'], 'm_rel_bias': out['m_rel_bias'], 'm_pool_w': out['m_pool_w'], 'm_pool_scale': out['m_pool_scale'], 'm_w_out_attn': out['m_w_out_attn'], 'm_w_out_pool': out['m_w_out_pool'], 'm_gate_bias': out['m_gate_bias'], 'm_w_out': out['m_w_out'], 'm_final_gain': out['m_final_gain'], 'v_norm_gain': out['v_norm_gain'], 'v_w_in': out['v_w_in'], 'v_rel_bias': out['v_rel_bias'], 'v_pool_w': out['v_pool_w'], 'v_pool_scale': out['v_pool_scale'], 'v_w_out_attn': out['v_w_out_attn'], 'v_w_out_pool': out['v_w_out_pool'], 'v_gate_bias': out['v_gate_bias'], 'v_w_out': out['v_w_out'], 'v_final_gain': out['v_final_gain']}


def _loss(weights, diff, rest, loss_target):
    with _jax.named_scope("forward"):
        args = {**rest, TWIN_DIFF_INPUT: diff, **{k: w.astype(_WEIGHT_DTYPES[k]) for k, w in weights.items()}}
        y = _forward(args)
    with _jax.named_scope("loss_head"):
        err = _jnp.square(y.astype(_jnp.float32) - loss_target)
        return 0.5 * _jnp.sum(_jnp.mean(err, axis=-1)) if err.ndim else 0.5 * err


def _adamw(w, g, m, v):
    m = ADAM_B1 * m + (1.0 - ADAM_B1) * g
    v = ADAM_B2 * v + (1.0 - ADAM_B2) * _jnp.square(g)
    m_hat = m / (1.0 - ADAM_B1 ** ADAM_STEP)
    v_hat = v / (1.0 - ADAM_B2 ** ADAM_STEP)
    delta = -ADAM_LR * (m_hat / (_jnp.sqrt(v_hat) + ADAM_EPS) + ADAM_WD * w)
    return delta, m, v


def reference(x, norm_gain, w_in, rel_bias, pool_w, pool_scale, w_out_attn, w_out_pool, gate_bias, w_out, final_gain, loss_target, m_norm_gain, m_w_in, m_rel_bias, m_pool_w, m_pool_scale, m_w_out_attn, m_w_out_pool, m_gate_bias, m_w_out, m_final_gain, v_norm_gain, v_w_in, v_rel_bias, v_pool_w, v_pool_scale, v_w_out_attn, v_w_out_pool, v_gate_bias, v_w_out, v_final_gain):
    given = dict(x=x, norm_gain=norm_gain, w_in=w_in, rel_bias=rel_bias, pool_w=pool_w, pool_scale=pool_scale, w_out_attn=w_out_attn, w_out_pool=w_out_pool, gate_bias=gate_bias, w_out=w_out, final_gain=final_gain, loss_target=loss_target, m_norm_gain=m_norm_gain, m_w_in=m_w_in, m_rel_bias=m_rel_bias, m_pool_w=m_pool_w, m_pool_scale=m_pool_scale, m_w_out_attn=m_w_out_attn, m_w_out_pool=m_w_out_pool, m_gate_bias=m_gate_bias, m_w_out=m_w_out, m_final_gain=m_final_gain, v_norm_gain=v_norm_gain, v_w_in=v_w_in, v_rel_bias=v_rel_bias, v_pool_w=v_pool_w, v_pool_scale=v_pool_scale, v_w_out_attn=v_w_out_attn, v_w_out_pool=v_w_out_pool, v_gate_bias=v_gate_bias, v_w_out=v_w_out, v_final_gain=v_final_gain)
    weights = {n: given[n] for n in TWIN_WEIGHTS}
    shared = {n: given[n] for n in SHARED_INPUTS}
    per_example = {n: given[n] for n in ['x']}
    grad_fn = _jax.value_and_grad(_loss, argnums=(0, 1))

    def one_microbatch(ex, loss_target):
        ex = dict(ex)
        diff = ex.pop(TWIN_DIFF_INPUT)
        return grad_fn(weights, diff, {**shared, **ex}, loss_target)

    if N_MICROBATCH == 1:
        loss, (grad_w, grad_x) = one_microbatch(per_example, given["loss_target"])
    else:
        def body(carry, xs):
            loss_sum, grad_sum = carry
            l_k, (gw_k, gx_k) = one_microbatch(xs[0], xs[1])
            with _jax.named_scope("update"):
                return (loss_sum + l_k, _jax.tree.map(_jnp.add, grad_sum, gw_k)), gx_k

        init = (_jnp.zeros((), _jnp.float32), _jax.tree.map(_jnp.zeros_like, weights))
        (loss, grad_w), grad_x = _jax.lax.scan(body, init, (per_example, given["loss_target"]))
    with _jax.named_scope("update"):
        delta_w, new_m, new_v = {}, {}, {}
        for n in TWIN_WEIGHTS:
            delta_w[n], new_m[n], new_v[n] = _adamw(weights[n], grad_w[n], given["m_" + n], given["v_" + n])
    return (loss, grad_x, *[grad_w[n] for n in TWIN_WEIGHTS], *[delta_w[n] for n in TWIN_WEIGHTS],
            *[new_m[n] for n in TWIN_WEIGHTS], *[new_v[n] for n in TWIN_WEIGHTS])
```

```python
import functools

import jax
import jax.numpy as jnp
from jax import lax
from jax.experimental import pallas as pl
from jax.experimental.pallas import tpu as pltpu

F32 = jnp.float32
BF16 = jnp.bfloat16
MESH = pl.DeviceIdType.MESH
ANY = pl.BlockSpec(memory_space=pl.ANY)

N_CHIPS = 4
N_DEV = 8
CHUNK = 64
N_LEFT_CHUNKS = 8
PAD = N_LEFT_CHUNKS * CHUNK
HEAD_DIM = 128
MAX_REL = 128
POOL_WINDOWS = (2, 4, 8, 16)
N_GROUPS = len(POOL_WINDOWS)
HALO = 16
Q_GROUP = 4 * CHUNK
K_GROUP = Q_GROUP + PAD
NEG = -1e30
EPS = 1e-6
ADAM_LR, ADAM_B1, ADAM_B2, ADAM_EPS, ADAM_WD, ADAM_STEP = 0.001, 0.9, 0.999, 1e-08, 0.01, 10
VMEM_LIMIT = 56 * 1024 * 1024

NN = (((1,), (0,)), ((), ()))
NT = (((1,), (1,)), ((), ()))
TN = (((0,), (0,)), ((), ()))


def _div(n, pref):
    if n <= pref:
        return n
    for t in range(pref - pref % 128, 0, -128):
        if n % t == 0:
            return t
    raise ValueError((n, pref))


def _params(sem, **kw):
    return pltpu.CompilerParams(dimension_semantics=sem, vmem_limit_bytes=VMEM_LIMIT, **kw)


def _sigmoid(z):
    return jax.nn.sigmoid(z)


def _silu_and_grad(z):
    sg = _sigmoid(z)
    return z * sg, sg * (1.0 + z * (1.0 - sg))


def _matmul(name, dn, grid, a, a_spec, b, b_spec, acc_shape, outs, out_specs, extra=(), extra_specs=(),
            epilogue=None, accumulate_outs=False):
    nk = grid[2]
    ne, no = len(extra), len(outs)

    def body(*refs):
        a_ref, b_ref = refs[0], refs[1]
        ex = refs[2:2 + ne]
        out_refs = refs[2 + ne:2 + ne + no]
        acc = refs[-1]
        k = pl.program_id(2)

        @pl.when(k == 0)
        def _():
            acc[...] = jnp.zeros_like(acc)

        acc[...] += lax.dot_general(a_ref[...], b_ref[...], dn, preferred_element_type=F32)

        @pl.when(k == nk - 1)
        def _():
            if epilogue is None:
                for o in out_refs:
                    o[...] = acc[...].astype(o.dtype)
            else:
                epilogue(acc[...], ex, out_refs)

    sem = ("arbitrary",) * 3 if accumulate_outs else ("parallel", "parallel", "arbitrary")
    return pl.pallas_call(
        body, name=name, grid=grid,
        in_specs=[a_spec, b_spec, *extra_specs], out_specs=list(out_specs), out_shape=list(outs),
        scratch_shapes=[pltpu.VMEM(acc_shape, F32)],
        compiler_params=_params(sem),
    )(a, b, *extra)


def _place():
    x, y, c = lax.axis_index("x"), lax.axis_index("y"), lax.axis_index("c")
    chips = [(1 - x, y), (x, 1 - y), (1 - x, 1 - y)]
    return x, y, c, chips


def _remote(src, dst, send_sems, recv_sems, k, dev):
    return pltpu.make_async_remote_copy(src_ref=src, dst_ref=dst, send_sem=send_sems.at[k], recv_sem=recv_sems.at[k],
                                        device_id=dev, device_id_type=MESH)


def _all_gather(shards, split):
    n = len(shards)

    def body(*refs):
        ins, outs = refs[:n], refs[n:2 * n]
        send_sems, recv_sems, local_sems = refs[2 * n:]
        x, y, c, chips = _place()
        me = 2 * x + y
        sibling = (x, y, 1 - c)
        started = []
        for t in range(n):
            cp = pltpu.make_async_copy(ins[t], outs[t].at[me], local_sems.at[t])
            cp.start()
            started.append(cp)
        sends, last_recvs = [], []

        def halves_of(t):
            half = ins[t].shape[0] // 2
            return pl.ds(c * half, half), pl.ds((1 - c) * half, half)

        for t in range(n):
            for j, (cx, cy) in enumerate(chips):
                k = 6 * t + j
                if split[t]:
                    rows, _ = halves_of(t)
                    cp = _remote(ins[t].at[rows], outs[t].at[me, rows], send_sems, recv_sems, k, (cx, cy, c))
                else:
                    cp = _remote(ins[t], outs[t].at[me], send_sems, recv_sems, k, (cx, cy, c))
                cp.start()
                sends.append(cp)
        for t in range(n):
            for j, (cx, cy) in enumerate(chips):
                src_chip = 2 * cx + cy
                k = 6 * t + j
                if split[t]:
                    rows, other = halves_of(t)
                    got = outs[t].at[src_chip, rows]
                    _remote(got, got, send_sems, recv_sems, k, (cx, cy, c)).wait_recv()
                    fwd = _remote(got, got, send_sems, recv_sems, k + 3, sibling)
                    fwd.start()
                    sends.append(fwd)
                    theirs = outs[t].at[src_chip, other]
                    last_recvs.append(_remote(theirs, theirs, send_sems, recv_sems, k + 3, sibling))
                else:
                    got = outs[t].at[src_chip]
                    last_recvs.append(_remote(got, got, send_sems, recv_sems, k, (cx, cy, c)))
        for cp in last_recvs:
            cp.wait_recv()
        for cp in sends:
            cp.wait_send()
        for cp in started:
            cp.wait()

    return pl.pallas_call(
        body, name="all_gather_weights",
        in_specs=[ANY] * n, out_specs=[ANY] * n,
        out_shape=[jax.ShapeDtypeStruct((N_CHIPS,) + s.shape, s.dtype) for s in shards],
        scratch_shapes=[pltpu.SemaphoreType.DMA((6 * n,)), pltpu.SemaphoreType.DMA((6 * n,)),
                        pltpu.SemaphoreType.DMA((n,))],
    )(*shards)


def _swap_other_half(parts):
    n = len(parts)

    def body(*refs):
        ins, outs = refs[:n], refs[n:2 * n]
        send_sems, recv_sems = refs[2 * n:]
        x, y, c, _ = _place()
        cps = []
        for t in range(n):
            half = ins[t].shape[1] // 2
            cp = _remote(ins[t].at[:, pl.ds((1 - c) * half, half)], outs[t], send_sems, recv_sems, t, (x, y, 1 - c))
            cp.start()
            cps.append(cp)
        for cp in cps:
            cp.wait()

    return pl.pallas_call(
        body, name="reduce_pair_exchange",
        in_specs=[ANY] * n, out_specs=[ANY] * n,
        out_shape=[jax.ShapeDtypeStruct((p.shape[0], p.shape[1] // 2, p.shape[2]), p.dtype) for p in parts],
        scratch_shapes=[pltpu.SemaphoreType.DMA((n,)), pltpu.SemaphoreType.DMA((n,))],
    )(*parts)


def _scatter_to_owners(parts):
    n = len(parts)

    def body(*refs):
        ins, outs = refs[:n], refs[n:2 * n]
        send_sems, recv_sems, local_sems = refs[2 * n:]
        x, y, c, chips = _place()
        me = 2 * x + y
        cps, own = [], []
        for t in range(n):
            cp = pltpu.make_async_copy(ins[t].at[me], outs[t].at[me], local_sems.at[t])
            cp.start()
            own.append(cp)
            for j, (cx, cy) in enumerate(chips):
                cp = _remote(ins[t].at[2 * cx + cy], outs[t].at[me], send_sems, recv_sems, 3 * t + j, (cx, cy, c))
                cp.start()
                cps.append(cp)
        for t in range(n):
            for j, (cx, cy) in enumerate(chips):
                got = outs[t].at[2 * cx + cy]
                _remote(got, got, send_sems, recv_sems, 3 * t + j, (cx, cy, c)).wait_recv()
        for cp in cps:
            cp.wait_send()
        for cp in own:
            cp.wait()

    return pl.pallas_call(
        body, name="reduce_scatter_chips",
        in_specs=[ANY] * n, out_specs=[ANY] * n,
        out_shape=[jax.ShapeDtypeStruct(p.shape, p.dtype) for p in parts],
        scratch_shapes=[pltpu.SemaphoreType.DMA((3 * n,)), pltpu.SemaphoreType.DMA((3 * n,)),
                        pltpu.SemaphoreType.DMA((n,))],
    )(*parts)


def _share_half(fulls):
    n = len(fulls)

    def body(*refs):
        ins, outs = refs[:n], refs[n:2 * n]
        send_sems, recv_sems = refs[2 * n:]
        x, y, c, _ = _place()
        cps = []
        for t in range(n):
            half = ins[t].shape[0] // 2
            mine = outs[t].at[pl.ds(c * half, half)]
            theirs = outs[t].at[pl.ds((1 - c) * half, half)]
            cp = _remote(mine, mine, send_sems, recv_sems, t, (x, y, 1 - c))
            cp.start()
            cps.append((cp, _remote(theirs, theirs, send_sems, recv_sems, t, (x, y, 1 - c))))
        for cp, rv in cps:
            rv.wait_recv()
            cp.wait_send()

    return pl.pallas_call(
        body, name="reduce_share_halves",
        in_specs=[ANY] * n, out_specs=[ANY] * n,
        out_shape=[jax.ShapeDtypeStruct(f.shape, f.dtype) for f in fulls],
        input_output_aliases={t: t for t in range(n)},
        scratch_shapes=[pltpu.SemaphoreType.DMA((n,)), pltpu.SemaphoreType.DMA((n,))],
    )(*fulls)


def _all_to_all_small(packed):
    def body(in_ref, out_ref, send_sems, recv_sems, local_sem):
        x, y, c, _ = _place()
        me = 4 * x + 2 * y + c
        own = pltpu.make_async_copy(in_ref, out_ref.at[me], local_sem)
        own.start()
        cps, rvs = [], []
        for k in range(1, N_DEV):
            fx, fy, fc = (k >> 2) & 1, (k >> 1) & 1, k & 1
            px, py, pc = x ^ fx, y ^ fy, c ^ fc
            cp = _remote(in_ref, out_ref.at[me], send_sems, recv_sems, k - 1, (px, py, pc))
            cp.start()
            cps.append(cp)
            got = out_ref.at[4 * px + 2 * py + pc]
            rvs.append(_remote(got, got, send_sems, recv_sems, k - 1, (px, py, pc)))
        for rv in rvs:
            rv.wait_recv()
        for cp in cps:
            cp.wait_send()
        own.wait()

    return pl.pallas_call(
        body, name="small_grads_exchange",
        in_specs=[ANY], out_specs=ANY,
        out_shape=jax.ShapeDtypeStruct((N_DEV,) + packed.shape, packed.dtype),
        scratch_shapes=[pltpu.SemaphoreType.DMA((N_DEV - 1,)), pltpu.SemaphoreType.DMA((N_DEV - 1,)),
                        pltpu.SemaphoreType.DMA],
    )(packed)


def _pair_sum(name, g, recv, c_arr):
    _, rows, cols = g.shape
    half = rows // 2
    tr, tc = _div(half, 512), _div(cols, 1024)
    nrb = half // tr

    def body(c_ref, g_ref, r_ref, o_ref):
        o_ref[...] = (g_ref[...] + r_ref[...].astype(F32)).astype(BF16)

    return pl.pallas_call(
        body, name=name,
        grid_spec=pltpu.PrefetchScalarGridSpec(
            num_scalar_prefetch=1, grid=(N_CHIPS, nrb, cols // tc),
            in_specs=[pl.BlockSpec((None, tr, tc), lambda s, i, j, c: (s, c[0] * nrb + i, j)),
                      pl.BlockSpec((None, tr, tc), lambda s, i, j, c: (s, i, j))],
            out_specs=pl.BlockSpec((None, tr, tc), lambda s, i, j, c: (s, i, j))),
        out_shape=jax.ShapeDtypeStruct(recv.shape, BF16),
        compiler_params=_params(("parallel", "parallel", "parallel")),
    )(c_arr, g, recv)


def _chip_sum(name, recv, c_arr):
    _, half, cols = recv.shape
    tr, tc = _div(half, 512), _div(cols, 1024)
    nrb = half // tr

    def body(c_ref, r_ref, o_ref):
        acc = r_ref[0].astype(F32)
        for s in range(1, N_CHIPS):
            acc = acc + r_ref[s].astype(F32)
        o_ref[...] = acc

    return pl.pallas_call(
        body, name=name,
        grid_spec=pltpu.PrefetchScalarGridSpec(
            num_scalar_prefetch=1, grid=(nrb, cols // tc),
            in_specs=[pl.BlockSpec((N_CHIPS, tr, tc), lambda i, j, c: (0, i, j))],
            out_specs=pl.BlockSpec((tr, tc), lambda i, j, c: (c[0] * nrb + i, j))),
        out_shape=jax.ShapeDtypeStruct((2 * half, cols), F32),
        compiler_params=_params(("parallel", "parallel")),
    )(c_arr, recv)


def _adamw_math(w, g, m, v):
    m2 = ADAM_B1 * m + (1.0 - ADAM_B1) * g
    v2 = ADAM_B2 * v + (1.0 - ADAM_B2) * (g * g)
    m_hat = m2 / (1.0 - ADAM_B1 ** ADAM_STEP)
    v_hat = v2 / (1.0 - ADAM_B2 ** ADAM_STEP)
    delta = -ADAM_LR * (m_hat / (jnp.sqrt(v_hat) + ADAM_EPS) + ADAM_WD * w)
    return delta, m2, v2


def _adamw(name, g, w, m, v):
    rows, cols = g.shape
    tr, tc = _div(rows, 256), _div(cols, 1024)
    spec = pl.BlockSpec((tr, tc), lambda i, j: (i, j))

    def body(g_ref, w_ref, m_ref, v_ref, go_ref, d_ref, mo_ref, vo_ref):
        gg = g_ref[...]
        delta, m2, v2 = _adamw_math(w_ref[...], gg, m_ref[...], v_ref[...])
        go_ref[...] = gg
        d_ref[...] = delta
        mo_ref[...] = m2
        vo_ref[...] = v2

    return pl.pallas_call(
        body, name=name, grid=(rows // tr, cols // tc),
        in_specs=[spec] * 4, out_specs=[spec] * 4,
        out_shape=[jax.ShapeDtypeStruct(g.shape, F32)] * 4,
        compiler_params=_params(("parallel", "parallel")),
    )(g, w, m, v)


def _sum_slots(name, slots):
    def body(s_ref, o_ref):
        acc = s_ref[0]
        for d in range(1, N_DEV):
            acc = acc + s_ref[d]
        o_ref[...] = acc

    return pl.pallas_call(body, name=name, out_shape=jax.ShapeDtypeStruct(slots.shape[1:], F32))(slots)


def _norm_in(x, gain):
    s, d = x.shape
    tr = _div(s, 256)

    def body(x_ref, g_ref, h_ref):
        xv = x_ref[...]
        r = lax.rsqrt(jnp.mean(xv * xv, axis=-1, keepdims=True) + EPS)
        h_ref[...] = (xv * r * g_ref[...]).astype(BF16)

    return pl.pallas_call(
        body, name="norm_in", grid=(s // tr,),
        in_specs=[pl.BlockSpec((tr, d), lambda i: (i, 0)), pl.BlockSpec((1, d), lambda i: (0, 0))],
        out_specs=pl.BlockSpec((tr, d), lambda i: (i, 0)),
        out_shape=jax.ShapeDtypeStruct((s, d), BF16),
        compiler_params=_params(("parallel",)),
    )(x, gain.reshape(1, d))


def _bias_table(rel_bias):
    h = rel_bias.shape[0]
    n_rel = Q_GROUP + K_GROUP - 1
    lo = PAD - K_GROUP + 1
    left = max(0, -MAX_REL - lo)
    right = max(0, lo + n_rel - 1 - MAX_REL)
    by_rel = jnp.concatenate([jnp.broadcast_to(rel_bias[:, :1], (h, left)), rel_bias,
                              jnp.broadcast_to(rel_bias[:, -1:], (h, right))], axis=1)
    by_rel = by_rel[:, lo + MAX_REL + left:][:, :n_rel]
    rev = by_rel[:, ::-1]
    rev = jnp.concatenate([rev, jnp.zeros((h, 1), rel_bias.dtype)], axis=1)
    skew = jnp.tile(rev, (1, Q_GROUP))[:, :Q_GROUP * n_rel].reshape(h, Q_GROUP, n_rel)
    tab = skew[:, :, Q_GROUP - 1:Q_GROUP - 1 + K_GROUP]
    qi = jnp.arange(Q_GROUP)[:, None] // CHUNK
    kj = jnp.arange(K_GROUP)[None, :] // CHUNK
    in_band = (kj >= qi) & (kj <= qi + N_LEFT_CHUNKS)
    return jnp.where(in_band[None], tab, NEG)


def _attention_fwd(proj, table, s, a):
    heads = a // HEAD_DIM
    scale = HEAD_DIM ** -0.5
    groups = s // Q_GROUP

    def body(q_ref, k_ref, v_ref, z_ref, tab_ref, o_ref, ya_ref, kp, vp):
        kp[0:PAD, :] = jnp.zeros((PAD, HEAD_DIM), BF16)
        vp[0:PAD, :] = jnp.zeros((PAD, HEAD_DIM), BF16)
        kp[PAD:, :] = k_ref[...].astype(BF16)
        vp[PAD:, :] = v_ref[...].astype(BF16)

        def group(g, carry):
            r0 = pl.multiple_of(g * Q_GROUP, Q_GROUP)
            q = q_ref[pl.ds(r0, Q_GROUP), :].astype(BF16)
            kb = kp[pl.ds(r0, K_GROUP), :]
            vb = vp[pl.ds(r0, K_GROUP), :]
            sc = lax.dot_general(q, kb, NT, preferred_element_type=F32) * scale + tab_ref[...]
            col = lax.broadcasted_iota(jnp.int32, (Q_GROUP, K_GROUP), 1)
            sc = jnp.where(col >= PAD - r0, sc, NEG)
            mx = jnp.max(sc, axis=-1, keepdims=True)
            e = jnp.exp(sc - mx)
            p = e / jnp.sum(e, axis=-1, keepdims=True)
            o = jnp.dot(p.astype(BF16), vb, preferred_element_type=F32)
            o_ref[pl.ds(r0, Q_GROUP), :] = o
            z = z_ref[pl.ds(r0, Q_GROUP), :]
            ya_ref[pl.ds(r0, Q_GROUP), :] = (o * (z * _sigmoid(z))).astype(BF16)
            return carry

        lax.fori_loop(0, groups, group, 0)

    col = lambda seg: (lambda h: (0, seg * heads + h))
    blk = lambda seg: pl.BlockSpec((s, HEAD_DIM), col(seg))
    return pl.pallas_call(
        body, name="attention_fwd", grid=(heads,),
        in_specs=[blk(0), blk(1), blk(2), blk(3), pl.BlockSpec((None, Q_GROUP, K_GROUP), lambda h: (h, 0, 0))],
        out_specs=[blk(0), blk(0)],
        out_shape=[jax.ShapeDtypeStruct((s, a), F32), jax.ShapeDtypeStruct((s, a), BF16)],
        scratch_shapes=[pltpu.VMEM((PAD + s, HEAD_DIM), BF16), pltpu.VMEM((PAD + s, HEAD_DIM), BF16)],
        compiler_params=_params(("parallel",)),
    )(proj, proj, proj, proj, table)


def _attention_bwd(proj, o, dya, table, s, a):
    heads = a // HEAD_DIM
    scale = HEAD_DIM ** -0.5
    groups = s // Q_GROUP

    def body(q_ref, k_ref, v_ref, z_ref, o_ref, dy_ref, tab_ref, dq_ref, dk_ref, dv_ref, dz_ref, dtab_ref,
             kp, vp, dkp, dvp):
        kp[0:PAD, :] = jnp.zeros((PAD, HEAD_DIM), BF16)
        vp[0:PAD, :] = jnp.zeros((PAD, HEAD_DIM), BF16)
        kp[PAD:, :] = k_ref[...].astype(BF16)
        vp[PAD:, :] = v_ref[...].astype(BF16)
        dkp[...] = jnp.zeros_like(dkp)
        dvp[...] = jnp.zeros_like(dvp)
        dtab_ref[...] = jnp.zeros_like(dtab_ref)

        def group(g, carry):
            r0 = pl.multiple_of(g * Q_GROUP, Q_GROUP)
            rows = pl.ds(r0, Q_GROUP)
            band = pl.ds(r0, K_GROUP)
            q = q_ref[rows, :].astype(BF16)
            kb = kp[band, :]
            vb = vp[band, :]
            sc = lax.dot_general(q, kb, NT, preferred_element_type=F32) * scale + tab_ref[...]
            col = lax.broadcasted_iota(jnp.int32, (Q_GROUP, K_GROUP), 1)
            sc = jnp.where(col >= PAD - r0, sc, NEG)
            mx = jnp.max(sc, axis=-1, keepdims=True)
            e = jnp.exp(sc - mx)
            p = e / jnp.sum(e, axis=-1, keepdims=True)
            z = z_ref[rows, :]
            dy = dy_ref[rows, :]
            si, dsi = _silu_and_grad(z)
            dz_ref[rows, :] = (dy * o_ref[rows, :] * dsi).astype(BF16)
            dob = (dy * si).astype(BF16)
            dp = lax.dot_general(dob, vb, NT, preferred_element_type=F32)
            ds = p * (dp - jnp.sum(p * dp, axis=-1, keepdims=True))
            dtab_ref[...] += ds
            dsb = (ds * scale).astype(BF16)
            dq_ref[rows, :] = jnp.dot(dsb, kb, preferred_element_type=F32).astype(BF16)
            dkp[band, :] += lax.dot_general(dsb, q, TN, preferred_element_type=F32)
            dvp[band, :] += lax.dot_general(p.astype(BF16), dob, TN, preferred_element_type=F32)
            return carry

        lax.fori_loop(0, groups, group, 0)
        dk_ref[...] = dkp[PAD:, :].astype(BF16)
        dv_ref[...] = dvp[PAD:, :].astype(BF16)

    col = lambda seg: (lambda h: (0, seg * heads + h))
    blk = lambda seg: pl.BlockSpec((s, HEAD_DIM), col(seg))
    tab_spec = pl.BlockSpec((None, Q_GROUP, K_GROUP), lambda h: (h, 0, 0))
    return pl.pallas_call(
        body, name="attention_bwd", grid=(heads,),
        in_specs=[blk(0), blk(1), blk(2), blk(3), blk(0), blk(0), tab_spec],
        out_specs=[blk(0)] * 4 + [tab_spec],
        out_shape=[jax.ShapeDtypeStruct((s, a), BF16)] * 4 + [jax.ShapeDtypeStruct(table.shape, F32)],
        scratch_shapes=[pltpu.VMEM((PAD + s, HEAD_DIM), BF16), pltpu.VMEM((PAD + s, HEAD_DIM), BF16),
                        pltpu.VMEM((PAD + s, HEAD_DIM), F32), pltpu.VMEM((PAD + s, HEAD_DIM), F32)],
        compiler_params=_params(("parallel",)),
    )(proj, proj, proj, proj, o, dya, table)


def _pick_window(gi, by_window):
    out = by_window[-1]
    for n in range(N_GROUPS - 2, -1, -1):
        out = jnp.where(gi == n, by_window[n], out)
    return out


def _inv_count(gi, first_row, rows):
    t = first_row + lax.broadcasted_iota(jnp.int32, (rows, 1), 0)
    w = jnp.left_shift(2, gi)
    return 1.0 / jnp.minimum(t + 1, w).astype(F32)


def _pool_fwd(proj, pw_full, pool_scale, s, a, p):
    pg = p // N_GROUPS
    ts = _div(s, 512)
    u0, z0 = 4 * a // pg, (4 * a + p) // pg
    hb = ts // HALO

    def body(u_ref, uh_ref, z_ref, pw_ref, ps_ref, d_ref, t_ref, y_ref, ext):
        gi, i = pl.program_id(0), pl.program_id(1)
        u = u_ref[...]
        ext[0:HALO, :] = jnp.where(i > 0, uh_ref[...], 0.0)
        ext[HALO:, :] = u
        e = ext[...]
        sums, shift = [], 1
        for _ in POOL_WINDOWS:
            e = e + pltpu.roll(e, shift, 0)
            sums.append(e)
            shift *= 2
        win = _pick_window(gi, sums)[HALO:, :]
        d = (win * _inv_count(gi, i * ts, ts) - u).astype(BF16)
        d_ref[...] = d
        t = jnp.dot(d, pw_ref[...].reshape(pg, pg), preferred_element_type=F32)
        t_ref[...] = t
        z = z_ref[...]
        y_ref[...] = (t * ps_ref[...] * (z * _sigmoid(z))).astype(BF16)

    out_spec = pl.BlockSpec((ts, pg), lambda g, i: (i, g))
    return pl.pallas_call(
        body, name="pool_fwd", grid=(N_GROUPS, s // ts),
        in_specs=[pl.BlockSpec((ts, pg), lambda g, i: (i, u0 + g)),
                  pl.BlockSpec((HALO, pg), lambda g, i: (jnp.maximum(i * hb - 1, 0), u0 + g)),
                  pl.BlockSpec((ts, pg), lambda g, i: (i, z0 + g)),
                  pl.BlockSpec((N_CHIPS, None, pg // N_CHIPS, pg), lambda g, i: (0, g, 0, 0)),
                  pl.BlockSpec((1, pg), lambda g, i: (0, g))],
        out_specs=[out_spec] * 3,
        out_shape=[jax.ShapeDtypeStruct((s, p), BF16), jax.ShapeDtypeStruct((s, p), F32),
                   jax.ShapeDtypeStruct((s, p), BF16)],
        scratch_shapes=[pltpu.VMEM((ts + HALO, pg), F32)],
        compiler_params=_params(("parallel", "parallel")),
    )(proj, proj, proj, pw_full, pool_scale.reshape(1, p))


def _pool_bwd(proj, dyp, t, d, pw_full, pool_scale, s, a, p):
    pg = p // N_GROUPS
    ts = _div(s, 512)
    nt = s // ts
    z0 = (4 * a + p) // pg
    hb = ts // HALO
    last_halo = s // HALO - 1

    def body(dy_ref, dyh_ref, z_ref, zh_ref, t_ref, th_ref, d_ref, pw_ref, ps_ref,
             du_ref, dz_ref, dpw_ref, dps_ref, ext):
        gi, i = pl.program_id(0), pl.program_id(1)
        ps = ps_ref[...]
        pw = pw_ref[...].reshape(pg, pg)

        @pl.when(i == 0)
        def _():
            dpw_ref[...] = jnp.zeros_like(dpw_ref)
            dps_ref[...] = jnp.zeros_like(dps_ref)

        def through_gate(dy, z, tt):
            si, dsi = _silu_and_grad(z)
            return dy * si, dy * (tt * ps) * dsi

        tt = t_ref[...]
        dyl, dz = through_gate(dy_ref[...], z_ref[...], tt)
        dz_ref[...] = dz.astype(BF16)
        dps_ref[...] += jnp.sum(dyl * tt, axis=0, keepdims=True)
        dtb = (dyl * ps).astype(BF16)
        dpw_ref[...] += lax.dot_general(d_ref[...], dtb, TN, preferred_element_type=F32).reshape(dpw_ref.shape)
        dd = lax.dot_general(dtb, pw, NT, preferred_element_type=F32)
        dylh, _ = through_gate(dyh_ref[...], zh_ref[...], th_ref[...])
        ddh = lax.dot_general((dylh * ps).astype(BF16), pw, NT, preferred_element_type=F32)
        ddh = jnp.where(i < nt - 1, ddh, 0.0)
        ext[0:ts, :] = dd * _inv_count(gi, i * ts, ts)
        ext[ts:, :] = ddh * _inv_count(gi, (i + 1) * ts, HALO)
        e = ext[...]
        rows = ts + HALO
        sums, shift = [], 1
        for _ in POOL_WINDOWS:
            e = e + pltpu.roll(e, rows - shift, 0)
            sums.append(e)
            shift *= 2
        du_ref[...] = (_pick_window(gi, sums)[:ts, :] - dd).astype(BF16)

    tile = lambda c0: pl.BlockSpec((ts, pg), lambda g, i: (i, c0 + g))
    halo = lambda c0: pl.BlockSpec((HALO, pg), lambda g, i: (jnp.minimum((i + 1) * hb, last_halo), c0 + g))
    pw_spec = pl.BlockSpec((N_CHIPS, None, pg // N_CHIPS, pg), lambda g, i: (0, g, 0, 0))
    return pl.pallas_call(
        body, name="pool_bwd", grid=(N_GROUPS, nt),
        in_specs=[tile(0), halo(0), tile(z0), halo(z0), tile(0), halo(0), tile(0), pw_spec,
                  pl.BlockSpec((1, pg), lambda g, i: (0, g))],
        out_specs=[tile(0), tile(0), pw_spec, pl.BlockSpec((1, pg), lambda g, i: (0, g))],
        out_shape=[jax.ShapeDtypeStruct((s, p), BF16), jax.ShapeDtypeStruct((s, p), BF16),
                   jax.ShapeDtypeStruct(pw_full.shape, F32), jax.ShapeDtypeStruct((1, p), F32)],
        scratch_shapes=[pltpu.VMEM((ts + HALO, pg), F32)],
        compiler_params=_params(("parallel", "arbitrary")),
    )(dyp, dyp, proj, proj, t, t, d, pw_full, pool_scale.reshape(1, p))


def _merge_fwd(ya, yp, woa_full, wop_full, proj, gb_full, s, d, a):
    sw = d // N_CHIPS
    tm, tn, tk = _div(s, 512), _div(sw, 1024), _div(a, 512)
    per = sw // tn
    nk = a // tk
    ga0, gp0 = (4 * a + 2 * a) // tn, (4 * a + 2 * a + d) // tn

    def body(ya_ref, yp_ref, wa_ref, wp_ref, ga_ref, gp_ref, gb_ref, a_out, b_out, m_out, acc_a, acc_b):
        k = pl.program_id(2)

        @pl.when(k == 0)
        def _():
            acc_a[...] = jnp.zeros_like(acc_a)
            acc_b[...] = jnp.zeros_like(acc_b)

        acc_a[...] += jnp.dot(ya_ref[...], wa_ref[...], preferred_element_type=F32)
        acc_b[...] += jnp.dot(yp_ref[...], wp_ref[...], preferred_element_type=F32)

        @pl.when(k == nk - 1)
        def _():
            av, bv = acc_a[...], acc_b[...]
            a_out[...] = av
            b_out[...] = bv
            sa = _sigmoid(ga_ref[...] + gb_ref[0:1, :])
            sp = _sigmoid(gp_ref[...] + gb_ref[1:2, :])
            m_out[...] = (sa * av + sp * bv).astype(BF16)

    act = pl.BlockSpec((tm, tk), lambda i, j, k: (i, k))
    wgt = pl.BlockSpec((None, tk, tn), lambda i, j, k: (j // per, k, j % per))
    out = pl.BlockSpec((tm, tn), lambda i, j, k: (i, j))
    return pl.pallas_call(
        body, name="merge_fwd", grid=(s // tm, d // tn, nk),
        in_specs=[act, act, wgt, wgt,
                  pl.BlockSpec((tm, tn), lambda i, j, k: (i, ga0 + j)),
                  pl.BlockSpec((tm, tn), lambda i, j, k: (i, gp0 + j)),
                  pl.BlockSpec((None, 2, tn), lambda i, j, k: (j // per, 0, j % per))],
        out_specs=[out, out, out],
        out_shape=[jax.ShapeDtypeStruct((s, d), F32), jax.ShapeDtypeStruct((s, d), F32),
                   jax.ShapeDtypeStruct((s, d), BF16)],
        scratch_shapes=[pltpu.VMEM((tm, tn), F32), pltpu.VMEM((tm, tn), F32)],
        compiler_params=_params(("parallel", "parallel", "arbitrary")),
    )(ya, yp, woa_full, wop_full, proj, proj, gb_full)


def _out_and_loss(mb, wo, x, target, final_gain, s, d):
    tm, tn, tk = _div(s, 256), _div(d, 1024), _div(d, 1024)
    nj, nk = d // tn, d // tk

    def body(m_ref, w_ref, x_ref, t_ref, g_ref, loss_ref, dx_ref, dxb_ref, dg_ref, acc, row):
        i, j, k = pl.program_id(0), pl.program_id(1), pl.program_id(2)

        @pl.when(k == 0)
        def _():
            acc[...] = jnp.zeros_like(acc)

        acc[...] += jnp.dot(m_ref[...], w_ref[...], preferred_element_type=F32)

        @pl.when(k == nk - 1)
        def _():
            row[j] = acc[...] + x_ref[...]

        @pl.when((i == 0) & (j == 0) & (k == 0))
        def _():
            dg_ref[...] = jnp.zeros_like(dg_ref)

        @pl.when((j == nj - 1) & (k == nk - 1))
        def _():
            ssq = jnp.zeros((tm, 1), F32)
            for c in range(nj):
                v = row[c]
                ssq = ssq + jnp.sum(v * v, axis=-1, keepdims=True)
            r = lax.rsqrt(ssq / d + EPS)
            err2 = jnp.zeros((tm, 1), F32)
            dot = jnp.zeros((tm, 1), F32)
            for c in range(nj):
                cols = slice(c * tn, (c + 1) * tn)
                xn = row[c] * r
                e = xn * g_ref[:, cols] - t_ref[:, cols]
                err2 = err2 + jnp.sum(e * e, axis=-1, keepdims=True)
                dy = e / d
                dg_ref[:, cols] += jnp.sum(dy * xn, axis=0, keepdims=True)
                dot = dot + jnp.sum(dy * g_ref[:, cols] * xn, axis=-1, keepdims=True)
            loss_ref[...] = 0.5 * err2 / d
            mean_dot = dot / d
            for c in range(nj):
                cols = slice(c * tn, (c + 1) * tn)
                xn = row[c] * r
                dy = (xn * g_ref[:, cols] - t_ref[:, cols]) / d
                dx = r * (dy * g_ref[:, cols] - xn * mean_dot)
                dx_ref[:, cols] = dx
                dxb_ref[:, cols] = dx.astype(BF16)

    rows = pl.BlockSpec((tm, d), lambda i, j, k: (i, 0))
    return pl.pallas_call(
        body, name="out_proj_loss", grid=(s // tm, nj, nk),
        in_specs=[pl.BlockSpec((tm, tk), lambda i, j, k: (i, k)), pl.BlockSpec((tk, tn), lambda i, j, k: (k, j)),
                  pl.BlockSpec((tm, tn), lambda i, j, k: (i, j)), rows, pl.BlockSpec((1, d), lambda i, j, k: (0, 0))],
        out_specs=[pl.BlockSpec((tm, 1), lambda i, j, k: (i, 0)), rows, rows,
                   pl.BlockSpec((1, d), lambda i, j, k: (0, 0))],
        out_shape=[jax.ShapeDtypeStruct((s, 1), F32), jax.ShapeDtypeStruct((s, d), F32),
                   jax.ShapeDtypeStruct((s, d), BF16), jax.ShapeDtypeStruct((1, d), F32)],
        scratch_shapes=[pltpu.VMEM((tm, tn), F32), pltpu.VMEM((nj, tm, tn), F32)],
        compiler_params=_params(("arbitrary", "arbitrary", "arbitrary")),
    )(mb, wo, x, target, final_gain.reshape(1, d))


def _merge_bwd(dxb, wo, a_val, b_val, proj, gb_full, s, d, a):
    sw = d // N_CHIPS
    tm, tn, tk = _div(s, 512), _div(sw, 1024), _div(d, 512)
    per = sw // tn
    ga0, gp0 = (4 * a + 2 * a) // tn, (4 * a + 2 * a + d) // tn

    def epilogue(dm, ex, outs):
        a_ref, b_ref, ga_ref, gp_ref, gb_ref = ex
        da_ref, db_ref, dga_ref, dgp_ref, dgb_ref = outs
        sa = _sigmoid(ga_ref[...] + gb_ref[0:1, :])
        sp = _sigmoid(gp_ref[...] + gb_ref[1:2, :])
        da_ref[...] = (dm * sa).astype(BF16)
        db_ref[...] = (dm * sp).astype(BF16)
        dga = dm * a_ref[...] * sa * (1.0 - sa)
        dgp = dm * b_ref[...] * sp * (1.0 - sp)
        dga_ref[...] = dga.astype(BF16)
        dgp_ref[...] = dgp.astype(BF16)

        @pl.when(pl.program_id(1) == 0)
        def _():
            dgb_ref[...] = jnp.zeros_like(dgb_ref)

        dgb_ref[0:1, :] += jnp.sum(dga, axis=0, keepdims=True)
        dgb_ref[1:2, :] += jnp.sum(dgp, axis=0, keepdims=True)

    tile = pl.BlockSpec((tm, tn), lambda j, i, k: (i, j))
    sd = jax.ShapeDtypeStruct((s, d), BF16)
    return _matmul(
        "merge_bwd", NT, (d // tn, s // tm, d // tk),
        dxb, pl.BlockSpec((tm, tk), lambda j, i, k: (i, k)),
        wo, pl.BlockSpec((tn, tk), lambda j, i, k: (j, k)),
        (tm, tn), [sd, sd, sd, sd, jax.ShapeDtypeStruct((2, d), F32)],
        [tile, tile, tile, tile, pl.BlockSpec((2, tn), lambda j, i, k: (0, j))],
        extra=(a_val, b_val, proj, proj, gb_full),
        extra_specs=(tile, tile, pl.BlockSpec((tm, tn), lambda j, i, k: (i, ga0 + j)),
                     pl.BlockSpec((tm, tn), lambda j, i, k: (i, gp0 + j)),
                     pl.BlockSpec((None, 2, tn), lambda j, i, k: (j // per, 0, j % per))),
        epilogue=epilogue, accumulate_outs=True)


def _weight_grad(name, act, dout, shard_cols):
    s, kdim = act.shape
    n = dout.shape[1]
    if shard_cols:
        sw = n // N_CHIPS
        tm, tn = _div(kdim, 1024), _div(sw, 1024)
        per = sw // tn
        shape = (N_CHIPS, kdim, sw)
        out = pl.BlockSpec((None, tm, tn), lambda i, j, k: (j // per, i, j % per))
    else:
        sh = kdim // N_CHIPS
        tm, tn = _div(sh, 1024), _div(n, 1024)
        per = sh // tm
        shape = (N_CHIPS, sh, n)
        out = pl.BlockSpec((None, tm, tn), lambda i, j, k: (i // per, i % per, j))
    tk = _div(s, 512)
    return _matmul(
        name, TN, (kdim // tm, n // tn, s // tk),
        act, pl.BlockSpec((tk, tm), lambda i, j, k: (k, i)),
        dout, pl.BlockSpec((tk, tn), lambda i, j, k: (k, j)),
        (tm, tn), [jax.ShapeDtypeStruct(shape, F32), jax.ShapeDtypeStruct(shape, BF16)], [out, out])


def _norm_in_bwd(x, dh, dx2, gain):
    s, d = x.shape
    tr = _div(s, 256)

    def body(x_ref, dh_ref, dx2_ref, g_ref, gx_ref, dg_ref):
        @pl.when(pl.program_id(0) == 0)
        def _():
            dg_ref[...] = jnp.zeros_like(dg_ref)

        xv = x_ref[...]
        r = lax.rsqrt(jnp.mean(xv * xv, axis=-1, keepdims=True) + EPS)
        xn = xv * r
        dhv = dh_ref[...]
        dg_ref[...] += jnp.sum(dhv * xn, axis=0, keepdims=True)
        dxn = dhv * g_ref[...]
        gx_ref[...] = r * (dxn - xn * jnp.mean(dxn * xn, axis=-1, keepdims=True)) + dx2_ref[...]

    rows = pl.BlockSpec((tr, d), lambda i: (i, 0))
    vec = pl.BlockSpec((1, d), lambda i: (0, 0))
    return pl.pallas_call(
        body, name="norm_in_bwd", grid=(s // tr,),
        in_specs=[rows, rows, rows, vec], out_specs=[rows, vec],
        out_shape=[jax.ShapeDtypeStruct((s, d), F32), jax.ShapeDtypeStruct((1, d), F32)],
        compiler_params=_params(("arbitrary",)),
    )(x, dh, dx2, gain.reshape(1, d))


def _pack(vectors):
    flat = jnp.concatenate([v.reshape(-1).astype(F32) for v in vectors])
    rows = -(-flat.shape[0] // 1024) * 8
    return jnp.pad(flat, (0, rows * 128 - flat.shape[0])).reshape(rows, 128)


def _unpack(packed, like):
    flat, out, at = packed.reshape(-1), [], 0
    for v in like:
        out.append(flat[at:at + v.size].reshape(v.shape))
        at += v.size
    return out


def _small_adamw(g, w, m, v):
    def body(g_ref, w_ref, m_ref, v_ref, d_ref, mo_ref, vo_ref):
        delta, m2, v2 = _adamw_math(w_ref[...], g_ref[...], m_ref[...], v_ref[...])
        d_ref[...] = delta
        mo_ref[...] = m2
        vo_ref[...] = v2

    return pl.pallas_call(body, name="small_adamw", out_shape=[jax.ShapeDtypeStruct(g.shape, F32)] * 3)(g, w, m, v)


def kernel(x, norm_gain, w_in, rel_bias, pool_w, pool_scale, w_out_attn, w_out_pool, gate_bias, w_out, final_gain, loss_target, m_norm_gain, m_w_in, m_rel_bias, m_pool_w, m_pool_scale, m_w_out_attn, m_w_out_pool, m_gate_bias, m_w_out, m_final_gain, v_norm_gain, v_w_in, v_rel_bias, v_pool_w, v_pool_scale, v_w_out_attn, v_w_out_pool, v_gate_bias, v_w_out, v_final_gain):
    _, s, d = x.shape
    a = p = d // 2
    n_in = w_in.shape[1] * N_CHIPS
    sw_in = w_in.shape[1]
    pg = p // N_GROUPS
    xs = x.reshape(s, d)
    target = loss_target.reshape(s, d)
    c_arr = lax.axis_index("c").astype(jnp.int32).reshape(1)
    chip = 2 * lax.axis_index("x") + lax.axis_index("y")

    win_full, woa_full, wop_full, wo_full, pw_full, gb_full = _all_gather(
        [w_in.astype(BF16), w_out_attn.astype(BF16), w_out_pool.astype(BF16), w_out.astype(BF16),
         pool_w.astype(BF16), gate_bias],
        [True, True, True, True, False, False])
    wo_mat = wo_full.reshape(d, d)

    hb = _norm_in(xs, norm_gain)
    tm, tn, tk = _div(s, 1024), _div(sw_in, 1024), _div(d, 512)
    per_in = sw_in // tn
    proj = _matmul(
        "in_proj", NN, (s // tm, n_in // tn, d // tk),
        hb, pl.BlockSpec((tm, tk), lambda i, j, k: (i, k)),
        win_full, pl.BlockSpec((None, tk, tn), lambda i, j, k: (j // per_in, k, j % per_in)),
        (tm, tn), [jax.ShapeDtypeStruct((s, n_in), F32)], [pl.BlockSpec((tm, tn), lambda i, j, k: (i, j))])[0]
    table = _bias_table(rel_bias)
    o_attn, ya = _attention_fwd(proj, table, s, a)
    d_pool, t_pool, yp = _pool_fwd(proj, pw_full, pool_scale, s, a, p)
    a_val, b_val, mb = _merge_fwd(ya, yp, woa_full, wop_full, proj, gb_full, s, d, a)
    loss_rows, dx2, dx2b, g_final = _out_and_loss(mb, wo_mat, xs, target, final_gain, s, d)
    loss = lax.psum(jnp.sum(loss_rows), ("x", "y", "c"))

    da, db, dga, dgp, g_gate_full = _merge_bwd(dx2b, wo_mat, a_val, b_val, proj, gb_full, s, d, a)
    gwo, gwo_b = _weight_grad("grad_w_out", mb, dx2b, shard_cols=False)
    gwoa, gwoa_b = _weight_grad("grad_w_out_attn", ya, da, shard_cols=True)
    gwop, gwop_b = _weight_grad("grad_w_out_pool", yp, db, shard_cols=True)

    sw = d // N_CHIPS
    tm, tn, tk = _div(s, 1024), _div(a, 1024), _div(sw, 1024)
    per_o = sw // tk

    def back_through(name, dout, w_full):
        return _matmul(
            name, NT, (s // tm, a // tn, d // tk),
            dout, pl.BlockSpec((tm, tk), lambda i, j, k: (i, k)),
            w_full, pl.BlockSpec((None, tn, tk), lambda i, j, k: (k // per_o, j, k % per_o)),
            (tm, tn), [jax.ShapeDtypeStruct((s, a), F32)], [pl.BlockSpec((tm, tn), lambda i, j, k: (i, j))])[0]

    dya = back_through("grad_y_attn", da, woa_full)
    dyp = back_through("grad_y_pool", db, wop_full)
    du, dzp, gpw, g_pscale = _pool_bwd(proj, dyp, t_pool, d_pool, pw_full, pool_scale, s, a, p)
    dq, dk, dv, dza, dtable = _attention_bwd(proj, o_attn, dya, table, s, a)
    g_rel = jax.vjp(_bias_table, rel_bias)[1](dtable)[0]
    dproj = jnp.concatenate([dq, dk, dv, dza, du, dzp, dga, dgp], axis=1)

    tm, tn, tk = _div(s, 1024), _div(d, 1024), _div(sw_in, 1024)
    per_k = sw_in // tk
    dh = _matmul(
        "grad_h", NT, (s // tm, d // tn, n_in // tk),
        dproj, pl.BlockSpec((tm, tk), lambda i, j, k: (i, k)),
        win_full, pl.BlockSpec((None, tn, tk), lambda i, j, k: (k // per_k, j, k % per_k)),
        (tm, tn), [jax.ShapeDtypeStruct((s, d), F32)], [pl.BlockSpec((tm, tn), lambda i, j, k: (i, j))])[0]
    gwin, gwin_b = _weight_grad("grad_w_in", hb, dproj, shard_cols=True)
    grad_x, g_norm = _norm_in_bwd(xs, dh, dx2, norm_gain)

    gpw3 = gpw.reshape(N_CHIPS, pg, pg)
    partial_f32 = [gwin, gwoa, gwop, gwo, gpw3]
    partial_b16 = [gwin_b, gwoa_b, gwop_b, gwo_b, gpw3.astype(BF16)]
    names = ["w_in", "w_out_attn", "w_out_pool", "w_out", "pool_w"]
    from_sibling = _swap_other_half(partial_b16)
    pair = [_pair_sum("pair_sum_" + n, g, r, c_arr) for n, g, r in zip(names, partial_f32, from_sibling)]
    from_chips = _scatter_to_owners(pair)
    halves = [_chip_sum("chip_sum_" + n, r, c_arr) for n, r in zip(names, from_chips)]
    grads = _share_half(halves)

    big = {}
    weights = [w_in, w_out_attn, w_out_pool, w_out, pool_w.reshape(pg, pg)]
    ms = [m_w_in, m_w_out_attn, m_w_out_pool, m_w_out, m_pool_w.reshape(pg, pg)]
    vs = [v_w_in, v_w_out_attn, v_w_out_pool, v_w_out, v_pool_w.reshape(pg, pg)]
    for n, g, w, m, v in zip(names, grads, weights, ms, vs):
        big[n] = [r.reshape(pool_w.shape) if n == "pool_w" else r for r in _adamw("adamw_" + n, g, w, m, v)]

    small_like = [norm_gain, final_gain, pool_scale, rel_bias, jnp.zeros((2, d), F32)]
    summed = _sum_slots("small_grads_sum", _all_to_all_small(_pack([g_norm, g_final, g_pscale, g_rel, g_gate_full])))
    g_norm_t, g_final_t, g_pscale_t, g_rel_t, g_gate_t = _unpack(summed, small_like)
    g_gate_t = lax.dynamic_slice_in_dim(g_gate_t, chip * sw, sw, axis=1)
    small_g = [g_norm_t, g_final_t, g_pscale_t, g_rel_t, g_gate_t]
    small_w = [norm_gain, final_gain, pool_scale, rel_bias, gate_bias]
    small_m = [m_norm_gain, m_final_gain, m_pool_scale, m_rel_bias, m_gate_bias]
    small_v = [v_norm_gain, v_final_gain, v_pool_scale, v_rel_bias, v_gate_bias]
    packed = _small_adamw(_pack(small_g), _pack(small_w), _pack(small_m), _pack(small_v))
    sd, sm, sv = [_unpack(t, small_w) for t in packed]
    small = {n: [small_g[i], sd[i], sm[i], sv[i]]
             for i, n in enumerate(["norm_gain", "final_gain", "pool_scale", "rel_bias", "gate_bias"])}

    every = {**big, **small}
    order = ["norm_gain", "w_in", "rel_bias", "pool_w", "pool_scale", "w_out_attn", "w_out_pool", "gate_bias",
             "w_out", "final_gain"]
    return (loss, grad_x.reshape(x.shape), *[every[n][0] for n in order], *[every[n][1] for n in order],
            *[every[n][2] for n in order], *[every[n][3] for n in order])
```

```python
import math

import jax
import jax.numpy as jnp
from jax import lax
from jax.experimental import pallas as pl
from jax.experimental.pallas import tpu as pltpu

F32 = jnp.float32
BF16 = jnp.bfloat16
MESH = pl.DeviceIdType.MESH
ANY = pl.BlockSpec(memory_space=pl.ANY)

N_CHIPS = 4
N_DEV = 8
CHUNK = 64
N_LEFT_CHUNKS = 8
PAD = N_LEFT_CHUNKS * CHUNK
HEAD_DIM = 128
MAX_REL = 128
POOL_WINDOWS = (2, 4, 8, 16)
N_GROUPS = len(POOL_WINDOWS)
HALO = 16
Q_GROUP = 4 * CHUNK
K_GROUP = Q_GROUP + PAD
NEG = -1e30
EPS = 1e-6
ADAM_LR, ADAM_B1, ADAM_B2, ADAM_EPS, ADAM_WD, ADAM_STEP = 0.001, 0.9, 0.999, 1e-08, 0.01, 10
VMEM_LIMIT = 56 * 1024 * 1024

NN = (((1,), (0,)), ((), ()))
NT = (((1,), (1,)), ((), ()))
TN = (((0,), (0,)), ((), ()))


def _div(n, pref):
    if n <= pref:
        return n
    for t in range(pref - pref % 128, 0, -128):
        if n % t == 0:
            return t
    raise ValueError((n, pref))


def _params(sem, **kw):
    return pltpu.CompilerParams(dimension_semantics=sem, vmem_limit_bytes=VMEM_LIMIT, **kw)


def _sigmoid(z):
    return jax.nn.sigmoid(z)


def _silu_and_grad(z):
    sg = _sigmoid(z)
    return z * sg, sg * (1.0 + z * (1.0 - sg))


def _matmul(name, dn, grid, a, a_spec, b, b_spec, acc_shape, outs, out_specs, extra=(), extra_specs=(),
            epilogue=None, accumulate_outs=False, aliases=None):
    nk = grid[2]
    ne, no = len(extra), len(outs)

    def finish(res, ex, out_refs):
        if epilogue is None:
            for o in out_refs:
                o[...] = res.astype(o.dtype)
        else:
            epilogue(res, ex, out_refs)

    def body(*refs):
        a_ref, b_ref = refs[0], refs[1]
        ex = refs[2:2 + ne]
        out_refs = refs[2 + ne:2 + ne + no]
        if nk == 1:
            finish(lax.dot_general(a_ref[...], b_ref[...], dn, preferred_element_type=F32), ex, out_refs)
            return
        acc = refs[-1]
        k = pl.program_id(2)

        @pl.when(k == 0)
        def _():
            acc[...] = jnp.zeros_like(acc)

        acc[...] += lax.dot_general(a_ref[...], b_ref[...], dn, preferred_element_type=F32)

        @pl.when(k == nk - 1)
        def _():
            finish(acc[...], ex, out_refs)

    sem = ("arbitrary",) * 3 if accumulate_outs else ("parallel", "parallel", "arbitrary")
    return pl.pallas_call(
        body, name=name, grid=grid,
        in_specs=[a_spec, b_spec, *extra_specs], out_specs=list(out_specs), out_shape=list(outs),
        scratch_shapes=[] if nk == 1 else [pltpu.VMEM(acc_shape, F32)],
        input_output_aliases=aliases or {},
        compiler_params=_params(sem),
    )(a, b, *extra)


def _place():
    x, y, c = lax.axis_index("x"), lax.axis_index("y"), lax.axis_index("c")
    chips = [(1 - x, y), (x, 1 - y), (1 - x, 1 - y)]
    return x, y, c, chips


def _remote(src, dst, send_sems, recv_sems, k, dev):
    return pltpu.make_async_remote_copy(src_ref=src, dst_ref=dst, send_sem=send_sems.at[k], recv_sem=recv_sems.at[k],
                                        device_id=dev, device_id_type=MESH)


def _all_gather(shards, split):
    n = len(shards)

    def body(*refs):
        ins, outs = refs[:n], refs[n:2 * n]
        send_sems, recv_sems, local_sems = refs[2 * n:]
        x, y, c, chips = _place()
        me = 2 * x + y
        sibling = (x, y, 1 - c)
        started = []
        for t in range(n):
            cp = pltpu.make_async_copy(ins[t], outs[t].at[me], local_sems.at[t])
            cp.start()
            started.append(cp)
        sends, last_recvs = [], []

        def halves_of(t):
            half = ins[t].shape[0] // 2
            return pl.ds(c * half, half), pl.ds((1 - c) * half, half)

        for t in range(n):
            for j, (cx, cy) in enumerate(chips):
                k = 6 * t + j
                if split[t]:
                    rows, _ = halves_of(t)
                    cp = _remote(ins[t].at[rows], outs[t].at[me, rows], send_sems, recv_sems, k, (cx, cy, c))
                else:
                    cp = _remote(ins[t], outs[t].at[me], send_sems, recv_sems, k, (cx, cy, c))
                cp.start()
                sends.append(cp)
        for t in range(n):
            for j, (cx, cy) in enumerate(chips):
                src_chip = 2 * cx + cy
                k = 6 * t + j
                if split[t]:
                    rows, other = halves_of(t)
                    got = outs[t].at[src_chip, rows]
                    _remote(got, got, send_sems, recv_sems, k, (cx, cy, c)).wait_recv()
                    fwd = _remote(got, got, send_sems, recv_sems, k + 3, sibling)
                    fwd.start()
                    sends.append(fwd)
                    theirs = outs[t].at[src_chip, other]
                    last_recvs.append(_remote(theirs, theirs, send_sems, recv_sems, k + 3, sibling))
                else:
                    got = outs[t].at[src_chip]
                    last_recvs.append(_remote(got, got, send_sems, recv_sems, k, (cx, cy, c)))
        for cp in last_recvs:
            cp.wait_recv()
        for cp in sends:
            cp.wait_send()
        for cp in started:
            cp.wait()

    return pl.pallas_call(
        body, name="all_gather_weights",
        in_specs=[ANY] * n, out_specs=[ANY] * n,
        out_shape=[jax.ShapeDtypeStruct((N_CHIPS,) + s.shape, s.dtype) for s in shards],
        scratch_shapes=[pltpu.SemaphoreType.DMA((6 * n,)), pltpu.SemaphoreType.DMA((6 * n,)),
                        pltpu.SemaphoreType.DMA((n,))],
    )(*shards)


def _swap_other_half(parts):
    n = len(parts)

    def body(*refs):
        ins, outs = refs[:n], refs[n:2 * n]
        send_sems, recv_sems = refs[2 * n:]
        x, y, c, _ = _place()
        cps = []
        for t in range(n):
            half = ins[t].shape[1] // 2
            cp = _remote(ins[t].at[:, pl.ds((1 - c) * half, half)], outs[t], send_sems, recv_sems, t, (x, y, 1 - c))
            cp.start()
            cps.append(cp)
        for cp in cps:
            cp.wait()

    return pl.pallas_call(
        body, name="reduce_pair_exchange",
        in_specs=[ANY] * n, out_specs=[ANY] * n,
        out_shape=[jax.ShapeDtypeStruct((p.shape[0], p.shape[1] // 2, p.shape[2]), p.dtype) for p in parts],
        scratch_shapes=[pltpu.SemaphoreType.DMA((n,)), pltpu.SemaphoreType.DMA((n,))],
    )(*parts)


def _scatter_to_owners(parts):
    n = len(parts)

    def body(*refs):
        ins, outs = refs[:n], refs[n:2 * n]
        send_sems, recv_sems, local_sems = refs[2 * n:]
        x, y, c, chips = _place()
        me = 2 * x + y
        cps, own = [], []
        for t in range(n):
            cp = pltpu.make_async_copy(ins[t].at[me], outs[t].at[me], local_sems.at[t])
            cp.start()
            own.append(cp)
            for j, (cx, cy) in enumerate(chips):
                cp = _remote(ins[t].at[2 * cx + cy], outs[t].at[me], send_sems, recv_sems, 3 * t + j, (cx, cy, c))
                cp.start()
                cps.append(cp)
        for t in range(n):
            for j, (cx, cy) in enumerate(chips):
                got = outs[t].at[2 * cx + cy]
                _remote(got, got, send_sems, recv_sems, 3 * t + j, (cx, cy, c)).wait_recv()
        for cp in cps:
            cp.wait_send()
        for cp in own:
            cp.wait()

    return pl.pallas_call(
        body, name="reduce_scatter_chips",
        in_specs=[ANY] * n, out_specs=[ANY] * n,
        out_shape=[jax.ShapeDtypeStruct(p.shape, p.dtype) for p in parts],
        scratch_shapes=[pltpu.SemaphoreType.DMA((3 * n,)), pltpu.SemaphoreType.DMA((3 * n,)),
                        pltpu.SemaphoreType.DMA((n,))],
    )(*parts)


def _share_half(fulls):
    n = len(fulls)

    def body(*refs):
        ins, outs = refs[:n], refs[n:2 * n]
        send_sems, recv_sems = refs[2 * n:]
        x, y, c, _ = _place()
        cps = []
        for t in range(n):
            half = ins[t].shape[0] // 2
            mine = outs[t].at[pl.ds(c * half, half)]
            theirs = outs[t].at[pl.ds((1 - c) * half, half)]
            cp = _remote(mine, mine, send_sems, recv_sems, t, (x, y, 1 - c))
            cp.start()
            cps.append((cp, _remote(theirs, theirs, send_sems, recv_sems, t, (x, y, 1 - c))))
        for cp, rv in cps:
            rv.wait_recv()
            cp.wait_send()

    return pl.pallas_call(
        body, name="reduce_share_halves",
        in_specs=[ANY] * n, out_specs=[ANY] * n,
        out_shape=[jax.ShapeDtypeStruct(f.shape, f.dtype) for f in fulls],
        input_output_aliases={t: t for t in range(n)},
        scratch_shapes=[pltpu.SemaphoreType.DMA((n,)), pltpu.SemaphoreType.DMA((n,))],
    )(*fulls)


def _all_to_all_small(packed):
    def body(in_ref, out_ref, send_sems, recv_sems, local_sem):
        x, y, c, _ = _place()
        me = 4 * x + 2 * y + c
        own = pltpu.make_async_copy(in_ref, out_ref.at[me], local_sem)
        own.start()
        cps, rvs = [], []
        for k in range(1, N_DEV):
            fx, fy, fc = (k >> 2) & 1, (k >> 1) & 1, k & 1
            px, py, pc = x ^ fx, y ^ fy, c ^ fc
            cp = _remote(in_ref, out_ref.at[me], send_sems, recv_sems, k - 1, (px, py, pc))
            cp.start()
            cps.append(cp)
            got = out_ref.at[4 * px + 2 * py + pc]
            rvs.append(_remote(got, got, send_sems, recv_sems, k - 1, (px, py, pc)))
        for rv in rvs:
            rv.wait_recv()
        for cp in cps:
            cp.wait_send()
        own.wait()

    return pl.pallas_call(
        body, name="small_grads_exchange",
        in_specs=[ANY], out_specs=ANY,
        out_shape=jax.ShapeDtypeStruct((N_DEV,) + packed.shape, packed.dtype),
        scratch_shapes=[pltpu.SemaphoreType.DMA((N_DEV - 1,)), pltpu.SemaphoreType.DMA((N_DEV - 1,)),
                        pltpu.SemaphoreType.DMA],
    )(packed)


def _pair_sum(name, g, recv, c_arr):
    _, rows, cols = g.shape
    half = rows // 2
    tr, tc = _div(half, 512), _div(cols, 1024)
    nrb = half // tr

    def body(c_ref, g_ref, r_ref, o_ref):
        o_ref[...] = (g_ref[...] + r_ref[...].astype(F32)).astype(BF16)

    return pl.pallas_call(
        body, name=name,
        grid_spec=pltpu.PrefetchScalarGridSpec(
            num_scalar_prefetch=1, grid=(N_CHIPS, nrb, cols // tc),
            in_specs=[pl.BlockSpec((None, tr, tc), lambda s, i, j, c: (s, c[0] * nrb + i, j)),
                      pl.BlockSpec((None, tr, tc), lambda s, i, j, c: (s, i, j))],
            out_specs=pl.BlockSpec((None, tr, tc), lambda s, i, j, c: (s, i, j))),
        out_shape=jax.ShapeDtypeStruct(recv.shape, BF16),
        compiler_params=_params(("parallel", "parallel", "parallel")),
    )(c_arr, g, recv)


def _chip_sum(name, recv, c_arr):
    _, half, cols = recv.shape
    tr, tc = _div(half, 512), _div(cols, 1024)
    nrb = half // tr

    def body(c_ref, r_ref, o_ref):
        acc = r_ref[0].astype(F32)
        for s in range(1, N_CHIPS):
            acc = acc + r_ref[s].astype(F32)
        o_ref[...] = acc

    return pl.pallas_call(
        body, name=name,
        grid_spec=pltpu.PrefetchScalarGridSpec(
            num_scalar_prefetch=1, grid=(nrb, cols // tc),
            in_specs=[pl.BlockSpec((N_CHIPS, tr, tc), lambda i, j, c: (0, i, j))],
            out_specs=pl.BlockSpec((tr, tc), lambda i, j, c: (c[0] * nrb + i, j))),
        out_shape=jax.ShapeDtypeStruct((2 * half, cols), F32),
        compiler_params=_params(("parallel", "parallel")),
    )(c_arr, recv)


def _adamw_math(w, g, m, v):
    m2 = ADAM_B1 * m + (1.0 - ADAM_B1) * g
    v2 = ADAM_B2 * v + (1.0 - ADAM_B2) * (g * g)
    m_hat = m2 / (1.0 - ADAM_B1 ** ADAM_STEP)
    v_hat = v2 / (1.0 - ADAM_B2 ** ADAM_STEP)
    delta = -ADAM_LR * (m_hat / (jnp.sqrt(v_hat) + ADAM_EPS) + ADAM_WD * w)
    return delta, m2, v2


def _adamw(name, g, w, m, v):
    rows, cols = g.shape
    tr, tc = _div(rows, 256), _div(cols, 1024)
    spec = pl.BlockSpec((tr, tc), lambda i, j: (i, j))

    def body(g_ref, w_ref, m_ref, v_ref, go_ref, d_ref, mo_ref, vo_ref):
        gg = g_ref[...]
        delta, m2, v2 = _adamw_math(w_ref[...], gg, m_ref[...], v_ref[...])
        go_ref[...] = gg
        d_ref[...] = delta
        mo_ref[...] = m2
        vo_ref[...] = v2

    return pl.pallas_call(
        body, name=name, grid=(rows // tr, cols // tc),
        in_specs=[spec] * 4, out_specs=[spec] * 4,
        out_shape=[jax.ShapeDtypeStruct(g.shape, F32)] * 4,
        compiler_params=_params(("parallel", "parallel")),
    )(g, w, m, v)


def _sum_slots(name, slots):
    def body(s_ref, o_ref):
        acc = s_ref[0]
        for d in range(1, N_DEV):
            acc = acc + s_ref[d]
        o_ref[...] = acc

    return pl.pallas_call(body, name=name, out_shape=jax.ShapeDtypeStruct(slots.shape[1:], F32))(slots)


def _norm_in(x, gain):
    s, d = x.shape
    tr = _div(s, 256)

    def body(x_ref, g_ref, h_ref):
        xv = x_ref[...]
        r = lax.rsqrt(jnp.mean(xv * xv, axis=-1, keepdims=True) + EPS)
        h_ref[...] = (xv * r * g_ref[...]).astype(BF16)

    return pl.pallas_call(
        body, name="norm_in", grid=(s // tr,),
        in_specs=[pl.BlockSpec((tr, d), lambda i: (i, 0)), pl.BlockSpec((1, d), lambda i: (0, 0))],
        out_specs=pl.BlockSpec((tr, d), lambda i: (i, 0)),
        out_shape=jax.ShapeDtypeStruct((s, d), BF16),
        compiler_params=_params(("parallel",)),
    )(x, gain.reshape(1, d))


def _bias_table(rel_bias):
    h = rel_bias.shape[0]
    n_rel = Q_GROUP + K_GROUP - 1
    lo = PAD - K_GROUP + 1
    left = max(0, -MAX_REL - lo)
    right = max(0, lo + n_rel - 1 - MAX_REL)
    by_rel = jnp.concatenate([jnp.broadcast_to(rel_bias[:, :1], (h, left)), rel_bias,
                              jnp.broadcast_to(rel_bias[:, -1:], (h, right))], axis=1)
    by_rel = by_rel[:, lo + MAX_REL + left:][:, :n_rel]
    rev = by_rel[:, ::-1]
    rev = jnp.concatenate([rev, jnp.zeros((h, 1), rel_bias.dtype)], axis=1)
    skew = jnp.tile(rev, (1, Q_GROUP))[:, :Q_GROUP * n_rel].reshape(h, Q_GROUP, n_rel)
    tab = skew[:, :, Q_GROUP - 1:Q_GROUP - 1 + K_GROUP]
    qi = jnp.arange(Q_GROUP)[:, None] // CHUNK
    kj = jnp.arange(K_GROUP)[None, :] // CHUNK
    in_band = (kj >= qi) & (kj <= qi + N_LEFT_CHUNKS)
    return jnp.where(in_band[None], tab, NEG)


def _attention_fwd(proj, table, s, a):
    heads = a // HEAD_DIM
    scale = HEAD_DIM ** -0.5
    groups = s // Q_GROUP

    def body(q_ref, k_ref, v_ref, z_ref, tab_ref, o_ref, ya_ref, kp, vp):
        kp[0:PAD, :] = jnp.zeros((PAD, HEAD_DIM), BF16)
        vp[0:PAD, :] = jnp.zeros((PAD, HEAD_DIM), BF16)
        kp[PAD:, :] = k_ref[...].astype(BF16)
        vp[PAD:, :] = v_ref[...].astype(BF16)

        def group(g, carry):
            r0 = pl.multiple_of(g * Q_GROUP, Q_GROUP)
            q = q_ref[pl.ds(r0, Q_GROUP), :].astype(BF16)
            kb = kp[pl.ds(r0, K_GROUP), :]
            vb = vp[pl.ds(r0, K_GROUP), :]
            sc = lax.dot_general(q, kb, NT, preferred_element_type=F32) * scale + tab_ref[...]
            col = lax.broadcasted_iota(jnp.int32, (Q_GROUP, K_GROUP), 1)
            sc = jnp.where(col >= PAD - r0, sc, NEG)
            mx = jnp.max(sc, axis=-1, keepdims=True)
            e = jnp.exp(sc - mx)
            p = e / jnp.sum(e, axis=-1, keepdims=True)
            o = jnp.dot(p.astype(BF16), vb, preferred_element_type=F32)
            o_ref[pl.ds(r0, Q_GROUP), :] = o
            z = z_ref[pl.ds(r0, Q_GROUP), :]
            ya_ref[pl.ds(r0, Q_GROUP), :] = (o * (z * _sigmoid(z))).astype(BF16)
            return carry

        lax.fori_loop(0, groups, group, 0)

    col = lambda seg: (lambda h: (0, seg * heads + h))
    blk = lambda seg: pl.BlockSpec((s, HEAD_DIM), col(seg))
    return pl.pallas_call(
        body, name="attention_fwd", grid=(heads,),
        in_specs=[blk(0), blk(1), blk(2), blk(3), pl.BlockSpec((None, Q_GROUP, K_GROUP), lambda h: (h, 0, 0))],
        out_specs=[blk(0), blk(0)],
        out_shape=[jax.ShapeDtypeStruct((s, a), F32), jax.ShapeDtypeStruct((s, a), BF16)],
        scratch_shapes=[pltpu.VMEM((PAD + s, HEAD_DIM), BF16), pltpu.VMEM((PAD + s, HEAD_DIM), BF16)],
        compiler_params=_params(("parallel",)),
    )(proj, proj, proj, proj, table)


def _attention_bwd(proj, o, dya, table, dproj, s, a):
    heads = a // HEAD_DIM
    scale = HEAD_DIM ** -0.5
    groups = s // Q_GROUP

    def body(q_ref, k_ref, v_ref, z_ref, o_ref, dy_ref, tab_ref, _, dp_ref, dtab_ref, kp, vp, dkp, dvp):
        kp[0:PAD, :] = jnp.zeros((PAD, HEAD_DIM), BF16)
        vp[0:PAD, :] = jnp.zeros((PAD, HEAD_DIM), BF16)
        kp[PAD:, :] = k_ref[...].astype(BF16)
        vp[PAD:, :] = v_ref[...].astype(BF16)
        dkp[...] = jnp.zeros_like(dkp)
        dvp[...] = jnp.zeros_like(dvp)
        dtab_ref[...] = jnp.zeros_like(dtab_ref)

        def group(g, carry):
            r0 = pl.multiple_of(g * Q_GROUP, Q_GROUP)
            rows = pl.ds(r0, Q_GROUP)
            band = pl.ds(r0, K_GROUP)
            q = q_ref[rows, :].astype(BF16)
            kb = kp[band, :]
            vb = vp[band, :]
            sc = lax.dot_general(q, kb, NT, preferred_element_type=F32) * scale + tab_ref[...]
            col = lax.broadcasted_iota(jnp.int32, (Q_GROUP, K_GROUP), 1)
            sc = jnp.where(col >= PAD - r0, sc, NEG)
            mx = jnp.max(sc, axis=-1, keepdims=True)
            e = jnp.exp(sc - mx)
            p = e / jnp.sum(e, axis=-1, keepdims=True)
            z = z_ref[rows, :]
            dy = dy_ref[rows, :]
            si, dsi = _silu_and_grad(z)
            dp_ref[3, rows, :] = (dy * o_ref[rows, :] * dsi).astype(BF16)
            dob = (dy * si).astype(BF16)
            dp = lax.dot_general(dob, vb, NT, preferred_element_type=F32)
            ds = p * (dp - jnp.sum(p * dp, axis=-1, keepdims=True))
            dtab_ref[...] += ds
            dsb = (ds * scale).astype(BF16)
            dp_ref[0, rows, :] = jnp.dot(dsb, kb, preferred_element_type=F32).astype(BF16)
            dkp[band, :] += lax.dot_general(dsb, q, TN, preferred_element_type=F32)
            dvp[band, :] += lax.dot_general(p.astype(BF16), dob, TN, preferred_element_type=F32)
            return carry

        lax.fori_loop(0, groups, group, 0)
        dp_ref[1] = dkp[PAD:, :].astype(BF16)
        dp_ref[2] = dvp[PAD:, :].astype(BF16)

    col = lambda seg: (lambda h: (0, seg * heads + h))
    blk = lambda seg: pl.BlockSpec((s, HEAD_DIM), col(seg))
    tab_spec = pl.BlockSpec((None, Q_GROUP, K_GROUP), lambda h: (h, 0, 0))
    return pl.pallas_call(
        body, name="attention_bwd", grid=(heads,),
        in_specs=[blk(0), blk(1), blk(2), blk(3), blk(0), blk(0), tab_spec, ANY],
        out_specs=[pl.BlockSpec((4, s, HEAD_DIM), lambda h: (0, 0, h)), tab_spec],
        out_shape=[jax.ShapeDtypeStruct(dproj.shape, BF16), jax.ShapeDtypeStruct(table.shape, F32)],
        input_output_aliases={7: 0},
        scratch_shapes=[pltpu.VMEM((PAD + s, HEAD_DIM), BF16), pltpu.VMEM((PAD + s, HEAD_DIM), BF16),
                        pltpu.VMEM((PAD + s, HEAD_DIM), F32), pltpu.VMEM((PAD + s, HEAD_DIM), F32)],
        compiler_params=_params(("parallel",)),
    )(proj, proj, proj, proj, o, dya, table, dproj)


def _pick_window(gi, by_window):
    out = by_window[-1]
    for n in range(N_GROUPS - 2, -1, -1):
        out = jnp.where(gi == n, by_window[n], out)
    return out


def _inv_count(gi, first_row, rows):
    t = first_row + lax.broadcasted_iota(jnp.int32, (rows, 1), 0)
    w = jnp.left_shift(2, gi)
    return 1.0 / jnp.minimum(t + 1, w).astype(F32)


def _pool_fwd(proj, pw_full, pool_scale, s, a, p):
    pg = p // N_GROUPS
    ts = _div(s, 512)
    u0, z0 = 4 * a // pg, (4 * a + p) // pg
    hb = ts // HALO

    def body(u_ref, uh_ref, z_ref, pw_ref, ps_ref, d_ref, t_ref, y_ref, ext):
        gi, i = pl.program_id(0), pl.program_id(1)
        u = u_ref[...]
        ext[0:HALO, :] = jnp.where(i > 0, uh_ref[...], 0.0)
        ext[HALO:, :] = u
        e = ext[...]
        sums, shift = [], 1
        for _ in POOL_WINDOWS:
            e = e + pltpu.roll(e, shift, 0)
            sums.append(e)
            shift *= 2
        win = _pick_window(gi, sums)[HALO:, :]
        d = (win * _inv_count(gi, i * ts, ts) - u).astype(BF16)
        d_ref[...] = d
        t = jnp.dot(d, pw_ref[...].reshape(pg, pg), preferred_element_type=F32)
        t_ref[...] = t
        z = z_ref[...]
        y_ref[...] = (t * ps_ref[...] * (z * _sigmoid(z))).astype(BF16)

    out_spec = pl.BlockSpec((ts, pg), lambda g, i: (i, g))
    return pl.pallas_call(
        body, name="pool_fwd", grid=(N_GROUPS, s // ts),
        in_specs=[pl.BlockSpec((ts, pg), lambda g, i: (i, u0 + g)),
                  pl.BlockSpec((HALO, pg), lambda g, i: (jnp.maximum(i * hb - 1, 0), u0 + g)),
                  pl.BlockSpec((ts, pg), lambda g, i: (i, z0 + g)),
                  pl.BlockSpec((N_CHIPS, None, pg // N_CHIPS, pg), lambda g, i: (0, g, 0, 0)),
                  pl.BlockSpec((1, pg), lambda g, i: (0, g))],
        out_specs=[out_spec] * 3,
        out_shape=[jax.ShapeDtypeStruct((s, p), BF16), jax.ShapeDtypeStruct((s, p), F32),
                   jax.ShapeDtypeStruct((s, p), BF16)],
        scratch_shapes=[pltpu.VMEM((ts + HALO, pg), F32)],
        compiler_params=_params(("parallel", "parallel")),
    )(proj, proj, proj, pw_full, pool_scale.reshape(1, p))


def _pool_bwd(proj, dyp, t, d, pw_full, pool_scale, dproj, s, a, p):
    pg = p // N_GROUPS
    ts = _div(s, 512)
    nt = s // ts
    z0 = (4 * a + p) // pg
    hb = ts // HALO
    last_halo = s // HALO - 1

    def body(dy_ref, dyh_ref, z_ref, zh_ref, t_ref, th_ref, d_ref, pw_ref, ps_ref, _,
             dp_ref, dpw_ref, dps_ref, ext):
        gi, i = pl.program_id(0), pl.program_id(1)
        ps = ps_ref[...]
        pw = pw_ref[...].reshape(pg, pg)

        @pl.when(i == 0)
        def _():
            dpw_ref[...] = jnp.zeros_like(dpw_ref)
            dps_ref[...] = jnp.zeros_like(dps_ref)

        def through_gate(dy, z, tt):
            si, dsi = _silu_and_grad(z)
            return dy * si, dy * (tt * ps) * dsi

        tt = t_ref[...]
        dyl, dz = through_gate(dy_ref[...], z_ref[...], tt)
        dp_ref[1] = dz.astype(BF16)
        dps_ref[...] += jnp.sum(dyl * tt, axis=0, keepdims=True)
        dtb = (dyl * ps).astype(BF16)
        dpw_ref[...] += lax.dot_general(d_ref[...], dtb, TN, preferred_element_type=F32).reshape(dpw_ref.shape)
        dd = lax.dot_general(dtb, pw, NT, preferred_element_type=F32)
        dylh, _ = through_gate(dyh_ref[...], zh_ref[...], th_ref[...])
        ddh = lax.dot_general((dylh * ps).astype(BF16), pw, NT, preferred_element_type=F32)
        ddh = jnp.where(i < nt - 1, ddh, 0.0)
        ext[0:ts, :] = dd * _inv_count(gi, i * ts, ts)
        ext[ts:, :] = ddh * _inv_count(gi, (i + 1) * ts, HALO)
        e = ext[...]
        rows = ts + HALO
        sums, shift = [], 1
        for _ in POOL_WINDOWS:
            e = e + pltpu.roll(e, rows - shift, 0)
            sums.append(e)
            shift *= 2
        dp_ref[0] = (_pick_window(gi, sums)[:ts, :] - dd).astype(BF16)

    tile = lambda c0: pl.BlockSpec((ts, pg), lambda g, i: (i, c0 + g))
    halo = lambda c0: pl.BlockSpec((HALO, pg), lambda g, i: (jnp.minimum((i + 1) * hb, last_halo), c0 + g))
    pw_spec = pl.BlockSpec((N_CHIPS, None, pg // N_CHIPS, pg), lambda g, i: (0, g, 0, 0))
    return pl.pallas_call(
        body, name="pool_bwd", grid=(N_GROUPS, nt),
        in_specs=[tile(0), halo(0), tile(z0), halo(z0), tile(0), halo(0), tile(0), pw_spec,
                  pl.BlockSpec((1, pg), lambda g, i: (0, g)), ANY],
        out_specs=[pl.BlockSpec((2, ts, pg), lambda g, i: (2, i, g)), pw_spec,
                   pl.BlockSpec((1, pg), lambda g, i: (0, g))],
        out_shape=[jax.ShapeDtypeStruct(dproj.shape, BF16),
                   jax.ShapeDtypeStruct(pw_full.shape, F32), jax.ShapeDtypeStruct((1, p), F32)],
        input_output_aliases={9: 0},
        scratch_shapes=[pltpu.VMEM((ts + HALO, pg), F32)],
        compiler_params=_params(("parallel", "arbitrary")),
    )(dyp, dyp, proj, proj, t, t, d, pw_full, pool_scale.reshape(1, p), dproj)


def _merge_fwd(ya, yp, woa_full, wop_full, proj, gb_full, s, d, a):
    sw = d // N_CHIPS
    tm, tn = _div(s, 512), _div(sw, 1024)
    per = sw // tn
    ga0, gp0 = (4 * a + 2 * a) // tn, (4 * a + 2 * a + d) // tn

    def body(ya_ref, yp_ref, wa_ref, wp_ref, ga_ref, gp_ref, gb_ref, a_out, b_out, m_out):
        av = jnp.dot(ya_ref[...], wa_ref[...], preferred_element_type=F32)
        bv = jnp.dot(yp_ref[...], wp_ref[...], preferred_element_type=F32)
        a_out[...] = av
        b_out[...] = bv
        sa = _sigmoid(ga_ref[...] + gb_ref[0:1, :])
        sp = _sigmoid(gp_ref[...] + gb_ref[1:2, :])
        m_out[...] = (sa * av + sp * bv).astype(BF16)

    act = pl.BlockSpec((tm, a), lambda i, j: (i, 0))
    wgt = pl.BlockSpec((None, a, tn), lambda i, j: (j // per, 0, j % per))
    out = pl.BlockSpec((tm, tn), lambda i, j: (i, j))
    return pl.pallas_call(
        body, name="merge_fwd", grid=(s // tm, d // tn),
        in_specs=[act, act, wgt, wgt,
                  pl.BlockSpec((tm, tn), lambda i, j: (i, ga0 + j)),
                  pl.BlockSpec((tm, tn), lambda i, j: (i, gp0 + j)),
                  pl.BlockSpec((None, 2, tn), lambda i, j: (j // per, 0, j % per))],
        out_specs=[out, out, out],
        out_shape=[jax.ShapeDtypeStruct((s, d), F32), jax.ShapeDtypeStruct((s, d), F32),
                   jax.ShapeDtypeStruct((s, d), BF16)],
        compiler_params=_params(("parallel", "parallel")),
    )(ya, yp, woa_full, wop_full, proj, proj, gb_full)


def _out_proj(mb, wo, x, s, d):
    tm, tn, tk = _div(s, 1024), _div(d, 1024), _div(d, 2048)

    def epilogue(res, ex, outs):
        outs[0][...] = res + ex[0][...]

    tile = pl.BlockSpec((tm, tn), lambda i, j, k: (i, j))
    return _matmul(
        "out_proj", NN, (s // tm, d // tn, d // tk),
        mb, pl.BlockSpec((tm, tk), lambda i, j, k: (i, k)),
        wo, pl.BlockSpec((tk, tn), lambda i, j, k: (k, j)),
        (tm, tn), [jax.ShapeDtypeStruct((s, d), F32)], [tile], extra=(x,), extra_specs=(tile,), epilogue=epilogue)[0]


def _loss_head(x2, target, final_gain):
    s, d = x2.shape
    tr = _div(s, 256)

    def body(x_ref, t_ref, g_ref, loss_ref, dx_ref, dxb_ref, dg_ref):
        @pl.when(pl.program_id(0) == 0)
        def _():
            dg_ref[...] = jnp.zeros_like(dg_ref)

        xv = x_ref[...]
        g = g_ref[...]
        r = lax.rsqrt(jnp.mean(xv * xv, axis=-1, keepdims=True) + EPS)
        xn = xv * r
        e = xn * g - t_ref[...]
        loss_ref[...] = 0.5 * jnp.mean(e * e, axis=-1, keepdims=True)
        dy = e / d
        dg_ref[...] += jnp.sum(dy * xn, axis=0, keepdims=True)
        dxn = dy * g
        dx = r * (dxn - xn * jnp.mean(dxn * xn, axis=-1, keepdims=True))
        dx_ref[...] = dx
        dxb_ref[...] = dx.astype(BF16)

    rows = pl.BlockSpec((tr, d), lambda i: (i, 0))
    vec = pl.BlockSpec((1, d), lambda i: (0, 0))
    return pl.pallas_call(
        body, name="loss_head", grid=(s // tr,),
        in_specs=[rows, rows, vec], out_specs=[pl.BlockSpec((tr, 1), lambda i: (i, 0)), rows, rows, vec],
        out_shape=[jax.ShapeDtypeStruct((s, 1), F32), jax.ShapeDtypeStruct((s, d), F32),
                   jax.ShapeDtypeStruct((s, d), BF16), jax.ShapeDtypeStruct((1, d), F32)],
        compiler_params=_params(("arbitrary",)),
    )(x2, target, final_gain.reshape(1, d))


N_SLOTS = 10


def _slot(seg):
    t = seg - 6
    return jnp.where(seg < 6, seg, 6 + 2 * (t % 2) + t // 2)


def _merge_bwd(dxb, wo, a_val, b_val, proj, gb_full, s, d, a):
    sw = d // N_CHIPS
    tm, tn, tk = _div(s, 512), _div(sw, 1024), _div(d, 2048)
    per = sw // tn
    per_slot = a // tn
    ga0, gp0 = (4 * a + 2 * a) // tn, (4 * a + 2 * a + d) // tn

    def epilogue(dm, ex, outs):
        a_ref, b_ref, ga_ref, gp_ref, gb_ref = ex
        da_ref, db_ref, dg_ref, dgb_ref = outs
        sa = _sigmoid(ga_ref[...] + gb_ref[0:1, :])
        sp = _sigmoid(gp_ref[...] + gb_ref[1:2, :])
        da_ref[...] = (dm * sa).astype(BF16)
        db_ref[...] = (dm * sp).astype(BF16)
        dga = dm * a_ref[...] * sa * (1.0 - sa)
        dgp = dm * b_ref[...] * sp * (1.0 - sp)
        dg_ref[0] = dga.astype(BF16)
        dg_ref[1] = dgp.astype(BF16)

        @pl.when(pl.program_id(1) == 0)
        def _():
            dgb_ref[...] = jnp.zeros_like(dgb_ref)

        dgb_ref[0:1, :] += jnp.sum(dga, axis=0, keepdims=True)
        dgb_ref[1:2, :] += jnp.sum(dgp, axis=0, keepdims=True)

    tile = pl.BlockSpec((tm, tn), lambda j, i, k: (i, j))
    sd = jax.ShapeDtypeStruct((s, d), BF16)
    return _matmul(
        "merge_bwd", NT, (d // tn, s // tm, d // tk),
        dxb, pl.BlockSpec((tm, tk), lambda j, i, k: (i, k)),
        wo, pl.BlockSpec((tn, tk), lambda j, i, k: (j, k)),
        (tm, tn), [sd, sd, jax.ShapeDtypeStruct((N_SLOTS, s, a), BF16), jax.ShapeDtypeStruct((2, d), F32)],
        [tile, tile, pl.BlockSpec((2, tm, tn), lambda j, i, k: (3 + j // per_slot, i, j % per_slot)),
         pl.BlockSpec((2, tn), lambda j, i, k: (0, j))],
        extra=(a_val, b_val, proj, proj, gb_full),
        extra_specs=(tile, tile, pl.BlockSpec((tm, tn), lambda j, i, k: (i, ga0 + j)),
                     pl.BlockSpec((tm, tn), lambda j, i, k: (i, gp0 + j)),
                     pl.BlockSpec((None, 2, tn), lambda j, i, k: (j // per, 0, j % per))),
        epilogue=epilogue, accumulate_outs=True)


def _weight_grad(name, act, dout, shard_cols, slots=False):
    s, kdim = act.shape
    n = dout.shape[0] * dout.shape[2] if slots else dout.shape[1]
    if shard_cols:
        sw = n // N_CHIPS
        tm, tn = _div(kdim, 1024), _div(math.gcd(sw, dout.shape[2]) if slots else sw, 1024)
        per = sw // tn
        shape = (N_CHIPS, kdim, sw)
        out = pl.BlockSpec((None, tm, tn), lambda i, j, k: (j // per, i, j % per))
    else:
        sh = kdim // N_CHIPS
        tm, tn = _div(sh, 1024), _div(n, 1024)
        per = sh // tm
        shape = (N_CHIPS, sh, n)
        out = pl.BlockSpec((None, tm, tn), lambda i, j, k: (i // per, i % per, j))
    tk = _div(s, 4096)
    if slots:
        per_slot = dout.shape[2] // tn
        dout_spec = pl.BlockSpec((None, tk, tn), lambda i, j, k: (_slot(j // per_slot), k, j % per_slot))
    else:
        dout_spec = pl.BlockSpec((tk, tn), lambda i, j, k: (k, j))
    return _matmul(
        name, TN, (kdim // tm, n // tn, s // tk),
        act, pl.BlockSpec((tk, tm), lambda i, j, k: (k, i)), dout, dout_spec,
        (tm, tn), [jax.ShapeDtypeStruct(shape, F32), jax.ShapeDtypeStruct(shape, BF16)], [out, out])


def _norm_in_bwd(x, dh, dx2, gain):
    s, d = x.shape
    tr = _div(s, 256)

    def body(x_ref, dh_ref, dx2_ref, g_ref, gx_ref, dg_ref):
        @pl.when(pl.program_id(0) == 0)
        def _():
            dg_ref[...] = jnp.zeros_like(dg_ref)

        xv = x_ref[...]
        r = lax.rsqrt(jnp.mean(xv * xv, axis=-1, keepdims=True) + EPS)
        xn = xv * r
        dhv = dh_ref[...]
        dg_ref[...] += jnp.sum(dhv * xn, axis=0, keepdims=True)
        dxn = dhv * g_ref[...]
        gx_ref[...] = r * (dxn - xn * jnp.mean(dxn * xn, axis=-1, keepdims=True)) + dx2_ref[...]

    rows = pl.BlockSpec((tr, d), lambda i: (i, 0))
    vec = pl.BlockSpec((1, d), lambda i: (0, 0))
    return pl.pallas_call(
        body, name="norm_in_bwd", grid=(s // tr,),
        in_specs=[rows, rows, rows, vec], out_specs=[rows, vec],
        out_shape=[jax.ShapeDtypeStruct((s, d), F32), jax.ShapeDtypeStruct((1, d), F32)],
        compiler_params=_params(("arbitrary",)),
    )(x, dh, dx2, gain.reshape(1, d))


def _pack(vectors):
    flat = jnp.concatenate([v.reshape(-1).astype(F32) for v in vectors])
    rows = -(-flat.shape[0] // 1024) * 8
    return jnp.pad(flat, (0, rows * 128 - flat.shape[0])).reshape(rows, 128)


def _unpack(packed, like):
    flat, out, at = packed.reshape(-1), [], 0
    for v in like:
        out.append(flat[at:at + v.size].reshape(v.shape))
        at += v.size
    return out


def _small_adamw(g, w, m, v):
    def body(g_ref, w_ref, m_ref, v_ref, d_ref, mo_ref, vo_ref):
        delta, m2, v2 = _adamw_math(w_ref[...], g_ref[...], m_ref[...], v_ref[...])
        d_ref[...] = delta
        mo_ref[...] = m2
        vo_ref[...] = v2

    return pl.pallas_call(body, name="small_adamw", out_shape=[jax.ShapeDtypeStruct(g.shape, F32)] * 3)(g, w, m, v)


def kernel(x, norm_gain, w_in, rel_bias, pool_w, pool_scale, w_out_attn, w_out_pool, gate_bias, w_out, final_gain, loss_target, m_norm_gain, m_w_in, m_rel_bias, m_pool_w, m_pool_scale, m_w_out_attn, m_w_out_pool, m_gate_bias, m_w_out, m_final_gain, v_norm_gain, v_w_in, v_rel_bias, v_pool_w, v_pool_scale, v_w_out_attn, v_w_out_pool, v_gate_bias, v_w_out, v_final_gain):
    _, s, d = x.shape
    a = p = d // 2
    n_in = w_in.shape[1] * N_CHIPS
    sw_in = w_in.shape[1]
    pg = p // N_GROUPS
    xs = x.reshape(s, d)
    target = loss_target.reshape(s, d)
    c_arr = lax.axis_index("c").astype(jnp.int32).reshape(1)
    chip = 2 * lax.axis_index("x") + lax.axis_index("y")

    win_full, woa_full, wop_full, wo_full, pw_full, gb_full = _all_gather(
        [w_in.astype(BF16), w_out_attn.astype(BF16), w_out_pool.astype(BF16), w_out.astype(BF16),
         pool_w.astype(BF16), gate_bias],
        [True, True, True, True, False, False])
    wo_mat = wo_full.reshape(d, d)

    hb = _norm_in(xs, norm_gain)
    tm, tn, tk = _div(s, 1024), _div(sw_in, 1024), _div(d, 4096)
    per_in = sw_in // tn
    proj = _matmul(
        "in_proj", NN, (s // tm, n_in // tn, d // tk),
        hb, pl.BlockSpec((tm, tk), lambda i, j, k: (i, k)),
        win_full, pl.BlockSpec((None, tk, tn), lambda i, j, k: (j // per_in, k, j % per_in)),
        (tm, tn), [jax.ShapeDtypeStruct((s, n_in), F32)], [pl.BlockSpec((tm, tn), lambda i, j, k: (i, j))])[0]
    table = _bias_table(rel_bias)
    o_attn, ya = _attention_fwd(proj, table, s, a)
    d_pool, t_pool, yp = _pool_fwd(proj, pw_full, pool_scale, s, a, p)
    a_val, b_val, mb = _merge_fwd(ya, yp, woa_full, wop_full, proj, gb_full, s, d, a)
    loss_rows, dx2, dx2b, g_final = _loss_head(_out_proj(mb, wo_mat, xs, s, d), target, final_gain)
    loss = lax.psum(jnp.sum(loss_rows), ("x", "y", "c"))

    da, db, dproj, g_gate_full = _merge_bwd(dx2b, wo_mat, a_val, b_val, proj, gb_full, s, d, a)
    gwo, gwo_b = _weight_grad("grad_w_out", mb, dx2b, shard_cols=False)
    gwoa, gwoa_b = _weight_grad("grad_w_out_attn", ya, da, shard_cols=True)
    gwop, gwop_b = _weight_grad("grad_w_out_pool", yp, db, shard_cols=True)

    sw = d // N_CHIPS
    tm, tn, tk = _div(s, 1024), _div(a, 1024), _div(sw, 1024)
    per_o = sw // tk

    def back_through(name, dout, w_full):
        return _matmul(
            name, NT, (s // tm, a // tn, d // tk),
            dout, pl.BlockSpec((tm, tk), lambda i, j, k: (i, k)),
            w_full, pl.BlockSpec((None, tn, tk), lambda i, j, k: (k // per_o, j, k % per_o)),
            (tm, tn), [jax.ShapeDtypeStruct((s, a), F32)], [pl.BlockSpec((tm, tn), lambda i, j, k: (i, j))])[0]

    dya = back_through("grad_y_attn", da, woa_full)
    dyp = back_through("grad_y_pool", db, wop_full)
    dproj, gpw, g_pscale = _pool_bwd(proj, dyp, t_pool, d_pool, pw_full, pool_scale, dproj, s, a, p)
    dproj, dtable = _attention_bwd(proj, o_attn, dya, table, dproj, s, a)
    g_rel = jax.vjp(_bias_table, rel_bias)[1](dtable)[0]

    tm, tn, tk = _div(s, 1024), _div(d, 1024), _div(a // 2, 1024)
    per_k, per_slot = sw_in // tk, a // tk
    dh = _matmul(
        "grad_h", NT, (s // tm, d // tn, n_in // tk),
        dproj, pl.BlockSpec((None, tm, tk), lambda i, j, k: (_slot(k // per_slot), i, k % per_slot)),
        win_full, pl.BlockSpec((None, tn, tk), lambda i, j, k: (k // per_k, j, k % per_k)),
        (tm, tn), [jax.ShapeDtypeStruct((s, d), F32)], [pl.BlockSpec((tm, tn), lambda i, j, k: (i, j))])[0]
    gwin, gwin_b = _weight_grad("grad_w_in", hb, dproj, shard_cols=True, slots=True)
    grad_x, g_norm = _norm_in_bwd(xs, dh, dx2, norm_gain)

    gpw3 = gpw.reshape(N_CHIPS, pg, pg)
    partial_f32 = [gwin, gwoa, gwop, gwo, gpw3]
    partial_b16 = [gwin_b, gwoa_b, gwop_b, gwo_b, gpw3.astype(BF16)]
    names = ["w_in", "w_out_attn", "w_out_pool", "w_out", "pool_w"]
    from_sibling = _swap_other_half(partial_b16)
    pair = [_pair_sum("pair_sum_" + n, g, r, c_arr) for n, g, r in zip(names, partial_f32, from_sibling)]
    from_chips = _scatter_to_owners(pair)
    halves = [_chip_sum("chip_sum_" + n, r, c_arr) for n, r in zip(names, from_chips)]
    grads = _share_half(halves)

    big = {}
    weights = [w_in, w_out_attn, w_out_pool, w_out, pool_w.reshape(pg, pg)]
    ms = [m_w_in, m_w_out_attn, m_w_out_pool, m_w_out, m_pool_w.reshape(pg, pg)]
    vs = [v_w_in, v_w_out_attn, v_w_out_pool, v_w_out, v_pool_w.reshape(pg, pg)]
    for n, g, w, m, v in zip(names, grads, weights, ms, vs):
        big[n] = [r.reshape(pool_w.shape) if n == "pool_w" else r for r in _adamw("adamw_" + n, g, w, m, v)]

    small_like = [norm_gain, final_gain, pool_scale, rel_bias, jnp.zeros((2, d), F32)]
    summed = _sum_slots("small_grads_sum", _all_to_all_small(_pack([g_norm, g_final, g_pscale, g_rel, g_gate_full])))
    g_norm_t, g_final_t, g_pscale_t, g_rel_t, g_gate_t = _unpack(summed, small_like)
    g_gate_t = lax.dynamic_slice_in_dim(g_gate_t, chip * sw, sw, axis=1)
    small_g = [g_norm_t, g_final_t, g_pscale_t, g_rel_t, g_gate_t]
    small_w = [norm_gain, final_gain, pool_scale, rel_bias, gate_bias]
    small_m = [m_norm_gain, m_final_gain, m_pool_scale, m_rel_bias, m_gate_bias]
    small_v = [v_norm_gain, v_final_gain, v_pool_scale, v_rel_bias, v_gate_bias]
    packed = _small_adamw(_pack(small_g), _pack(small_w), _pack(small_m), _pack(small_v))
    sd, sm, sv = [_unpack(t, small_w) for t in packed]
    small = {n: [small_g[i], sd[i], sm[i], sv[i]]
             for i, n in enumerate(["norm_gain", "final_gain", "pool_scale", "rel_bias", "gate_bias"])}

    every = {**big, **small}
    order = ["norm_gain", "w_in", "rel_bias", "pool_w", "pool_scale", "w_out_attn", "w_out_pool", "gate_bias",
             "w_out", "final_gain"]
    return (loss, grad_x.reshape(x.shape), *[every[n][0] for n in order], *[every[n][1] for n in order],
            *[every[n][2] for n in order], *[every[n][3] for n in order])
```

```python
import math

import jax
import jax.numpy as jnp
from jax import lax
from jax.experimental import pallas as pl
from jax.experimental.pallas import tpu as pltpu

F32 = jnp.float32
BF16 = jnp.bfloat16
MESH = pl.DeviceIdType.MESH
ANY = pl.BlockSpec(memory_space=pl.ANY)

N_CHIPS = 4
N_DEV = 8
CHUNK = 64
N_LEFT_CHUNKS = 8
PAD = N_LEFT_CHUNKS * CHUNK
HEAD_DIM = 128
MAX_REL = 128
POOL_WINDOWS = (2, 4, 8, 16)
N_GROUPS = len(POOL_WINDOWS)
HALO = 16
Q_GROUP = 4 * CHUNK
K_GROUP = Q_GROUP + PAD
NEG = -1e30
EPS = 1e-6
ADAM_LR, ADAM_B1, ADAM_B2, ADAM_EPS, ADAM_WD, ADAM_STEP = 0.001, 0.9, 0.999, 1e-08, 0.01, 10
VMEM_LIMIT = 56 * 1024 * 1024

NN = (((1,), (0,)), ((), ()))
NT = (((1,), (1,)), ((), ()))
TN = (((0,), (0,)), ((), ()))


def _div(n, pref):
    if n <= pref:
        return n
    for t in range(pref - pref % 128, 0, -128):
        if n % t == 0:
            return t
    raise ValueError((n, pref))


def _params(sem, **kw):
    return pltpu.CompilerParams(dimension_semantics=sem, vmem_limit_bytes=VMEM_LIMIT, **kw)


def _sigmoid(z):
    return jax.nn.sigmoid(z)


def _silu_and_grad(z):
    sg = _sigmoid(z)
    return z * sg, sg * (1.0 + z * (1.0 - sg))


def _matmul(name, dn, grid, a, a_spec, b, b_spec, acc_shape, outs, out_specs, extra=(), extra_specs=(),
            epilogue=None, accumulate_outs=False, sides=()):
    nk = grid[2]
    ne, no = len(extra), len(outs)

    def finish(res, ex, out_refs):
        if epilogue is None:
            for o in out_refs:
                o[...] = res.astype(o.dtype)
        else:
            epilogue(res, ex, out_refs)

    def body(*refs):
        a_ref, b_ref = refs[0], refs[1]
        ex = refs[2:2 + ne]
        out_refs = refs[2 + ne:2 + ne + no]
        if nk == 1:
            finish(lax.dot_general(a_ref[...], b_ref[...], dn, preferred_element_type=F32), ex, out_refs)
            return
        acc = refs[-1]
        k = pl.program_id(2)

        @pl.when(k == 0)
        def _():
            acc[...] = jnp.zeros_like(acc)

        acc[...] += lax.dot_general(a_ref[...], b_ref[...], dn, preferred_element_type=F32)

        @pl.when(k == nk - 1)
        def _():
            finish(acc[...], ex, out_refs)

    sem = ("arbitrary",) * 3 if accumulate_outs else ("parallel", "parallel", "arbitrary")
    return _run(name, list(sides), dict(
        body=body, grid=grid, in_specs=[a_spec, b_spec, *extra_specs], out_specs=list(out_specs),
        out_shape=list(outs), scratch_shapes=[] if nk == 1 else [pltpu.VMEM(acc_shape, F32)],
        operands=[a, b, *extra], sem=sem))


def _place():
    x, y, c = lax.axis_index("x"), lax.axis_index("y"), lax.axis_index("c")
    chips = [(1 - x, y), (x, 1 - y), (1 - x, 1 - y)]
    return x, y, c, chips


def _remote(src, dst, send_sems, recv_sems, k, dev):
    return pltpu.make_async_remote_copy(src_ref=src, dst_ref=dst, send_sem=send_sems.at[k], recv_sem=recv_sems.at[k],
                                        device_id=dev, device_id_type=MESH)


class _Side:
    def __init__(self, ins, out_shapes, n_remote, n_local, start, finish, aliases=None):
        self.ins, self.out_shapes = list(ins), list(out_shapes)
        self.n_remote, self.n_local = max(n_remote, 1), max(n_local, 1)
        self.start, self.finish, self.aliases = start, finish, aliases or {}


def _run(name, sides, compute=None):
    cm = compute or dict(body=None, grid=(), in_specs=[], out_specs=[], out_shape=[], scratch_shapes=[], operands=[])
    grid = tuple(cm["grid"])
    ni, no, ns = len(cm["operands"]), len(cm["out_shape"]), len(cm["scratch_shapes"])
    n_in = [len(sd.ins) for sd in sides]
    n_out = [len(sd.out_shapes) for sd in sides]

    def body(*refs):
        at = ni
        side_ins = []
        for n in n_in:
            side_ins.append(refs[at:at + n])
            at += n
        outs = refs[at:at + no]
        at += no
        side_outs = []
        for n in n_out:
            side_outs.append(refs[at:at + n])
            at += n
        scratch = refs[at:at + ns]
        at += ns
        sems = [refs[at + 3 * q:at + 3 * q + 3] for q in range(len(sides))]

        def each(step):
            for sd, i_, o_, m_ in zip(sides, side_ins, side_outs, sems):
                getattr(sd, step)(i_, o_, *m_)

        if not grid:
            each("start")
            each("finish")
            return
        first = last = None
        for ax, g in enumerate(grid):
            f, l = pl.program_id(ax) == 0, pl.program_id(ax) == g - 1
            first = f if first is None else first & f
            last = l if last is None else last & l
        if sides:
            pl.when(first)(lambda: each("start"))
        cm["body"](*refs[:ni], *outs, *scratch)
        if sides:
            pl.when(last)(lambda: each("finish"))

    aliases = dict(cm.get("aliases") or {})
    in_at, out_at = ni, no
    for sd, a, b in zip(sides, n_in, n_out):
        for i_, o_ in sd.aliases.items():
            aliases[in_at + i_] = out_at + o_
        in_at, out_at = in_at + a, out_at + b
    scratch_shapes = list(cm["scratch_shapes"])
    for sd in sides:
        scratch_shapes += [pltpu.SemaphoreType.DMA((sd.n_remote,)), pltpu.SemaphoreType.DMA((sd.n_remote,)),
                           pltpu.SemaphoreType.DMA((sd.n_local,))]
    kw = {}
    if grid:
        sem = ("arbitrary",) * len(grid) if sides else cm["sem"]
        kw = dict(grid=grid, compiler_params=_params(sem))
    res = pl.pallas_call(
        body, name=name,
        in_specs=list(cm["in_specs"]) + [ANY] * sum(n_in), out_specs=list(cm["out_specs"]) + [ANY] * sum(n_out),
        out_shape=list(cm["out_shape"]) + [s for sd in sides for s in sd.out_shapes],
        scratch_shapes=scratch_shapes, input_output_aliases=aliases, **kw,
    )(*cm["operands"], *[a for sd in sides for a in sd.ins])
    res = list(res)
    side_res, at = [], no
    for n in n_out:
        side_res.append(res[at:at + n])
        at += n
    return res[:no], side_res


def _gather_side(shards, split):
    n = len(shards)

    def plan(ins, outs, send_sems, recv_sems, local_sems):
        x, y, c, chips = _place()
        me = 2 * x + y
        sibling = (x, y, 1 - c)
        own = [pltpu.make_async_copy(ins[t], outs[t].at[me], local_sems.at[t]) for t in range(n)]
        direct, relays, arrivals = [], [], []
        for t in range(n):
            half = ins[t].shape[0] // 2
            rows, other = pl.ds(c * half, half), pl.ds((1 - c) * half, half)
            for j, (cx, cy) in enumerate(chips):
                src_chip = 2 * cx + cy
                k = 6 * t + j
                if split[t]:
                    direct.append(_remote(ins[t].at[rows], outs[t].at[me, rows], send_sems, recv_sems, k, (cx, cy, c)))
                    got = outs[t].at[src_chip, rows]
                    relays.append((_remote(got, got, send_sems, recv_sems, k, (cx, cy, c)),
                                   _remote(got, got, send_sems, recv_sems, k + 3, sibling)))
                    theirs = outs[t].at[src_chip, other]
                    arrivals.append(_remote(theirs, theirs, send_sems, recv_sems, k + 3, sibling))
                else:
                    direct.append(_remote(ins[t], outs[t].at[me], send_sems, recv_sems, k, (cx, cy, c)))
                    got = outs[t].at[src_chip]
                    arrivals.append(_remote(got, got, send_sems, recv_sems, k, (cx, cy, c)))
        return own, direct, relays, arrivals

    def start(*refs):
        own, direct, _, _ = plan(*refs)
        for cp in own + direct:
            cp.start()

    def finish(*refs):
        own, direct, relays, arrivals = plan(*refs)
        for landed, onward in relays:
            landed.wait_recv()
            onward.start()
        for cp in arrivals:
            cp.wait_recv()
        for cp in direct + [onward for _, onward in relays]:
            cp.wait_send()
        for cp in own:
            cp.wait()

    return _Side(shards, [jax.ShapeDtypeStruct((N_CHIPS,) + s.shape, s.dtype) for s in shards], 6 * n, n,
                 start, finish)


def _swap_side(parts):
    n = len(parts)

    def plan(ins, outs, send_sems, recv_sems, _):
        x, y, c, _ = _place()
        cps = []
        for t in range(n):
            half = ins[t].shape[1] // 2
            cps.append(_remote(ins[t].at[:, pl.ds((1 - c) * half, half)], outs[t], send_sems, recv_sems, t,
                               (x, y, 1 - c)))
        return cps

    def start(*refs):
        for cp in plan(*refs):
            cp.start()

    def finish(*refs):
        for cp in plan(*refs):
            cp.wait()

    return _Side(parts, [jax.ShapeDtypeStruct((p.shape[0], p.shape[1] // 2, p.shape[2]), p.dtype) for p in parts],
                 n, 0, start, finish)


def _scatter_side(parts):
    n = len(parts)

    def plan(ins, outs, send_sems, recv_sems, local_sems):
        x, y, c, chips = _place()
        me = 2 * x + y
        own = [pltpu.make_async_copy(ins[t].at[me], outs[t].at[me], local_sems.at[t]) for t in range(n)]
        sends, arrivals = [], []
        for t in range(n):
            for j, (cx, cy) in enumerate(chips):
                sends.append(_remote(ins[t].at[2 * cx + cy], outs[t].at[me], send_sems, recv_sems, 3 * t + j,
                                     (cx, cy, c)))
                got = outs[t].at[2 * cx + cy]
                arrivals.append(_remote(got, got, send_sems, recv_sems, 3 * t + j, (cx, cy, c)))
        return own, sends, arrivals

    def start(*refs):
        own, sends, _ = plan(*refs)
        for cp in own + sends:
            cp.start()

    def finish(*refs):
        own, sends, arrivals = plan(*refs)
        for cp in arrivals:
            cp.wait_recv()
        for cp in sends:
            cp.wait_send()
        for cp in own:
            cp.wait()

    return _Side(parts, [jax.ShapeDtypeStruct(p.shape, p.dtype) for p in parts], 3 * n, n, start, finish)


def _share_side(fulls):
    n = len(fulls)

    def plan(_, outs, send_sems, recv_sems, __):
        x, y, c, _ = _place()
        cps = []
        for t in range(n):
            half = outs[t].shape[0] // 2
            mine = outs[t].at[pl.ds(c * half, half)]
            theirs = outs[t].at[pl.ds((1 - c) * half, half)]
            cps.append((_remote(mine, mine, send_sems, recv_sems, t, (x, y, 1 - c)),
                        _remote(theirs, theirs, send_sems, recv_sems, t, (x, y, 1 - c))))
        return cps

    def start(*refs):
        for cp, _ in plan(*refs):
            cp.start()

    def finish(*refs):
        for cp, rv in plan(*refs):
            rv.wait_recv()
            cp.wait_send()

    return _Side(fulls, [jax.ShapeDtypeStruct(f.shape, f.dtype) for f in fulls], n, 0, start, finish,
                 aliases={t: t for t in range(n)})


def _all_to_all_small(packed):
    def body(in_ref, out_ref, send_sems, recv_sems, local_sem):
        x, y, c, _ = _place()
        me = 4 * x + 2 * y + c
        own = pltpu.make_async_copy(in_ref, out_ref.at[me], local_sem)
        own.start()
        cps, rvs = [], []
        for k in range(1, N_DEV):
            fx, fy, fc = (k >> 2) & 1, (k >> 1) & 1, k & 1
            px, py, pc = x ^ fx, y ^ fy, c ^ fc
            cp = _remote(in_ref, out_ref.at[me], send_sems, recv_sems, k - 1, (px, py, pc))
            cp.start()
            cps.append(cp)
            got = out_ref.at[4 * px + 2 * py + pc]
            rvs.append(_remote(got, got, send_sems, recv_sems, k - 1, (px, py, pc)))
        for rv in rvs:
            rv.wait_recv()
        for cp in cps:
            cp.wait_send()
        own.wait()

    return pl.pallas_call(
        body, name="small_grads_exchange",
        in_specs=[ANY], out_specs=ANY,
        out_shape=jax.ShapeDtypeStruct((N_DEV,) + packed.shape, packed.dtype),
        scratch_shapes=[pltpu.SemaphoreType.DMA((N_DEV - 1,)), pltpu.SemaphoreType.DMA((N_DEV - 1,)),
                        pltpu.SemaphoreType.DMA],
    )(packed)


def _pair_sum(name, g, recv, c_arr):
    _, rows, cols = g.shape
    half = rows // 2
    tr, tc = _div(half, 512), _div(cols, 1024)
    nrb = half // tr

    def body(c_ref, g_ref, r_ref, o_ref):
        o_ref[...] = (g_ref[...] + r_ref[...].astype(F32)).astype(BF16)

    return pl.pallas_call(
        body, name=name,
        grid_spec=pltpu.PrefetchScalarGridSpec(
            num_scalar_prefetch=1, grid=(N_CHIPS, nrb, cols // tc),
            in_specs=[pl.BlockSpec((None, tr, tc), lambda s, i, j, c: (s, c[0] * nrb + i, j)),
                      pl.BlockSpec((None, tr, tc), lambda s, i, j, c: (s, i, j))],
            out_specs=pl.BlockSpec((None, tr, tc), lambda s, i, j, c: (s, i, j))),
        out_shape=jax.ShapeDtypeStruct(recv.shape, BF16),
        compiler_params=_params(("parallel", "parallel", "parallel")),
    )(c_arr, g, recv)


def _chip_sum(name, recv, c_arr, piece=0, pieces=1, into=None):
    _, half, cols = recv.shape
    tr, tc = _div(half, 512), _div(cols, 1024)
    nrb = half // tr

    def body(c_ref, r_ref, *rest):
        o_ref = rest[-1]
        acc = r_ref[0].astype(F32)
        for s in range(1, N_CHIPS):
            acc = acc + r_ref[s].astype(F32)
        o_ref[...] = acc

    return pl.pallas_call(
        body, name=name,
        grid_spec=pltpu.PrefetchScalarGridSpec(
            num_scalar_prefetch=1, grid=(nrb, cols // tc),
            in_specs=[pl.BlockSpec((N_CHIPS, tr, tc), lambda i, j, c: (0, i, j))] + ([] if into is None else [ANY]),
            out_specs=pl.BlockSpec((tr, tc), lambda i, j, c: ((pieces * c[0] + piece) * nrb + i, j))),
        out_shape=jax.ShapeDtypeStruct((2 * pieces * half, cols), F32),
        input_output_aliases={} if into is None else {2: 0},
        compiler_params=_params(("parallel", "parallel")),
    )(c_arr, recv, *([] if into is None else [into]))


def _adamw_math(w, g, m, v):
    m2 = ADAM_B1 * m + (1.0 - ADAM_B1) * g
    v2 = ADAM_B2 * v + (1.0 - ADAM_B2) * (g * g)
    m_hat = m2 / (1.0 - ADAM_B1 ** ADAM_STEP)
    v_hat = v2 / (1.0 - ADAM_B2 ** ADAM_STEP)
    delta = -ADAM_LR * (m_hat / (jnp.sqrt(v_hat) + ADAM_EPS) + ADAM_WD * w)
    return delta, m2, v2


def _adamw(name, g, w, m, v):
    rows, cols = g.shape
    tr, tc = _div(rows, 256), _div(cols, 1024)
    spec = pl.BlockSpec((tr, tc), lambda i, j: (i, j))

    def body(g_ref, w_ref, m_ref, v_ref, go_ref, d_ref, mo_ref, vo_ref):
        gg = g_ref[...]
        delta, m2, v2 = _adamw_math(w_ref[...], gg, m_ref[...], v_ref[...])
        go_ref[...] = gg
        d_ref[...] = delta
        mo_ref[...] = m2
        vo_ref[...] = v2

    return pl.pallas_call(
        body, name=name, grid=(rows // tr, cols // tc),
        in_specs=[spec] * 4, out_specs=[spec] * 4,
        out_shape=[jax.ShapeDtypeStruct(g.shape, F32)] * 4,
        compiler_params=_params(("parallel", "parallel")),
    )(g, w, m, v)


def _sum_slots(name, slots):
    def body(s_ref, o_ref):
        acc = s_ref[0]
        for d in range(1, N_DEV):
            acc = acc + s_ref[d]
        o_ref[...] = acc

    return pl.pallas_call(body, name=name, out_shape=jax.ShapeDtypeStruct(slots.shape[1:], F32))(slots)


def _norm_in(x, gain):
    s, d = x.shape
    tr = _div(s, 256)

    def body(x_ref, g_ref, h_ref):
        xv = x_ref[...]
        r = lax.rsqrt(jnp.mean(xv * xv, axis=-1, keepdims=True) + EPS)
        h_ref[...] = (xv * r * g_ref[...]).astype(BF16)

    return pl.pallas_call(
        body, name="norm_in", grid=(s // tr,),
        in_specs=[pl.BlockSpec((tr, d), lambda i: (i, 0)), pl.BlockSpec((1, d), lambda i: (0, 0))],
        out_specs=pl.BlockSpec((tr, d), lambda i: (i, 0)),
        out_shape=jax.ShapeDtypeStruct((s, d), BF16),
        compiler_params=_params(("parallel",)),
    )(x, gain.reshape(1, d))


def _bias_table(rel_bias):
    h = rel_bias.shape[0]
    n_rel = Q_GROUP + K_GROUP - 1
    lo = PAD - K_GROUP + 1
    left = max(0, -MAX_REL - lo)
    right = max(0, lo + n_rel - 1 - MAX_REL)
    by_rel = jnp.concatenate([jnp.broadcast_to(rel_bias[:, :1], (h, left)), rel_bias,
                              jnp.broadcast_to(rel_bias[:, -1:], (h, right))], axis=1)
    by_rel = by_rel[:, lo + MAX_REL + left:][:, :n_rel]
    rev = by_rel[:, ::-1]
    rev = jnp.concatenate([rev, jnp.zeros((h, 1), rel_bias.dtype)], axis=1)
    skew = jnp.tile(rev, (1, Q_GROUP))[:, :Q_GROUP * n_rel].reshape(h, Q_GROUP, n_rel)
    tab = skew[:, :, Q_GROUP - 1:Q_GROUP - 1 + K_GROUP]
    qi = jnp.arange(Q_GROUP)[:, None] // CHUNK
    kj = jnp.arange(K_GROUP)[None, :] // CHUNK
    in_band = (kj >= qi) & (kj <= qi + N_LEFT_CHUNKS)
    return jnp.where(in_band[None], tab, NEG)


def _attention_fwd(proj, table, s, a):
    heads = a // HEAD_DIM
    scale = HEAD_DIM ** -0.5
    groups = s // Q_GROUP

    def body(q_ref, k_ref, v_ref, z_ref, tab_ref, o_ref, ya_ref, kp, vp):
        kp[0:PAD, :] = jnp.zeros((PAD, HEAD_DIM), BF16)
        vp[0:PAD, :] = jnp.zeros((PAD, HEAD_DIM), BF16)
        kp[PAD:, :] = k_ref[...].astype(BF16)
        vp[PAD:, :] = v_ref[...].astype(BF16)

        def group(g, carry):
            r0 = pl.multiple_of(g * Q_GROUP, Q_GROUP)
            q = q_ref[pl.ds(r0, Q_GROUP), :].astype(BF16)
            kb = kp[pl.ds(r0, K_GROUP), :]
            vb = vp[pl.ds(r0, K_GROUP), :]
            sc = lax.dot_general(q, kb, NT, preferred_element_type=F32) * scale + tab_ref[...]
            col = lax.broadcasted_iota(jnp.int32, (Q_GROUP, K_GROUP), 1)
            sc = jnp.where(col >= PAD - r0, sc, NEG)
            mx = jnp.max(sc, axis=-1, keepdims=True)
            e = jnp.exp(sc - mx)
            p = e / jnp.sum(e, axis=-1, keepdims=True)
            o = jnp.dot(p.astype(BF16), vb, preferred_element_type=F32)
            o_ref[pl.ds(r0, Q_GROUP), :] = o
            z = z_ref[pl.ds(r0, Q_GROUP), :]
            ya_ref[pl.ds(r0, Q_GROUP), :] = (o * (z * _sigmoid(z))).astype(BF16)
            return carry

        lax.fori_loop(0, groups, group, 0)

    col = lambda seg: (lambda h: (0, seg * heads + h))
    blk = lambda seg: pl.BlockSpec((s, HEAD_DIM), col(seg))
    return pl.pallas_call(
        body, name="attention_fwd", grid=(heads,),
        in_specs=[blk(0), blk(1), blk(2), blk(3), pl.BlockSpec((None, Q_GROUP, K_GROUP), lambda h: (h, 0, 0))],
        out_specs=[blk(0), blk(0)],
        out_shape=[jax.ShapeDtypeStruct((s, a), F32), jax.ShapeDtypeStruct((s, a), BF16)],
        scratch_shapes=[pltpu.VMEM((PAD + s, HEAD_DIM), BF16), pltpu.VMEM((PAD + s, HEAD_DIM), BF16)],
        compiler_params=_params(("parallel",)),
    )(proj, proj, proj, proj, table)


def _attention_bwd(proj, o, dya, table, dproj, s, a):
    heads = a // HEAD_DIM
    scale = HEAD_DIM ** -0.5
    groups = s // Q_GROUP

    def body(q_ref, k_ref, v_ref, z_ref, o_ref, dy_ref, tab_ref, _, dp_ref, dtab_ref, kp, vp, dkp, dvp):
        kp[0:PAD, :] = jnp.zeros((PAD, HEAD_DIM), BF16)
        vp[0:PAD, :] = jnp.zeros((PAD, HEAD_DIM), BF16)
        kp[PAD:, :] = k_ref[...].astype(BF16)
        vp[PAD:, :] = v_ref[...].astype(BF16)
        dkp[...] = jnp.zeros_like(dkp)
        dvp[...] = jnp.zeros_like(dvp)
        dtab_ref[...] = jnp.zeros_like(dtab_ref)

        def group(g, carry):
            r0 = pl.multiple_of(g * Q_GROUP, Q_GROUP)
            rows = pl.ds(r0, Q_GROUP)
            band = pl.ds(r0, K_GROUP)
            q = q_ref[rows, :].astype(BF16)
            kb = kp[band, :]
            vb = vp[band, :]
            sc = lax.dot_general(q, kb, NT, preferred_element_type=F32) * scale + tab_ref[...]
            col = lax.broadcasted_iota(jnp.int32, (Q_GROUP, K_GROUP), 1)
            sc = jnp.where(col >= PAD - r0, sc, NEG)
            mx = jnp.max(sc, axis=-1, keepdims=True)
            e = jnp.exp(sc - mx)
            p = e / jnp.sum(e, axis=-1, keepdims=True)
            z = z_ref[rows, :]
            dy = dy_ref[rows, :]
            si, dsi = _silu_and_grad(z)
            dp_ref[3, rows, :] = (dy * o_ref[rows, :] * dsi).astype(BF16)
            dob = (dy * si).astype(BF16)
            dp = lax.dot_general(dob, vb, NT, preferred_element_type=F32)
            ds = p * (dp - jnp.sum(p * dp, axis=-1, keepdims=True))
            dtab_ref[...] += ds
            dsb = (ds * scale).astype(BF16)
            dp_ref[0, rows, :] = jnp.dot(dsb, kb, preferred_element_type=F32).astype(BF16)
            dkp[band, :] += lax.dot_general(dsb, q, TN, preferred_element_type=F32)
            dvp[band, :] += lax.dot_general(p.astype(BF16), dob, TN, preferred_element_type=F32)
            return carry

        lax.fori_loop(0, groups, group, 0)
        dp_ref[1] = dkp[PAD:, :].astype(BF16)
        dp_ref[2] = dvp[PAD:, :].astype(BF16)

    col = lambda seg: (lambda h: (0, seg * heads + h))
    blk = lambda seg: pl.BlockSpec((s, HEAD_DIM), col(seg))
    tab_spec = pl.BlockSpec((None, Q_GROUP, K_GROUP), lambda h: (h, 0, 0))
    return pl.pallas_call(
        body, name="attention_bwd", grid=(heads,),
        in_specs=[blk(0), blk(1), blk(2), blk(3), blk(0), blk(0), tab_spec, ANY],
        out_specs=[pl.BlockSpec((4, s, HEAD_DIM), lambda h: (0, 0, h)), tab_spec],
        out_shape=[jax.ShapeDtypeStruct(dproj.shape, BF16), jax.ShapeDtypeStruct(table.shape, F32)],
        input_output_aliases={7: 0},
        scratch_shapes=[pltpu.VMEM((PAD + s, HEAD_DIM), BF16), pltpu.VMEM((PAD + s, HEAD_DIM), BF16),
                        pltpu.VMEM((PAD + s, HEAD_DIM), F32), pltpu.VMEM((PAD + s, HEAD_DIM), F32)],
        compiler_params=_params(("parallel",)),
    )(proj, proj, proj, proj, o, dya, table, dproj)


def _pick_window(gi, by_window):
    out = by_window[-1]
    for n in range(N_GROUPS - 2, -1, -1):
        out = jnp.where(gi == n, by_window[n], out)
    return out


def _inv_count(gi, first_row, rows):
    t = first_row + lax.broadcasted_iota(jnp.int32, (rows, 1), 0)
    w = jnp.left_shift(2, gi)
    return 1.0 / jnp.minimum(t + 1, w).astype(F32)


def _pool_fwd(proj, pw_full, pool_scale, s, a, p):
    pg = p // N_GROUPS
    ts = _div(s, 512)
    u0, z0 = 4 * a // pg, (4 * a + p) // pg
    hb = ts // HALO

    def body(u_ref, uh_ref, z_ref, pw_ref, ps_ref, d_ref, t_ref, y_ref, ext):
        gi, i = pl.program_id(0), pl.program_id(1)
        u = u_ref[...]
        ext[0:HALO, :] = jnp.where(i > 0, uh_ref[...], 0.0)
        ext[HALO:, :] = u
        e = ext[...]
        sums, shift = [], 1
        for _ in POOL_WINDOWS:
            e = e + pltpu.roll(e, shift, 0)
            sums.append(e)
            shift *= 2
        win = _pick_window(gi, sums)[HALO:, :]
        d = (win * _inv_count(gi, i * ts, ts) - u).astype(BF16)
        d_ref[...] = d
        t = jnp.dot(d, pw_ref[...].reshape(pg, pg), preferred_element_type=F32)
        t_ref[...] = t
        z = z_ref[...]
        y_ref[...] = (t * ps_ref[...] * (z * _sigmoid(z))).astype(BF16)

    out_spec = pl.BlockSpec((ts, pg), lambda g, i: (i, g))
    return pl.pallas_call(
        body, name="pool_fwd", grid=(N_GROUPS, s // ts),
        in_specs=[pl.BlockSpec((ts, pg), lambda g, i: (i, u0 + g)),
                  pl.BlockSpec((HALO, pg), lambda g, i: (jnp.maximum(i * hb - 1, 0), u0 + g)),
                  pl.BlockSpec((ts, pg), lambda g, i: (i, z0 + g)),
                  pl.BlockSpec((N_CHIPS, None, pg // N_CHIPS, pg), lambda g, i: (0, g, 0, 0)),
                  pl.BlockSpec((1, pg), lambda g, i: (0, g))],
        out_specs=[out_spec] * 3,
        out_shape=[jax.ShapeDtypeStruct((s, p), BF16), jax.ShapeDtypeStruct((s, p), F32),
                   jax.ShapeDtypeStruct((s, p), BF16)],
        scratch_shapes=[pltpu.VMEM((ts + HALO, pg), F32)],
        compiler_params=_params(("parallel", "parallel")),
    )(proj, proj, proj, pw_full, pool_scale.reshape(1, p))


def _pool_bwd(proj, dyp, t, d, pw_full, pool_scale, dproj, s, a, p):
    pg = p // N_GROUPS
    ts = _div(s, 512)
    nt = s // ts
    z0 = (4 * a + p) // pg
    hb = ts // HALO
    last_halo = s // HALO - 1

    def body(dy_ref, dyh_ref, z_ref, zh_ref, t_ref, th_ref, d_ref, pw_ref, ps_ref, _,
             dp_ref, dpw_ref, dps_ref, ext):
        gi, i = pl.program_id(0), pl.program_id(1)
        ps = ps_ref[...]
        pw = pw_ref[...].reshape(pg, pg)

        @pl.when(i == 0)
        def _():
            dpw_ref[...] = jnp.zeros_like(dpw_ref)
            dps_ref[...] = jnp.zeros_like(dps_ref)

        def through_gate(dy, z, tt):
            si, dsi = _silu_and_grad(z)
            return dy * si, dy * (tt * ps) * dsi

        tt = t_ref[...]
        dyl, dz = through_gate(dy_ref[...], z_ref[...], tt)
        dp_ref[1] = dz.astype(BF16)
        dps_ref[...] += jnp.sum(dyl * tt, axis=0, keepdims=True)
        dtb = (dyl * ps).astype(BF16)
        dpw_ref[...] += lax.dot_general(d_ref[...], dtb, TN, preferred_element_type=F32).reshape(dpw_ref.shape)
        dd = lax.dot_general(dtb, pw, NT, preferred_element_type=F32)
        dylh, _ = through_gate(dyh_ref[...], zh_ref[...], th_ref[...])
        ddh = lax.dot_general((dylh * ps).astype(BF16), pw, NT, preferred_element_type=F32)
        ddh = jnp.where(i < nt - 1, ddh, 0.0)
        ext[0:ts, :] = dd * _inv_count(gi, i * ts, ts)
        ext[ts:, :] = ddh * _inv_count(gi, (i + 1) * ts, HALO)
        e = ext[...]
        rows = ts + HALO
        sums, shift = [], 1
        for _ in POOL_WINDOWS:
            e = e + pltpu.roll(e, rows - shift, 0)
            sums.append(e)
            shift *= 2
        dp_ref[0] = (_pick_window(gi, sums)[:ts, :] - dd).astype(BF16)

    tile = lambda c0: pl.BlockSpec((ts, pg), lambda g, i: (i, c0 + g))
    halo = lambda c0: pl.BlockSpec((HALO, pg), lambda g, i: (jnp.minimum((i + 1) * hb, last_halo), c0 + g))
    pw_spec = pl.BlockSpec((N_CHIPS, None, pg // N_CHIPS, pg), lambda g, i: (0, g, 0, 0))
    return pl.pallas_call(
        body, name="pool_bwd", grid=(N_GROUPS, nt),
        in_specs=[tile(0), halo(0), tile(z0), halo(z0), tile(0), halo(0), tile(0), pw_spec,
                  pl.BlockSpec((1, pg), lambda g, i: (0, g)), ANY],
        out_specs=[pl.BlockSpec((2, ts, pg), lambda g, i: (2, i, g)), pw_spec,
                   pl.BlockSpec((1, pg), lambda g, i: (0, g))],
        out_shape=[jax.ShapeDtypeStruct(dproj.shape, BF16),
                   jax.ShapeDtypeStruct(pw_full.shape, F32), jax.ShapeDtypeStruct((1, p), F32)],
        input_output_aliases={9: 0},
        scratch_shapes=[pltpu.VMEM((ts + HALO, pg), F32)],
        compiler_params=_params(("parallel", "arbitrary")),
    )(dyp, dyp, proj, proj, t, t, d, pw_full, pool_scale.reshape(1, p), dproj)


def _merge_fwd(ya, yp, woa_full, wop_full, proj, gb_full, s, d, a):
    sw = d // N_CHIPS
    tm, tn = _div(s, 512), _div(sw, 1024)
    per = sw // tn
    ga0, gp0 = (4 * a + 2 * a) // tn, (4 * a + 2 * a + d) // tn

    def body(ya_ref, yp_ref, wa_ref, wp_ref, ga_ref, gp_ref, gb_ref, a_out, b_out, m_out):
        av = jnp.dot(ya_ref[...], wa_ref[...], preferred_element_type=F32)
        bv = jnp.dot(yp_ref[...], wp_ref[...], preferred_element_type=F32)
        a_out[...] = av
        b_out[...] = bv
        sa = _sigmoid(ga_ref[...] + gb_ref[0:1, :])
        sp = _sigmoid(gp_ref[...] + gb_ref[1:2, :])
        m_out[...] = (sa * av + sp * bv).astype(BF16)

    act = pl.BlockSpec((tm, a), lambda i, j: (i, 0))
    wgt = pl.BlockSpec((None, a, tn), lambda i, j: (j // per, 0, j % per))
    out = pl.BlockSpec((tm, tn), lambda i, j: (i, j))
    return pl.pallas_call(
        body, name="merge_fwd", grid=(s // tm, d // tn),
        in_specs=[act, act, wgt, wgt,
                  pl.BlockSpec((tm, tn), lambda i, j: (i, ga0 + j)),
                  pl.BlockSpec((tm, tn), lambda i, j: (i, gp0 + j)),
                  pl.BlockSpec((None, 2, tn), lambda i, j: (j // per, 0, j % per))],
        out_specs=[out, out, out],
        out_shape=[jax.ShapeDtypeStruct((s, d), F32), jax.ShapeDtypeStruct((s, d), F32),
                   jax.ShapeDtypeStruct((s, d), BF16)],
        compiler_params=_params(("parallel", "parallel")),
    )(ya, yp, woa_full, wop_full, proj, proj, gb_full)


def _out_proj(mb, wo, x, s, d):
    tm, tn, tk = _div(s, 1024), _div(d, 1024), _div(d, 2048)

    def epilogue(res, ex, outs):
        outs[0][...] = res + ex[0][...]

    tile = pl.BlockSpec((tm, tn), lambda i, j, k: (i, j))
    return _matmul(
        "out_proj", NN, (s // tm, d // tn, d // tk),
        mb, pl.BlockSpec((tm, tk), lambda i, j, k: (i, k)),
        wo, pl.BlockSpec((tk, tn), lambda i, j, k: (k, j)),
        (tm, tn), [jax.ShapeDtypeStruct((s, d), F32)], [tile], extra=(x,), extra_specs=(tile,),
        epilogue=epilogue)[0][0]


def _loss_head(x2, target, final_gain):
    s, d = x2.shape
    tr = _div(s, 256)

    def body(x_ref, t_ref, g_ref, loss_ref, dx_ref, dxb_ref, dg_ref):
        @pl.when(pl.program_id(0) == 0)
        def _():
            dg_ref[...] = jnp.zeros_like(dg_ref)

        xv = x_ref[...]
        g = g_ref[...]
        r = lax.rsqrt(jnp.mean(xv * xv, axis=-1, keepdims=True) + EPS)
        xn = xv * r
        e = xn * g - t_ref[...]
        loss_ref[...] = 0.5 * jnp.mean(e * e, axis=-1, keepdims=True)
        dy = e / d
        dg_ref[...] += jnp.sum(dy * xn, axis=0, keepdims=True)
        dxn = dy * g
        dx = r * (dxn - xn * jnp.mean(dxn * xn, axis=-1, keepdims=True))
        dx_ref[...] = dx
        dxb_ref[...] = dx.astype(BF16)

    rows = pl.BlockSpec((tr, d), lambda i: (i, 0))
    vec = pl.BlockSpec((1, d), lambda i: (0, 0))
    return pl.pallas_call(
        body, name="loss_head", grid=(s // tr,),
        in_specs=[rows, rows, vec], out_specs=[pl.BlockSpec((tr, 1), lambda i: (i, 0)), rows, rows, vec],
        out_shape=[jax.ShapeDtypeStruct((s, 1), F32), jax.ShapeDtypeStruct((s, d), F32),
                   jax.ShapeDtypeStruct((s, d), BF16), jax.ShapeDtypeStruct((1, d), F32)],
        compiler_params=_params(("arbitrary",)),
    )(x2, target, final_gain.reshape(1, d))


N_SLOTS = 10


def _slot(seg):
    t = seg - 6
    return jnp.where(seg < 6, seg, 6 + 2 * (t % 2) + t // 2)


def _merge_bwd(dxb, wo, a_val, b_val, proj, gb_full, s, d, a):
    sw = d // N_CHIPS
    tm, tn, tk = _div(s, 512), _div(sw, 1024), _div(d, 2048)
    per = sw // tn
    per_slot = a // tn
    ga0, gp0 = (4 * a + 2 * a) // tn, (4 * a + 2 * a + d) // tn

    def epilogue(dm, ex, outs):
        a_ref, b_ref, ga_ref, gp_ref, gb_ref = ex
        da_ref, db_ref, dg_ref, dgb_ref = outs
        sa = _sigmoid(ga_ref[...] + gb_ref[0:1, :])
        sp = _sigmoid(gp_ref[...] + gb_ref[1:2, :])
        da_ref[...] = (dm * sa).astype(BF16)
        db_ref[...] = (dm * sp).astype(BF16)
        dga = dm * a_ref[...] * sa * (1.0 - sa)
        dgp = dm * b_ref[...] * sp * (1.0 - sp)
        dg_ref[0] = dga.astype(BF16)
        dg_ref[1] = dgp.astype(BF16)

        @pl.when(pl.program_id(1) == 0)
        def _():
            dgb_ref[...] = jnp.zeros_like(dgb_ref)

        dgb_ref[0:1, :] += jnp.sum(dga, axis=0, keepdims=True)
        dgb_ref[1:2, :] += jnp.sum(dgp, axis=0, keepdims=True)

    tile = pl.BlockSpec((tm, tn), lambda j, i, k: (i, j))
    sd = jax.ShapeDtypeStruct((s, d), BF16)
    return _matmul(
        "merge_bwd", NT, (d // tn, s // tm, d // tk),
        dxb, pl.BlockSpec((tm, tk), lambda j, i, k: (i, k)),
        wo, pl.BlockSpec((tn, tk), lambda j, i, k: (j, k)),
        (tm, tn), [sd, sd, jax.ShapeDtypeStruct((N_SLOTS, s, a), BF16), jax.ShapeDtypeStruct((2, d), F32)],
        [tile, tile, pl.BlockSpec((2, tm, tn), lambda j, i, k: (3 + j // per_slot, i, j % per_slot)),
         pl.BlockSpec((2, tn), lambda j, i, k: (0, j))],
        extra=(a_val, b_val, proj, proj, gb_full),
        extra_specs=(tile, tile, pl.BlockSpec((tm, tn), lambda j, i, k: (i, ga0 + j)),
                     pl.BlockSpec((tm, tn), lambda j, i, k: (i, gp0 + j)),
                     pl.BlockSpec((None, 2, tn), lambda j, i, k: (j // per, 0, j % per))),
        epilogue=epilogue, accumulate_outs=True)[0]


def _weight_grad(name, act, dout, shard_cols, slots=False, piece=None, sides=()):
    s, kdim = act.shape
    n = dout.shape[0] * dout.shape[2] if slots else dout.shape[1]
    row_tile = lambda i: i
    if shard_cols:
        sw = n // N_CHIPS
        tm, tn = _div(kdim, 1024), _div(math.gcd(sw, dout.shape[2]) if slots else sw, 1024)
        per = sw // tn
        if piece is not None:
            tm = kdim // (2 * piece[1])
            kdim = 2 * tm
            row_tile = lambda i: i * piece[1] + piece[0]
        shape = (N_CHIPS, kdim, sw)
        out = pl.BlockSpec((None, tm, tn), lambda i, j, k: (j // per, i, j % per))
    else:
        sh = kdim // N_CHIPS
        tm, tn = _div(sh, 1024), _div(n, 1024)
        per = sh // tm
        shape = (N_CHIPS, sh, n)
        out = pl.BlockSpec((None, tm, tn), lambda i, j, k: (i // per, i % per, j))
    tk = _div(s, 4096)
    if slots:
        per_slot = dout.shape[2] // tn
        dout_spec = pl.BlockSpec((None, tk, tn), lambda i, j, k: (_slot(j // per_slot), k, j % per_slot))
    else:
        dout_spec = pl.BlockSpec((tk, tn), lambda i, j, k: (k, j))
    return _matmul(
        name, TN, (kdim // tm, n // tn, s // tk),
        act, pl.BlockSpec((tk, tm), lambda i, j, k: (k, row_tile(i))), dout, dout_spec,
        (tm, tn), [jax.ShapeDtypeStruct(shape, F32), jax.ShapeDtypeStruct(shape, BF16)], [out, out], sides=sides)


def _norm_in_bwd(x, dh, dx2, gain):
    s, d = x.shape
    tr = _div(s, 256)

    def body(x_ref, dh_ref, dx2_ref, g_ref, gx_ref, dg_ref):
        @pl.when(pl.program_id(0) == 0)
        def _():
            dg_ref[...] = jnp.zeros_like(dg_ref)

        xv = x_ref[...]
        r = lax.rsqrt(jnp.mean(xv * xv, axis=-1, keepdims=True) + EPS)
        xn = xv * r
        dhv = dh_ref[...]
        dg_ref[...] += jnp.sum(dhv * xn, axis=0, keepdims=True)
        dxn = dhv * g_ref[...]
        gx_ref[...] = r * (dxn - xn * jnp.mean(dxn * xn, axis=-1, keepdims=True)) + dx2_ref[...]

    rows = pl.BlockSpec((tr, d), lambda i: (i, 0))
    vec = pl.BlockSpec((1, d), lambda i: (0, 0))
    return pl.pallas_call(
        body, name="norm_in_bwd", grid=(s // tr,),
        in_specs=[rows, rows, rows, vec], out_specs=[rows, vec],
        out_shape=[jax.ShapeDtypeStruct((s, d), F32), jax.ShapeDtypeStruct((1, d), F32)],
        compiler_params=_params(("arbitrary",)),
    )(x, dh, dx2, gain.reshape(1, d))


def _pack(vectors):
    flat = jnp.concatenate([v.reshape(-1).astype(F32) for v in vectors])
    rows = -(-flat.shape[0] // 1024) * 8
    return jnp.pad(flat, (0, rows * 128 - flat.shape[0])).reshape(rows, 128)


def _unpack(packed, like):
    flat, out, at = packed.reshape(-1), [], 0
    for v in like:
        out.append(flat[at:at + v.size].reshape(v.shape))
        at += v.size
    return out


def _small_adamw(g, w, m, v):
    def body(g_ref, w_ref, m_ref, v_ref, d_ref, mo_ref, vo_ref):
        delta, m2, v2 = _adamw_math(w_ref[...], g_ref[...], m_ref[...], v_ref[...])
        d_ref[...] = delta
        mo_ref[...] = m2
        vo_ref[...] = v2

    return pl.pallas_call(body, name="small_adamw", out_shape=[jax.ShapeDtypeStruct(g.shape, F32)] * 3)(g, w, m, v)


def kernel(x, norm_gain, w_in, rel_bias, pool_w, pool_scale, w_out_attn, w_out_pool, gate_bias, w_out, final_gain, loss_target, m_norm_gain, m_w_in, m_rel_bias, m_pool_w, m_pool_scale, m_w_out_attn, m_w_out_pool, m_gate_bias, m_w_out, m_final_gain, v_norm_gain, v_w_in, v_rel_bias, v_pool_w, v_pool_scale, v_w_out_attn, v_w_out_pool, v_gate_bias, v_w_out, v_final_gain):
    _, s, d = x.shape
    a = p = d // 2
    n_in = w_in.shape[1] * N_CHIPS
    sw_in = w_in.shape[1]
    pg = p // N_GROUPS
    xs = x.reshape(s, d)
    target = loss_target.reshape(s, d)
    c_arr = lax.axis_index("c").astype(jnp.int32).reshape(1)
    chip = 2 * lax.axis_index("x") + lax.axis_index("y")

    _, ((win_full,),) = _run("all_gather_w_in", [_gather_side([w_in.astype(BF16)], [True])])
    rest = _gather_side([w_out_attn.astype(BF16), w_out_pool.astype(BF16), w_out.astype(BF16), pool_w.astype(BF16),
                         gate_bias], [True, True, True, False, False])

    hb = _norm_in(xs, norm_gain)
    tm, tn, tk = _div(s, 1024), _div(sw_in, 1024), _div(d, 4096)
    per_in = sw_in // tn
    (proj,), ((woa_full, wop_full, wo_full, pw_full, gb_full),) = _matmul(
        "in_proj", NN, (s // tm, n_in // tn, d // tk),
        hb, pl.BlockSpec((tm, tk), lambda i, j, k: (i, k)),
        win_full, pl.BlockSpec((None, tk, tn), lambda i, j, k: (j // per_in, k, j % per_in)),
        (tm, tn), [jax.ShapeDtypeStruct((s, n_in), F32)], [pl.BlockSpec((tm, tn), lambda i, j, k: (i, j))],
        sides=[rest])
    wo_mat = wo_full.reshape(d, d)
    table = _bias_table(rel_bias)
    o_attn, ya = _attention_fwd(proj, table, s, a)
    d_pool, t_pool, yp = _pool_fwd(proj, pw_full, pool_scale, s, a, p)
    a_val, b_val, mb = _merge_fwd(ya, yp, woa_full, wop_full, proj, gb_full, s, d, a)
    loss_rows, dx2, dx2b, g_final = _loss_head(_out_proj(mb, wo_mat, xs, s, d), target, final_gain)
    loss = lax.psum(jnp.sum(loss_rows), ("x", "y", "c"))

    da, db, dproj, g_gate_full = _merge_bwd(dx2b, wo_mat, a_val, b_val, proj, gb_full, s, d, a)
    (gwo, gwo_b), _ = _weight_grad("grad_w_out", mb, dx2b, shard_cols=False)
    (gwoa, gwoa_b), _ = _weight_grad("grad_w_out_attn", ya, da, shard_cols=True)
    (gwop, gwop_b), _ = _weight_grad("grad_w_out_pool", yp, db, shard_cols=True)
    early = ["w_out_attn", "w_out_pool", "w_out"]

    sw = d // N_CHIPS
    tm, tn, tk = _div(s, 1024), _div(a, 1024), _div(sw, 1024)
    per_o = sw // tk

    def back_through(name, dout, w_full, sides=()):
        return _matmul(
            name, NT, (s // tm, a // tn, d // tk),
            dout, pl.BlockSpec((tm, tk), lambda i, j, k: (i, k)),
            w_full, pl.BlockSpec((None, tn, tk), lambda i, j, k: (k // per_o, j, k % per_o)),
            (tm, tn), [jax.ShapeDtypeStruct((s, a), F32)], [pl.BlockSpec((tm, tn), lambda i, j, k: (i, j))],
            sides=sides)

    (dya,), (early_sib,) = back_through("grad_y_attn", da, woa_full, [_swap_side([gwoa_b, gwop_b, gwo_b])])
    early_pair = [_pair_sum("pair_sum_" + n, g, r, c_arr) for n, g, r in zip(early, [gwoa, gwop, gwo], early_sib)]
    (dyp,), _ = back_through("grad_y_pool", db, wop_full)
    dproj, gpw, g_pscale = _pool_bwd(proj, dyp, t_pool, d_pool, pw_full, pool_scale, dproj, s, a, p)
    dproj, dtable = _attention_bwd(proj, o_attn, dya, table, dproj, s, a)
    g_rel = jax.vjp(_bias_table, rel_bias)[1](dtable)[0]
    gpw3 = gpw.reshape(N_CHIPS, pg, pg)

    (gw0, gw0_b), (early_chips,) = _weight_grad("grad_w_in_0", hb, dproj, shard_cols=True, slots=True, piece=(0, 2),
                                                sides=[_scatter_side(early_pair)])
    (gw1, gw1_b), (sib0,) = _weight_grad("grad_w_in_1", hb, dproj, shard_cols=True, slots=True, piece=(1, 2),
                                         sides=[_swap_side([gw0_b, gpw3.astype(BF16)])])
    early_halves = [_chip_sum("chip_sum_" + n, r, c_arr) for n, r in zip(early, early_chips)]
    pair0 = [_pair_sum("pair_sum_w_in_0", gw0, sib0[0], c_arr), _pair_sum("pair_sum_pool_w", gpw3, sib0[1], c_arr)]

    tm, tn, tk = _div(s, 1024), _div(d, 1024), _div(a // 2, 1024)
    per_k, per_slot = sw_in // tk, a // tk
    nk_half = n_in // tk // 2

    def grad_h(name, k0, sides, plus=None):
        def add(res, ex, outs):
            outs[0][...] = res + ex[0][...]

        tile = pl.BlockSpec((tm, tn), lambda i, j, k: (i, j))
        return _matmul(
            name, NT, (s // tm, d // tn, nk_half),
            dproj, pl.BlockSpec((None, tm, tk), lambda i, j, k: (_slot((k + k0) // per_slot), i, (k + k0) % per_slot)),
            win_full, pl.BlockSpec((None, tn, tk), lambda i, j, k: ((k + k0) // per_k, j, (k + k0) % per_k)),
            (tm, tn), [jax.ShapeDtypeStruct((s, d), F32)], [tile],
            extra=() if plus is None else (plus,), extra_specs=() if plus is None else (tile,),
            epilogue=None if plus is None else add, sides=sides)

    (dh_a,), (chips0, (sib1,), early_grads) = grad_h(
        "grad_h_a", 0, [_scatter_side(pair0), _swap_side([gw1_b]), _share_side(early_halves)])
    pair1 = _pair_sum("pair_sum_w_in_1", gw1, sib1, c_arr)
    (dh,), ((chips1,),) = grad_h("grad_h_b", nk_half, [_scatter_side([pair1])], plus=dh_a)
    gwin_half = _chip_sum("chip_sum_w_in_0", chips0[0], c_arr, piece=0, pieces=2)
    gwin_half = _chip_sum("chip_sum_w_in_1", chips1, c_arr, piece=1, pieces=2, into=gwin_half)
    gpw_half = _chip_sum("chip_sum_pool_w", chips0[1], c_arr)
    grad_x, g_norm = _norm_in_bwd(xs, dh, dx2, norm_gain)
    _, (late_grads,) = _run("reduce_share_halves", [_share_side([gwin_half, gpw_half])])

    names = ["w_in", "w_out_attn", "w_out_pool", "w_out", "pool_w"]
    grads = [late_grads[0], *early_grads, late_grads[1]]
    big = {}
    weights = [w_in, w_out_attn, w_out_pool, w_out, pool_w.reshape(pg, pg)]
    ms = [m_w_in, m_w_out_attn, m_w_out_pool, m_w_out, m_pool_w.reshape(pg, pg)]
    vs = [v_w_in, v_w_out_attn, v_w_out_pool, v_w_out, v_pool_w.reshape(pg, pg)]
    for n, g, w, m, v in zip(names, grads, weights, ms, vs):
        big[n] = [r.reshape(pool_w.shape) if n == "pool_w" else r for r in _adamw("adamw_" + n, g, w, m, v)]

    small_like = [norm_gain, final_gain, pool_scale, rel_bias, jnp.zeros((2, d), F32)]
    summed = _sum_slots("small_grads_sum", _all_to_all_small(_pack([g_norm, g_final, g_pscale, g_rel, g_gate_full])))
    g_norm_t, g_final_t, g_pscale_t, g_rel_t, g_gate_t = _unpack(summed, small_like)
    g_gate_t = lax.dynamic_slice_in_dim(g_gate_t, chip * sw, sw, axis=1)
    small_g = [g_norm_t, g_final_t, g_pscale_t, g_rel_t, g_gate_t]
    small_w = [norm_gain, final_gain, pool_scale, rel_bias, gate_bias]
    small_m = [m_norm_gain, m_final_gain, m_pool_scale, m_rel_bias, m_gate_bias]
    small_v = [v_norm_gain, v_final_gain, v_pool_scale, v_rel_bias, v_gate_bias]
    packed = _small_adamw(_pack(small_g), _pack(small_w), _pack(small_m), _pack(small_v))
    sd, sm, sv = [_unpack(t, small_w) for t in packed]
    small = {n: [small_g[i], sd[i], sm[i], sv[i]]
             for i, n in enumerate(["norm_gain", "final_gain", "pool_scale", "rel_bias", "gate_bias"])}

    every = {**big, **small}
    order = ["norm_gain", "w_in", "rel_bias", "pool_w", "pool_scale", "w_out_attn", "w_out_pool", "gate_bias",
             "w_out", "final_gain"]
    return (loss, grad_x.reshape(x.shape), *[every[n][0] for n in order], *[every[n][1] for n in order],
            *[every[n][2] for n in order], *[every[n][3] for n in order])
```

```python
import math

import jax
import jax.numpy as jnp
from jax import lax
from jax.experimental import pallas as pl
from jax.experimental.pallas import tpu as pltpu

F32 = jnp.float32
BF16 = jnp.bfloat16
MESH = pl.DeviceIdType.MESH
ANY = pl.BlockSpec(memory_space=pl.ANY)

N_CHIPS = 4
N_DEV = 8
CHUNK = 64
N_LEFT_CHUNKS = 8
PAD = N_LEFT_CHUNKS * CHUNK
HEAD_DIM = 128
MAX_REL = 128
POOL_WINDOWS = (2, 4, 8, 16)
N_GROUPS = len(POOL_WINDOWS)
HALO = 16
Q_GROUP = 4 * CHUNK
K_GROUP = Q_GROUP + PAD
NEG = -1e30
EPS = 1e-6
ADAM_LR, ADAM_B1, ADAM_B2, ADAM_EPS, ADAM_WD, ADAM_STEP = 0.001, 0.9, 0.999, 1e-08, 0.01, 10
VMEM_LIMIT = 56 * 1024 * 1024

NN = (((1,), (0,)), ((), ()))
NT = (((1,), (1,)), ((), ()))
TN = (((0,), (0,)), ((), ()))


def _div(n, pref):
    if n <= pref:
        return n
    for t in range(pref - pref % 128, 0, -128):
        if n % t == 0:
            return t
    raise ValueError((n, pref))


def _params(sem, **kw):
    return pltpu.CompilerParams(dimension_semantics=sem, vmem_limit_bytes=VMEM_LIMIT, **kw)


def _sigmoid(z):
    return jax.nn.sigmoid(z)


def _silu_and_grad(z):
    sg = _sigmoid(z)
    return z * sg, sg * (1.0 + z * (1.0 - sg))


def _matmul(name, dn, grid, a, a_spec, b, b_spec, acc_shape, outs, out_specs, extra=(), extra_specs=(),
            epilogue=None, accumulate_outs=False, sides=(), prefetch=None, carry=None):
    nk = grid[2]
    aliases = {}
    if carry is not None:
        aliases = {2 + len(extra): 0}
        extra, extra_specs = (*extra, carry), (*extra_specs, ANY)
    ne, no = len(extra), len(outs)

    def finish(res, ex, out_refs):
        if epilogue is None:
            for o in out_refs:
                o[...] = res.astype(o.dtype)
        else:
            epilogue(res, ex, out_refs)

    def body(*refs):
        a_ref, b_ref = refs[0], refs[1]
        ex = refs[2:2 + ne]
        out_refs = refs[2 + ne:2 + ne + no]
        if nk == 1:
            finish(lax.dot_general(a_ref[...], b_ref[...], dn, preferred_element_type=F32), ex, out_refs)
            return
        acc = refs[-1]
        k = pl.program_id(2)

        @pl.when(k == 0)
        def _():
            acc[...] = jnp.zeros_like(acc)

        acc[...] += lax.dot_general(a_ref[...], b_ref[...], dn, preferred_element_type=F32)

        @pl.when(k == nk - 1)
        def _():
            finish(acc[...], ex, out_refs)

    sem = ("arbitrary",) * 3 if accumulate_outs else ("parallel", "parallel", "arbitrary")
    return _run(name, list(sides), dict(
        body=body, grid=grid, in_specs=[a_spec, b_spec, *extra_specs], out_specs=list(out_specs),
        out_shape=list(outs), scratch_shapes=[] if nk == 1 else [pltpu.VMEM(acc_shape, F32)],
        operands=[a, b, *extra], sem=sem, aliases=aliases, prefetch=prefetch))


def _place():
    x, y, c = lax.axis_index("x"), lax.axis_index("y"), lax.axis_index("c")
    chips = [(1 - x, y), (x, 1 - y), (1 - x, 1 - y)]
    return x, y, c, chips


def _remote(src, dst, send_sems, recv_sems, k, dev):
    return pltpu.make_async_remote_copy(src_ref=src, dst_ref=dst, send_sem=send_sems.at[k], recv_sem=recv_sems.at[k],
                                        device_id=dev, device_id_type=MESH)


class _Side:
    def __init__(self, ins, out_shapes, n_remote, n_local, start, finish, aliases=None):
        self.ins, self.out_shapes = list(ins), list(out_shapes)
        self.n_remote, self.n_local = max(n_remote, 1), max(n_local, 1)
        self.start, self.finish, self.aliases = start, finish, aliases or {}


def _run(name, sides, compute=None):
    cm = compute or dict(body=None, grid=(), in_specs=[], out_specs=[], out_shape=[], scratch_shapes=[], operands=[])
    grid = tuple(cm["grid"])
    ni, no, ns = len(cm["operands"]), len(cm["out_shape"]), len(cm["scratch_shapes"])
    n_in = [len(sd.ins) for sd in sides]
    n_out = [len(sd.out_shapes) for sd in sides]
    prefetch = cm.get("prefetch")
    shift = 0 if prefetch is None else 1

    def body(*refs):
        refs = refs[shift:]
        at = ni
        side_ins = []
        for n in n_in:
            side_ins.append(refs[at:at + n])
            at += n
        outs = refs[at:at + no]
        at += no
        side_outs = []
        for n in n_out:
            side_outs.append(refs[at:at + n])
            at += n
        scratch = refs[at:at + ns]
        at += ns
        sems = [refs[at + 3 * q:at + 3 * q + 3] for q in range(len(sides))]

        def each(step):
            for sd, i_, o_, m_ in zip(sides, side_ins, side_outs, sems):
                getattr(sd, step)(i_, o_, *m_)

        if not grid:
            each("start")
            each("finish")
            return
        first = last = None
        for ax, g in enumerate(grid):
            f, l = pl.program_id(ax) == 0, pl.program_id(ax) == g - 1
            first = f if first is None else first & f
            last = l if last is None else last & l
        if sides:
            pl.when(first)(lambda: each("start"))
        cm["body"](*refs[:ni], *outs, *scratch)
        if sides:
            pl.when(last)(lambda: each("finish"))

    aliases = {shift + i_: o_ for i_, o_ in (cm.get("aliases") or {}).items()}
    in_at, out_at = shift + ni, no
    for sd, a, b in zip(sides, n_in, n_out):
        for i_, o_ in sd.aliases.items():
            aliases[in_at + i_] = out_at + o_
        in_at, out_at = in_at + a, out_at + b
    scratch_shapes = list(cm["scratch_shapes"])
    for sd in sides:
        scratch_shapes += [pltpu.SemaphoreType.DMA((sd.n_remote,)), pltpu.SemaphoreType.DMA((sd.n_remote,)),
                           pltpu.SemaphoreType.DMA((sd.n_local,))]
    in_specs = list(cm["in_specs"]) + [ANY] * sum(n_in)
    out_specs = list(cm["out_specs"]) + [ANY] * sum(n_out)
    kw = dict(in_specs=in_specs, out_specs=out_specs, scratch_shapes=scratch_shapes)
    if grid:
        kw["grid"] = grid
    if prefetch is not None:
        kw = dict(grid_spec=pltpu.PrefetchScalarGridSpec(num_scalar_prefetch=1, **kw))
    if grid:
        kw["compiler_params"] = _params(("arbitrary",) * len(grid) if sides else cm["sem"])
    res = pl.pallas_call(
        body, name=name, out_shape=list(cm["out_shape"]) + [s for sd in sides for s in sd.out_shapes],
        input_output_aliases=aliases, **kw,
    )(*([] if prefetch is None else [prefetch]), *cm["operands"], *[a for sd in sides for a in sd.ins])
    res = list(res)
    side_res, at = [], no
    for n in n_out:
        side_res.append(res[at:at + n])
        at += n
    return res[:no], side_res


def _gather_side(shards, split, peers=(0, 1, 2)):
    n = len(shards)

    def plan(ins, outs, send_sems, recv_sems, local_sems):
        x, y, c, chips = _place()
        me = 2 * x + y
        sibling = (x, y, 1 - c)
        own = [pltpu.make_async_copy(ins[t], outs[t].at[me], local_sems.at[t]) for t in range(n)]
        direct, relays, arrivals = [], [], []
        for t in range(n):
            half = ins[t].shape[0] // 2
            rows, other = pl.ds(c * half, half), pl.ds((1 - c) * half, half)
            for j in peers:
                cx, cy = chips[j]
                src_chip = 2 * cx + cy
                k = 6 * t + j
                if split[t]:
                    direct.append(_remote(ins[t].at[rows], outs[t].at[me, rows], send_sems, recv_sems, k, (cx, cy, c)))
                    got = outs[t].at[src_chip, rows]
                    relays.append((_remote(got, got, send_sems, recv_sems, k, (cx, cy, c)),
                                   _remote(got, got, send_sems, recv_sems, k + 3, sibling)))
                    theirs = outs[t].at[src_chip, other]
                    arrivals.append(_remote(theirs, theirs, send_sems, recv_sems, k + 3, sibling))
                else:
                    direct.append(_remote(ins[t], outs[t].at[me], send_sems, recv_sems, k, (cx, cy, c)))
                    got = outs[t].at[src_chip]
                    arrivals.append(_remote(got, got, send_sems, recv_sems, k, (cx, cy, c)))
        return own, direct, relays, arrivals

    def start(*refs):
        own, direct, _, _ = plan(*refs)
        for cp in own + direct:
            cp.start()

    def finish(*refs):
        own, direct, relays, arrivals = plan(*refs)
        for landed, onward in relays:
            landed.wait_recv()
            onward.start()
        for cp in arrivals:
            cp.wait_recv()
        for cp in direct + [onward for _, onward in relays]:
            cp.wait_send()
        for cp in own:
            cp.wait()

    return _Side(shards, [jax.ShapeDtypeStruct((N_CHIPS,) + s.shape, s.dtype) for s in shards], 6 * n, n,
                 start, finish)


def _relay_far_side(full):
    def plan(ins, outs, send_sems, recv_sems, __):
        x, y, c, _ = _place()
        src, far = ins[0], outs[0]
        half = far.shape[0] // 2
        quarter = half // 2
        first, second = pl.ds(c * half, quarter), pl.ds(c * half + quarter, quarter)
        x_nb, y_nb, sibling = (1 - x, y, c), (x, 1 - y, c), (x, y, 1 - c)
        x_id, y_id = 2 * (1 - x) + y, 2 * x + (1 - y)
        sends = [_remote(src.at[y_id, first], far.at[first], send_sems, recv_sems, 0, x_nb),
                 _remote(src.at[x_id, second], far.at[second], send_sems, recv_sems, 1, y_nb)]
        landed = [_remote(far.at[first], far.at[first], send_sems, recv_sems, 0, x_nb),
                  _remote(far.at[second], far.at[second], send_sems, recv_sems, 1, y_nb)]
        mine, theirs = far.at[pl.ds(c * half, half)], far.at[pl.ds((1 - c) * half, half)]
        onward = _remote(mine, mine, send_sems, recv_sems, 2, sibling)
        from_sibling = _remote(theirs, theirs, send_sems, recv_sems, 2, sibling)
        return sends, landed, onward, from_sibling

    def start(*refs):
        for cp in plan(*refs)[0]:
            cp.start()

    def finish(*refs):
        sends, landed, onward, from_sibling = plan(*refs)
        for cp in landed:
            cp.wait_recv()
        onward.start()
        from_sibling.wait_recv()
        for cp in sends + [onward]:
            cp.wait_send()

    return _Side([full], [jax.ShapeDtypeStruct(full.shape[1:], full.dtype)], 3, 0, start, finish)


def _swap_side(parts):
    n = len(parts)

    def plan(ins, outs, send_sems, recv_sems, _):
        x, y, c, _ = _place()
        cps = []
        for t in range(n):
            half = ins[t].shape[1] // 2
            cps.append(_remote(ins[t].at[:, pl.ds((1 - c) * half, half)], outs[t], send_sems, recv_sems, t,
                               (x, y, 1 - c)))
        return cps

    def start(*refs):
        for cp in plan(*refs):
            cp.start()

    def finish(*refs):
        for cp in plan(*refs):
            cp.wait()

    return _Side(parts, [jax.ShapeDtypeStruct((p.shape[0], p.shape[1] // 2, p.shape[2]), p.dtype) for p in parts],
                 n, 0, start, finish)


def _scatter_side(parts):
    n = len(parts)

    def plan(ins, outs, send_sems, recv_sems, local_sems):
        x, y, c, chips = _place()
        me = 2 * x + y
        own = [pltpu.make_async_copy(ins[t].at[me], outs[t].at[me], local_sems.at[t]) for t in range(n)]
        sends, arrivals = [], []
        for t in range(n):
            for j, (cx, cy) in enumerate(chips):
                sends.append(_remote(ins[t].at[2 * cx + cy], outs[t].at[me], send_sems, recv_sems, 3 * t + j,
                                     (cx, cy, c)))
                got = outs[t].at[2 * cx + cy]
                arrivals.append(_remote(got, got, send_sems, recv_sems, 3 * t + j, (cx, cy, c)))
        return own, sends, arrivals

    def start(*refs):
        own, sends, _ = plan(*refs)
        for cp in own + sends:
            cp.start()

    def finish(*refs):
        own, sends, arrivals = plan(*refs)
        for cp in arrivals:
            cp.wait_recv()
        for cp in sends:
            cp.wait_send()
        for cp in own:
            cp.wait()

    return _Side(parts, [jax.ShapeDtypeStruct(p.shape, p.dtype) for p in parts], 3 * n, n, start, finish)


def _share_side(fulls):
    n = len(fulls)

    def plan(_, outs, send_sems, recv_sems, __):
        x, y, c, _ = _place()
        cps = []
        for t in range(n):
            half = outs[t].shape[0] // 2
            mine = outs[t].at[pl.ds(c * half, half)]
            theirs = outs[t].at[pl.ds((1 - c) * half, half)]
            cps.append((_remote(mine, mine, send_sems, recv_sems, t, (x, y, 1 - c)),
                        _remote(theirs, theirs, send_sems, recv_sems, t, (x, y, 1 - c))))
        return cps

    def start(*refs):
        for cp, _ in plan(*refs):
            cp.start()

    def finish(*refs):
        for cp, rv in plan(*refs):
            rv.wait_recv()
            cp.wait_send()

    return _Side(fulls, [jax.ShapeDtypeStruct(f.shape, f.dtype) for f in fulls], n, 0, start, finish,
                 aliases={t: t for t in range(n)})


def _all_to_all_small(packed):
    def body(in_ref, out_ref, send_sems, recv_sems, local_sem):
        x, y, c, _ = _place()
        me = 4 * x + 2 * y + c
        own = pltpu.make_async_copy(in_ref, out_ref.at[me], local_sem)
        own.start()
        cps, rvs = [], []
        for k in range(1, N_DEV):
            fx, fy, fc = (k >> 2) & 1, (k >> 1) & 1, k & 1
            px, py, pc = x ^ fx, y ^ fy, c ^ fc
            cp = _remote(in_ref, out_ref.at[me], send_sems, recv_sems, k - 1, (px, py, pc))
            cp.start()
            cps.append(cp)
            got = out_ref.at[4 * px + 2 * py + pc]
            rvs.append(_remote(got, got, send_sems, recv_sems, k - 1, (px, py, pc)))
        for rv in rvs:
            rv.wait_recv()
        for cp in cps:
            cp.wait_send()
        own.wait()

    return pl.pallas_call(
        body, name="small_grads_exchange",
        in_specs=[ANY], out_specs=ANY,
        out_shape=jax.ShapeDtypeStruct((N_DEV,) + packed.shape, packed.dtype),
        scratch_shapes=[pltpu.SemaphoreType.DMA((N_DEV - 1,)), pltpu.SemaphoreType.DMA((N_DEV - 1,)),
                        pltpu.SemaphoreType.DMA],
    )(packed)


def _pair_sum(name, g, recv, c_arr):
    _, rows, cols = g.shape
    half = rows // 2
    tr, tc = _div(half, 512), _div(cols, 1024)
    nrb = half // tr

    def body(c_ref, g_ref, r_ref, o_ref):
        o_ref[...] = (g_ref[...] + r_ref[...].astype(F32)).astype(BF16)

    return pl.pallas_call(
        body, name=name,
        grid_spec=pltpu.PrefetchScalarGridSpec(
            num_scalar_prefetch=1, grid=(N_CHIPS, nrb, cols // tc),
            in_specs=[pl.BlockSpec((None, tr, tc), lambda s, i, j, c: (s, c[0] * nrb + i, j)),
                      pl.BlockSpec((None, tr, tc), lambda s, i, j, c: (s, i, j))],
            out_specs=pl.BlockSpec((None, tr, tc), lambda s, i, j, c: (s, i, j))),
        out_shape=jax.ShapeDtypeStruct(recv.shape, BF16),
        compiler_params=_params(("parallel", "parallel", "parallel")),
    )(c_arr, g, recv)


def _chip_sum(name, recv, c_arr, piece=0, pieces=1, into=None):
    _, half, cols = recv.shape
    tr, tc = _div(half, 512), _div(cols, 1024)
    nrb = half // tr

    def body(c_ref, r_ref, *rest):
        o_ref = rest[-1]
        acc = r_ref[0].astype(F32)
        for s in range(1, N_CHIPS):
            acc = acc + r_ref[s].astype(F32)
        o_ref[...] = acc

    return pl.pallas_call(
        body, name=name,
        grid_spec=pltpu.PrefetchScalarGridSpec(
            num_scalar_prefetch=1, grid=(nrb, cols // tc),
            in_specs=[pl.BlockSpec((N_CHIPS, tr, tc), lambda i, j, c: (0, i, j))] + ([] if into is None else [ANY]),
            out_specs=pl.BlockSpec((tr, tc), lambda i, j, c: ((pieces * c[0] + piece) * nrb + i, j))),
        out_shape=jax.ShapeDtypeStruct((2 * pieces * half, cols), F32),
        input_output_aliases={} if into is None else {2: 0},
        compiler_params=_params(("parallel", "parallel")),
    )(c_arr, recv, *([] if into is None else [into]))


def _adamw_math(w, g, m, v):
    m2 = ADAM_B1 * m + (1.0 - ADAM_B1) * g
    v2 = ADAM_B2 * v + (1.0 - ADAM_B2) * (g * g)
    m_hat = m2 / (1.0 - ADAM_B1 ** ADAM_STEP)
    v_hat = v2 / (1.0 - ADAM_B2 ** ADAM_STEP)
    delta = -ADAM_LR * (m_hat / (jnp.sqrt(v_hat) + ADAM_EPS) + ADAM_WD * w)
    return delta, m2, v2


def _adamw(name, g, w, m, v):
    rows, cols = g.shape
    tr, tc = _div(rows, 256), _div(cols, 1024)
    spec = pl.BlockSpec((tr, tc), lambda i, j: (i, j))

    def body(g_ref, w_ref, m_ref, v_ref, go_ref, d_ref, mo_ref, vo_ref):
        gg = g_ref[...]
        delta, m2, v2 = _adamw_math(w_ref[...], gg, m_ref[...], v_ref[...])
        go_ref[...] = gg
        d_ref[...] = delta
        mo_ref[...] = m2
        vo_ref[...] = v2

    return pl.pallas_call(
        body, name=name, grid=(rows // tr, cols // tc),
        in_specs=[spec] * 4, out_specs=[spec] * 4,
        out_shape=[jax.ShapeDtypeStruct(g.shape, F32)] * 4,
        compiler_params=_params(("parallel", "parallel")),
    )(g, w, m, v)


def _sum_slots(name, slots):
    def body(s_ref, o_ref):
        acc = s_ref[0]
        for d in range(1, N_DEV):
            acc = acc + s_ref[d]
        o_ref[...] = acc

    return pl.pallas_call(body, name=name, out_shape=jax.ShapeDtypeStruct(slots.shape[1:], F32))(slots)


def _norm_in(x, gain):
    s, d = x.shape
    tr = _div(s, 256)

    def body(x_ref, g_ref, h_ref):
        xv = x_ref[...]
        r = lax.rsqrt(jnp.mean(xv * xv, axis=-1, keepdims=True) + EPS)
        h_ref[...] = (xv * r * g_ref[...]).astype(BF16)

    return pl.pallas_call(
        body, name="norm_in", grid=(s // tr,),
        in_specs=[pl.BlockSpec((tr, d), lambda i: (i, 0)), pl.BlockSpec((1, d), lambda i: (0, 0))],
        out_specs=pl.BlockSpec((tr, d), lambda i: (i, 0)),
        out_shape=jax.ShapeDtypeStruct((s, d), BF16),
        compiler_params=_params(("parallel",)),
    )(x, gain.reshape(1, d))


def _bias_table(rel_bias):
    h = rel_bias.shape[0]
    n_rel = Q_GROUP + K_GROUP - 1
    lo = PAD - K_GROUP + 1
    left = max(0, -MAX_REL - lo)
    right = max(0, lo + n_rel - 1 - MAX_REL)
    by_rel = jnp.concatenate([jnp.broadcast_to(rel_bias[:, :1], (h, left)), rel_bias,
                              jnp.broadcast_to(rel_bias[:, -1:], (h, right))], axis=1)
    by_rel = by_rel[:, lo + MAX_REL + left:][:, :n_rel]
    rev = by_rel[:, ::-1]
    rev = jnp.concatenate([rev, jnp.zeros((h, 1), rel_bias.dtype)], axis=1)
    skew = jnp.tile(rev, (1, Q_GROUP))[:, :Q_GROUP * n_rel].reshape(h, Q_GROUP, n_rel)
    tab = skew[:, :, Q_GROUP - 1:Q_GROUP - 1 + K_GROUP]
    qi = jnp.arange(Q_GROUP)[:, None] // CHUNK
    kj = jnp.arange(K_GROUP)[None, :] // CHUNK
    in_band = (kj >= qi) & (kj <= qi + N_LEFT_CHUNKS)
    return jnp.where(in_band[None], tab, NEG)


def _attention_fwd(proj, table, s, a, sides=()):
    heads = a // HEAD_DIM
    scale = HEAD_DIM ** -0.5
    groups = s // Q_GROUP

    def body(q_ref, k_ref, v_ref, z_ref, tab_ref, o_ref, ya_ref, kp, vp):
        kp[0:PAD, :] = jnp.zeros((PAD, HEAD_DIM), BF16)
        vp[0:PAD, :] = jnp.zeros((PAD, HEAD_DIM), BF16)
        kp[PAD:, :] = k_ref[...].astype(BF16)
        vp[PAD:, :] = v_ref[...].astype(BF16)

        def group(g, carry):
            r0 = pl.multiple_of(g * Q_GROUP, Q_GROUP)
            q = q_ref[pl.ds(r0, Q_GROUP), :].astype(BF16)
            kb = kp[pl.ds(r0, K_GROUP), :]
            vb = vp[pl.ds(r0, K_GROUP), :]
            sc = lax.dot_general(q, kb, NT, preferred_element_type=F32) * scale + tab_ref[...]
            col = lax.broadcasted_iota(jnp.int32, (Q_GROUP, K_GROUP), 1)
            sc = jnp.where(col >= PAD - r0, sc, NEG)
            mx = jnp.max(sc, axis=-1, keepdims=True)
            e = jnp.exp(sc - mx)
            p = e / jnp.sum(e, axis=-1, keepdims=True)
            o = jnp.dot(p.astype(BF16), vb, preferred_element_type=F32)
            o_ref[pl.ds(r0, Q_GROUP), :] = o
            z = z_ref[pl.ds(r0, Q_GROUP), :]
            ya_ref[pl.ds(r0, Q_GROUP), :] = (o * (z * _sigmoid(z))).astype(BF16)
            return carry

        lax.fori_loop(0, groups, group, 0)

    col = lambda seg: (lambda h: (0, seg * heads + h))
    blk = lambda seg: pl.BlockSpec((s, HEAD_DIM), col(seg))
    return _run("attention_fwd", list(sides), dict(
        body=body, grid=(heads,),
        in_specs=[blk(0), blk(1), blk(2), blk(3), pl.BlockSpec((None, Q_GROUP, K_GROUP), lambda h: (h, 0, 0))],
        out_specs=[blk(0), blk(0)],
        out_shape=[jax.ShapeDtypeStruct((s, a), F32), jax.ShapeDtypeStruct((s, a), BF16)],
        scratch_shapes=[pltpu.VMEM((PAD + s, HEAD_DIM), BF16), pltpu.VMEM((PAD + s, HEAD_DIM), BF16)],
        operands=[proj, proj, proj, proj, table], sem=("parallel",)))


def _attention_bwd(proj, o, dya, table, dproj, s, a):
    heads = a // HEAD_DIM
    scale = HEAD_DIM ** -0.5
    groups = s // Q_GROUP

    def body(q_ref, k_ref, v_ref, z_ref, o_ref, dy_ref, tab_ref, _, dp_ref, dtab_ref, kp, vp, dkp, dvp):
        kp[0:PAD, :] = jnp.zeros((PAD, HEAD_DIM), BF16)
        vp[0:PAD, :] = jnp.zeros((PAD, HEAD_DIM), BF16)
        kp[PAD:, :] = k_ref[...].astype(BF16)
        vp[PAD:, :] = v_ref[...].astype(BF16)
        dkp[...] = jnp.zeros_like(dkp)
        dvp[...] = jnp.zeros_like(dvp)
        dtab_ref[...] = jnp.zeros_like(dtab_ref)

        def group(g, carry):
            r0 = pl.multiple_of(g * Q_GROUP, Q_GROUP)
            rows = pl.ds(r0, Q_GROUP)
            band = pl.ds(r0, K_GROUP)
            q = q_ref[rows, :].astype(BF16)
            kb = kp[band, :]
            vb = vp[band, :]
            sc = lax.dot_general(q, kb, NT, preferred_element_type=F32) * scale + tab_ref[...]
            col = lax.broadcasted_iota(jnp.int32, (Q_GROUP, K_GROUP), 1)
            sc = jnp.where(col >= PAD - r0, sc, NEG)
            mx = jnp.max(sc, axis=-1, keepdims=True)
            e = jnp.exp(sc - mx)
            p = e / jnp.sum(e, axis=-1, keepdims=True)
            z = z_ref[rows, :]
            dy = dy_ref[rows, :]
            si, dsi = _silu_and_grad(z)
            dp_ref[3, rows, :] = (dy * o_ref[rows, :] * dsi).astype(BF16)
            dob = (dy * si).astype(BF16)
            dp = lax.dot_general(dob, vb, NT, preferred_element_type=F32)
            ds = p * (dp - jnp.sum(p * dp, axis=-1, keepdims=True))
            dtab_ref[...] += ds
            dsb = (ds * scale).astype(BF16)
            dp_ref[0, rows, :] = jnp.dot(dsb, kb, preferred_element_type=F32).astype(BF16)
            dkp[band, :] += lax.dot_general(dsb, q, TN, preferred_element_type=F32)
            dvp[band, :] += lax.dot_general(p.astype(BF16), dob, TN, preferred_element_type=F32)
            return carry

        lax.fori_loop(0, groups, group, 0)
        dp_ref[1] = dkp[PAD:, :].astype(BF16)
        dp_ref[2] = dvp[PAD:, :].astype(BF16)

    col = lambda seg: (lambda h: (0, seg * heads + h))
    blk = lambda seg: pl.BlockSpec((s, HEAD_DIM), col(seg))
    tab_spec = pl.BlockSpec((None, Q_GROUP, K_GROUP), lambda h: (h, 0, 0))
    return pl.pallas_call(
        body, name="attention_bwd", grid=(heads,),
        in_specs=[blk(0), blk(1), blk(2), blk(3), blk(0), blk(0), tab_spec, ANY],
        out_specs=[pl.BlockSpec((4, s, HEAD_DIM), lambda h: (0, 0, h)), tab_spec],
        out_shape=[jax.ShapeDtypeStruct(dproj.shape, BF16), jax.ShapeDtypeStruct(table.shape, F32)],
        input_output_aliases={7: 0},
        scratch_shapes=[pltpu.VMEM((PAD + s, HEAD_DIM), BF16), pltpu.VMEM((PAD + s, HEAD_DIM), BF16),
                        pltpu.VMEM((PAD + s, HEAD_DIM), F32), pltpu.VMEM((PAD + s, HEAD_DIM), F32)],
        compiler_params=_params(("parallel",)),
    )(proj, proj, proj, proj, o, dya, table, dproj)


def _pick_window(gi, by_window):
    out = by_window[-1]
    for n in range(N_GROUPS - 2, -1, -1):
        out = jnp.where(gi == n, by_window[n], out)
    return out


def _inv_count(gi, first_row, rows):
    t = first_row + lax.broadcasted_iota(jnp.int32, (rows, 1), 0)
    w = jnp.left_shift(2, gi)
    return 1.0 / jnp.minimum(t + 1, w).astype(F32)


def _pool_fwd(proj, pw_full, pool_scale, s, a, p):
    pg = p // N_GROUPS
    ts = _div(s, 512)
    u0, z0 = 4 * a // pg, (4 * a + p) // pg
    hb = ts // HALO

    def body(u_ref, uh_ref, z_ref, pw_ref, ps_ref, d_ref, t_ref, y_ref, ext):
        gi, i = pl.program_id(0), pl.program_id(1)
        u = u_ref[...]
        ext[0:HALO, :] = jnp.where(i > 0, uh_ref[...], 0.0)
        ext[HALO:, :] = u
        e = ext[...]
        sums, shift = [], 1
        for _ in POOL_WINDOWS:
            e = e + pltpu.roll(e, shift, 0)
            sums.append(e)
            shift *= 2
        win = _pick_window(gi, sums)[HALO:, :]
        d = (win * _inv_count(gi, i * ts, ts) - u).astype(BF16)
        d_ref[...] = d
        t = jnp.dot(d, pw_ref[...].reshape(pg, pg), preferred_element_type=F32)
        t_ref[...] = t
        z = z_ref[...]
        y_ref[...] = (t * ps_ref[...] * (z * _sigmoid(z))).astype(BF16)

    out_spec = pl.BlockSpec((ts, pg), lambda g, i: (i, g))
    return pl.pallas_call(
        body, name="pool_fwd", grid=(N_GROUPS, s // ts),
        in_specs=[pl.BlockSpec((ts, pg), lambda g, i: (i, u0 + g)),
                  pl.BlockSpec((HALO, pg), lambda g, i: (jnp.maximum(i * hb - 1, 0), u0 + g)),
                  pl.BlockSpec((ts, pg), lambda g, i: (i, z0 + g)),
                  pl.BlockSpec((N_CHIPS, None, pg // N_CHIPS, pg), lambda g, i: (0, g, 0, 0)),
                  pl.BlockSpec((1, pg), lambda g, i: (0, g))],
        out_specs=[out_spec] * 3,
        out_shape=[jax.ShapeDtypeStruct((s, p), BF16), jax.ShapeDtypeStruct((s, p), F32),
                   jax.ShapeDtypeStruct((s, p), BF16)],
        scratch_shapes=[pltpu.VMEM((ts + HALO, pg), F32)],
        compiler_params=_params(("parallel", "parallel")),
    )(proj, proj, proj, pw_full, pool_scale.reshape(1, p))


def _pool_bwd(proj, dyp, t, d, pw_full, pool_scale, dproj, s, a, p):
    pg = p // N_GROUPS
    ts = _div(s, 512)
    nt = s // ts
    z0 = (4 * a + p) // pg
    hb = ts // HALO
    last_halo = s // HALO - 1

    def body(dy_ref, dyh_ref, z_ref, zh_ref, t_ref, th_ref, d_ref, pw_ref, ps_ref, _,
             dp_ref, dpw_ref, dps_ref, ext):
        gi, i = pl.program_id(0), pl.program_id(1)
        ps = ps_ref[...]
        pw = pw_ref[...].reshape(pg, pg)

        @pl.when(i == 0)
        def _():
            dpw_ref[...] = jnp.zeros_like(dpw_ref)
            dps_ref[...] = jnp.zeros_like(dps_ref)

        def through_gate(dy, z, tt):
            si, dsi = _silu_and_grad(z)
            return dy * si, dy * (tt * ps) * dsi

        tt = t_ref[...]
        dyl, dz = through_gate(dy_ref[...], z_ref[...], tt)
        dp_ref[1] = dz.astype(BF16)
        dps_ref[...] += jnp.sum(dyl * tt, axis=0, keepdims=True)
        dtb = (dyl * ps).astype(BF16)
        dpw_ref[...] += lax.dot_general(d_ref[...], dtb, TN, preferred_element_type=F32).reshape(dpw_ref.shape)
        dd = lax.dot_general(dtb, pw, NT, preferred_element_type=F32)
        dylh, _ = through_gate(dyh_ref[...], zh_ref[...], th_ref[...])
        ddh = lax.dot_general((dylh * ps).astype(BF16), pw, NT, preferred_element_type=F32)
        ddh = jnp.where(i < nt - 1, ddh, 0.0)
        ext[0:ts, :] = dd * _inv_count(gi, i * ts, ts)
        ext[ts:, :] = ddh * _inv_count(gi, (i + 1) * ts, HALO)
        e = ext[...]
        rows = ts + HALO
        sums, shift = [], 1
        for _ in POOL_WINDOWS:
            e = e + pltpu.roll(e, rows - shift, 0)
            sums.append(e)
            shift *= 2
        dp_ref[0] = (_pick_window(gi, sums)[:ts, :] - dd).astype(BF16)

    tile = lambda c0: pl.BlockSpec((ts, pg), lambda g, i: (i, c0 + g))
    halo = lambda c0: pl.BlockSpec((HALO, pg), lambda g, i: (jnp.minimum((i + 1) * hb, last_halo), c0 + g))
    pw_spec = pl.BlockSpec((N_CHIPS, None, pg // N_CHIPS, pg), lambda g, i: (0, g, 0, 0))
    return pl.pallas_call(
        body, name="pool_bwd", grid=(N_GROUPS, nt),
        in_specs=[tile(0), halo(0), tile(z0), halo(z0), tile(0), halo(0), tile(0), pw_spec,
                  pl.BlockSpec((1, pg), lambda g, i: (0, g)), ANY],
        out_specs=[pl.BlockSpec((2, ts, pg), lambda g, i: (2, i, g)), pw_spec,
                   pl.BlockSpec((1, pg), lambda g, i: (0, g))],
        out_shape=[jax.ShapeDtypeStruct(dproj.shape, BF16),
                   jax.ShapeDtypeStruct(pw_full.shape, F32), jax.ShapeDtypeStruct((1, p), F32)],
        input_output_aliases={9: 0},
        scratch_shapes=[pltpu.VMEM((ts + HALO, pg), F32)],
        compiler_params=_params(("parallel", "arbitrary")),
    )(dyp, dyp, proj, proj, t, t, d, pw_full, pool_scale.reshape(1, p), dproj)


def _merge_fwd(ya, yp, woa_full, wop_full, proj, gb_full, s, d, a, sides=()):
    sw = d // N_CHIPS
    tm, tn = _div(s, 512), _div(sw, 1024)
    per = sw // tn
    ga0, gp0 = (4 * a + 2 * a) // tn, (4 * a + 2 * a + d) // tn

    def body(ya_ref, yp_ref, wa_ref, wp_ref, ga_ref, gp_ref, gb_ref, a_out, b_out, m_out):
        av = jnp.dot(ya_ref[...], wa_ref[...], preferred_element_type=F32)
        bv = jnp.dot(yp_ref[...], wp_ref[...], preferred_element_type=F32)
        a_out[...] = av
        b_out[...] = bv
        sa = _sigmoid(ga_ref[...] + gb_ref[0:1, :])
        sp = _sigmoid(gp_ref[...] + gb_ref[1:2, :])
        m_out[...] = (sa * av + sp * bv).astype(BF16)

    act = pl.BlockSpec((tm, a), lambda i, j: (i, 0))
    wgt = pl.BlockSpec((None, a, tn), lambda i, j: (j // per, 0, j % per))
    out = pl.BlockSpec((tm, tn), lambda i, j: (i, j))
    return _run("merge_fwd", list(sides), dict(
        body=body, grid=(s // tm, d // tn),
        in_specs=[act, act, wgt, wgt,
                  pl.BlockSpec((tm, tn), lambda i, j: (i, ga0 + j)),
                  pl.BlockSpec((tm, tn), lambda i, j: (i, gp0 + j)),
                  pl.BlockSpec((None, 2, tn), lambda i, j: (j // per, 0, j % per))],
        out_specs=[out, out, out],
        out_shape=[jax.ShapeDtypeStruct((s, d), F32), jax.ShapeDtypeStruct((s, d), F32),
                   jax.ShapeDtypeStruct((s, d), BF16)],
        scratch_shapes=[], operands=[ya, yp, woa_full, wop_full, proj, proj, gb_full], sem=("parallel", "parallel")))


def _out_proj(mb, wo, x, s, d):
    tm, tn, tk = _div(s, 1024), _div(d, 1024), _div(d, 2048)

    def epilogue(res, ex, outs):
        outs[0][...] = res + ex[0][...]

    tile = pl.BlockSpec((tm, tn), lambda i, j, k: (i, j))
    return _matmul(
        "out_proj", NN, (s // tm, d // tn, d // tk),
        mb, pl.BlockSpec((tm, tk), lambda i, j, k: (i, k)),
        wo, pl.BlockSpec((tk, tn), lambda i, j, k: (k, j)),
        (tm, tn), [jax.ShapeDtypeStruct((s, d), F32)], [tile], extra=(x,), extra_specs=(tile,),
        epilogue=epilogue)[0][0]


def _loss_head(x2, target, final_gain):
    s, d = x2.shape
    tr = _div(s, 256)

    def body(x_ref, t_ref, g_ref, loss_ref, dx_ref, dxb_ref, dg_ref):
        @pl.when(pl.program_id(0) == 0)
        def _():
            dg_ref[...] = jnp.zeros_like(dg_ref)

        xv = x_ref[...]
        g = g_ref[...]
        r = lax.rsqrt(jnp.mean(xv * xv, axis=-1, keepdims=True) + EPS)
        xn = xv * r
        e = xn * g - t_ref[...]
        loss_ref[...] = 0.5 * jnp.mean(e * e, axis=-1, keepdims=True)
        dy = e / d
        dg_ref[...] += jnp.sum(dy * xn, axis=0, keepdims=True)
        dxn = dy * g
        dx = r * (dxn - xn * jnp.mean(dxn * xn, axis=-1, keepdims=True))
        dx_ref[...] = dx
        dxb_ref[...] = dx.astype(BF16)

    rows = pl.BlockSpec((tr, d), lambda i: (i, 0))
    vec = pl.BlockSpec((1, d), lambda i: (0, 0))
    return pl.pallas_call(
        body, name="loss_head", grid=(s // tr,),
        in_specs=[rows, rows, vec], out_specs=[pl.BlockSpec((tr, 1), lambda i: (i, 0)), rows, rows, vec],
        out_shape=[jax.ShapeDtypeStruct((s, 1), F32), jax.ShapeDtypeStruct((s, d), F32),
                   jax.ShapeDtypeStruct((s, d), BF16), jax.ShapeDtypeStruct((1, d), F32)],
        compiler_params=_params(("arbitrary",)),
    )(x2, target, final_gain.reshape(1, d))


N_SLOTS = 10


def _slot(seg):
    t = seg - 6
    return jnp.where(seg < 6, seg, 6 + 2 * (t % 2) + t // 2)


def _merge_bwd(dxb, wo, a_val, b_val, proj, gb_full, s, d, a):
    sw = d // N_CHIPS
    tm, tn, tk = _div(s, 512), _div(sw, 1024), _div(d, 2048)
    per = sw // tn
    per_slot = a // tn
    ga0, gp0 = (4 * a + 2 * a) // tn, (4 * a + 2 * a + d) // tn

    def epilogue(dm, ex, outs):
        a_ref, b_ref, ga_ref, gp_ref, gb_ref = ex
        da_ref, db_ref, dg_ref, dgb_ref = outs
        sa = _sigmoid(ga_ref[...] + gb_ref[0:1, :])
        sp = _sigmoid(gp_ref[...] + gb_ref[1:2, :])
        da_ref[...] = (dm * sa).astype(BF16)
        db_ref[...] = (dm * sp).astype(BF16)
        dga = dm * a_ref[...] * sa * (1.0 - sa)
        dgp = dm * b_ref[...] * sp * (1.0 - sp)
        dg_ref[0] = dga.astype(BF16)
        dg_ref[1] = dgp.astype(BF16)

        @pl.when(pl.program_id(1) == 0)
        def _():
            dgb_ref[...] = jnp.zeros_like(dgb_ref)

        dgb_ref[0:1, :] += jnp.sum(dga, axis=0, keepdims=True)
        dgb_ref[1:2, :] += jnp.sum(dgp, axis=0, keepdims=True)

    tile = pl.BlockSpec((tm, tn), lambda j, i, k: (i, j))
    sd = jax.ShapeDtypeStruct((s, d), BF16)
    return _matmul(
        "merge_bwd", NT, (d // tn, s // tm, d // tk),
        dxb, pl.BlockSpec((tm, tk), lambda j, i, k: (i, k)),
        wo, pl.BlockSpec((tn, tk), lambda j, i, k: (j, k)),
        (tm, tn), [sd, sd, jax.ShapeDtypeStruct((N_SLOTS, s, a), BF16), jax.ShapeDtypeStruct((2, d), F32)],
        [tile, tile, pl.BlockSpec((2, tm, tn), lambda j, i, k: (3 + j // per_slot, i, j % per_slot)),
         pl.BlockSpec((2, tn), lambda j, i, k: (0, j))],
        extra=(a_val, b_val, proj, proj, gb_full),
        extra_specs=(tile, tile, pl.BlockSpec((tm, tn), lambda j, i, k: (i, ga0 + j)),
                     pl.BlockSpec((tm, tn), lambda j, i, k: (i, gp0 + j)),
                     pl.BlockSpec((None, 2, tn), lambda j, i, k: (j // per, 0, j % per))),
        epilogue=epilogue, accumulate_outs=True)[0]


def _weight_grad(name, act, dout, shard_cols, slots=False, piece=None, sides=()):
    s, kdim = act.shape
    n = dout.shape[0] * dout.shape[2] if slots else dout.shape[1]
    row_tile = lambda i: i
    if shard_cols:
        sw = n // N_CHIPS
        tm, tn = _div(kdim, 1024), _div(math.gcd(sw, dout.shape[2]) if slots else sw, 1024)
        per = sw // tn
        if piece is not None:
            tm = kdim // (2 * piece[1])
            kdim = 2 * tm
            row_tile = lambda i: i * piece[1] + piece[0]
        shape = (N_CHIPS, kdim, sw)
        out = pl.BlockSpec((None, tm, tn), lambda i, j, k: (j // per, i, j % per))
    else:
        sh = kdim // N_CHIPS
        tm, tn = _div(sh, 1024), _div(n, 1024)
        per = sh // tm
        shape = (N_CHIPS, sh, n)
        out = pl.BlockSpec((None, tm, tn), lambda i, j, k: (i // per, i % per, j))
    tk = _div(s, 4096)
    if slots:
        per_slot = dout.shape[2] // tn
        dout_spec = pl.BlockSpec((None, tk, tn), lambda i, j, k: (_slot(j // per_slot), k, j % per_slot))
    else:
        dout_spec = pl.BlockSpec((tk, tn), lambda i, j, k: (k, j))
    return _matmul(
        name, TN, (kdim // tm, n // tn, s // tk),
        act, pl.BlockSpec((tk, tm), lambda i, j, k: (k, row_tile(i))), dout, dout_spec,
        (tm, tn), [jax.ShapeDtypeStruct(shape, F32), jax.ShapeDtypeStruct(shape, BF16)], [out, out], sides=sides)


def _norm_in_bwd(x, dh, dx2, gain):
    s, d = x.shape
    tr = _div(s, 256)

    def body(x_ref, dh_ref, dx2_ref, g_ref, gx_ref, dg_ref):
        @pl.when(pl.program_id(0) == 0)
        def _():
            dg_ref[...] = jnp.zeros_like(dg_ref)

        xv = x_ref[...]
        r = lax.rsqrt(jnp.mean(xv * xv, axis=-1, keepdims=True) + EPS)
        xn = xv * r
        dhv = dh_ref[...]
        dg_ref[...] += jnp.sum(dhv * xn, axis=0, keepdims=True)
        dxn = dhv * g_ref[...]
        gx_ref[...] = r * (dxn - xn * jnp.mean(dxn * xn, axis=-1, keepdims=True)) + dx2_ref[...]

    rows = pl.BlockSpec((tr, d), lambda i: (i, 0))
    vec = pl.BlockSpec((1, d), lambda i: (0, 0))
    return pl.pallas_call(
        body, name="norm_in_bwd", grid=(s // tr,),
        in_specs=[rows, rows, rows, vec], out_specs=[rows, vec],
        out_shape=[jax.ShapeDtypeStruct((s, d), F32), jax.ShapeDtypeStruct((1, d), F32)],
        compiler_params=_params(("arbitrary",)),
    )(x, dh, dx2, gain.reshape(1, d))


def _pack(vectors):
    flat = jnp.concatenate([v.reshape(-1).astype(F32) for v in vectors])
    rows = -(-flat.shape[0] // 1024) * 8
    return jnp.pad(flat, (0, rows * 128 - flat.shape[0])).reshape(rows, 128)


def _unpack(packed, like):
    flat, out, at = packed.reshape(-1), [], 0
    for v in like:
        out.append(flat[at:at + v.size].reshape(v.shape))
        at += v.size
    return out


def _small_adamw(g, w, m, v):
    def body(g_ref, w_ref, m_ref, v_ref, d_ref, mo_ref, vo_ref):
        delta, m2, v2 = _adamw_math(w_ref[...], g_ref[...], m_ref[...], v_ref[...])
        d_ref[...] = delta
        mo_ref[...] = m2
        vo_ref[...] = v2

    return pl.pallas_call(body, name="small_adamw", out_shape=[jax.ShapeDtypeStruct(g.shape, F32)] * 3)(g, w, m, v)


def kernel(x, norm_gain, w_in, rel_bias, pool_w, pool_scale, w_out_attn, w_out_pool, gate_bias, w_out, final_gain, loss_target, m_norm_gain, m_w_in, m_rel_bias, m_pool_w, m_pool_scale, m_w_out_attn, m_w_out_pool, m_gate_bias, m_w_out, m_final_gain, v_norm_gain, v_w_in, v_rel_bias, v_pool_w, v_pool_scale, v_w_out_attn, v_w_out_pool, v_gate_bias, v_w_out, v_final_gain):
    _, s, d = x.shape
    a = p = d // 2
    n_in = w_in.shape[1] * N_CHIPS
    sw_in = w_in.shape[1]
    pg = p // N_GROUPS
    xs = x.reshape(s, d)
    target = loss_target.reshape(s, d)
    c_arr = lax.axis_index("c").astype(jnp.int32).reshape(1)
    chip = 2 * lax.axis_index("x") + lax.axis_index("y")

    hb = _norm_in(xs, norm_gain)
    tm, tn, tk = _div(s, 1024), _div(sw_in, 1024), _div(d, 4096)
    per_in = sw_in // tn
    cx, cy = lax.axis_index("x"), lax.axis_index("y")
    order = jnp.stack([2 * cx + cy, 2 * (1 - cx) + cy, 2 * cx + (1 - cy), 2 * (1 - cx) + (1 - cy)]).astype(jnp.int32)

    def in_proj(name, first, count, weights, sides, carry=None):
        if weights.ndim == 3:
            w_spec = pl.BlockSpec((None, tk, tn), lambda i, j, k, o: (o[first + j // per_in], k, j % per_in))
        else:
            w_spec = pl.BlockSpec((tk, tn), lambda i, j, k, o: (k, j))
        return _matmul(
            name, NN, (s // tm, count * per_in, d // tk),
            hb, pl.BlockSpec((tm, tk), lambda i, j, k, o: (i, k)), weights, w_spec,
            (tm, tn), [jax.ShapeDtypeStruct((s, n_in), F32)],
            [pl.BlockSpec((tm, tn), lambda i, j, k, o: (i, o[first + j // per_in] * per_in + j % per_in))],
            sides=sides, prefetch=order, carry=carry)

    w_in_b = w_in.astype(BF16)
    (proj,), ((win_near,),) = in_proj("in_proj_own", 0, 1, w_in_b, [_gather_side([w_in_b], [True], peers=(0, 1))])
    (proj,), ((win_far,),) = in_proj("in_proj_near", 1, 2, win_near, [_relay_far_side(win_near)], carry=proj)
    (proj,), ((pw_full, gb_full),) = in_proj(
        "in_proj_far", 3, 1, win_far, [_gather_side([pool_w.astype(BF16), gate_bias], [False, False])], carry=proj)
    win_full = lax.dynamic_update_slice(win_near, win_far[None], (order[3], 0, 0))
    table = _bias_table(rel_bias)
    (o_attn, ya), ((woa_full, wop_full),) = _attention_fwd(
        proj, table, s, a, [_gather_side([w_out_attn.astype(BF16), w_out_pool.astype(BF16)], [True, True])])
    d_pool, t_pool, yp = _pool_fwd(proj, pw_full, pool_scale, s, a, p)
    (a_val, b_val, mb), ((wo_full,),) = _merge_fwd(ya, yp, woa_full, wop_full, proj, gb_full, s, d, a,
                                                  [_gather_side([w_out.astype(BF16)], [True])])
    wo_mat = wo_full.reshape(d, d)
    loss_rows, dx2, dx2b, g_final = _loss_head(_out_proj(mb, wo_mat, xs, s, d), target, final_gain)
    loss = lax.psum(jnp.sum(loss_rows), ("x", "y", "c"))

    da, db, dproj, g_gate_full = _merge_bwd(dx2b, wo_mat, a_val, b_val, proj, gb_full, s, d, a)
    (gwo, gwo_b), _ = _weight_grad("grad_w_out", mb, dx2b, shard_cols=False)
    (gwoa, gwoa_b), _ = _weight_grad("grad_w_out_attn", ya, da, shard_cols=True)
    (gwop, gwop_b), _ = _weight_grad("grad_w_out_pool", yp, db, shard_cols=True)
    early = ["w_out_attn", "w_out_pool", "w_out"]

    sw = d // N_CHIPS
    tm, tn, tk = _div(s, 1024), _div(a, 1024), _div(sw, 1024)
    per_o = sw // tk

    def back_through(name, dout, w_full, sides=()):
        return _matmul(
            name, NT, (s // tm, a // tn, d // tk),
            dout, pl.BlockSpec((tm, tk), lambda i, j, k: (i, k)),
            w_full, pl.BlockSpec((None, tn, tk), lambda i, j, k: (k // per_o, j, k % per_o)),
            (tm, tn), [jax.ShapeDtypeStruct((s, a), F32)], [pl.BlockSpec((tm, tn), lambda i, j, k: (i, j))],
            sides=sides)

    (dya,), (early_sib,) = back_through("grad_y_attn", da, woa_full, [_swap_side([gwoa_b, gwop_b, gwo_b])])
    early_pair = [_pair_sum("pair_sum_" + n, g, r, c_arr) for n, g, r in zip(early, [gwoa, gwop, gwo], early_sib)]
    (dyp,), _ = back_through("grad_y_pool", db, wop_full)
    dproj, gpw, g_pscale = _pool_bwd(proj, dyp, t_pool, d_pool, pw_full, pool_scale, dproj, s, a, p)
    dproj, dtable = _attention_bwd(proj, o_attn, dya, table, dproj, s, a)
    g_rel = jax.vjp(_bias_table, rel_bias)[1](dtable)[0]
    gpw3 = gpw.reshape(N_CHIPS, pg, pg)

    (gw0, gw0_b), (early_chips,) = _weight_grad("grad_w_in_0", hb, dproj, shard_cols=True, slots=True, piece=(0, 2),
                                                sides=[_scatter_side(early_pair)])
    (gw1, gw1_b), (sib0,) = _weight_grad("grad_w_in_1", hb, dproj, shard_cols=True, slots=True, piece=(1, 2),
                                         sides=[_swap_side([gw0_b, gpw3.astype(BF16)])])
    early_halves = [_chip_sum("chip_sum_" + n, r, c_arr) for n, r in zip(early, early_chips)]
    pair0 = [_pair_sum("pair_sum_w_in_0", gw0, sib0[0], c_arr), _pair_sum("pair_sum_pool_w", gpw3, sib0[1], c_arr)]

    tm, tn, tk = _div(s, 1024), _div(d, 1024), _div(a // 2, 1024)
    per_k, per_slot = sw_in // tk, a // tk
    nk_half = n_in // tk // 2

    def grad_h(name, k0, sides, plus=None):
        def add(res, ex, outs):
            outs[0][...] = res + ex[0][...]

        tile = pl.BlockSpec((tm, tn), lambda i, j, k: (i, j))
        return _matmul(
            name, NT, (s // tm, d // tn, nk_half),
            dproj, pl.BlockSpec((None, tm, tk), lambda i, j, k: (_slot((k + k0) // per_slot), i, (k + k0) % per_slot)),
            win_full, pl.BlockSpec((None, tn, tk), lambda i, j, k: ((k + k0) // per_k, j, (k + k0) % per_k)),
            (tm, tn), [jax.ShapeDtypeStruct((s, d), F32)], [tile],
            extra=() if plus is None else (plus,), extra_specs=() if plus is None else (tile,),
            epilogue=None if plus is None else add, sides=sides)

    (dh_a,), (chips0, (sib1,), early_grads) = grad_h(
        "grad_h_a", 0, [_scatter_side(pair0), _swap_side([gw1_b]), _share_side(early_halves)])
    pair1 = _pair_sum("pair_sum_w_in_1", gw1, sib1, c_arr)
    (dh,), ((chips1,),) = grad_h("grad_h_b", nk_half, [_scatter_side([pair1])], plus=dh_a)
    gwin_half = _chip_sum("chip_sum_w_in_0", chips0[0], c_arr, piece=0, pieces=2)
    gwin_half = _chip_sum("chip_sum_w_in_1", chips1, c_arr, piece=1, pieces=2, into=gwin_half)
    gpw_half = _chip_sum("chip_sum_pool_w", chips0[1], c_arr)
    grad_x, g_norm = _norm_in_bwd(xs, dh, dx2, norm_gain)
    _, (late_grads,) = _run("reduce_share_halves", [_share_side([gwin_half, gpw_half])])

    names = ["w_in", "w_out_attn", "w_out_pool", "w_out", "pool_w"]
    grads = [late_grads[0], *early_grads, late_grads[1]]
    big = {}
    weights = [w_in, w_out_attn, w_out_pool, w_out, pool_w.reshape(pg, pg)]
    ms = [m_w_in, m_w_out_attn, m_w_out_pool, m_w_out, m_pool_w.reshape(pg, pg)]
    vs = [v_w_in, v_w_out_attn, v_w_out_pool, v_w_out, v_pool_w.reshape(pg, pg)]
    for n, g, w, m, v in zip(names, grads, weights, ms, vs):
        big[n] = [r.reshape(pool_w.shape) if n == "pool_w" else r for r in _adamw("adamw_" + n, g, w, m, v)]

    small_like = [norm_gain, final_gain, pool_scale, rel_bias, jnp.zeros((2, d), F32)]
    summed = _sum_slots("small_grads_sum", _all_to_all_small(_pack([g_norm, g_final, g_pscale, g_rel, g_gate_full])))
    g_norm_t, g_final_t, g_pscale_t, g_rel_t, g_gate_t = _unpack(summed, small_like)
    g_gate_t = lax.dynamic_slice_in_dim(g_gate_t, chip * sw, sw, axis=1)
    small_g = [g_norm_t, g_final_t, g_pscale_t, g_rel_t, g_gate_t]
    small_w = [norm_gain, final_gain, pool_scale, rel_bias, gate_bias]
    small_m = [m_norm_gain, m_final_gain, m_pool_scale, m_rel_bias, m_gate_bias]
    small_v = [v_norm_gain, v_final_gain, v_pool_scale, v_rel_bias, v_gate_bias]
    packed = _small_adamw(_pack(small_g), _pack(small_w), _pack(small_m), _pack(small_v))
    sd, sm, sv = [_unpack(t, small_w) for t in packed]
    small = {n: [small_g[i], sd[i], sm[i], sv[i]]
             for i, n in enumerate(["norm_gain", "final_gain", "pool_scale", "rel_bias", "gate_bias"])}

    every = {**big, **small}
    order = ["norm_gain", "w_in", "rel_bias", "pool_w", "pool_scale", "w_out_attn", "w_out_pool", "gate_bias",
             "w_out", "final_gain"]
    return (loss, grad_x.reshape(x.shape), *[every[n][0] for n in order], *[every[n][1] for n in order],
            *[every[n][2] for n in order], *[every[n][3] for n in order])
```

```python
import math

import jax
import jax.numpy as jnp
from jax import lax
from jax.experimental import pallas as pl
from jax.experimental.pallas import tpu as pltpu

F32 = jnp.float32
BF16 = jnp.bfloat16
MESH = pl.DeviceIdType.MESH
ANY = pl.BlockSpec(memory_space=pl.ANY)

N_CHIPS = 4
N_DEV = 8
CHUNK = 64
N_LEFT_CHUNKS = 8
PAD = N_LEFT_CHUNKS * CHUNK
HEAD_DIM = 128
MAX_REL = 128
POOL_WINDOWS = (2, 4, 8, 16)
N_GROUPS = len(POOL_WINDOWS)
HALO = 16
Q_GROUP = 4 * CHUNK
K_GROUP = Q_GROUP + PAD
NEG = -1e30
EPS = 1e-6
ADAM_LR, ADAM_B1, ADAM_B2, ADAM_EPS, ADAM_WD, ADAM_STEP = 0.001, 0.9, 0.999, 1e-08, 0.01, 10
VMEM_LIMIT = 56 * 1024 * 1024

NN = (((1,), (0,)), ((), ()))
NT = (((1,), (1,)), ((), ()))
TN = (((0,), (0,)), ((), ()))


def _div(n, pref):
    if n <= pref:
        return n
    for t in range(pref - pref % 128, 0, -128):
        if n % t == 0:
            return t
    raise ValueError((n, pref))


def _params(sem, **kw):
    return pltpu.CompilerParams(dimension_semantics=sem, vmem_limit_bytes=VMEM_LIMIT, **kw)


def _sigmoid(z):
    return jax.nn.sigmoid(z)


def _silu_and_grad(z):
    sg = _sigmoid(z)
    return z * sg, sg * (1.0 + z * (1.0 - sg))


def _matmul(name, dn, grid, a, a_spec, b, b_spec, acc_shape, outs, out_specs, extra=(), extra_specs=(),
            epilogue=None, accumulate_outs=False, sides=(), prefetch=None, carry=None):
    nk = grid[2]
    aliases = {}
    if carry is not None:
        aliases = {2 + len(extra): 0}
        extra, extra_specs = (*extra, carry), (*extra_specs, ANY)
    ne, no = len(extra), len(outs)

    def finish(res, ex, out_refs):
        if epilogue is None:
            for o in out_refs:
                o[...] = res.astype(o.dtype)
        else:
            epilogue(res, ex, out_refs)

    def body(*refs):
        a_ref, b_ref = refs[0], refs[1]
        ex = refs[2:2 + ne]
        out_refs = refs[2 + ne:2 + ne + no]
        if nk == 1:
            finish(lax.dot_general(a_ref[...], b_ref[...], dn, preferred_element_type=F32), ex, out_refs)
            return
        acc = refs[-1]
        k = pl.program_id(2)

        @pl.when(k == 0)
        def _():
            acc[...] = jnp.zeros_like(acc)

        acc[...] += lax.dot_general(a_ref[...], b_ref[...], dn, preferred_element_type=F32)

        @pl.when(k == nk - 1)
        def _():
            finish(acc[...], ex, out_refs)

    sem = ("arbitrary",) * 3 if accumulate_outs else ("parallel", "parallel", "arbitrary")
    return _run(name, list(sides), dict(
        body=body, grid=grid, in_specs=[a_spec, b_spec, *extra_specs], out_specs=list(out_specs),
        out_shape=list(outs), scratch_shapes=[] if nk == 1 else [pltpu.VMEM(acc_shape, F32)],
        operands=[a, b, *extra], sem=sem, aliases=aliases, prefetch=prefetch))


def _place():
    x, y, c = lax.axis_index("x"), lax.axis_index("y"), lax.axis_index("c")
    chips = [(1 - x, y), (x, 1 - y), (1 - x, 1 - y)]
    return x, y, c, chips


N_STREAMS = 8


class _Copies:
    def __init__(self, cps):
        self.cps = cps

    def start(self):
        for cp in self.cps:
            cp.start()

    def wait_send(self):
        for cp in self.cps:
            cp.wait_send()

    def wait_recv(self):
        for cp in self.cps:
            cp.wait_recv()

    def wait(self):
        for cp in self.cps:
            cp.wait()


def _remote(src, dst, send_sems, recv_sems, k, dev):
    lead = src.shape[0]
    n = N_STREAMS
    while n > 1 and (lead % n or (len(src.shape) == 2 and (lead // n) % 16)):
        n //= 2
    step = lead // n
    return _Copies([pltpu.make_async_remote_copy(
        src_ref=src.at[pl.ds(i * step, step)], dst_ref=dst.at[pl.ds(i * step, step)],
        send_sem=send_sems.at[k * N_STREAMS + i], recv_sem=recv_sems.at[k * N_STREAMS + i],
        device_id=dev, device_id_type=MESH) for i in range(n)])


class _Side:
    def __init__(self, ins, out_shapes, n_remote, n_local, start, finish, aliases=None):
        self.ins, self.out_shapes = list(ins), list(out_shapes)
        self.n_remote, self.n_local = max(n_remote, 1), max(n_local, 1)
        self.start, self.finish, self.aliases = start, finish, aliases or {}


def _run(name, sides, compute=None):
    cm = compute or dict(body=None, grid=(), in_specs=[], out_specs=[], out_shape=[], scratch_shapes=[], operands=[])
    grid = tuple(cm["grid"])
    ni, no, ns = len(cm["operands"]), len(cm["out_shape"]), len(cm["scratch_shapes"])
    n_in = [len(sd.ins) for sd in sides]
    n_out = [len(sd.out_shapes) for sd in sides]
    prefetch = cm.get("prefetch")
    shift = 0 if prefetch is None else 1

    def body(*refs):
        refs = refs[shift:]
        at = ni
        side_ins = []
        for n in n_in:
            side_ins.append(refs[at:at + n])
            at += n
        outs = refs[at:at + no]
        at += no
        side_outs = []
        for n in n_out:
            side_outs.append(refs[at:at + n])
            at += n
        scratch = refs[at:at + ns]
        at += ns
        sems = [refs[at + 3 * q:at + 3 * q + 3] for q in range(len(sides))]

        def each(step):
            for sd, i_, o_, m_ in zip(sides, side_ins, side_outs, sems):
                getattr(sd, step)(i_, o_, *m_)

        if not grid:
            each("start")
            each("finish")
            return
        first = last = None
        for ax, g in enumerate(grid):
            f, l = pl.program_id(ax) == 0, pl.program_id(ax) == g - 1
            first = f if first is None else first & f
            last = l if last is None else last & l
        if sides:
            pl.when(first)(lambda: each("start"))
        cm["body"](*refs[:ni], *outs, *scratch)
        if sides:
            pl.when(last)(lambda: each("finish"))

    aliases = {shift + i_: o_ for i_, o_ in (cm.get("aliases") or {}).items()}
    in_at, out_at = shift + ni, no
    for sd, a, b in zip(sides, n_in, n_out):
        for i_, o_ in sd.aliases.items():
            aliases[in_at + i_] = out_at + o_
        in_at, out_at = in_at + a, out_at + b
    scratch_shapes = list(cm["scratch_shapes"])
    for sd in sides:
        scratch_shapes += [pltpu.SemaphoreType.DMA((sd.n_remote * N_STREAMS,)),
                           pltpu.SemaphoreType.DMA((sd.n_remote * N_STREAMS,)), pltpu.SemaphoreType.DMA((sd.n_local,))]
    in_specs = list(cm["in_specs"]) + [ANY] * sum(n_in)
    out_specs = list(cm["out_specs"]) + [ANY] * sum(n_out)
    kw = dict(in_specs=in_specs, out_specs=out_specs, scratch_shapes=scratch_shapes)
    if grid:
        kw["grid"] = grid
    if prefetch is not None:
        kw = dict(grid_spec=pltpu.PrefetchScalarGridSpec(num_scalar_prefetch=1, **kw))
    if grid:
        kw["compiler_params"] = _params(("arbitrary",) * len(grid) if sides else cm["sem"])
    res = pl.pallas_call(
        body, name=name, out_shape=list(cm["out_shape"]) + [s for sd in sides for s in sd.out_shapes],
        input_output_aliases=aliases, **kw,
    )(*([] if prefetch is None else [prefetch]), *cm["operands"], *[a for sd in sides for a in sd.ins])
    res = list(res)
    side_res, at = [], no
    for n in n_out:
        side_res.append(res[at:at + n])
        at += n
    return res[:no], side_res


def _gather_side(shards, split, peers=(0, 1, 2)):
    n = len(shards)

    def plan(ins, outs, send_sems, recv_sems, local_sems):
        x, y, c, chips = _place()
        me = 2 * x + y
        sibling = (x, y, 1 - c)
        own = [pltpu.make_async_copy(ins[t], outs[t].at[me], local_sems.at[t]) for t in range(n)]
        direct, relays, arrivals = [], [], []
        for t in range(n):
            half = ins[t].shape[0] // 2
            rows, other = pl.ds(c * half, half), pl.ds((1 - c) * half, half)
            for j in peers:
                cx, cy = chips[j]
                src_chip = 2 * cx + cy
                k = 6 * t + j
                if split[t]:
                    direct.append(_remote(ins[t].at[rows], outs[t].at[me, rows], send_sems, recv_sems, k, (cx, cy, c)))
                    got = outs[t].at[src_chip, rows]
                    relays.append((_remote(got, got, send_sems, recv_sems, k, (cx, cy, c)),
                                   _remote(got, got, send_sems, recv_sems, k + 3, sibling)))
                    theirs = outs[t].at[src_chip, other]
                    arrivals.append(_remote(theirs, theirs, send_sems, recv_sems, k + 3, sibling))
                else:
                    direct.append(_remote(ins[t], outs[t].at[me], send_sems, recv_sems, k, (cx, cy, c)))
                    got = outs[t].at[src_chip]
                    arrivals.append(_remote(got, got, send_sems, recv_sems, k, (cx, cy, c)))
        return own, direct, relays, arrivals

    def start(*refs):
        own, direct, _, _ = plan(*refs)
        for cp in own + direct:
            cp.start()

    def finish(*refs):
        own, direct, relays, arrivals = plan(*refs)
        for landed, onward in relays:
            landed.wait_recv()
            onward.start()
        for cp in arrivals:
            cp.wait_recv()
        for cp in direct + [onward for _, onward in relays]:
            cp.wait_send()
        for cp in own:
            cp.wait()

    return _Side(shards, [jax.ShapeDtypeStruct((N_CHIPS,) + s.shape, s.dtype) for s in shards], 6 * n, n,
                 start, finish)


def _relay_far_side(full):
    def plan(ins, outs, send_sems, recv_sems, __):
        x, y, c, _ = _place()
        src, far = ins[0], outs[0]
        half = far.shape[0] // 2
        quarter = half // 2
        first, second = pl.ds(c * half, quarter), pl.ds(c * half + quarter, quarter)
        x_nb, y_nb, sibling = (1 - x, y, c), (x, 1 - y, c), (x, y, 1 - c)
        x_id, y_id = 2 * (1 - x) + y, 2 * x + (1 - y)
        sends = [_remote(src.at[y_id, first], far.at[first], send_sems, recv_sems, 0, x_nb),
                 _remote(src.at[x_id, second], far.at[second], send_sems, recv_sems, 1, y_nb)]
        landed = [_remote(far.at[first], far.at[first], send_sems, recv_sems, 0, x_nb),
                  _remote(far.at[second], far.at[second], send_sems, recv_sems, 1, y_nb)]
        mine, theirs = far.at[pl.ds(c * half, half)], far.at[pl.ds((1 - c) * half, half)]
        onward = _remote(mine, mine, send_sems, recv_sems, 2, sibling)
        from_sibling = _remote(theirs, theirs, send_sems, recv_sems, 2, sibling)
        return sends, landed, onward, from_sibling

    def start(*refs):
        for cp in plan(*refs)[0]:
            cp.start()

    def finish(*refs):
        sends, landed, onward, from_sibling = plan(*refs)
        for cp in landed:
            cp.wait_recv()
        onward.start()
        from_sibling.wait_recv()
        for cp in sends + [onward]:
            cp.wait_send()

    return _Side([full], [jax.ShapeDtypeStruct(full.shape[1:], full.dtype)], 3, 0, start, finish)


def _swap_side(parts):
    n = len(parts)

    def plan(ins, outs, send_sems, recv_sems, _):
        x, y, c, _ = _place()
        cps = []
        for t in range(n):
            half = ins[t].shape[1] // 2
            cps.append(_remote(ins[t].at[:, pl.ds((1 - c) * half, half)], outs[t], send_sems, recv_sems, t,
                               (x, y, 1 - c)))
        return cps

    def start(*refs):
        for cp in plan(*refs):
            cp.start()

    def finish(*refs):
        for cp in plan(*refs):
            cp.wait()

    return _Side(parts, [jax.ShapeDtypeStruct((p.shape[0], p.shape[1] // 2, p.shape[2]), p.dtype) for p in parts],
                 n, 0, start, finish)


def _scatter_side(parts):
    n = len(parts)

    def plan(ins, outs, send_sems, recv_sems, local_sems):
        x, y, c, chips = _place()
        me = 2 * x + y
        own = [pltpu.make_async_copy(ins[t].at[me], outs[t].at[me], local_sems.at[t]) for t in range(n)]
        sends, arrivals = [], []
        for t in range(n):
            for j, (cx, cy) in enumerate(chips):
                sends.append(_remote(ins[t].at[2 * cx + cy], outs[t].at[me], send_sems, recv_sems, 3 * t + j,
                                     (cx, cy, c)))
                got = outs[t].at[2 * cx + cy]
                arrivals.append(_remote(got, got, send_sems, recv_sems, 3 * t + j, (cx, cy, c)))
        return own, sends, arrivals

    def start(*refs):
        own, sends, _ = plan(*refs)
        for cp in own + sends:
            cp.start()

    def finish(*refs):
        own, sends, arrivals = plan(*refs)
        for cp in arrivals:
            cp.wait_recv()
        for cp in sends:
            cp.wait_send()
        for cp in own:
            cp.wait()

    return _Side(parts, [jax.ShapeDtypeStruct(p.shape, p.dtype) for p in parts], 3 * n, n, start, finish)


def _share_side(fulls):
    n = len(fulls)

    def plan(_, outs, send_sems, recv_sems, __):
        x, y, c, _ = _place()
        cps = []
        for t in range(n):
            half = outs[t].shape[0] // 2
            mine = outs[t].at[pl.ds(c * half, half)]
            theirs = outs[t].at[pl.ds((1 - c) * half, half)]
            cps.append((_remote(mine, mine, send_sems, recv_sems, t, (x, y, 1 - c)),
                        _remote(theirs, theirs, send_sems, recv_sems, t, (x, y, 1 - c))))
        return cps

    def start(*refs):
        for cp, _ in plan(*refs):
            cp.start()

    def finish(*refs):
        for cp, rv in plan(*refs):
            rv.wait_recv()
            cp.wait_send()

    return _Side(fulls, [jax.ShapeDtypeStruct(f.shape, f.dtype) for f in fulls], n, 0, start, finish,
                 aliases={t: t for t in range(n)})


def _all_to_all_small(packed):
    def body(in_ref, out_ref, send_sems, recv_sems, local_sem):
        x, y, c, _ = _place()
        me = 4 * x + 2 * y + c
        own = pltpu.make_async_copy(in_ref, out_ref.at[me], local_sem)
        own.start()
        cps, rvs = [], []
        for k in range(1, N_DEV):
            fx, fy, fc = (k >> 2) & 1, (k >> 1) & 1, k & 1
            px, py, pc = x ^ fx, y ^ fy, c ^ fc
            cp = _remote(in_ref, out_ref.at[me], send_sems, recv_sems, k - 1, (px, py, pc))
            cp.start()
            cps.append(cp)
            got = out_ref.at[4 * px + 2 * py + pc]
            rvs.append(_remote(got, got, send_sems, recv_sems, k - 1, (px, py, pc)))
        for rv in rvs:
            rv.wait_recv()
        for cp in cps:
            cp.wait_send()
        own.wait()

    return pl.pallas_call(
        body, name="small_grads_exchange",
        in_specs=[ANY], out_specs=ANY,
        out_shape=jax.ShapeDtypeStruct((N_DEV,) + packed.shape, packed.dtype),
        scratch_shapes=[pltpu.SemaphoreType.DMA(((N_DEV - 1) * N_STREAMS,)),
                        pltpu.SemaphoreType.DMA(((N_DEV - 1) * N_STREAMS,)), pltpu.SemaphoreType.DMA],
    )(packed)


def _pair_sum(name, g, recv, c_arr):
    _, rows, cols = g.shape
    half = rows // 2
    tr, tc = _div(half, 512), _div(cols, 1024)
    nrb = half // tr

    def body(c_ref, g_ref, r_ref, o_ref):
        o_ref[...] = (g_ref[...] + r_ref[...].astype(F32)).astype(BF16)

    return pl.pallas_call(
        body, name=name,
        grid_spec=pltpu.PrefetchScalarGridSpec(
            num_scalar_prefetch=1, grid=(N_CHIPS, nrb, cols // tc),
            in_specs=[pl.BlockSpec((None, tr, tc), lambda s, i, j, c: (s, c[0] * nrb + i, j)),
                      pl.BlockSpec((None, tr, tc), lambda s, i, j, c: (s, i, j))],
            out_specs=pl.BlockSpec((None, tr, tc), lambda s, i, j, c: (s, i, j))),
        out_shape=jax.ShapeDtypeStruct(recv.shape, BF16),
        compiler_params=_params(("parallel", "parallel", "parallel")),
    )(c_arr, g, recv)


def _chip_sum(name, recv, c_arr, piece=0, pieces=1, into=None):
    _, half, cols = recv.shape
    tr, tc = _div(half, 512), _div(cols, 1024)
    nrb = half // tr

    def body(c_ref, r_ref, *rest):
        o_ref = rest[-1]
        acc = r_ref[0].astype(F32)
        for s in range(1, N_CHIPS):
            acc = acc + r_ref[s].astype(F32)
        o_ref[...] = acc

    return pl.pallas_call(
        body, name=name,
        grid_spec=pltpu.PrefetchScalarGridSpec(
            num_scalar_prefetch=1, grid=(nrb, cols // tc),
            in_specs=[pl.BlockSpec((N_CHIPS, tr, tc), lambda i, j, c: (0, i, j))] + ([] if into is None else [ANY]),
            out_specs=pl.BlockSpec((tr, tc), lambda i, j, c: ((pieces * c[0] + piece) * nrb + i, j))),
        out_shape=jax.ShapeDtypeStruct((2 * pieces * half, cols), F32),
        input_output_aliases={} if into is None else {2: 0},
        compiler_params=_params(("parallel", "parallel")),
    )(c_arr, recv, *([] if into is None else [into]))


def _adamw_math(w, g, m, v):
    m2 = ADAM_B1 * m + (1.0 - ADAM_B1) * g
    v2 = ADAM_B2 * v + (1.0 - ADAM_B2) * (g * g)
    m_hat = m2 / (1.0 - ADAM_B1 ** ADAM_STEP)
    v_hat = v2 / (1.0 - ADAM_B2 ** ADAM_STEP)
    delta = -ADAM_LR * (m_hat / (jnp.sqrt(v_hat) + ADAM_EPS) + ADAM_WD * w)
    return delta, m2, v2


def _adamw(name, g, w, m, v):
    rows, cols = g.shape
    tr, tc = _div(rows, 256), _div(cols, 1024)
    spec = pl.BlockSpec((tr, tc), lambda i, j: (i, j))

    def body(g_ref, w_ref, m_ref, v_ref, go_ref, d_ref, mo_ref, vo_ref):
        gg = g_ref[...]
        delta, m2, v2 = _adamw_math(w_ref[...], gg, m_ref[...], v_ref[...])
        go_ref[...] = gg
        d_ref[...] = delta
        mo_ref[...] = m2
        vo_ref[...] = v2

    return pl.pallas_call(
        body, name=name, grid=(rows // tr, cols // tc),
        in_specs=[spec] * 4, out_specs=[spec] * 4,
        out_shape=[jax.ShapeDtypeStruct(g.shape, F32)] * 4,
        compiler_params=_params(("parallel", "parallel")),
    )(g, w, m, v)


def _sum_slots(name, slots):
    def body(s_ref, o_ref):
        acc = s_ref[0]
        for d in range(1, N_DEV):
            acc = acc + s_ref[d]
        o_ref[...] = acc

    return pl.pallas_call(body, name=name, out_shape=jax.ShapeDtypeStruct(slots.shape[1:], F32))(slots)


def _norm_in(x, gain):
    s, d = x.shape
    tr = _div(s, 256)

    def body(x_ref, g_ref, h_ref):
        xv = x_ref[...]
        r = lax.rsqrt(jnp.mean(xv * xv, axis=-1, keepdims=True) + EPS)
        h_ref[...] = (xv * r * g_ref[...]).astype(BF16)

    return pl.pallas_call(
        body, name="norm_in", grid=(s // tr,),
        in_specs=[pl.BlockSpec((tr, d), lambda i: (i, 0)), pl.BlockSpec((1, d), lambda i: (0, 0))],
        out_specs=pl.BlockSpec((tr, d), lambda i: (i, 0)),
        out_shape=jax.ShapeDtypeStruct((s, d), BF16),
        compiler_params=_params(("parallel",)),
    )(x, gain.reshape(1, d))


def _bias_table(rel_bias):
    h = rel_bias.shape[0]
    n_rel = Q_GROUP + K_GROUP - 1
    lo = PAD - K_GROUP + 1
    left = max(0, -MAX_REL - lo)
    right = max(0, lo + n_rel - 1 - MAX_REL)
    by_rel = jnp.concatenate([jnp.broadcast_to(rel_bias[:, :1], (h, left)), rel_bias,
                              jnp.broadcast_to(rel_bias[:, -1:], (h, right))], axis=1)
    by_rel = by_rel[:, lo + MAX_REL + left:][:, :n_rel]
    rev = by_rel[:, ::-1]
    rev = jnp.concatenate([rev, jnp.zeros((h, 1), rel_bias.dtype)], axis=1)
    skew = jnp.tile(rev, (1, Q_GROUP))[:, :Q_GROUP * n_rel].reshape(h, Q_GROUP, n_rel)
    tab = skew[:, :, Q_GROUP - 1:Q_GROUP - 1 + K_GROUP]
    qi = jnp.arange(Q_GROUP)[:, None] // CHUNK
    kj = jnp.arange(K_GROUP)[None, :] // CHUNK
    in_band = (kj >= qi) & (kj <= qi + N_LEFT_CHUNKS)
    return jnp.where(in_band[None], tab, NEG)


def _attention_fwd(proj, table, s, a, sides=()):
    heads = a // HEAD_DIM
    scale = HEAD_DIM ** -0.5
    groups = s // Q_GROUP

    def body(q_ref, k_ref, v_ref, z_ref, tab_ref, o_ref, ya_ref, kp, vp):
        kp[0:PAD, :] = jnp.zeros((PAD, HEAD_DIM), BF16)
        vp[0:PAD, :] = jnp.zeros((PAD, HEAD_DIM), BF16)
        kp[PAD:, :] = k_ref[...].astype(BF16)
        vp[PAD:, :] = v_ref[...].astype(BF16)

        def group(g, carry):
            r0 = pl.multiple_of(g * Q_GROUP, Q_GROUP)
            q = q_ref[pl.ds(r0, Q_GROUP), :].astype(BF16)
            kb = kp[pl.ds(r0, K_GROUP), :]
            vb = vp[pl.ds(r0, K_GROUP), :]
            sc = lax.dot_general(q, kb, NT, preferred_element_type=F32) * scale + tab_ref[...]
            col = lax.broadcasted_iota(jnp.int32, (Q_GROUP, K_GROUP), 1)
            sc = jnp.where(col >= PAD - r0, sc, NEG)
            mx = jnp.max(sc, axis=-1, keepdims=True)
            e = jnp.exp(sc - mx)
            p = e / jnp.sum(e, axis=-1, keepdims=True)
            o = jnp.dot(p.astype(BF16), vb, preferred_element_type=F32)
            o_ref[pl.ds(r0, Q_GROUP), :] = o
            z = z_ref[pl.ds(r0, Q_GROUP), :]
            ya_ref[pl.ds(r0, Q_GROUP), :] = (o * (z * _sigmoid(z))).astype(BF16)
            return carry

        lax.fori_loop(0, groups, group, 0)

    col = lambda seg: (lambda h: (0, seg * heads + h))
    blk = lambda seg: pl.BlockSpec((s, HEAD_DIM), col(seg))
    return _run("attention_fwd", list(sides), dict(
        body=body, grid=(heads,),
        in_specs=[blk(0), blk(1), blk(2), blk(3), pl.BlockSpec((None, Q_GROUP, K_GROUP), lambda h: (h, 0, 0))],
        out_specs=[blk(0), blk(0)],
        out_shape=[jax.ShapeDtypeStruct((s, a), F32), jax.ShapeDtypeStruct((s, a), BF16)],
        scratch_shapes=[pltpu.VMEM((PAD + s, HEAD_DIM), BF16), pltpu.VMEM((PAD + s, HEAD_DIM), BF16)],
        operands=[proj, proj, proj, proj, table], sem=("parallel",)))


def _attention_bwd(proj, o, dya, table, dproj, s, a):
    heads = a // HEAD_DIM
    scale = HEAD_DIM ** -0.5
    groups = s // Q_GROUP

    def body(q_ref, k_ref, v_ref, z_ref, o_ref, dy_ref, tab_ref, _, dp_ref, dtab_ref, kp, vp, dkp, dvp):
        kp[0:PAD, :] = jnp.zeros((PAD, HEAD_DIM), BF16)
        vp[0:PAD, :] = jnp.zeros((PAD, HEAD_DIM), BF16)
        kp[PAD:, :] = k_ref[...].astype(BF16)
        vp[PAD:, :] = v_ref[...].astype(BF16)
        dkp[...] = jnp.zeros_like(dkp)
        dvp[...] = jnp.zeros_like(dvp)
        dtab_ref[...] = jnp.zeros_like(dtab_ref)

        def group(g, carry):
            r0 = pl.multiple_of(g * Q_GROUP, Q_GROUP)
            rows = pl.ds(r0, Q_GROUP)
            band = pl.ds(r0, K_GROUP)
            q = q_ref[rows, :].astype(BF16)
            kb = kp[band, :]
            vb = vp[band, :]
            sc = lax.dot_general(q, kb, NT, preferred_element_type=F32) * scale + tab_ref[...]
            col = lax.broadcasted_iota(jnp.int32, (Q_GROUP, K_GROUP), 1)
            sc = jnp.where(col >= PAD - r0, sc, NEG)
            mx = jnp.max(sc, axis=-1, keepdims=True)
            e = jnp.exp(sc - mx)
            p = e / jnp.sum(e, axis=-1, keepdims=True)
            z = z_ref[rows, :]
            dy = dy_ref[rows, :]
            si, dsi = _silu_and_grad(z)
            dp_ref[3, rows, :] = (dy * o_ref[rows, :] * dsi).astype(BF16)
            dob = (dy * si).astype(BF16)
            dp = lax.dot_general(dob, vb, NT, preferred_element_type=F32)
            ds = p * (dp - jnp.sum(p * dp, axis=-1, keepdims=True))
            dtab_ref[...] += ds
            dsb = (ds * scale).astype(BF16)
            dp_ref[0, rows, :] = jnp.dot(dsb, kb, preferred_element_type=F32).astype(BF16)
            dkp[band, :] += lax.dot_general(dsb, q, TN, preferred_element_type=F32)
            dvp[band, :] += lax.dot_general(p.astype(BF16), dob, TN, preferred_element_type=F32)
            return carry

        lax.fori_loop(0, groups, group, 0)
        dp_ref[1] = dkp[PAD:, :].astype(BF16)
        dp_ref[2] = dvp[PAD:, :].astype(BF16)

    col = lambda seg: (lambda h: (0, seg * heads + h))
    blk = lambda seg: pl.BlockSpec((s, HEAD_DIM), col(seg))
    tab_spec = pl.BlockSpec((None, Q_GROUP, K_GROUP), lambda h: (h, 0, 0))
    return pl.pallas_call(
        body, name="attention_bwd", grid=(heads,),
        in_specs=[blk(0), blk(1), blk(2), blk(3), blk(0), blk(0), tab_spec, ANY],
        out_specs=[pl.BlockSpec((4, s, HEAD_DIM), lambda h: (0, 0, h)), tab_spec],
        out_shape=[jax.ShapeDtypeStruct(dproj.shape, BF16), jax.ShapeDtypeStruct(table.shape, F32)],
        input_output_aliases={7: 0},
        scratch_shapes=[pltpu.VMEM((PAD + s, HEAD_DIM), BF16), pltpu.VMEM((PAD + s, HEAD_DIM), BF16),
                        pltpu.VMEM((PAD + s, HEAD_DIM), F32), pltpu.VMEM((PAD + s, HEAD_DIM), F32)],
        compiler_params=_params(("parallel",)),
    )(proj, proj, proj, proj, o, dya, table, dproj)


def _pick_window(gi, by_window):
    out = by_window[-1]
    for n in range(N_GROUPS - 2, -1, -1):
        out = jnp.where(gi == n, by_window[n], out)
    return out


def _inv_count(gi, first_row, rows):
    t = first_row + lax.broadcasted_iota(jnp.int32, (rows, 1), 0)
    w = jnp.left_shift(2, gi)
    return 1.0 / jnp.minimum(t + 1, w).astype(F32)


def _pool_fwd(proj, pw_full, pool_scale, s, a, p):
    pg = p // N_GROUPS
    ts = _div(s, 512)
    u0, z0 = 4 * a // pg, (4 * a + p) // pg
    hb = ts // HALO

    def body(u_ref, uh_ref, z_ref, pw_ref, ps_ref, d_ref, t_ref, y_ref, ext):
        gi, i = pl.program_id(0), pl.program_id(1)
        u = u_ref[...]
        ext[0:HALO, :] = jnp.where(i > 0, uh_ref[...], 0.0)
        ext[HALO:, :] = u
        e = ext[...]
        sums, shift = [], 1
        for _ in POOL_WINDOWS:
            e = e + pltpu.roll(e, shift, 0)
            sums.append(e)
            shift *= 2
        win = _pick_window(gi, sums)[HALO:, :]
        d = (win * _inv_count(gi, i * ts, ts) - u).astype(BF16)
        d_ref[...] = d
        t = jnp.dot(d, pw_ref[...].reshape(pg, pg), preferred_element_type=F32)
        t_ref[...] = t
        z = z_ref[...]
        y_ref[...] = (t * ps_ref[...] * (z * _sigmoid(z))).astype(BF16)

    out_spec = pl.BlockSpec((ts, pg), lambda g, i: (i, g))
    return pl.pallas_call(
        body, name="pool_fwd", grid=(N_GROUPS, s // ts),
        in_specs=[pl.BlockSpec((ts, pg), lambda g, i: (i, u0 + g)),
                  pl.BlockSpec((HALO, pg), lambda g, i: (jnp.maximum(i * hb - 1, 0), u0 + g)),
                  pl.BlockSpec((ts, pg), lambda g, i: (i, z0 + g)),
                  pl.BlockSpec((N_CHIPS, None, pg // N_CHIPS, pg), lambda g, i: (0, g, 0, 0)),
                  pl.BlockSpec((1, pg), lambda g, i: (0, g))],
        out_specs=[out_spec] * 3,
        out_shape=[jax.ShapeDtypeStruct((s, p), BF16), jax.ShapeDtypeStruct((s, p), F32),
                   jax.ShapeDtypeStruct((s, p), BF16)],
        scratch_shapes=[pltpu.VMEM((ts + HALO, pg), F32)],
        compiler_params=_params(("parallel", "parallel")),
    )(proj, proj, proj, pw_full, pool_scale.reshape(1, p))


def _pool_bwd(proj, dyp, t, d, pw_full, pool_scale, dproj, s, a, p):
    pg = p // N_GROUPS
    ts = _div(s, 512)
    nt = s // ts
    z0 = (4 * a + p) // pg
    hb = ts // HALO
    last_halo = s // HALO - 1

    def body(dy_ref, dyh_ref, z_ref, zh_ref, t_ref, th_ref, d_ref, pw_ref, ps_ref, _,
             dp_ref, dpw_ref, dps_ref, ext):
        gi, i = pl.program_id(0), pl.program_id(1)
        ps = ps_ref[...]
        pw = pw_ref[...].reshape(pg, pg)

        @pl.when(i == 0)
        def _():
            dpw_ref[...] = jnp.zeros_like(dpw_ref)
            dps_ref[...] = jnp.zeros_like(dps_ref)

        def through_gate(dy, z, tt):
            si, dsi = _silu_and_grad(z)
            return dy * si, dy * (tt * ps) * dsi

        tt = t_ref[...]
        dyl, dz = through_gate(dy_ref[...], z_ref[...], tt)
        dp_ref[1] = dz.astype(BF16)
        dps_ref[...] += jnp.sum(dyl * tt, axis=0, keepdims=True)
        dtb = (dyl * ps).astype(BF16)
        dpw_ref[...] += lax.dot_general(d_ref[...], dtb, TN, preferred_element_type=F32).reshape(dpw_ref.shape)
        dd = lax.dot_general(dtb, pw, NT, preferred_element_type=F32)
        dylh, _ = through_gate(dyh_ref[...], zh_ref[...], th_ref[...])
        ddh = lax.dot_general((dylh * ps).astype(BF16), pw, NT, preferred_element_type=F32)
        ddh = jnp.where(i < nt - 1, ddh, 0.0)
        ext[0:ts, :] = dd * _inv_count(gi, i * ts, ts)
        ext[ts:, :] = ddh * _inv_count(gi, (i + 1) * ts, HALO)
        e = ext[...]
        rows = ts + HALO
        sums, shift = [], 1
        for _ in POOL_WINDOWS:
            e = e + pltpu.roll(e, rows - shift, 0)
            sums.append(e)
            shift *= 2
        dp_ref[0] = (_pick_window(gi, sums)[:ts, :] - dd).astype(BF16)

    tile = lambda c0: pl.BlockSpec((ts, pg), lambda g, i: (i, c0 + g))
    halo = lambda c0: pl.BlockSpec((HALO, pg), lambda g, i: (jnp.minimum((i + 1) * hb, last_halo), c0 + g))
    pw_spec = pl.BlockSpec((N_CHIPS, None, pg // N_CHIPS, pg), lambda g, i: (0, g, 0, 0))
    return pl.pallas_call(
        body, name="pool_bwd", grid=(N_GROUPS, nt),
        in_specs=[tile(0), halo(0), tile(z0), halo(z0), tile(0), halo(0), tile(0), pw_spec,
                  pl.BlockSpec((1, pg), lambda g, i: (0, g)), ANY],
        out_specs=[pl.BlockSpec((2, ts, pg), lambda g, i: (2, i, g)), pw_spec,
                   pl.BlockSpec((1, pg), lambda g, i: (0, g))],
        out_shape=[jax.ShapeDtypeStruct(dproj.shape, BF16),
                   jax.ShapeDtypeStruct(pw_full.shape, F32), jax.ShapeDtypeStruct((1, p), F32)],
        input_output_aliases={9: 0},
        scratch_shapes=[pltpu.VMEM((ts + HALO, pg), F32)],
        compiler_params=_params(("parallel", "arbitrary")),
    )(dyp, dyp, proj, proj, t, t, d, pw_full, pool_scale.reshape(1, p), dproj)


def _merge_fwd(ya, yp, woa_full, wop_full, proj, gb_full, s, d, a, sides=()):
    sw = d // N_CHIPS
    tm, tn = _div(s, 512), _div(sw, 1024)
    per = sw // tn
    ga0, gp0 = (4 * a + 2 * a) // tn, (4 * a + 2 * a + d) // tn

    def body(ya_ref, yp_ref, wa_ref, wp_ref, ga_ref, gp_ref, gb_ref, a_out, b_out, m_out):
        av = jnp.dot(ya_ref[...], wa_ref[...], preferred_element_type=F32)
        bv = jnp.dot(yp_ref[...], wp_ref[...], preferred_element_type=F32)
        a_out[...] = av
        b_out[...] = bv
        sa = _sigmoid(ga_ref[...] + gb_ref[0:1, :])
        sp = _sigmoid(gp_ref[...] + gb_ref[1:2, :])
        m_out[...] = (sa * av + sp * bv).astype(BF16)

    act = pl.BlockSpec((tm, a), lambda i, j: (i, 0))
    wgt = pl.BlockSpec((None, a, tn), lambda i, j: (j // per, 0, j % per))
    out = pl.BlockSpec((tm, tn), lambda i, j: (i, j))
    return _run("merge_fwd", list(sides), dict(
        body=body, grid=(s // tm, d // tn),
        in_specs=[act, act, wgt, wgt,
                  pl.BlockSpec((tm, tn), lambda i, j: (i, ga0 + j)),
                  pl.BlockSpec((tm, tn), lambda i, j: (i, gp0 + j)),
                  pl.BlockSpec((None, 2, tn), lambda i, j: (j // per, 0, j % per))],
        out_specs=[out, out, out],
        out_shape=[jax.ShapeDtypeStruct((s, d), F32), jax.ShapeDtypeStruct((s, d), F32),
                   jax.ShapeDtypeStruct((s, d), BF16)],
        scratch_shapes=[], operands=[ya, yp, woa_full, wop_full, proj, proj, gb_full], sem=("parallel", "parallel")))


def _out_proj(mb, wo, x, s, d):
    tm, tn, tk = _div(s, 1024), _div(d, 1024), _div(d, 2048)

    def epilogue(res, ex, outs):
        outs[0][...] = res + ex[0][...]

    tile = pl.BlockSpec((tm, tn), lambda i, j, k: (i, j))
    return _matmul(
        "out_proj", NN, (s // tm, d // tn, d // tk),
        mb, pl.BlockSpec((tm, tk), lambda i, j, k: (i, k)),
        wo, pl.BlockSpec((tk, tn), lambda i, j, k: (k, j)),
        (tm, tn), [jax.ShapeDtypeStruct((s, d), F32)], [tile], extra=(x,), extra_specs=(tile,),
        epilogue=epilogue)[0][0]


def _loss_head(x2, target, final_gain):
    s, d = x2.shape
    tr = _div(s, 256)

    def body(x_ref, t_ref, g_ref, loss_ref, dx_ref, dxb_ref, dg_ref):
        @pl.when(pl.program_id(0) == 0)
        def _():
            dg_ref[...] = jnp.zeros_like(dg_ref)

        xv = x_ref[...]
        g = g_ref[...]
        r = lax.rsqrt(jnp.mean(xv * xv, axis=-1, keepdims=True) + EPS)
        xn = xv * r
        e = xn * g - t_ref[...]
        loss_ref[...] = 0.5 * jnp.mean(e * e, axis=-1, keepdims=True)
        dy = e / d
        dg_ref[...] += jnp.sum(dy * xn, axis=0, keepdims=True)
        dxn = dy * g
        dx = r * (dxn - xn * jnp.mean(dxn * xn, axis=-1, keepdims=True))
        dx_ref[...] = dx
        dxb_ref[...] = dx.astype(BF16)

    rows = pl.BlockSpec((tr, d), lambda i: (i, 0))
    vec = pl.BlockSpec((1, d), lambda i: (0, 0))
    return pl.pallas_call(
        body, name="loss_head", grid=(s // tr,),
        in_specs=[rows, rows, vec], out_specs=[pl.BlockSpec((tr, 1), lambda i: (i, 0)), rows, rows, vec],
        out_shape=[jax.ShapeDtypeStruct((s, 1), F32), jax.ShapeDtypeStruct((s, d), F32),
                   jax.ShapeDtypeStruct((s, d), BF16), jax.ShapeDtypeStruct((1, d), F32)],
        compiler_params=_params(("arbitrary",)),
    )(x2, target, final_gain.reshape(1, d))


N_SLOTS = 10


def _slot(seg):
    t = seg - 6
    return jnp.where(seg < 6, seg, 6 + 2 * (t % 2) + t // 2)


def _merge_bwd(dxb, wo, a_val, b_val, proj, gb_full, s, d, a):
    sw = d // N_CHIPS
    tm, tn, tk = _div(s, 512), _div(sw, 1024), _div(d, 2048)
    per = sw // tn
    per_slot = a // tn
    ga0, gp0 = (4 * a + 2 * a) // tn, (4 * a + 2 * a + d) // tn

    def epilogue(dm, ex, outs):
        a_ref, b_ref, ga_ref, gp_ref, gb_ref = ex
        da_ref, db_ref, dg_ref, dgb_ref = outs
        sa = _sigmoid(ga_ref[...] + gb_ref[0:1, :])
        sp = _sigmoid(gp_ref[...] + gb_ref[1:2, :])
        da_ref[...] = (dm * sa).astype(BF16)
        db_ref[...] = (dm * sp).astype(BF16)
        dga = dm * a_ref[...] * sa * (1.0 - sa)
        dgp = dm * b_ref[...] * sp * (1.0 - sp)
        dg_ref[0] = dga.astype(BF16)
        dg_ref[1] = dgp.astype(BF16)

        @pl.when(pl.program_id(1) == 0)
        def _():
            dgb_ref[...] = jnp.zeros_like(dgb_ref)

        dgb_ref[0:1, :] += jnp.sum(dga, axis=0, keepdims=True)
        dgb_ref[1:2, :] += jnp.sum(dgp, axis=0, keepdims=True)

    tile = pl.BlockSpec((tm, tn), lambda j, i, k: (i, j))
    sd = jax.ShapeDtypeStruct((s, d), BF16)
    return _matmul(
        "merge_bwd", NT, (d // tn, s // tm, d // tk),
        dxb, pl.BlockSpec((tm, tk), lambda j, i, k: (i, k)),
        wo, pl.BlockSpec((tn, tk), lambda j, i, k: (j, k)),
        (tm, tn), [sd, sd, jax.ShapeDtypeStruct((N_SLOTS, s, a), BF16), jax.ShapeDtypeStruct((2, d), F32)],
        [tile, tile, pl.BlockSpec((2, tm, tn), lambda j, i, k: (3 + j // per_slot, i, j % per_slot)),
         pl.BlockSpec((2, tn), lambda j, i, k: (0, j))],
        extra=(a_val, b_val, proj, proj, gb_full),
        extra_specs=(tile, tile, pl.BlockSpec((tm, tn), lambda j, i, k: (i, ga0 + j)),
                     pl.BlockSpec((tm, tn), lambda j, i, k: (i, gp0 + j)),
                     pl.BlockSpec((None, 2, tn), lambda j, i, k: (j // per, 0, j % per))),
        epilogue=epilogue, accumulate_outs=True)[0]


def _weight_grad(name, act, dout, shard_cols, slots=False, piece=None, sides=()):
    s, kdim = act.shape
    n = dout.shape[0] * dout.shape[2] if slots else dout.shape[1]
    row_tile = lambda i: i
    if shard_cols:
        sw = n // N_CHIPS
        tm, tn = _div(kdim, 1024), _div(math.gcd(sw, dout.shape[2]) if slots else sw, 1024)
        per = sw // tn
        if piece is not None:
            tm = kdim // (2 * piece[1])
            kdim = 2 * tm
            row_tile = lambda i: i * piece[1] + piece[0]
        shape = (N_CHIPS, kdim, sw)
        out = pl.BlockSpec((None, tm, tn), lambda i, j, k: (j // per, i, j % per))
    else:
        sh = kdim // N_CHIPS
        tm, tn = _div(sh, 1024), _div(n, 1024)
        per = sh // tm
        shape = (N_CHIPS, sh, n)
        out = pl.BlockSpec((None, tm, tn), lambda i, j, k: (i // per, i % per, j))
    tk = _div(s, 4096)
    if slots:
        per_slot = dout.shape[2] // tn
        dout_spec = pl.BlockSpec((None, tk, tn), lambda i, j, k: (_slot(j // per_slot), k, j % per_slot))
    else:
        dout_spec = pl.BlockSpec((tk, tn), lambda i, j, k: (k, j))
    return _matmul(
        name, TN, (kdim // tm, n // tn, s // tk),
        act, pl.BlockSpec((tk, tm), lambda i, j, k: (k, row_tile(i))), dout, dout_spec,
        (tm, tn), [jax.ShapeDtypeStruct(shape, F32), jax.ShapeDtypeStruct(shape, BF16)], [out, out], sides=sides)


def _norm_in_bwd(x, dh, dx2, gain):
    s, d = x.shape
    tr = _div(s, 256)

    def body(x_ref, dh_ref, dx2_ref, g_ref, gx_ref, dg_ref):
        @pl.when(pl.program_id(0) == 0)
        def _():
            dg_ref[...] = jnp.zeros_like(dg_ref)

        xv = x_ref[...]
        r = lax.rsqrt(jnp.mean(xv * xv, axis=-1, keepdims=True) + EPS)
        xn = xv * r
        dhv = dh_ref[...]
        dg_ref[...] += jnp.sum(dhv * xn, axis=0, keepdims=True)
        dxn = dhv * g_ref[...]
        gx_ref[...] = r * (dxn - xn * jnp.mean(dxn * xn, axis=-1, keepdims=True)) + dx2_ref[...]

    rows = pl.BlockSpec((tr, d), lambda i: (i, 0))
    vec = pl.BlockSpec((1, d), lambda i: (0, 0))
    return pl.pallas_call(
        body, name="norm_in_bwd", grid=(s // tr,),
        in_specs=[rows, rows, rows, vec], out_specs=[rows, vec],
        out_shape=[jax.ShapeDtypeStruct((s, d), F32), jax.ShapeDtypeStruct((1, d), F32)],
        compiler_params=_params(("arbitrary",)),
    )(x, dh, dx2, gain.reshape(1, d))


def _pack(vectors):
    flat = jnp.concatenate([v.reshape(-1).astype(F32) for v in vectors])
    rows = -(-flat.shape[0] // 1024) * 8
    return jnp.pad(flat, (0, rows * 128 - flat.shape[0])).reshape(rows, 128)


def _unpack(packed, like):
    flat, out, at = packed.reshape(-1), [], 0
    for v in like:
        out.append(flat[at:at + v.size].reshape(v.shape))
        at += v.size
    return out


def _small_adamw(g, w, m, v):
    def body(g_ref, w_ref, m_ref, v_ref, d_ref, mo_ref, vo_ref):
        delta, m2, v2 = _adamw_math(w_ref[...], g_ref[...], m_ref[...], v_ref[...])
        d_ref[...] = delta
        mo_ref[...] = m2
        vo_ref[...] = v2

    return pl.pallas_call(body, name="small_adamw", out_shape=[jax.ShapeDtypeStruct(g.shape, F32)] * 3)(g, w, m, v)


def kernel(x, norm_gain, w_in, rel_bias, pool_w, pool_scale, w_out_attn, w_out_pool, gate_bias, w_out, final_gain, loss_target, m_norm_gain, m_w_in, m_rel_bias, m_pool_w, m_pool_scale, m_w_out_attn, m_w_out_pool, m_gate_bias, m_w_out, m_final_gain, v_norm_gain, v_w_in, v_rel_bias, v_pool_w, v_pool_scale, v_w_out_attn, v_w_out_pool, v_gate_bias, v_w_out, v_final_gain):
    _, s, d = x.shape
    a = p = d // 2
    n_in = w_in.shape[1] * N_CHIPS
    sw_in = w_in.shape[1]
    pg = p // N_GROUPS
    xs = x.reshape(s, d)
    target = loss_target.reshape(s, d)
    c_arr = lax.axis_index("c").astype(jnp.int32).reshape(1)
    chip = 2 * lax.axis_index("x") + lax.axis_index("y")

    hb = _norm_in(xs, norm_gain)
    tm, tn, tk = _div(s, 1024), _div(sw_in, 1024), _div(d, 4096)
    per_in = sw_in // tn
    cx, cy = lax.axis_index("x"), lax.axis_index("y")
    order = jnp.stack([2 * cx + cy, 2 * (1 - cx) + cy, 2 * cx + (1 - cy), 2 * (1 - cx) + (1 - cy)]).astype(jnp.int32)

    def in_proj(name, first, count, weights, sides, carry=None):
        if weights.ndim == 3:
            w_spec = pl.BlockSpec((None, tk, tn), lambda i, j, k, o: (o[first + j // per_in], k, j % per_in))
        else:
            w_spec = pl.BlockSpec((tk, tn), lambda i, j, k, o: (k, j))
        return _matmul(
            name, NN, (s // tm, count * per_in, d // tk),
            hb, pl.BlockSpec((tm, tk), lambda i, j, k, o: (i, k)), weights, w_spec,
            (tm, tn), [jax.ShapeDtypeStruct((s, n_in), F32)],
            [pl.BlockSpec((tm, tn), lambda i, j, k, o: (i, o[first + j // per_in] * per_in + j % per_in))],
            sides=sides, prefetch=order, carry=carry)

    w_in_b = w_in.astype(BF16)
    (proj,), ((win_near,),) = in_proj("in_proj_own", 0, 1, w_in_b, [_gather_side([w_in_b], [True], peers=(0, 1))])
    (proj,), ((win_far,),) = in_proj("in_proj_near", 1, 2, win_near, [_relay_far_side(win_near)], carry=proj)
    (proj,), ((pw_full, gb_full),) = in_proj(
        "in_proj_far", 3, 1, win_far, [_gather_side([pool_w.astype(BF16), gate_bias], [False, False])], carry=proj)
    win_full = lax.dynamic_update_slice(win_near, win_far[None], (order[3], 0, 0))
    table = _bias_table(rel_bias)
    (o_attn, ya), ((woa_full, wop_full),) = _attention_fwd(
        proj, table, s, a, [_gather_side([w_out_attn.astype(BF16), w_out_pool.astype(BF16)], [True, True])])
    d_pool, t_pool, yp = _pool_fwd(proj, pw_full, pool_scale, s, a, p)
    (a_val, b_val, mb), ((wo_full,),) = _merge_fwd(ya, yp, woa_full, wop_full, proj, gb_full, s, d, a,
                                                  [_gather_side([w_out.astype(BF16)], [True])])
    wo_mat = wo_full.reshape(d, d)
    loss_rows, dx2, dx2b, g_final = _loss_head(_out_proj(mb, wo_mat, xs, s, d), target, final_gain)
    loss = lax.psum(jnp.sum(loss_rows), ("x", "y", "c"))

    da, db, dproj, g_gate_full = _merge_bwd(dx2b, wo_mat, a_val, b_val, proj, gb_full, s, d, a)
    (gwo, gwo_b), _ = _weight_grad("grad_w_out", mb, dx2b, shard_cols=False)
    (gwoa, gwoa_b), _ = _weight_grad("grad_w_out_attn", ya, da, shard_cols=True)
    (gwop, gwop_b), _ = _weight_grad("grad_w_out_pool", yp, db, shard_cols=True)
    early = ["w_out_attn", "w_out_pool", "w_out"]

    sw = d // N_CHIPS
    tm, tn, tk = _div(s, 1024), _div(a, 1024), _div(sw, 1024)
    per_o = sw // tk

    def back_through(name, dout, w_full, sides=()):
        return _matmul(
            name, NT, (s // tm, a // tn, d // tk),
            dout, pl.BlockSpec((tm, tk), lambda i, j, k: (i, k)),
            w_full, pl.BlockSpec((None, tn, tk), lambda i, j, k: (k // per_o, j, k % per_o)),
            (tm, tn), [jax.ShapeDtypeStruct((s, a), F32)], [pl.BlockSpec((tm, tn), lambda i, j, k: (i, j))],
            sides=sides)

    (dya,), (early_sib,) = back_through("grad_y_attn", da, woa_full, [_swap_side([gwoa_b, gwop_b, gwo_b])])
    early_pair = [_pair_sum("pair_sum_" + n, g, r, c_arr) for n, g, r in zip(early, [gwoa, gwop, gwo], early_sib)]
    (dyp,), _ = back_through("grad_y_pool", db, wop_full)
    dproj, gpw, g_pscale = _pool_bwd(proj, dyp, t_pool, d_pool, pw_full, pool_scale, dproj, s, a, p)
    dproj, dtable = _attention_bwd(proj, o_attn, dya, table, dproj, s, a)
    g_rel = jax.vjp(_bias_table, rel_bias)[1](dtable)[0]
    gpw3 = gpw.reshape(N_CHIPS, pg, pg)

    (gw0, gw0_b), (early_chips,) = _weight_grad("grad_w_in_0", hb, dproj, shard_cols=True, slots=True, piece=(0, 2),
                                                sides=[_scatter_side(early_pair)])
    (gw1, gw1_b), (sib0,) = _weight_grad("grad_w_in_1", hb, dproj, shard_cols=True, slots=True, piece=(1, 2),
                                         sides=[_swap_side([gw0_b, gpw3.astype(BF16)])])
    early_halves = [_chip_sum("chip_sum_" + n, r, c_arr) for n, r in zip(early, early_chips)]
    pair0 = [_pair_sum("pair_sum_w_in_0", gw0, sib0[0], c_arr), _pair_sum("pair_sum_pool_w", gpw3, sib0[1], c_arr)]

    tm, tn, tk = _div(s, 1024), _div(d, 1024), _div(a // 2, 1024)
    per_k, per_slot = sw_in // tk, a // tk
    nk_half = n_in // tk // 2

    def grad_h(name, k0, sides, plus=None):
        def add(res, ex, outs):
            outs[0][...] = res + ex[0][...]

        tile = pl.BlockSpec((tm, tn), lambda i, j, k: (i, j))
        return _matmul(
            name, NT, (s // tm, d // tn, nk_half),
            dproj, pl.BlockSpec((None, tm, tk), lambda i, j, k: (_slot((k + k0) // per_slot), i, (k + k0) % per_slot)),
            win_full, pl.BlockSpec((None, tn, tk), lambda i, j, k: ((k + k0) // per_k, j, (k + k0) % per_k)),
            (tm, tn), [jax.ShapeDtypeStruct((s, d), F32)], [tile],
            extra=() if plus is None else (plus,), extra_specs=() if plus is None else (tile,),
            epilogue=None if plus is None else add, sides=sides)

    (dh_a,), (chips0, (sib1,), early_grads) = grad_h(
        "grad_h_a", 0, [_scatter_side(pair0), _swap_side([gw1_b]), _share_side(early_halves)])
    pair1 = _pair_sum("pair_sum_w_in_1", gw1, sib1, c_arr)
    (dh,), ((chips1,),) = grad_h("grad_h_b", nk_half, [_scatter_side([pair1])], plus=dh_a)
    gwin_half = _chip_sum("chip_sum_w_in_0", chips0[0], c_arr, piece=0, pieces=2)
    gwin_half = _chip_sum("chip_sum_w_in_1", chips1, c_arr, piece=1, pieces=2, into=gwin_half)
    gpw_half = _chip_sum("chip_sum_pool_w", chips0[1], c_arr)
    grad_x, g_norm = _norm_in_bwd(xs, dh, dx2, norm_gain)
    _, (late_grads,) = _run("reduce_share_halves", [_share_side([gwin_half, gpw_half])])

    names = ["w_in", "w_out_attn", "w_out_pool", "w_out", "pool_w"]
    grads = [late_grads[0], *early_grads, late_grads[1]]
    big = {}
    weights = [w_in, w_out_attn, w_out_pool, w_out, pool_w.reshape(pg, pg)]
    ms = [m_w_in, m_w_out_attn, m_w_out_pool, m_w_out, m_pool_w.reshape(pg, pg)]
    vs = [v_w_in, v_w_out_attn, v_w_out_pool, v_w_out, v_pool_w.reshape(pg, pg)]
    for n, g, w, m, v in zip(names, grads, weights, ms, vs):
        big[n] = [r.reshape(pool_w.shape) if n == "pool_w" else r for r in _adamw("adamw_" + n, g, w, m, v)]

    small_like = [norm_gain, final_gain, pool_scale, rel_bias, jnp.zeros((2, d), F32)]
    summed = _sum_slots("small_grads_sum", _all_to_all_small(_pack([g_norm, g_final, g_pscale, g_rel, g_gate_full])))
    g_norm_t, g_final_t, g_pscale_t, g_rel_t, g_gate_t = _unpack(summed, small_like)
    g_gate_t = lax.dynamic_slice_in_dim(g_gate_t, chip * sw, sw, axis=1)
    small_g = [g_norm_t, g_final_t, g_pscale_t, g_rel_t, g_gate_t]
    small_w = [norm_gain, final_gain, pool_scale, rel_bias, gate_bias]
    small_m = [m_norm_gain, m_final_gain, m_pool_scale, m_rel_bias, m_gate_bias]
    small_v = [v_norm_gain, v_final_gain, v_pool_scale, v_rel_bias, v_gate_bias]
    packed = _small_adamw(_pack(small_g), _pack(small_w), _pack(small_m), _pack(small_v))
    sd, sm, sv = [_unpack(t, small_w) for t in packed]
    small = {n: [small_g[i], sd[i], sm[i], sv[i]]
             for i, n in enumerate(["norm_gain", "final_gain", "pool_scale", "rel_bias", "gate_bias"])}

    every = {**big, **small}
    order = ["norm_gain", "w_in", "rel_bias", "pool_w", "pool_scale", "w_out_attn", "w_out_pool", "gate_bias",
             "w_out", "final_gain"]
    return (loss, grad_x.reshape(x.shape), *[every[n][0] for n in order], *[every[n][1] for n in order],
            *[every[n][2] for n in order], *[every[n][3] for n in order])
```

```python
import math

import jax
import jax.numpy as jnp
from jax import lax
from jax.experimental import pallas as pl
from jax.experimental.pallas import tpu as pltpu

F32 = jnp.float32
BF16 = jnp.bfloat16
MESH = pl.DeviceIdType.MESH
ANY = pl.BlockSpec(memory_space=pl.ANY)

N_CHIPS = 4
N_DEV = 8
CHUNK = 64
N_LEFT_CHUNKS = 8
PAD = N_LEFT_CHUNKS * CHUNK
HEAD_DIM = 128
MAX_REL = 128
POOL_WINDOWS = (2, 4, 8, 16)
N_GROUPS = len(POOL_WINDOWS)
HALO = 16
Q_GROUP = 4 * CHUNK
K_GROUP = Q_GROUP + PAD
NEG = -1e30
EPS = 1e-6
ADAM_LR, ADAM_B1, ADAM_B2, ADAM_EPS, ADAM_WD, ADAM_STEP = 0.001, 0.9, 0.999, 1e-08, 0.01, 10
VMEM_LIMIT = 56 * 1024 * 1024

NN = (((1,), (0,)), ((), ()))
NT = (((1,), (1,)), ((), ()))
TN = (((0,), (0,)), ((), ()))


def _div(n, pref):
    if n <= pref:
        return n
    for t in range(pref - pref % 128, 0, -128):
        if n % t == 0:
            return t
    raise ValueError((n, pref))


def _params(sem, **kw):
    return pltpu.CompilerParams(dimension_semantics=sem, vmem_limit_bytes=VMEM_LIMIT, **kw)


def _sigmoid(z):
    return jax.nn.sigmoid(z)


def _silu_and_grad(z):
    sg = _sigmoid(z)
    return z * sg, sg * (1.0 + z * (1.0 - sg))


def _matmul(name, dn, grid, a, a_spec, b, b_spec, acc_shape, outs, out_specs, extra=(), extra_specs=(),
            epilogue=None, accumulate_outs=False, sides=(), prefetch=None, carry=None):
    nk = grid[2]
    aliases = {}
    if carry is not None:
        aliases = {2 + len(extra): 0}
        extra, extra_specs = (*extra, carry), (*extra_specs, ANY)
    ne, no = len(extra), len(outs)

    def finish(res, ex, out_refs):
        if epilogue is None:
            for o in out_refs:
                o[...] = res.astype(o.dtype)
        else:
            epilogue(res, ex, out_refs)

    def body(*refs):
        a_ref, b_ref = refs[0], refs[1]
        ex = refs[2:2 + ne]
        out_refs = refs[2 + ne:2 + ne + no]
        if nk == 1:
            finish(lax.dot_general(a_ref[...], b_ref[...], dn, preferred_element_type=F32), ex, out_refs)
            return
        acc = refs[-1]
        k = pl.program_id(2)

        @pl.when(k == 0)
        def _():
            acc[...] = jnp.zeros_like(acc)

        acc[...] += lax.dot_general(a_ref[...], b_ref[...], dn, preferred_element_type=F32)

        @pl.when(k == nk - 1)
        def _():
            finish(acc[...], ex, out_refs)

    sem = ("arbitrary",) * 3 if accumulate_outs else ("parallel", "parallel", "arbitrary")
    return _run(name, list(sides), dict(
        body=body, grid=grid, in_specs=[a_spec, b_spec, *extra_specs], out_specs=list(out_specs),
        out_shape=list(outs), scratch_shapes=[] if nk == 1 else [pltpu.VMEM(acc_shape, F32)],
        operands=[a, b, *extra], sem=sem, aliases=aliases, prefetch=prefetch))


def _place():
    x, y, c = lax.axis_index("x"), lax.axis_index("y"), lax.axis_index("c")
    chips = [(1 - x, y), (x, 1 - y), (1 - x, 1 - y)]
    return x, y, c, chips


N_STREAMS = 1


class _Copies:
    def __init__(self, cps):
        self.cps = cps

    def start(self):
        for cp in self.cps:
            cp.start()

    def wait_send(self):
        for cp in self.cps:
            cp.wait_send()

    def wait_recv(self):
        for cp in self.cps:
            cp.wait_recv()

    def wait(self):
        for cp in self.cps:
            cp.wait()


def _remote(src, dst, send_sems, recv_sems, k, dev):
    lead = src.shape[0]
    n = N_STREAMS
    while n > 1 and (lead % n or (len(src.shape) == 2 and (lead // n) % 16)):
        n //= 2
    step = lead // n
    return _Copies([pltpu.make_async_remote_copy(
        src_ref=src.at[pl.ds(i * step, step)], dst_ref=dst.at[pl.ds(i * step, step)],
        send_sem=send_sems.at[k * N_STREAMS + i], recv_sem=recv_sems.at[k * N_STREAMS + i],
        device_id=dev, device_id_type=MESH) for i in range(n)])


class _Side:
    def __init__(self, ins, out_shapes, n_remote, n_local, start, finish, aliases=None):
        self.ins, self.out_shapes = list(ins), list(out_shapes)
        self.n_remote, self.n_local = max(n_remote, 1), max(n_local, 1)
        self.start, self.finish, self.aliases = start, finish, aliases or {}


def _run(name, sides, compute=None):
    cm = compute or dict(body=None, grid=(), in_specs=[], out_specs=[], out_shape=[], scratch_shapes=[], operands=[])
    grid = tuple(cm["grid"])
    ni, no, ns = len(cm["operands"]), len(cm["out_shape"]), len(cm["scratch_shapes"])
    n_in = [len(sd.ins) for sd in sides]
    n_out = [len(sd.out_shapes) for sd in sides]
    prefetch = cm.get("prefetch")
    shift = 0 if prefetch is None else 1

    def body(*refs):
        refs = refs[shift:]
        at = ni
        side_ins = []
        for n in n_in:
            side_ins.append(refs[at:at + n])
            at += n
        outs = refs[at:at + no]
        at += no
        side_outs = []
        for n in n_out:
            side_outs.append(refs[at:at + n])
            at += n
        scratch = refs[at:at + ns]
        at += ns
        sems = [refs[at + 3 * q:at + 3 * q + 3] for q in range(len(sides))]

        def each(step):
            for sd, i_, o_, m_ in zip(sides, side_ins, side_outs, sems):
                getattr(sd, step)(i_, o_, *m_)

        if not grid:
            each("start")
            each("finish")
            return
        first = last = None
        for ax, g in enumerate(grid):
            f, l = pl.program_id(ax) == 0, pl.program_id(ax) == g - 1
            first = f if first is None else first & f
            last = l if last is None else last & l
        if sides:
            pl.when(first)(lambda: each("start"))
        cm["body"](*refs[:ni], *outs, *scratch)
        if sides:
            pl.when(last)(lambda: each("finish"))

    aliases = {shift + i_: o_ for i_, o_ in (cm.get("aliases") or {}).items()}
    in_at, out_at = shift + ni, no
    for sd, a, b in zip(sides, n_in, n_out):
        for i_, o_ in sd.aliases.items():
            aliases[in_at + i_] = out_at + o_
        in_at, out_at = in_at + a, out_at + b
    scratch_shapes = list(cm["scratch_shapes"])
    for sd in sides:
        scratch_shapes += [pltpu.SemaphoreType.DMA((sd.n_remote * N_STREAMS,)),
                           pltpu.SemaphoreType.DMA((sd.n_remote * N_STREAMS,)), pltpu.SemaphoreType.DMA((sd.n_local,))]
    in_specs = list(cm["in_specs"]) + [ANY] * sum(n_in)
    out_specs = list(cm["out_specs"]) + [ANY] * sum(n_out)
    kw = dict(in_specs=in_specs, out_specs=out_specs, scratch_shapes=scratch_shapes)
    if grid:
        kw["grid"] = grid
    if prefetch is not None:
        kw = dict(grid_spec=pltpu.PrefetchScalarGridSpec(num_scalar_prefetch=1, **kw))
    if grid:
        kw["compiler_params"] = _params(("arbitrary",) * len(grid) if sides else cm["sem"])
    res = pl.pallas_call(
        body, name=name, out_shape=list(cm["out_shape"]) + [s for sd in sides for s in sd.out_shapes],
        input_output_aliases=aliases, **kw,
    )(*([] if prefetch is None else [prefetch]), *cm["operands"], *[a for sd in sides for a in sd.ins])
    res = list(res)
    side_res, at = [], no
    for n in n_out:
        side_res.append(res[at:at + n])
        at += n
    return res[:no], side_res


def _gather_side(shards, split, peers=(0, 1, 2)):
    n = len(shards)

    def plan(ins, outs, send_sems, recv_sems, _):
        x, y, c, chips = _place()
        me = 2 * x + y
        sibling = (x, y, 1 - c)
        direct, relays, arrivals = [], [], []
        for t in range(n):
            quarter = ins[t].shape[0] // 4
            for j in peers:
                cx, cy = chips[j]
                src_chip = 2 * cx + cy
                k = 12 * t + 4 * j
                if not split[t]:
                    direct.append(_remote(ins[t], outs[t].at[me], send_sems, recv_sems, k, (cx, cy, c)))
                    got = outs[t].at[src_chip]
                    arrivals.append(_remote(got, got, send_sems, recv_sems, k, (cx, cy, c)))
                    continue
                for r in range(2):
                    e = c ^ r
                    out_q = pl.ds((2 * c + e) * quarter, quarter)
                    direct.append(_remote(ins[t].at[out_q], outs[t].at[me, out_q], send_sems, recv_sems, k + r,
                                          (cx, cy, e)))
                    got = outs[t].at[src_chip, pl.ds((2 * e + c) * quarter, quarter)]
                    relays.append((_remote(got, got, send_sems, recv_sems, k + r, (cx, cy, e)),
                                   _remote(got, got, send_sems, recv_sems, k + 2 + r, sibling)))
                    theirs = outs[t].at[src_chip, pl.ds((2 * e + 1 - c) * quarter, quarter)]
                    arrivals.append(_remote(theirs, theirs, send_sems, recv_sems, k + 2 + (1 - r), sibling))
        return direct, relays, arrivals

    def start(*refs):
        for cp in plan(*refs)[0]:
            cp.start()

    def finish(*refs):
        direct, relays, arrivals = plan(*refs)
        for landed, onward in relays:
            landed.wait_recv()
            onward.start()
        for cp in arrivals:
            cp.wait_recv()
        for cp in direct + [onward for _, onward in relays]:
            cp.wait_send()

    return _Side(shards, [jax.ShapeDtypeStruct((N_CHIPS,) + s.shape, s.dtype) for s in shards], 12 * n, 0,
                 start, finish)


def _with_own(slots, block, chip):
    return lax.dynamic_update_slice(slots, block[None], (chip,) + (0,) * block.ndim)


def _relay_far_side(full):
    def plan(ins, outs, send_sems, recv_sems, __):
        x, y, c, _ = _place()
        src, far = ins[0], outs[0]
        half = far.shape[0] // 2
        quarter = half // 2
        first, second = pl.ds(c * half, quarter), pl.ds(c * half + quarter, quarter)
        x_nb, y_nb, sibling = (1 - x, y, c), (x, 1 - y, c), (x, y, 1 - c)
        x_id, y_id = 2 * (1 - x) + y, 2 * x + (1 - y)
        sends = [_remote(src.at[y_id, first], far.at[first], send_sems, recv_sems, 0, x_nb),
                 _remote(src.at[x_id, second], far.at[second], send_sems, recv_sems, 1, y_nb)]
        landed = [_remote(far.at[first], far.at[first], send_sems, recv_sems, 0, x_nb),
                  _remote(far.at[second], far.at[second], send_sems, recv_sems, 1, y_nb)]
        mine, theirs = far.at[pl.ds(c * half, half)], far.at[pl.ds((1 - c) * half, half)]
        onward = _remote(mine, mine, send_sems, recv_sems, 2, sibling)
        from_sibling = _remote(theirs, theirs, send_sems, recv_sems, 2, sibling)
        return sends, landed, onward, from_sibling

    def start(*refs):
        for cp in plan(*refs)[0]:
            cp.start()

    def finish(*refs):
        sends, landed, onward, from_sibling = plan(*refs)
        for cp in landed:
            cp.wait_recv()
        onward.start()
        from_sibling.wait_recv()
        for cp in sends + [onward]:
            cp.wait_send()

    return _Side([full], [jax.ShapeDtypeStruct(full.shape[1:], full.dtype)], 3, 0, start, finish)


def _swap_side(parts):
    n = len(parts)

    def plan(ins, outs, send_sems, recv_sems, _):
        x, y, c, _ = _place()
        cps = []
        for t in range(n):
            half = ins[t].shape[1] // 2
            cps.append(_remote(ins[t].at[:, pl.ds((1 - c) * half, half)], outs[t], send_sems, recv_sems, t,
                               (x, y, 1 - c)))
        return cps

    def start(*refs):
        for cp in plan(*refs):
            cp.start()

    def finish(*refs):
        for cp in plan(*refs):
            cp.wait()

    return _Side(parts, [jax.ShapeDtypeStruct((p.shape[0], p.shape[1] // 2, p.shape[2]), p.dtype) for p in parts],
                 n, 0, start, finish)


def _scatter_side(parts):
    n = len(parts)

    def plan(ins, outs, send_sems, recv_sems, _):
        x, y, c, chips = _place()
        me = 2 * x + y
        sends, arrivals = [], []
        for t in range(n):
            for j, (cx, cy) in enumerate(chips):
                sends.append(_remote(ins[t].at[2 * cx + cy], outs[t].at[me], send_sems, recv_sems, 3 * t + j,
                                     (cx, cy, c)))
                got = outs[t].at[2 * cx + cy]
                arrivals.append(_remote(got, got, send_sems, recv_sems, 3 * t + j, (cx, cy, c)))
        return sends, arrivals

    def start(*refs):
        for cp in plan(*refs)[0]:
            cp.start()

    def finish(*refs):
        sends, arrivals = plan(*refs)
        for cp in arrivals:
            cp.wait_recv()
        for cp in sends:
            cp.wait_send()

    return _Side(parts, [jax.ShapeDtypeStruct(p.shape, p.dtype) for p in parts], 3 * n, 0, start, finish)


def _share_side(fulls):
    n = len(fulls)

    def plan(_, outs, send_sems, recv_sems, __):
        x, y, c, _ = _place()
        cps = []
        for t in range(n):
            half = outs[t].shape[0] // 2
            mine = outs[t].at[pl.ds(c * half, half)]
            theirs = outs[t].at[pl.ds((1 - c) * half, half)]
            cps.append((_remote(mine, mine, send_sems, recv_sems, t, (x, y, 1 - c)),
                        _remote(theirs, theirs, send_sems, recv_sems, t, (x, y, 1 - c))))
        return cps

    def start(*refs):
        for cp, _ in plan(*refs):
            cp.start()

    def finish(*refs):
        for cp, rv in plan(*refs):
            rv.wait_recv()
            cp.wait_send()

    return _Side(fulls, [jax.ShapeDtypeStruct(f.shape, f.dtype) for f in fulls], n, 0, start, finish,
                 aliases={t: t for t in range(n)})


def _all_to_all_small(packed):
    def body(in_ref, out_ref, send_sems, recv_sems, local_sem):
        x, y, c, _ = _place()
        me = 4 * x + 2 * y + c
        own = pltpu.make_async_copy(in_ref, out_ref.at[me], local_sem)
        own.start()
        cps, rvs = [], []
        for k in range(1, N_DEV):
            fx, fy, fc = (k >> 2) & 1, (k >> 1) & 1, k & 1
            px, py, pc = x ^ fx, y ^ fy, c ^ fc
            cp = _remote(in_ref, out_ref.at[me], send_sems, recv_sems, k - 1, (px, py, pc))
            cp.start()
            cps.append(cp)
            got = out_ref.at[4 * px + 2 * py + pc]
            rvs.append(_remote(got, got, send_sems, recv_sems, k - 1, (px, py, pc)))
        for rv in rvs:
            rv.wait_recv()
        for cp in cps:
            cp.wait_send()
        own.wait()

    return pl.pallas_call(
        body, name="small_grads_exchange",
        in_specs=[ANY], out_specs=ANY,
        out_shape=jax.ShapeDtypeStruct((N_DEV,) + packed.shape, packed.dtype),
        scratch_shapes=[pltpu.SemaphoreType.DMA(((N_DEV - 1) * N_STREAMS,)),
                        pltpu.SemaphoreType.DMA(((N_DEV - 1) * N_STREAMS,)), pltpu.SemaphoreType.DMA],
    )(packed)


def _pair_sum(name, g, recv, c_arr):
    _, rows, cols = g.shape
    half = rows // 2
    tr, tc = _div(half, 512), _div(cols, 1024)
    nrb = half // tr

    def body(c_ref, g_ref, r_ref, o_ref):
        o_ref[...] = (g_ref[...] + r_ref[...].astype(F32)).astype(BF16)

    return pl.pallas_call(
        body, name=name,
        grid_spec=pltpu.PrefetchScalarGridSpec(
            num_scalar_prefetch=1, grid=(N_CHIPS, nrb, cols // tc),
            in_specs=[pl.BlockSpec((None, tr, tc), lambda s, i, j, c: (s, c[0] * nrb + i, j)),
                      pl.BlockSpec((None, tr, tc), lambda s, i, j, c: (s, i, j))],
            out_specs=pl.BlockSpec((None, tr, tc), lambda s, i, j, c: (s, i, j))),
        out_shape=jax.ShapeDtypeStruct(recv.shape, BF16),
        compiler_params=_params(("parallel", "parallel", "parallel")),
    )(c_arr, g, recv)


def _chip_sum(name, recv, c_arr, piece=0, pieces=1, into=None):
    _, half, cols = recv.shape
    tr, tc = _div(half, 512), _div(cols, 1024)
    nrb = half // tr

    def body(c_ref, r_ref, *rest):
        o_ref = rest[-1]
        acc = r_ref[0].astype(F32)
        for s in range(1, N_CHIPS):
            acc = acc + r_ref[s].astype(F32)
        o_ref[...] = acc

    return pl.pallas_call(
        body, name=name,
        grid_spec=pltpu.PrefetchScalarGridSpec(
            num_scalar_prefetch=1, grid=(nrb, cols // tc),
            in_specs=[pl.BlockSpec((N_CHIPS, tr, tc), lambda i, j, c: (0, i, j))] + ([] if into is None else [ANY]),
            out_specs=pl.BlockSpec((tr, tc), lambda i, j, c: ((pieces * c[0] + piece) * nrb + i, j))),
        out_shape=jax.ShapeDtypeStruct((2 * pieces * half, cols), F32),
        input_output_aliases={} if into is None else {2: 0},
        compiler_params=_params(("parallel", "parallel")),
    )(c_arr, recv, *([] if into is None else [into]))


def _adamw_math(w, g, m, v):
    m2 = ADAM_B1 * m + (1.0 - ADAM_B1) * g
    v2 = ADAM_B2 * v + (1.0 - ADAM_B2) * (g * g)
    m_hat = m2 / (1.0 - ADAM_B1 ** ADAM_STEP)
    v_hat = v2 / (1.0 - ADAM_B2 ** ADAM_STEP)
    delta = -ADAM_LR * (m_hat / (jnp.sqrt(v_hat) + ADAM_EPS) + ADAM_WD * w)
    return delta, m2, v2


def _adamw(name, g, w, m, v):
    rows, cols = g.shape
    tr, tc = _div(rows, 256), _div(cols, 1024)
    spec = pl.BlockSpec((tr, tc), lambda i, j: (i, j))

    def body(g_ref, w_ref, m_ref, v_ref, go_ref, d_ref, mo_ref, vo_ref):
        gg = g_ref[...]
        delta, m2, v2 = _adamw_math(w_ref[...], gg, m_ref[...], v_ref[...])
        go_ref[...] = gg
        d_ref[...] = delta
        mo_ref[...] = m2
        vo_ref[...] = v2

    return pl.pallas_call(
        body, name=name, grid=(rows // tr, cols // tc),
        in_specs=[spec] * 4, out_specs=[spec] * 4,
        out_shape=[jax.ShapeDtypeStruct(g.shape, F32)] * 4,
        compiler_params=_params(("parallel", "parallel")),
    )(g, w, m, v)


def _sum_slots(name, slots):
    def body(s_ref, o_ref):
        acc = s_ref[0]
        for d in range(1, N_DEV):
            acc = acc + s_ref[d]
        o_ref[...] = acc

    return pl.pallas_call(body, name=name, out_shape=jax.ShapeDtypeStruct(slots.shape[1:], F32))(slots)


def _norm_in(x, gain):
    s, d = x.shape
    tr = _div(s, 256)

    def body(x_ref, g_ref, h_ref):
        xv = x_ref[...]
        r = lax.rsqrt(jnp.mean(xv * xv, axis=-1, keepdims=True) + EPS)
        h_ref[...] = (xv * r * g_ref[...]).astype(BF16)

    return pl.pallas_call(
        body, name="norm_in", grid=(s // tr,),
        in_specs=[pl.BlockSpec((tr, d), lambda i: (i, 0)), pl.BlockSpec((1, d), lambda i: (0, 0))],
        out_specs=pl.BlockSpec((tr, d), lambda i: (i, 0)),
        out_shape=jax.ShapeDtypeStruct((s, d), BF16),
        compiler_params=_params(("parallel",)),
    )(x, gain.reshape(1, d))


def _bias_table(rel_bias):
    h = rel_bias.shape[0]
    n_rel = Q_GROUP + K_GROUP - 1
    lo = PAD - K_GROUP + 1
    left = max(0, -MAX_REL - lo)
    right = max(0, lo + n_rel - 1 - MAX_REL)
    by_rel = jnp.concatenate([jnp.broadcast_to(rel_bias[:, :1], (h, left)), rel_bias,
                              jnp.broadcast_to(rel_bias[:, -1:], (h, right))], axis=1)
    by_rel = by_rel[:, lo + MAX_REL + left:][:, :n_rel]
    rev = by_rel[:, ::-1]
    rev = jnp.concatenate([rev, jnp.zeros((h, 1), rel_bias.dtype)], axis=1)
    skew = jnp.tile(rev, (1, Q_GROUP))[:, :Q_GROUP * n_rel].reshape(h, Q_GROUP, n_rel)
    tab = skew[:, :, Q_GROUP - 1:Q_GROUP - 1 + K_GROUP]
    qi = jnp.arange(Q_GROUP)[:, None] // CHUNK
    kj = jnp.arange(K_GROUP)[None, :] // CHUNK
    in_band = (kj >= qi) & (kj <= qi + N_LEFT_CHUNKS)
    return jnp.where(in_band[None], tab, NEG)


def _attention_fwd(proj, table, s, a, sides=()):
    heads = a // HEAD_DIM
    scale = HEAD_DIM ** -0.5
    groups = s // Q_GROUP

    def body(q_ref, k_ref, v_ref, z_ref, tab_ref, o_ref, ya_ref, kp, vp):
        kp[0:PAD, :] = jnp.zeros((PAD, HEAD_DIM), BF16)
        vp[0:PAD, :] = jnp.zeros((PAD, HEAD_DIM), BF16)
        kp[PAD:, :] = k_ref[...].astype(BF16)
        vp[PAD:, :] = v_ref[...].astype(BF16)

        def group(g, carry):
            r0 = pl.multiple_of(g * Q_GROUP, Q_GROUP)
            q = q_ref[pl.ds(r0, Q_GROUP), :].astype(BF16)
            kb = kp[pl.ds(r0, K_GROUP), :]
            vb = vp[pl.ds(r0, K_GROUP), :]
            sc = lax.dot_general(q, kb, NT, preferred_element_type=F32) * scale + tab_ref[...]
            col = lax.broadcasted_iota(jnp.int32, (Q_GROUP, K_GROUP), 1)
            sc = jnp.where(col >= PAD - r0, sc, NEG)
            mx = jnp.max(sc, axis=-1, keepdims=True)
            e = jnp.exp(sc - mx)
            p = e / jnp.sum(e, axis=-1, keepdims=True)
            o = jnp.dot(p.astype(BF16), vb, preferred_element_type=F32)
            o_ref[pl.ds(r0, Q_GROUP), :] = o
            z = z_ref[pl.ds(r0, Q_GROUP), :]
            ya_ref[pl.ds(r0, Q_GROUP), :] = (o * (z * _sigmoid(z))).astype(BF16)
            return carry

        lax.fori_loop(0, groups, group, 0)

    col = lambda seg: (lambda h: (0, seg * heads + h))
    blk = lambda seg: pl.BlockSpec((s, HEAD_DIM), col(seg))
    return _run("attention_fwd", list(sides), dict(
        body=body, grid=(heads,),
        in_specs=[blk(0), blk(1), blk(2), blk(3), pl.BlockSpec((None, Q_GROUP, K_GROUP), lambda h: (h, 0, 0))],
        out_specs=[blk(0), blk(0)],
        out_shape=[jax.ShapeDtypeStruct((s, a), F32), jax.ShapeDtypeStruct((s, a), BF16)],
        scratch_shapes=[pltpu.VMEM((PAD + s, HEAD_DIM), BF16), pltpu.VMEM((PAD + s, HEAD_DIM), BF16)],
        operands=[proj, proj, proj, proj, table], sem=("parallel",)))


def _attention_bwd(proj, o, dya, table, dproj, s, a):
    heads = a // HEAD_DIM
    scale = HEAD_DIM ** -0.5
    groups = s // Q_GROUP

    def body(q_ref, k_ref, v_ref, z_ref, o_ref, dy_ref, tab_ref, _, dp_ref, dtab_ref, kp, vp, dkp, dvp):
        kp[0:PAD, :] = jnp.zeros((PAD, HEAD_DIM), BF16)
        vp[0:PAD, :] = jnp.zeros((PAD, HEAD_DIM), BF16)
        kp[PAD:, :] = k_ref[...].astype(BF16)
        vp[PAD:, :] = v_ref[...].astype(BF16)
        dkp[...] = jnp.zeros_like(dkp)
        dvp[...] = jnp.zeros_like(dvp)
        dtab_ref[...] = jnp.zeros_like(dtab_ref)

        def group(g, carry):
            r0 = pl.multiple_of(g * Q_GROUP, Q_GROUP)
            rows = pl.ds(r0, Q_GROUP)
            band = pl.ds(r0, K_GROUP)
            q = q_ref[rows, :].astype(BF16)
            kb = kp[band, :]
            vb = vp[band, :]
            sc = lax.dot_general(q, kb, NT, preferred_element_type=F32) * scale + tab_ref[...]
            col = lax.broadcasted_iota(jnp.int32, (Q_GROUP, K_GROUP), 1)
            sc = jnp.where(col >= PAD - r0, sc, NEG)
            mx = jnp.max(sc, axis=-1, keepdims=True)
            e = jnp.exp(sc - mx)
            p = e / jnp.sum(e, axis=-1, keepdims=True)
            z = z_ref[rows, :]
            dy = dy_ref[rows, :]
            si, dsi = _silu_and_grad(z)
            dp_ref[3, rows, :] = (dy * o_ref[rows, :] * dsi).astype(BF16)
            dob = (dy * si).astype(BF16)
            dp = lax.dot_general(dob, vb, NT, preferred_element_type=F32)
            ds = p * (dp - jnp.sum(p * dp, axis=-1, keepdims=True))
            dtab_ref[...] += ds
            dsb = (ds * scale).astype(BF16)
            dp_ref[0, rows, :] = jnp.dot(dsb, kb, preferred_element_type=F32).astype(BF16)
            dkp[band, :] += lax.dot_general(dsb, q, TN, preferred_element_type=F32)
            dvp[band, :] += lax.dot_general(p.astype(BF16), dob, TN, preferred_element_type=F32)
            return carry

        lax.fori_loop(0, groups, group, 0)
        dp_ref[1] = dkp[PAD:, :].astype(BF16)
        dp_ref[2] = dvp[PAD:, :].astype(BF16)

    col = lambda seg: (lambda h: (0, seg * heads + h))
    blk = lambda seg: pl.BlockSpec((s, HEAD_DIM), col(seg))
    tab_spec = pl.BlockSpec((None, Q_GROUP, K_GROUP), lambda h: (h, 0, 0))
    return pl.pallas_call(
        body, name="attention_bwd", grid=(heads,),
        in_specs=[blk(0), blk(1), blk(2), blk(3), blk(0), blk(0), tab_spec, ANY],
        out_specs=[pl.BlockSpec((4, s, HEAD_DIM), lambda h: (0, 0, h)), tab_spec],
        out_shape=[jax.ShapeDtypeStruct(dproj.shape, BF16), jax.ShapeDtypeStruct(table.shape, F32)],
        input_output_aliases={7: 0},
        scratch_shapes=[pltpu.VMEM((PAD + s, HEAD_DIM), BF16), pltpu.VMEM((PAD + s, HEAD_DIM), BF16),
                        pltpu.VMEM((PAD + s, HEAD_DIM), F32), pltpu.VMEM((PAD + s, HEAD_DIM), F32)],
        compiler_params=_params(("parallel",)),
    )(proj, proj, proj, proj, o, dya, table, dproj)


def _pick_window(gi, by_window):
    out = by_window[-1]
    for n in range(N_GROUPS - 2, -1, -1):
        out = jnp.where(gi == n, by_window[n], out)
    return out


def _inv_count(gi, first_row, rows):
    t = first_row + lax.broadcasted_iota(jnp.int32, (rows, 1), 0)
    w = jnp.left_shift(2, gi)
    return 1.0 / jnp.minimum(t + 1, w).astype(F32)


def _pool_fwd(proj, pw_full, pool_scale, s, a, p):
    pg = p // N_GROUPS
    ts = _div(s, 512)
    u0, z0 = 4 * a // pg, (4 * a + p) // pg
    hb = ts // HALO

    def body(u_ref, uh_ref, z_ref, pw_ref, ps_ref, d_ref, t_ref, y_ref, ext):
        gi, i = pl.program_id(0), pl.program_id(1)
        u = u_ref[...]
        ext[0:HALO, :] = jnp.where(i > 0, uh_ref[...], 0.0)
        ext[HALO:, :] = u
        e = ext[...]
        sums, shift = [], 1
        for _ in POOL_WINDOWS:
            e = e + pltpu.roll(e, shift, 0)
            sums.append(e)
            shift *= 2
        win = _pick_window(gi, sums)[HALO:, :]
        d = (win * _inv_count(gi, i * ts, ts) - u).astype(BF16)
        d_ref[...] = d
        t = jnp.dot(d, pw_ref[...].reshape(pg, pg), preferred_element_type=F32)
        t_ref[...] = t
        z = z_ref[...]
        y_ref[...] = (t * ps_ref[...] * (z * _sigmoid(z))).astype(BF16)

    out_spec = pl.BlockSpec((ts, pg), lambda g, i: (i, g))
    return pl.pallas_call(
        body, name="pool_fwd", grid=(N_GROUPS, s // ts),
        in_specs=[pl.BlockSpec((ts, pg), lambda g, i: (i, u0 + g)),
                  pl.BlockSpec((HALO, pg), lambda g, i: (jnp.maximum(i * hb - 1, 0), u0 + g)),
                  pl.BlockSpec((ts, pg), lambda g, i: (i, z0 + g)),
                  pl.BlockSpec((N_CHIPS, None, pg // N_CHIPS, pg), lambda g, i: (0, g, 0, 0)),
                  pl.BlockSpec((1, pg), lambda g, i: (0, g))],
        out_specs=[out_spec] * 3,
        out_shape=[jax.ShapeDtypeStruct((s, p), BF16), jax.ShapeDtypeStruct((s, p), F32),
                   jax.ShapeDtypeStruct((s, p), BF16)],
        scratch_shapes=[pltpu.VMEM((ts + HALO, pg), F32)],
        compiler_params=_params(("parallel", "parallel")),
    )(proj, proj, proj, pw_full, pool_scale.reshape(1, p))


def _pool_bwd(proj, dyp, t, d, pw_full, pool_scale, dproj, s, a, p):
    pg = p // N_GROUPS
    ts = _div(s, 512)
    nt = s // ts
    z0 = (4 * a + p) // pg
    hb = ts // HALO
    last_halo = s // HALO - 1

    def body(dy_ref, dyh_ref, z_ref, zh_ref, t_ref, th_ref, d_ref, pw_ref, ps_ref, _,
             dp_ref, dpw_ref, dps_ref, ext):
        gi, i = pl.program_id(0), pl.program_id(1)
        ps = ps_ref[...]
        pw = pw_ref[...].reshape(pg, pg)

        @pl.when(i == 0)
        def _():
            dpw_ref[...] = jnp.zeros_like(dpw_ref)
            dps_ref[...] = jnp.zeros_like(dps_ref)

        def through_gate(dy, z, tt):
            si, dsi = _silu_and_grad(z)
            return dy * si, dy * (tt * ps) * dsi

        tt = t_ref[...]
        dyl, dz = through_gate(dy_ref[...], z_ref[...], tt)
        dp_ref[1] = dz.astype(BF16)
        dps_ref[...] += jnp.sum(dyl * tt, axis=0, keepdims=True)
        dtb = (dyl * ps).astype(BF16)
        dpw_ref[...] += lax.dot_general(d_ref[...], dtb, TN, preferred_element_type=F32).reshape(dpw_ref.shape)
        dd = lax.dot_general(dtb, pw, NT, preferred_element_type=F32)
        dylh, _ = through_gate(dyh_ref[...], zh_ref[...], th_ref[...])
        ddh = lax.dot_general((dylh * ps).astype(BF16), pw, NT, preferred_element_type=F32)
        ddh = jnp.where(i < nt - 1, ddh, 0.0)
        ext[0:ts, :] = dd * _inv_count(gi, i * ts, ts)
        ext[ts:, :] = ddh * _inv_count(gi, (i + 1) * ts, HALO)
        e = ext[...]
        rows = ts + HALO
        sums, shift = [], 1
        for _ in POOL_WINDOWS:
            e = e + pltpu.roll(e, rows - shift, 0)
            sums.append(e)
            shift *= 2
        dp_ref[0] = (_pick_window(gi, sums)[:ts, :] - dd).astype(BF16)

    tile = lambda c0: pl.BlockSpec((ts, pg), lambda g, i: (i, c0 + g))
    halo = lambda c0: pl.BlockSpec((HALO, pg), lambda g, i: (jnp.minimum((i + 1) * hb, last_halo), c0 + g))
    pw_spec = pl.BlockSpec((N_CHIPS, None, pg // N_CHIPS, pg), lambda g, i: (0, g, 0, 0))
    return pl.pallas_call(
        body, name="pool_bwd", grid=(N_GROUPS, nt),
        in_specs=[tile(0), halo(0), tile(z0), halo(z0), tile(0), halo(0), tile(0), pw_spec,
                  pl.BlockSpec((1, pg), lambda g, i: (0, g)), ANY],
        out_specs=[pl.BlockSpec((2, ts, pg), lambda g, i: (2, i, g)), pw_spec,
                   pl.BlockSpec((1, pg), lambda g, i: (0, g))],
        out_shape=[jax.ShapeDtypeStruct(dproj.shape, BF16),
                   jax.ShapeDtypeStruct(pw_full.shape, F32), jax.ShapeDtypeStruct((1, p), F32)],
        input_output_aliases={9: 0},
        scratch_shapes=[pltpu.VMEM((ts + HALO, pg), F32)],
        compiler_params=_params(("parallel", "arbitrary")),
    )(dyp, dyp, proj, proj, t, t, d, pw_full, pool_scale.reshape(1, p), dproj)


def _merge_fwd(ya, yp, woa_full, wop_full, proj, gb_full, s, d, a, sides=()):
    sw = d // N_CHIPS
    tm, tn = _div(s, 512), _div(sw, 1024)
    per = sw // tn
    ga0, gp0 = (4 * a + 2 * a) // tn, (4 * a + 2 * a + d) // tn

    def body(ya_ref, yp_ref, wa_ref, wp_ref, ga_ref, gp_ref, gb_ref, a_out, b_out, m_out):
        av = jnp.dot(ya_ref[...], wa_ref[...], preferred_element_type=F32)
        bv = jnp.dot(yp_ref[...], wp_ref[...], preferred_element_type=F32)
        a_out[...] = av
        b_out[...] = bv
        sa = _sigmoid(ga_ref[...] + gb_ref[0:1, :])
        sp = _sigmoid(gp_ref[...] + gb_ref[1:2, :])
        m_out[...] = (sa * av + sp * bv).astype(BF16)

    act = pl.BlockSpec((tm, a), lambda i, j: (i, 0))
    wgt = pl.BlockSpec((None, a, tn), lambda i, j: (j // per, 0, j % per))
    out = pl.BlockSpec((tm, tn), lambda i, j: (i, j))
    return _run("merge_fwd", list(sides), dict(
        body=body, grid=(s // tm, d // tn),
        in_specs=[act, act, wgt, wgt,
                  pl.BlockSpec((tm, tn), lambda i, j: (i, ga0 + j)),
                  pl.BlockSpec((tm, tn), lambda i, j: (i, gp0 + j)),
                  pl.BlockSpec((None, 2, tn), lambda i, j: (j // per, 0, j % per))],
        out_specs=[out, out, out],
        out_shape=[jax.ShapeDtypeStruct((s, d), F32), jax.ShapeDtypeStruct((s, d), F32),
                   jax.ShapeDtypeStruct((s, d), BF16)],
        scratch_shapes=[], operands=[ya, yp, woa_full, wop_full, proj, proj, gb_full], sem=("parallel", "parallel")))


def _out_proj(mb, wo, x, s, d):
    tm, tn, tk = _div(s, 1024), _div(d, 1024), _div(d, 2048)

    def epilogue(res, ex, outs):
        outs[0][...] = res + ex[0][...]

    tile = pl.BlockSpec((tm, tn), lambda i, j, k: (i, j))
    return _matmul(
        "out_proj", NN, (s // tm, d // tn, d // tk),
        mb, pl.BlockSpec((tm, tk), lambda i, j, k: (i, k)),
        wo, pl.BlockSpec((tk, tn), lambda i, j, k: (k, j)),
        (tm, tn), [jax.ShapeDtypeStruct((s, d), F32)], [tile], extra=(x,), extra_specs=(tile,),
        epilogue=epilogue)[0][0]


def _loss_head(x2, target, final_gain):
    s, d = x2.shape
    tr = _div(s, 256)

    def body(x_ref, t_ref, g_ref, loss_ref, dx_ref, dxb_ref, dg_ref):
        @pl.when(pl.program_id(0) == 0)
        def _():
            dg_ref[...] = jnp.zeros_like(dg_ref)

        xv = x_ref[...]
        g = g_ref[...]
        r = lax.rsqrt(jnp.mean(xv * xv, axis=-1, keepdims=True) + EPS)
        xn = xv * r
        e = xn * g - t_ref[...]
        loss_ref[...] = 0.5 * jnp.mean(e * e, axis=-1, keepdims=True)
        dy = e / d
        dg_ref[...] += jnp.sum(dy * xn, axis=0, keepdims=True)
        dxn = dy * g
        dx = r * (dxn - xn * jnp.mean(dxn * xn, axis=-1, keepdims=True))
        dx_ref[...] = dx
        dxb_ref[...] = dx.astype(BF16)

    rows = pl.BlockSpec((tr, d), lambda i: (i, 0))
    vec = pl.BlockSpec((1, d), lambda i: (0, 0))
    return pl.pallas_call(
        body, name="loss_head", grid=(s // tr,),
        in_specs=[rows, rows, vec], out_specs=[pl.BlockSpec((tr, 1), lambda i: (i, 0)), rows, rows, vec],
        out_shape=[jax.ShapeDtypeStruct((s, 1), F32), jax.ShapeDtypeStruct((s, d), F32),
                   jax.ShapeDtypeStruct((s, d), BF16), jax.ShapeDtypeStruct((1, d), F32)],
        compiler_params=_params(("arbitrary",)),
    )(x2, target, final_gain.reshape(1, d))


N_SLOTS = 10


def _slot(seg):
    t = seg - 6
    return jnp.where(seg < 6, seg, 6 + 2 * (t % 2) + t // 2)


def _merge_bwd(dxb, wo, a_val, b_val, proj, gb_full, s, d, a):
    sw = d // N_CHIPS
    tm, tn, tk = _div(s, 512), _div(sw, 1024), _div(d, 2048)
    per = sw // tn
    per_slot = a // tn
    ga0, gp0 = (4 * a + 2 * a) // tn, (4 * a + 2 * a + d) // tn

    def epilogue(dm, ex, outs):
        a_ref, b_ref, ga_ref, gp_ref, gb_ref = ex
        da_ref, db_ref, dg_ref, dgb_ref = outs
        sa = _sigmoid(ga_ref[...] + gb_ref[0:1, :])
        sp = _sigmoid(gp_ref[...] + gb_ref[1:2, :])
        da_ref[...] = (dm * sa).astype(BF16)
        db_ref[...] = (dm * sp).astype(BF16)
        dga = dm * a_ref[...] * sa * (1.0 - sa)
        dgp = dm * b_ref[...] * sp * (1.0 - sp)
        dg_ref[0] = dga.astype(BF16)
        dg_ref[1] = dgp.astype(BF16)

        @pl.when(pl.program_id(1) == 0)
        def _():
            dgb_ref[...] = jnp.zeros_like(dgb_ref)

        dgb_ref[0:1, :] += jnp.sum(dga, axis=0, keepdims=True)
        dgb_ref[1:2, :] += jnp.sum(dgp, axis=0, keepdims=True)

    tile = pl.BlockSpec((tm, tn), lambda j, i, k: (i, j))
    sd = jax.ShapeDtypeStruct((s, d), BF16)
    return _matmul(
        "merge_bwd", NT, (d // tn, s // tm, d // tk),
        dxb, pl.BlockSpec((tm, tk), lambda j, i, k: (i, k)),
        wo, pl.BlockSpec((tn, tk), lambda j, i, k: (j, k)),
        (tm, tn), [sd, sd, jax.ShapeDtypeStruct((N_SLOTS, s, a), BF16), jax.ShapeDtypeStruct((2, d), F32)],
        [tile, tile, pl.BlockSpec((2, tm, tn), lambda j, i, k: (3 + j // per_slot, i, j % per_slot)),
         pl.BlockSpec((2, tn), lambda j, i, k: (0, j))],
        extra=(a_val, b_val, proj, proj, gb_full),
        extra_specs=(tile, tile, pl.BlockSpec((tm, tn), lambda j, i, k: (i, ga0 + j)),
                     pl.BlockSpec((tm, tn), lambda j, i, k: (i, gp0 + j)),
                     pl.BlockSpec((None, 2, tn), lambda j, i, k: (j // per, 0, j % per))),
        epilogue=epilogue, accumulate_outs=True)[0]


def _weight_grad(name, act, dout, shard_cols, slots=False, piece=None, sides=()):
    s, kdim = act.shape
    n = dout.shape[0] * dout.shape[2] if slots else dout.shape[1]
    row_tile = lambda i: i
    if shard_cols:
        sw = n // N_CHIPS
        tm, tn = _div(kdim, 1024), _div(math.gcd(sw, dout.shape[2]) if slots else sw, 1024)
        per = sw // tn
        if piece is not None:
            tm = kdim // (2 * piece[1])
            kdim = 2 * tm
            row_tile = lambda i: i * piece[1] + piece[0]
        shape = (N_CHIPS, kdim, sw)
        out = pl.BlockSpec((None, tm, tn), lambda i, j, k: (j // per, i, j % per))
    else:
        sh = kdim // N_CHIPS
        tm, tn = _div(sh, 1024), _div(n, 1024)
        per = sh // tm
        shape = (N_CHIPS, sh, n)
        out = pl.BlockSpec((None, tm, tn), lambda i, j, k: (i // per, i % per, j))
    tk = _div(s, 4096)
    if slots:
        per_slot = dout.shape[2] // tn
        dout_spec = pl.BlockSpec((None, tk, tn), lambda i, j, k: (_slot(j // per_slot), k, j % per_slot))
    else:
        dout_spec = pl.BlockSpec((tk, tn), lambda i, j, k: (k, j))
    return _matmul(
        name, TN, (kdim // tm, n // tn, s // tk),
        act, pl.BlockSpec((tk, tm), lambda i, j, k: (k, row_tile(i))), dout, dout_spec,
        (tm, tn), [jax.ShapeDtypeStruct(shape, F32), jax.ShapeDtypeStruct(shape, BF16)], [out, out], sides=sides)


def _norm_in_bwd(x, dh, dx2, gain):
    s, d = x.shape
    tr = _div(s, 256)

    def body(x_ref, dh_ref, dx2_ref, g_ref, gx_ref, dg_ref):
        @pl.when(pl.program_id(0) == 0)
        def _():
            dg_ref[...] = jnp.zeros_like(dg_ref)

        xv = x_ref[...]
        r = lax.rsqrt(jnp.mean(xv * xv, axis=-1, keepdims=True) + EPS)
        xn = xv * r
        dhv = dh_ref[...]
        dg_ref[...] += jnp.sum(dhv * xn, axis=0, keepdims=True)
        dxn = dhv * g_ref[...]
        gx_ref[...] = r * (dxn - xn * jnp.mean(dxn * xn, axis=-1, keepdims=True)) + dx2_ref[...]

    rows = pl.BlockSpec((tr, d), lambda i: (i, 0))
    vec = pl.BlockSpec((1, d), lambda i: (0, 0))
    return pl.pallas_call(
        body, name="norm_in_bwd", grid=(s // tr,),
        in_specs=[rows, rows, rows, vec], out_specs=[rows, vec],
        out_shape=[jax.ShapeDtypeStruct((s, d), F32), jax.ShapeDtypeStruct((1, d), F32)],
        compiler_params=_params(("arbitrary",)),
    )(x, dh, dx2, gain.reshape(1, d))


def _pack(vectors):
    flat = jnp.concatenate([v.reshape(-1).astype(F32) for v in vectors])
    rows = -(-flat.shape[0] // 1024) * 8
    return jnp.pad(flat, (0, rows * 128 - flat.shape[0])).reshape(rows, 128)


def _unpack(packed, like):
    flat, out, at = packed.reshape(-1), [], 0
    for v in like:
        out.append(flat[at:at + v.size].reshape(v.shape))
        at += v.size
    return out


def _small_adamw(g, w, m, v):
    def body(g_ref, w_ref, m_ref, v_ref, d_ref, mo_ref, vo_ref):
        delta, m2, v2 = _adamw_math(w_ref[...], g_ref[...], m_ref[...], v_ref[...])
        d_ref[...] = delta
        mo_ref[...] = m2
        vo_ref[...] = v2

    return pl.pallas_call(body, name="small_adamw", out_shape=[jax.ShapeDtypeStruct(g.shape, F32)] * 3)(g, w, m, v)


def kernel(x, norm_gain, w_in, rel_bias, pool_w, pool_scale, w_out_attn, w_out_pool, gate_bias, w_out, final_gain, loss_target, m_norm_gain, m_w_in, m_rel_bias, m_pool_w, m_pool_scale, m_w_out_attn, m_w_out_pool, m_gate_bias, m_w_out, m_final_gain, v_norm_gain, v_w_in, v_rel_bias, v_pool_w, v_pool_scale, v_w_out_attn, v_w_out_pool, v_gate_bias, v_w_out, v_final_gain):
    _, s, d = x.shape
    a = p = d // 2
    n_in = w_in.shape[1] * N_CHIPS
    sw_in = w_in.shape[1]
    pg = p // N_GROUPS
    xs = x.reshape(s, d)
    target = loss_target.reshape(s, d)
    c_arr = lax.axis_index("c").astype(jnp.int32).reshape(1)
    chip = 2 * lax.axis_index("x") + lax.axis_index("y")

    hb = _norm_in(xs, norm_gain)
    tm, tn, tk = _div(s, 1024), _div(sw_in, 1024), _div(d, 4096)
    per_in = sw_in // tn
    cx, cy = lax.axis_index("x"), lax.axis_index("y")
    order = jnp.stack([2 * cx + cy, 2 * (1 - cx) + cy, 2 * cx + (1 - cy), 2 * (1 - cx) + (1 - cy)]).astype(jnp.int32)

    def in_proj(name, first, count, weights, sides, carry=None):
        if weights.ndim == 3:
            w_spec = pl.BlockSpec((None, tk, tn), lambda i, j, k, o: (o[first + j // per_in], k, j % per_in))
        else:
            w_spec = pl.BlockSpec((tk, tn), lambda i, j, k, o: (k, j))
        return _matmul(
            name, NN, (s // tm, count * per_in, d // tk),
            hb, pl.BlockSpec((tm, tk), lambda i, j, k, o: (i, k)), weights, w_spec,
            (tm, tn), [jax.ShapeDtypeStruct((s, n_in), F32)],
            [pl.BlockSpec((tm, tn), lambda i, j, k, o: (i, o[first + j // per_in] * per_in + j % per_in))],
            sides=sides, prefetch=order, carry=carry)

    w_in_b = w_in.astype(BF16)
    (proj,), ((win_near,),) = in_proj("in_proj_own", 0, 1, w_in_b, [_gather_side([w_in_b], [True], peers=(0, 1))])
    (proj,), ((win_far,),) = in_proj("in_proj_near", 1, 2, win_near, [_relay_far_side(win_near)], carry=proj)
    pw_b, woa_b, wop_b, wo_b = (w.astype(BF16) for w in (pool_w, w_out_attn, w_out_pool, w_out))
    (proj,), ((pw_full, gb_full),) = in_proj(
        "in_proj_far", 3, 1, win_far, [_gather_side([pw_b, gate_bias], [False, False])], carry=proj)
    win_full = _with_own(_with_own(win_near, win_far, order[3]), w_in_b, chip)
    pw_full, gb_full = _with_own(pw_full, pw_b, chip), _with_own(gb_full, gate_bias, chip)
    table = _bias_table(rel_bias)
    (o_attn, ya), ((woa_full, wop_full),) = _attention_fwd(
        proj, table, s, a, [_gather_side([woa_b, wop_b], [True, True])])
    woa_full, wop_full = _with_own(woa_full, woa_b, chip), _with_own(wop_full, wop_b, chip)
    d_pool, t_pool, yp = _pool_fwd(proj, pw_full, pool_scale, s, a, p)
    (a_val, b_val, mb), ((wo_full,),) = _merge_fwd(ya, yp, woa_full, wop_full, proj, gb_full, s, d, a,
                                                  [_gather_side([wo_b], [True])])
    wo_mat = _with_own(wo_full, wo_b, chip).reshape(d, d)
    loss_rows, dx2, dx2b, g_final = _loss_head(_out_proj(mb, wo_mat, xs, s, d), target, final_gain)
    loss = lax.psum(jnp.sum(loss_rows), ("x", "y", "c"))

    da, db, dproj, g_gate_full = _merge_bwd(dx2b, wo_mat, a_val, b_val, proj, gb_full, s, d, a)
    (gwo, gwo_b), _ = _weight_grad("grad_w_out", mb, dx2b, shard_cols=False)
    (gwoa, gwoa_b), _ = _weight_grad("grad_w_out_attn", ya, da, shard_cols=True)
    (gwop, gwop_b), _ = _weight_grad("grad_w_out_pool", yp, db, shard_cols=True)
    early = ["w_out_attn", "w_out_pool", "w_out"]

    sw = d // N_CHIPS
    tm, tn, tk = _div(s, 1024), _div(a, 1024), _div(sw, 1024)
    per_o = sw // tk

    def back_through(name, dout, w_full, sides=()):
        return _matmul(
            name, NT, (s // tm, a // tn, d // tk),
            dout, pl.BlockSpec((tm, tk), lambda i, j, k: (i, k)),
            w_full, pl.BlockSpec((None, tn, tk), lambda i, j, k: (k // per_o, j, k % per_o)),
            (tm, tn), [jax.ShapeDtypeStruct((s, a), F32)], [pl.BlockSpec((tm, tn), lambda i, j, k: (i, j))],
            sides=sides)

    (dya,), (early_sib,) = back_through("grad_y_attn", da, woa_full, [_swap_side([gwoa_b, gwop_b, gwo_b])])
    early_pair = [_pair_sum("pair_sum_" + n, g, r, c_arr) for n, g, r in zip(early, [gwoa, gwop, gwo], early_sib)]
    (dyp,), _ = back_through("grad_y_pool", db, wop_full)
    dproj, gpw, g_pscale = _pool_bwd(proj, dyp, t_pool, d_pool, pw_full, pool_scale, dproj, s, a, p)
    dproj, dtable = _attention_bwd(proj, o_attn, dya, table, dproj, s, a)
    g_rel = jax.vjp(_bias_table, rel_bias)[1](dtable)[0]
    gpw3 = gpw.reshape(N_CHIPS, pg, pg)

    (gw0, gw0_b), (early_chips,) = _weight_grad("grad_w_in_0", hb, dproj, shard_cols=True, slots=True, piece=(0, 2),
                                                sides=[_scatter_side(early_pair)])
    (gw1, gw1_b), (sib0,) = _weight_grad("grad_w_in_1", hb, dproj, shard_cols=True, slots=True, piece=(1, 2),
                                         sides=[_swap_side([gw0_b, gpw3.astype(BF16)])])
    def with_own_sum(from_chips, pair):
        return _with_own(from_chips, lax.dynamic_index_in_dim(pair, chip, 0, keepdims=False), chip)

    early_halves = [_chip_sum("chip_sum_" + n, with_own_sum(r, q), c_arr)
                    for n, r, q in zip(early, early_chips, early_pair)]
    pair0 = [_pair_sum("pair_sum_w_in_0", gw0, sib0[0], c_arr), _pair_sum("pair_sum_pool_w", gpw3, sib0[1], c_arr)]

    tm, tn, tk = _div(s, 1024), _div(d, 1024), _div(a // 2, 1024)
    per_k, per_slot = sw_in // tk, a // tk
    nk_half = n_in // tk // 2

    def grad_h(name, k0, sides, plus=None):
        def add(res, ex, outs):
            outs[0][...] = res + ex[0][...]

        tile = pl.BlockSpec((tm, tn), lambda i, j, k: (i, j))
        return _matmul(
            name, NT, (s // tm, d // tn, nk_half),
            dproj, pl.BlockSpec((None, tm, tk), lambda i, j, k: (_slot((k + k0) // per_slot), i, (k + k0) % per_slot)),
            win_full, pl.BlockSpec((None, tn, tk), lambda i, j, k: ((k + k0) // per_k, j, (k + k0) % per_k)),
            (tm, tn), [jax.ShapeDtypeStruct((s, d), F32)], [tile],
            extra=() if plus is None else (plus,), extra_specs=() if plus is None else (tile,),
            epilogue=None if plus is None else add, sides=sides)

    (dh_a,), (chips0, (sib1,), early_grads) = grad_h(
        "grad_h_a", 0, [_scatter_side(pair0), _swap_side([gw1_b]), _share_side(early_halves)])
    pair1 = _pair_sum("pair_sum_w_in_1", gw1, sib1, c_arr)
    (dh,), ((chips1,),) = grad_h("grad_h_b", nk_half, [_scatter_side([pair1])], plus=dh_a)
    gwin_half = _chip_sum("chip_sum_w_in_0", with_own_sum(chips0[0], pair0[0]), c_arr, piece=0, pieces=2)
    gwin_half = _chip_sum("chip_sum_w_in_1", with_own_sum(chips1, pair1), c_arr, piece=1, pieces=2, into=gwin_half)
    gpw_half = _chip_sum("chip_sum_pool_w", with_own_sum(chips0[1], pair0[1]), c_arr)
    grad_x, g_norm = _norm_in_bwd(xs, dh, dx2, norm_gain)
    _, (late_grads,) = _run("reduce_share_halves", [_share_side([gwin_half, gpw_half])])

    names = ["w_in", "w_out_attn", "w_out_pool", "w_out", "pool_w"]
    grads = [late_grads[0], *early_grads, late_grads[1]]
    big = {}
    weights = [w_in, w_out_attn, w_out_pool, w_out, pool_w.reshape(pg, pg)]
    ms = [m_w_in, m_w_out_attn, m_w_out_pool, m_w_out, m_pool_w.reshape(pg, pg)]
    vs = [v_w_in, v_w_out_attn, v_w_out_pool, v_w_out, v_pool_w.reshape(pg, pg)]
    for n, g, w, m, v in zip(names, grads, weights, ms, vs):
        big[n] = [r.reshape(pool_w.shape) if n == "pool_w" else r for r in _adamw("adamw_" + n, g, w, m, v)]

    small_like = [norm_gain, final_gain, pool_scale, rel_bias, jnp.zeros((2, d), F32)]
    summed = _sum_slots("small_grads_sum", _all_to_all_small(_pack([g_norm, g_final, g_pscale, g_rel, g_gate_full])))
    g_norm_t, g_final_t, g_pscale_t, g_rel_t, g_gate_t = _unpack(summed, small_like)
    g_gate_t = lax.dynamic_slice_in_dim(g_gate_t, chip * sw, sw, axis=1)
    small_g = [g_norm_t, g_final_t, g_pscale_t, g_rel_t, g_gate_t]
    small_w = [norm_gain, final_gain, pool_scale, rel_bias, gate_bias]
    small_m = [m_norm_gain, m_final_gain, m_pool_scale, m_rel_bias, m_gate_bias]
    small_v = [v_norm_gain, v_final_gain, v_pool_scale, v_rel_bias, v_gate_bias]
    packed = _small_adamw(_pack(small_g), _pack(small_w), _pack(small_m), _pack(small_v))
    sd, sm, sv = [_unpack(t, small_w) for t in packed]
    small = {n: [small_g[i], sd[i], sm[i], sv[i]]
             for i, n in enumerate(["norm_gain", "final_gain", "pool_scale", "rel_bias", "gate_bias"])}

    every = {**big, **small}
    order = ["norm_gain", "w_in", "rel_bias", "pool_w", "pool_scale", "w_out_attn", "w_out_pool", "gate_bias",
             "w_out", "final_gain"]
    return (loss, grad_x.reshape(x.shape), *[every[n][0] for n in order], *[every[n][1] for n in order],
            *[every[n][2] for n in order], *[every[n][3] for n in order])
```

```python
import math

import jax
import jax.numpy as jnp
from jax import lax
from jax.experimental import pallas as pl
from jax.experimental.pallas import tpu as pltpu

F32 = jnp.float32
BF16 = jnp.bfloat16
MESH = pl.DeviceIdType.MESH
ANY = pl.BlockSpec(memory_space=pl.ANY)

N_CHIPS = 4
N_DEV = 8
CHUNK = 64
N_LEFT_CHUNKS = 8
PAD = N_LEFT_CHUNKS * CHUNK
HEAD_DIM = 128
MAX_REL = 128
POOL_WINDOWS = (2, 4, 8, 16)
N_GROUPS = len(POOL_WINDOWS)
HALO = 16
Q_GROUP = 4 * CHUNK
K_GROUP = Q_GROUP + PAD
NEG = -1e30
EPS = 1e-6
ADAM_LR, ADAM_B1, ADAM_B2, ADAM_EPS, ADAM_WD, ADAM_STEP = 0.001, 0.9, 0.999, 1e-08, 0.01, 10
VMEM_LIMIT = 56 * 1024 * 1024

NN = (((1,), (0,)), ((), ()))
NT = (((1,), (1,)), ((), ()))
TN = (((0,), (0,)), ((), ()))


def _div(n, pref):
    if n <= pref:
        return n
    for t in range(pref - pref % 128, 0, -128):
        if n % t == 0:
            return t
    raise ValueError((n, pref))


def _params(sem, **kw):
    return pltpu.CompilerParams(dimension_semantics=sem, vmem_limit_bytes=VMEM_LIMIT, **kw)


def _sigmoid(z):
    return jax.nn.sigmoid(z)


def _silu_and_grad(z):
    sg = _sigmoid(z)
    return z * sg, sg * (1.0 + z * (1.0 - sg))


def _matmul(name, dn, grid, a, a_spec, b, b_spec, acc_shape, outs, out_specs, extra=(), extra_specs=(),
            epilogue=None, accumulate_outs=False, sides=(), prefetch=None, carry=None):
    nk = grid[2]
    aliases = {}
    if carry is not None:
        aliases = {2 + len(extra): 0}
        extra, extra_specs = (*extra, carry), (*extra_specs, ANY)
    ne, no = len(extra), len(outs)

    def finish(res, ex, out_refs):
        if epilogue is None:
            for o in out_refs:
                o[...] = res.astype(o.dtype)
        else:
            epilogue(res, ex, out_refs)

    def body(*refs):
        a_ref, b_ref = refs[0], refs[1]
        ex = refs[2:2 + ne]
        out_refs = refs[2 + ne:2 + ne + no]
        if nk == 1:
            finish(lax.dot_general(a_ref[...], b_ref[...], dn, preferred_element_type=F32), ex, out_refs)
            return
        acc = refs[-1]
        k = pl.program_id(2)

        @pl.when(k == 0)
        def _():
            acc[...] = jnp.zeros_like(acc)

        acc[...] += lax.dot_general(a_ref[...], b_ref[...], dn, preferred_element_type=F32)

        @pl.when(k == nk - 1)
        def _():
            finish(acc[...], ex, out_refs)

    sem = ("arbitrary",) * 3 if accumulate_outs else ("parallel", "parallel", "arbitrary")
    return _run(name, list(sides), dict(
        body=body, grid=grid, in_specs=[a_spec, b_spec, *extra_specs], out_specs=list(out_specs),
        out_shape=list(outs), scratch_shapes=[] if nk == 1 else [pltpu.VMEM(acc_shape, F32)],
        operands=[a, b, *extra], sem=sem, aliases=aliases, prefetch=prefetch))


def _place():
    x, y, c = lax.axis_index("x"), lax.axis_index("y"), lax.axis_index("c")
    chips = [(1 - x, y), (x, 1 - y), (1 - x, 1 - y)]
    return x, y, c, chips


N_STREAMS = 1


class _Copies:
    def __init__(self, cps):
        self.cps = cps

    def start(self):
        for cp in self.cps:
            cp.start()

    def wait_send(self):
        for cp in self.cps:
            cp.wait_send()

    def wait_recv(self):
        for cp in self.cps:
            cp.wait_recv()

    def wait(self):
        for cp in self.cps:
            cp.wait()


def _remote(src, dst, send_sems, recv_sems, k, dev):
    lead = src.shape[0]
    n = N_STREAMS
    while n > 1 and (lead % n or (len(src.shape) == 2 and (lead // n) % 16)):
        n //= 2
    step = lead // n
    return _Copies([pltpu.make_async_remote_copy(
        src_ref=src.at[pl.ds(i * step, step)], dst_ref=dst.at[pl.ds(i * step, step)],
        send_sem=send_sems.at[k * N_STREAMS + i], recv_sem=recv_sems.at[k * N_STREAMS + i],
        device_id=dev, device_id_type=MESH) for i in range(n)])


class _Side:
    def __init__(self, ins, out_shapes, n_remote, n_local, start, finish, aliases=None):
        self.ins, self.out_shapes = list(ins), list(out_shapes)
        self.n_remote, self.n_local = max(n_remote, 1), max(n_local, 1)
        self.start, self.finish, self.aliases = start, finish, aliases or {}


def _run(name, sides, compute=None):
    cm = compute or dict(body=None, grid=(), in_specs=[], out_specs=[], out_shape=[], scratch_shapes=[], operands=[])
    grid = tuple(cm["grid"])
    ni, no, ns = len(cm["operands"]), len(cm["out_shape"]), len(cm["scratch_shapes"])
    n_in = [len(sd.ins) for sd in sides]
    n_out = [len(sd.out_shapes) for sd in sides]
    prefetch = cm.get("prefetch")
    shift = 0 if prefetch is None else 1

    def body(*refs):
        refs = refs[shift:]
        at = ni
        side_ins = []
        for n in n_in:
            side_ins.append(refs[at:at + n])
            at += n
        outs = refs[at:at + no]
        at += no
        side_outs = []
        for n in n_out:
            side_outs.append(refs[at:at + n])
            at += n
        scratch = refs[at:at + ns]
        at += ns
        sems = [refs[at + 3 * q:at + 3 * q + 3] for q in range(len(sides))]

        def each(step):
            for sd, i_, o_, m_ in zip(sides, side_ins, side_outs, sems):
                getattr(sd, step)(i_, o_, *m_)

        if not grid:
            each("start")
            each("finish")
            return
        first = last = None
        for ax, g in enumerate(grid):
            f, l = pl.program_id(ax) == 0, pl.program_id(ax) == g - 1
            first = f if first is None else first & f
            last = l if last is None else last & l
        if sides:
            pl.when(first)(lambda: each("start"))
        cm["body"](*refs[:ni], *outs, *scratch)
        if sides:
            pl.when(last)(lambda: each("finish"))

    aliases = {shift + i_: o_ for i_, o_ in (cm.get("aliases") or {}).items()}
    in_at, out_at = shift + ni, no
    for sd, a, b in zip(sides, n_in, n_out):
        for i_, o_ in sd.aliases.items():
            aliases[in_at + i_] = out_at + o_
        in_at, out_at = in_at + a, out_at + b
    scratch_shapes = list(cm["scratch_shapes"])
    for sd in sides:
        scratch_shapes += [pltpu.SemaphoreType.DMA((sd.n_remote * N_STREAMS,)),
                           pltpu.SemaphoreType.DMA((sd.n_remote * N_STREAMS,)), pltpu.SemaphoreType.DMA((sd.n_local,))]
    in_specs = list(cm["in_specs"]) + [ANY] * sum(n_in)
    out_specs = list(cm["out_specs"]) + [ANY] * sum(n_out)
    kw = dict(in_specs=in_specs, out_specs=out_specs, scratch_shapes=scratch_shapes)
    if grid:
        kw["grid"] = grid
    if prefetch is not None:
        kw = dict(grid_spec=pltpu.PrefetchScalarGridSpec(num_scalar_prefetch=1, **kw))
    if grid:
        kw["compiler_params"] = _params(("arbitrary",) * len(grid) if sides else cm["sem"])
    res = pl.pallas_call(
        body, name=name, out_shape=list(cm["out_shape"]) + [s for sd in sides for s in sd.out_shapes],
        input_output_aliases=aliases, **kw,
    )(*([] if prefetch is None else [prefetch]), *cm["operands"], *[a for sd in sides for a in sd.ins])
    res = list(res)
    side_res, at = [], no
    for n in n_out:
        side_res.append(res[at:at + n])
        at += n
    return res[:no], side_res


def _gather_side(shards, split, peers=(0, 1, 2)):
    n = len(shards)

    def plan(ins, outs, send_sems, recv_sems, _):
        x, y, c, chips = _place()
        me = 2 * x + y
        sibling = (x, y, 1 - c)
        direct, relays, arrivals = [], [], []
        for t in range(n):
            quarter = ins[t].shape[0] // 4
            for j in peers:
                cx, cy = chips[j]
                src_chip = 2 * cx + cy
                k = 12 * t + 4 * j
                if not split[t]:
                    direct.append(_remote(ins[t], outs[t].at[me], send_sems, recv_sems, k, (cx, cy, c)))
                    got = outs[t].at[src_chip]
                    arrivals.append(_remote(got, got, send_sems, recv_sems, k, (cx, cy, c)))
                    continue
                for r in range(2):
                    e = c ^ r
                    out_q = pl.ds((2 * c + e) * quarter, quarter)
                    direct.append(_remote(ins[t].at[out_q], outs[t].at[me, out_q], send_sems, recv_sems, k + r,
                                          (cx, cy, e)))
                    got = outs[t].at[src_chip, pl.ds((2 * e + c) * quarter, quarter)]
                    relays.append((_remote(got, got, send_sems, recv_sems, k + r, (cx, cy, e)),
                                   _remote(got, got, send_sems, recv_sems, k + 2 + r, sibling)))
                    theirs = outs[t].at[src_chip, pl.ds((2 * e + 1 - c) * quarter, quarter)]
                    arrivals.append(_remote(theirs, theirs, send_sems, recv_sems, k + 2 + (1 - r), sibling))
        return direct, relays, arrivals

    def start(*refs):
        for cp in plan(*refs)[0]:
            cp.start()

    def finish(*refs):
        direct, relays, arrivals = plan(*refs)
        for landed, onward in relays:
            landed.wait_recv()
            onward.start()
        for cp in arrivals:
            cp.wait_recv()
        for cp in direct + [onward for _, onward in relays]:
            cp.wait_send()

    return _Side(shards, [jax.ShapeDtypeStruct((N_CHIPS,) + s.shape, s.dtype) for s in shards], 12 * n, 0,
                 start, finish)


def _with_own(slots, block, chip):
    return lax.dynamic_update_slice(slots, block[None], (chip,) + (0,) * block.ndim)


def _relay_far_side(full):
    def plan(ins, outs, send_sems, recv_sems, __):
        x, y, c, _ = _place()
        src, far = ins[0], outs[0]
        half = far.shape[0] // 2
        quarter = half // 2
        first, second = pl.ds(c * half, quarter), pl.ds(c * half + quarter, quarter)
        x_nb, y_nb, sibling = (1 - x, y, c), (x, 1 - y, c), (x, y, 1 - c)
        x_id, y_id = 2 * (1 - x) + y, 2 * x + (1 - y)
        sends = [_remote(src.at[y_id, first], far.at[first], send_sems, recv_sems, 0, x_nb),
                 _remote(src.at[x_id, second], far.at[second], send_sems, recv_sems, 1, y_nb)]
        landed = [_remote(far.at[first], far.at[first], send_sems, recv_sems, 0, x_nb),
                  _remote(far.at[second], far.at[second], send_sems, recv_sems, 1, y_nb)]
        mine, theirs = far.at[pl.ds(c * half, half)], far.at[pl.ds((1 - c) * half, half)]
        onward = _remote(mine, mine, send_sems, recv_sems, 2, sibling)
        from_sibling = _remote(theirs, theirs, send_sems, recv_sems, 2, sibling)
        return sends, landed, onward, from_sibling

    def start(*refs):
        for cp in plan(*refs)[0]:
            cp.start()

    def finish(*refs):
        sends, landed, onward, from_sibling = plan(*refs)
        for cp in landed:
            cp.wait_recv()
        onward.start()
        from_sibling.wait_recv()
        for cp in sends + [onward]:
            cp.wait_send()

    return _Side([full], [jax.ShapeDtypeStruct(full.shape[1:], full.dtype)], 3, 0, start, finish)


def _swap_side(parts):
    n = len(parts)

    def plan(ins, outs, send_sems, recv_sems, _):
        x, y, c, _ = _place()
        cps = []
        for t in range(n):
            half = ins[t].shape[1] // 2
            cps.append(_remote(ins[t].at[:, pl.ds((1 - c) * half, half)], outs[t], send_sems, recv_sems, t,
                               (x, y, 1 - c)))
        return cps

    def start(*refs):
        for cp in plan(*refs):
            cp.start()

    def finish(*refs):
        for cp in plan(*refs):
            cp.wait()

    return _Side(parts, [jax.ShapeDtypeStruct((p.shape[0], p.shape[1] // 2, p.shape[2]), p.dtype) for p in parts],
                 n, 0, start, finish)


def _scatter_side(parts):
    n = len(parts)

    def plan(ins, outs, send_sems, recv_sems, _):
        x, y, c, chips = _place()
        me = 2 * x + y
        sends, arrivals = [], []
        for t in range(n):
            for j, (cx, cy) in enumerate(chips):
                sends.append(_remote(ins[t].at[2 * cx + cy], outs[t].at[me], send_sems, recv_sems, 3 * t + j,
                                     (cx, cy, c)))
                got = outs[t].at[2 * cx + cy]
                arrivals.append(_remote(got, got, send_sems, recv_sems, 3 * t + j, (cx, cy, c)))
        return sends, arrivals

    def start(*refs):
        for cp in plan(*refs)[0]:
            cp.start()

    def finish(*refs):
        sends, arrivals = plan(*refs)
        for cp in arrivals:
            cp.wait_recv()
        for cp in sends:
            cp.wait_send()

    return _Side(parts, [jax.ShapeDtypeStruct(p.shape, p.dtype) for p in parts], 3 * n, 0, start, finish)


def _share_side(fulls):
    n = len(fulls)

    def plan(_, outs, send_sems, recv_sems, __):
        x, y, c, _ = _place()
        cps = []
        for t in range(n):
            half = outs[t].shape[0] // 2
            mine = outs[t].at[pl.ds(c * half, half)]
            theirs = outs[t].at[pl.ds((1 - c) * half, half)]
            cps.append((_remote(mine, mine, send_sems, recv_sems, t, (x, y, 1 - c)),
                        _remote(theirs, theirs, send_sems, recv_sems, t, (x, y, 1 - c))))
        return cps

    def start(*refs):
        for cp, _ in plan(*refs):
            cp.start()

    def finish(*refs):
        for cp, rv in plan(*refs):
            rv.wait_recv()
            cp.wait_send()

    return _Side(fulls, [jax.ShapeDtypeStruct(f.shape, f.dtype) for f in fulls], n, 0, start, finish,
                 aliases={t: t for t in range(n)})


def _all_to_all_small(packed):
    def body(in_ref, out_ref, send_sems, recv_sems, local_sem):
        x, y, c, _ = _place()
        me = 4 * x + 2 * y + c
        own = pltpu.make_async_copy(in_ref, out_ref.at[me], local_sem)
        own.start()
        cps, rvs = [], []
        for k in range(1, N_DEV):
            fx, fy, fc = (k >> 2) & 1, (k >> 1) & 1, k & 1
            px, py, pc = x ^ fx, y ^ fy, c ^ fc
            cp = _remote(in_ref, out_ref.at[me], send_sems, recv_sems, k - 1, (px, py, pc))
            cp.start()
            cps.append(cp)
            got = out_ref.at[4 * px + 2 * py + pc]
            rvs.append(_remote(got, got, send_sems, recv_sems, k - 1, (px, py, pc)))
        for rv in rvs:
            rv.wait_recv()
        for cp in cps:
            cp.wait_send()
        own.wait()

    return pl.pallas_call(
        body, name="small_grads_exchange",
        in_specs=[ANY], out_specs=ANY,
        out_shape=jax.ShapeDtypeStruct((N_DEV,) + packed.shape, packed.dtype),
        scratch_shapes=[pltpu.SemaphoreType.DMA(((N_DEV - 1) * N_STREAMS,)),
                        pltpu.SemaphoreType.DMA(((N_DEV - 1) * N_STREAMS,)), pltpu.SemaphoreType.DMA],
    )(packed)


def _pair_sum(name, g, recv, c_arr):
    _, rows, cols = g.shape
    half = rows // 2
    tr, tc = _div(half, 512), _div(cols, 1024)
    nrb = half // tr

    def body(c_ref, g_ref, r_ref, o_ref):
        o_ref[...] = (g_ref[...] + r_ref[...].astype(F32)).astype(BF16)

    return pl.pallas_call(
        body, name=name,
        grid_spec=pltpu.PrefetchScalarGridSpec(
            num_scalar_prefetch=1, grid=(N_CHIPS, nrb, cols // tc),
            in_specs=[pl.BlockSpec((None, tr, tc), lambda s, i, j, c: (s, c[0] * nrb + i, j)),
                      pl.BlockSpec((None, tr, tc), lambda s, i, j, c: (s, i, j))],
            out_specs=pl.BlockSpec((None, tr, tc), lambda s, i, j, c: (s, i, j))),
        out_shape=jax.ShapeDtypeStruct(recv.shape, BF16),
        compiler_params=_params(("parallel", "parallel", "parallel")),
    )(c_arr, g, recv)


def _chip_sum(name, recv, c_arr, piece=0, pieces=1, into=None):
    _, half, cols = recv.shape
    tr, tc = _div(half, 512), _div(cols, 1024)
    nrb = half // tr

    def body(c_ref, r_ref, *rest):
        o_ref = rest[-1]
        acc = r_ref[0].astype(F32)
        for s in range(1, N_CHIPS):
            acc = acc + r_ref[s].astype(F32)
        o_ref[...] = acc

    return pl.pallas_call(
        body, name=name,
        grid_spec=pltpu.PrefetchScalarGridSpec(
            num_scalar_prefetch=1, grid=(nrb, cols // tc),
            in_specs=[pl.BlockSpec((N_CHIPS, tr, tc), lambda i, j, c: (0, i, j))] + ([] if into is None else [ANY]),
            out_specs=pl.BlockSpec((tr, tc), lambda i, j, c: ((pieces * c[0] + piece) * nrb + i, j))),
        out_shape=jax.ShapeDtypeStruct((2 * pieces * half, cols), F32),
        input_output_aliases={} if into is None else {2: 0},
        compiler_params=_params(("parallel", "parallel")),
    )(c_arr, recv, *([] if into is None else [into]))


def _adamw_math(w, g, m, v):
    m2 = ADAM_B1 * m + (1.0 - ADAM_B1) * g
    v2 = ADAM_B2 * v + (1.0 - ADAM_B2) * (g * g)
    m_hat = m2 / (1.0 - ADAM_B1 ** ADAM_STEP)
    v_hat = v2 / (1.0 - ADAM_B2 ** ADAM_STEP)
    delta = -ADAM_LR * (m_hat / (jnp.sqrt(v_hat) + ADAM_EPS) + ADAM_WD * w)
    return delta, m2, v2


def _adamw(name, g, w, m, v):
    rows, cols = g.shape
    tr, tc = _div(rows, 256), _div(cols, 1024)
    spec = pl.BlockSpec((tr, tc), lambda i, j: (i, j))

    def body(g_ref, w_ref, m_ref, v_ref, go_ref, d_ref, mo_ref, vo_ref):
        gg = g_ref[...]
        delta, m2, v2 = _adamw_math(w_ref[...], gg, m_ref[...], v_ref[...])
        go_ref[...] = gg
        d_ref[...] = delta
        mo_ref[...] = m2
        vo_ref[...] = v2

    return pl.pallas_call(
        body, name=name, grid=(rows // tr, cols // tc),
        in_specs=[spec] * 4, out_specs=[spec] * 4,
        out_shape=[jax.ShapeDtypeStruct(g.shape, F32)] * 4,
        compiler_params=_params(("parallel", "parallel")),
    )(g, w, m, v)


def _sum_slots(name, slots):
    def body(s_ref, o_ref):
        acc = s_ref[0]
        for d in range(1, N_DEV):
            acc = acc + s_ref[d]
        o_ref[...] = acc

    return pl.pallas_call(body, name=name, out_shape=jax.ShapeDtypeStruct(slots.shape[1:], F32))(slots)


def _norm_in(x, gain):
    s, d = x.shape
    tr = _div(s, 256)

    def body(x_ref, g_ref, h_ref):
        xv = x_ref[...]
        r = lax.rsqrt(jnp.mean(xv * xv, axis=-1, keepdims=True) + EPS)
        h_ref[...] = (xv * r * g_ref[...]).astype(BF16)

    return pl.pallas_call(
        body, name="norm_in", grid=(s // tr,),
        in_specs=[pl.BlockSpec((tr, d), lambda i: (i, 0)), pl.BlockSpec((1, d), lambda i: (0, 0))],
        out_specs=pl.BlockSpec((tr, d), lambda i: (i, 0)),
        out_shape=jax.ShapeDtypeStruct((s, d), BF16),
        compiler_params=_params(("parallel",)),
    )(x, gain.reshape(1, d))


def _bias_table(rel_bias):
    h = rel_bias.shape[0]
    span = 2 * CHUNK - 1
    lo, hi = -(CHUNK - 1), (N_LEFT_CHUNKS + 1) * CHUNK - 1
    by_rel = jnp.concatenate([rel_bias[:, MAX_REL + lo:],
                              jnp.broadcast_to(rel_bias[:, -1:], (h, hi - MAX_REL))], axis=1)
    vec = jnp.stack([by_rel[:, m * CHUNK:m * CHUNK + span] for m in range(N_LEFT_CHUNKS + 1)], axis=1)
    rev = jnp.concatenate([vec[..., ::-1], jnp.zeros(vec.shape[:2] + (1,), vec.dtype)], axis=-1)
    skew = jnp.tile(rev, (1, 1, CHUNK))[..., :CHUNK * span].reshape(h, N_LEFT_CHUNKS + 1, CHUNK, span)
    blocks = skew[..., CHUNK - 1:]
    off = jnp.full((h, CHUNK, CHUNK), NEG, rel_bias.dtype)
    rows = []
    for qi in range(Q_GROUP // CHUNK):
        dist = [N_LEFT_CHUNKS + qi - kj for kj in range(K_GROUP // CHUNK)]
        rows.append(jnp.concatenate([blocks[:, m] if 0 <= m <= N_LEFT_CHUNKS else off for m in dist], axis=-1))
    return jnp.concatenate(rows, axis=-2)


def _softmax(scores, tab_ref, r0, scale):
    sc = scores * scale + tab_ref[...]
    col = lax.broadcasted_iota(jnp.int32, sc.shape, 1)
    sc = jnp.where(col >= PAD - r0, sc, NEG)
    e = jnp.exp(sc - jnp.max(sc, axis=-1, keepdims=True))
    return e * (1.0 / jnp.sum(e, axis=-1, keepdims=True))


def _attention_fwd(proj, tables, s, a, sides=()):
    heads = a // HEAD_DIM
    scale = HEAD_DIM ** -0.5
    groups = s // Q_GROUP

    def body(q_ref, k_ref, v_ref, z_ref, tab_ref, o_ref, ya_ref, kp, vp):
        kp[0:PAD, :] = jnp.zeros((PAD, HEAD_DIM), BF16)
        vp[0:PAD, :] = jnp.zeros((PAD, HEAD_DIM), BF16)
        kp[PAD:, :] = k_ref[...].astype(BF16)
        vp[PAD:, :] = v_ref[...].astype(BF16)

        def group(g, carry):
            r0 = pl.multiple_of(g * Q_GROUP, Q_GROUP)
            q = q_ref[pl.ds(r0, Q_GROUP), :].astype(BF16)
            p = _softmax(lax.dot_general(q, kp[pl.ds(r0, K_GROUP), :], NT, preferred_element_type=F32),
                         tab_ref, r0, scale)
            o = jnp.dot(p.astype(BF16), vp[pl.ds(r0, K_GROUP), :], preferred_element_type=F32)
            o_ref[pl.ds(r0, Q_GROUP), :] = o
            z = z_ref[pl.ds(r0, Q_GROUP), :]
            ya_ref[pl.ds(r0, Q_GROUP), :] = (o * (z * _sigmoid(z))).astype(BF16)
            return carry

        lax.fori_loop(0, groups, group, 0)

    col = lambda seg: (lambda h: (0, seg * heads + h))
    blk = lambda seg: pl.BlockSpec((s, HEAD_DIM), col(seg))
    return _run("attention_fwd", list(sides), dict(
        body=body, grid=(heads,),
        in_specs=[blk(0), blk(1), blk(2), blk(3),
                  pl.BlockSpec((None, Q_GROUP, K_GROUP), lambda h: (h, 0, 0))],
        out_specs=[blk(0), blk(0)],
        out_shape=[jax.ShapeDtypeStruct((s, a), F32), jax.ShapeDtypeStruct((s, a), BF16)],
        scratch_shapes=[pltpu.VMEM((PAD + s, HEAD_DIM), BF16), pltpu.VMEM((PAD + s, HEAD_DIM), BF16)],
        operands=[proj, proj, proj, proj, tables], sem=("parallel",)))


def _attention_bwd(proj, o, dya, tables, dproj, s, a):
    heads = a // HEAD_DIM
    scale = HEAD_DIM ** -0.5
    groups = s // Q_GROUP

    def body(q_ref, k_ref, v_ref, z_ref, o_ref, dy_ref, tab_ref, _, dp_ref, dtab_ref, kp, vp, dkp, dvp):
        kp[0:PAD, :] = jnp.zeros((PAD, HEAD_DIM), BF16)
        vp[0:PAD, :] = jnp.zeros((PAD, HEAD_DIM), BF16)
        kp[PAD:, :] = k_ref[...].astype(BF16)
        vp[PAD:, :] = v_ref[...].astype(BF16)
        dkp[...] = jnp.zeros_like(dkp)
        dvp[...] = jnp.zeros_like(dvp)
        dtab_ref[...] = jnp.zeros_like(dtab_ref)

        def group(g, carry):
            r0 = pl.multiple_of(g * Q_GROUP, Q_GROUP)
            rows = pl.ds(r0, Q_GROUP)
            band = pl.ds(r0, K_GROUP)
            q = q_ref[rows, :].astype(BF16)
            z = z_ref[rows, :]
            dy = dy_ref[rows, :]
            si, dsi = _silu_and_grad(z)
            dp_ref[3, rows, :] = (dy * o_ref[rows, :] * dsi).astype(BF16)
            dob = (dy * si).astype(BF16)
            p = _softmax(lax.dot_general(q, kp[band, :], NT, preferred_element_type=F32), tab_ref, r0, scale)
            dp = lax.dot_general(dob, vp[band, :], NT, preferred_element_type=F32)
            ds = p * (dp - jnp.sum(p * dp, axis=-1, keepdims=True))
            dtab_ref[...] += ds
            dsb = (ds * scale).astype(BF16)
            dp_ref[0, rows, :] = jnp.dot(dsb, kp[band, :], preferred_element_type=F32).astype(BF16)
            dkp[band, :] += lax.dot_general(dsb, q, TN, preferred_element_type=F32)
            dvp[band, :] += lax.dot_general(p.astype(BF16), dob, TN, preferred_element_type=F32)
            return carry

        lax.fori_loop(0, groups, group, 0)
        dp_ref[1] = dkp[PAD:, :].astype(BF16)
        dp_ref[2] = dvp[PAD:, :].astype(BF16)

    col = lambda seg: (lambda h: (0, seg * heads + h))
    blk = lambda seg: pl.BlockSpec((s, HEAD_DIM), col(seg))
    return pl.pallas_call(
        body, name="attention_bwd", grid=(heads,),
        in_specs=[blk(0), blk(1), blk(2), blk(3), blk(0), blk(0),
                  pl.BlockSpec((None, Q_GROUP, K_GROUP), lambda h: (h, 0, 0)), ANY],
        out_specs=[pl.BlockSpec((4, s, HEAD_DIM), lambda h: (0, 0, h)),
                   pl.BlockSpec((None, Q_GROUP, K_GROUP), lambda h: (h, 0, 0))],
        out_shape=[jax.ShapeDtypeStruct(dproj.shape, BF16), jax.ShapeDtypeStruct(tables.shape, F32)],
        input_output_aliases={7: 0},
        scratch_shapes=[pltpu.VMEM((PAD + s, HEAD_DIM), BF16), pltpu.VMEM((PAD + s, HEAD_DIM), BF16),
                        pltpu.VMEM((PAD + s, HEAD_DIM), F32), pltpu.VMEM((PAD + s, HEAD_DIM), F32)],
        compiler_params=_params(("parallel",)),
    )(proj, proj, proj, proj, o, dya, tables, dproj)


def _pick_window(gi, by_window):
    out = by_window[-1]
    for n in range(N_GROUPS - 2, -1, -1):
        out = jnp.where(gi == n, by_window[n], out)
    return out


def _inv_count(gi, first_row, rows):
    t = first_row + lax.broadcasted_iota(jnp.int32, (rows, 1), 0)
    w = jnp.left_shift(2, gi)
    return 1.0 / jnp.minimum(t + 1, w).astype(F32)


def _pool_fwd(proj, pw_full, pool_scale, s, a, p):
    pg = p // N_GROUPS
    ts = _div(s, 512)
    u0, z0 = 4 * a // pg, (4 * a + p) // pg
    hb = ts // HALO

    def body(u_ref, uh_ref, z_ref, pw_ref, ps_ref, d_ref, t_ref, y_ref, ext):
        gi, i = pl.program_id(0), pl.program_id(1)
        u = u_ref[...]
        ext[0:HALO, :] = jnp.where(i > 0, uh_ref[...], 0.0)
        ext[HALO:, :] = u
        e = ext[...]
        sums, shift = [], 1
        for _ in POOL_WINDOWS:
            e = e + pltpu.roll(e, shift, 0)
            sums.append(e)
            shift *= 2
        win = _pick_window(gi, sums)[HALO:, :]
        d = (win * _inv_count(gi, i * ts, ts) - u).astype(BF16)
        d_ref[...] = d
        t = jnp.dot(d, pw_ref[...].reshape(pg, pg), preferred_element_type=F32)
        t_ref[...] = t
        z = z_ref[...]
        y_ref[...] = (t * ps_ref[...] * (z * _sigmoid(z))).astype(BF16)

    out_spec = pl.BlockSpec((ts, pg), lambda g, i: (i, g))
    return pl.pallas_call(
        body, name="pool_fwd", grid=(N_GROUPS, s // ts),
        in_specs=[pl.BlockSpec((ts, pg), lambda g, i: (i, u0 + g)),
                  pl.BlockSpec((HALO, pg), lambda g, i: (jnp.maximum(i * hb - 1, 0), u0 + g)),
                  pl.BlockSpec((ts, pg), lambda g, i: (i, z0 + g)),
                  pl.BlockSpec((N_CHIPS, None, pg // N_CHIPS, pg), lambda g, i: (0, g, 0, 0)),
                  pl.BlockSpec((1, pg), lambda g, i: (0, g))],
        out_specs=[out_spec] * 3,
        out_shape=[jax.ShapeDtypeStruct((s, p), BF16), jax.ShapeDtypeStruct((s, p), F32),
                   jax.ShapeDtypeStruct((s, p), BF16)],
        scratch_shapes=[pltpu.VMEM((ts + HALO, pg), F32)],
        compiler_params=_params(("parallel", "parallel")),
    )(proj, proj, proj, pw_full, pool_scale.reshape(1, p))


def _pool_bwd(proj, dyp, t, d, pw_full, pool_scale, dproj, s, a, p):
    pg = p // N_GROUPS
    ts = _div(s, 512)
    nt = s // ts
    z0 = (4 * a + p) // pg
    hb = ts // HALO
    last_halo = s // HALO - 1

    def body(dy_ref, dyh_ref, z_ref, zh_ref, t_ref, th_ref, d_ref, pw_ref, ps_ref, _,
             dp_ref, dpw_ref, dps_ref, ext):
        gi, i = pl.program_id(0), pl.program_id(1)
        ps = ps_ref[...]
        pw = pw_ref[...].reshape(pg, pg)

        @pl.when(i == 0)
        def _():
            dpw_ref[...] = jnp.zeros_like(dpw_ref)
            dps_ref[...] = jnp.zeros_like(dps_ref)

        def through_gate(dy, z, tt):
            si, dsi = _silu_and_grad(z)
            return dy * si, dy * (tt * ps) * dsi

        tt = t_ref[...]
        dyl, dz = through_gate(dy_ref[...], z_ref[...], tt)
        dp_ref[1] = dz.astype(BF16)
        dps_ref[...] += jnp.sum(dyl * tt, axis=0, keepdims=True)
        dtb = (dyl * ps).astype(BF16)
        dpw_ref[...] += lax.dot_general(d_ref[...], dtb, TN, preferred_element_type=F32).reshape(dpw_ref.shape)
        dd = lax.dot_general(dtb, pw, NT, preferred_element_type=F32)
        dylh, _ = through_gate(dyh_ref[...], zh_ref[...], th_ref[...])
        ddh = lax.dot_general((dylh * ps).astype(BF16), pw, NT, preferred_element_type=F32)
        ddh = jnp.where(i < nt - 1, ddh, 0.0)
        ext[0:ts, :] = dd * _inv_count(gi, i * ts, ts)
        ext[ts:, :] = ddh * _inv_count(gi, (i + 1) * ts, HALO)
        e = ext[...]
        rows = ts + HALO
        sums, shift = [], 1
        for _ in POOL_WINDOWS:
            e = e + pltpu.roll(e, rows - shift, 0)
            sums.append(e)
            shift *= 2
        dp_ref[0] = (_pick_window(gi, sums)[:ts, :] - dd).astype(BF16)

    tile = lambda c0: pl.BlockSpec((ts, pg), lambda g, i: (i, c0 + g))
    halo = lambda c0: pl.BlockSpec((HALO, pg), lambda g, i: (jnp.minimum((i + 1) * hb, last_halo), c0 + g))
    pw_spec = pl.BlockSpec((N_CHIPS, None, pg // N_CHIPS, pg), lambda g, i: (0, g, 0, 0))
    return pl.pallas_call(
        body, name="pool_bwd", grid=(N_GROUPS, nt),
        in_specs=[tile(0), halo(0), tile(z0), halo(z0), tile(0), halo(0), tile(0), pw_spec,
                  pl.BlockSpec((1, pg), lambda g, i: (0, g)), ANY],
        out_specs=[pl.BlockSpec((2, ts, pg), lambda g, i: (2, i, g)), pw_spec,
                   pl.BlockSpec((1, pg), lambda g, i: (0, g))],
        out_shape=[jax.ShapeDtypeStruct(dproj.shape, BF16),
                   jax.ShapeDtypeStruct(pw_full.shape, F32), jax.ShapeDtypeStruct((1, p), F32)],
        input_output_aliases={9: 0},
        scratch_shapes=[pltpu.VMEM((ts + HALO, pg), F32)],
        compiler_params=_params(("parallel", "arbitrary")),
    )(dyp, dyp, proj, proj, t, t, d, pw_full, pool_scale.reshape(1, p), dproj)


def _merge_fwd(ya, yp, woa_full, wop_full, proj, gb_full, s, d, a, sides=()):
    sw = d // N_CHIPS
    tm, tn = _div(s, 512), _div(sw, 1024)
    per = sw // tn
    ga0, gp0 = (4 * a + 2 * a) // tn, (4 * a + 2 * a + d) // tn

    def body(ya_ref, yp_ref, wa_ref, wp_ref, ga_ref, gp_ref, gb_ref, a_out, b_out, m_out):
        av = jnp.dot(ya_ref[...], wa_ref[...], preferred_element_type=F32)
        bv = jnp.dot(yp_ref[...], wp_ref[...], preferred_element_type=F32)
        a_out[...] = av
        b_out[...] = bv
        sa = _sigmoid(ga_ref[...] + gb_ref[0:1, :])
        sp = _sigmoid(gp_ref[...] + gb_ref[1:2, :])
        m_out[...] = (sa * av + sp * bv).astype(BF16)

    act = pl.BlockSpec((tm, a), lambda i, j: (i, 0))
    wgt = pl.BlockSpec((None, a, tn), lambda i, j: (j // per, 0, j % per))
    out = pl.BlockSpec((tm, tn), lambda i, j: (i, j))
    return _run("merge_fwd", list(sides), dict(
        body=body, grid=(s // tm, d // tn),
        in_specs=[act, act, wgt, wgt,
                  pl.BlockSpec((tm, tn), lambda i, j: (i, ga0 + j)),
                  pl.BlockSpec((tm, tn), lambda i, j: (i, gp0 + j)),
                  pl.BlockSpec((None, 2, tn), lambda i, j: (j // per, 0, j % per))],
        out_specs=[out, out, out],
        out_shape=[jax.ShapeDtypeStruct((s, d), F32), jax.ShapeDtypeStruct((s, d), F32),
                   jax.ShapeDtypeStruct((s, d), BF16)],
        scratch_shapes=[], operands=[ya, yp, woa_full, wop_full, proj, proj, gb_full], sem=("parallel", "parallel")))


def _out_proj(mb, wo, x, s, d):
    tm, tn, tk = _div(s, 1024), _div(d, 1024), _div(d, 2048)

    def epilogue(res, ex, outs):
        outs[0][...] = res + ex[0][...]

    tile = pl.BlockSpec((tm, tn), lambda i, j, k: (i, j))
    return _matmul(
        "out_proj", NN, (s // tm, d // tn, d // tk),
        mb, pl.BlockSpec((tm, tk), lambda i, j, k: (i, k)),
        wo, pl.BlockSpec((tk, tn), lambda i, j, k: (k, j)),
        (tm, tn), [jax.ShapeDtypeStruct((s, d), F32)], [tile], extra=(x,), extra_specs=(tile,),
        epilogue=epilogue)[0][0]


def _loss_head(x2, target, final_gain):
    s, d = x2.shape
    tr = _div(s, 256)

    def body(x_ref, t_ref, g_ref, loss_ref, dx_ref, dxb_ref, dg_ref):
        @pl.when(pl.program_id(0) == 0)
        def _():
            dg_ref[...] = jnp.zeros_like(dg_ref)

        xv = x_ref[...]
        g = g_ref[...]
        r = lax.rsqrt(jnp.mean(xv * xv, axis=-1, keepdims=True) + EPS)
        xn = xv * r
        e = xn * g - t_ref[...]
        loss_ref[...] = 0.5 * jnp.mean(e * e, axis=-1, keepdims=True)
        dy = e / d
        dg_ref[...] += jnp.sum(dy * xn, axis=0, keepdims=True)
        dxn = dy * g
        dx = r * (dxn - xn * jnp.mean(dxn * xn, axis=-1, keepdims=True))
        dx_ref[...] = dx
        dxb_ref[...] = dx.astype(BF16)

    rows = pl.BlockSpec((tr, d), lambda i: (i, 0))
    vec = pl.BlockSpec((1, d), lambda i: (0, 0))
    return pl.pallas_call(
        body, name="loss_head", grid=(s // tr,),
        in_specs=[rows, rows, vec], out_specs=[pl.BlockSpec((tr, 1), lambda i: (i, 0)), rows, rows, vec],
        out_shape=[jax.ShapeDtypeStruct((s, 1), F32), jax.ShapeDtypeStruct((s, d), F32),
                   jax.ShapeDtypeStruct((s, d), BF16), jax.ShapeDtypeStruct((1, d), F32)],
        compiler_params=_params(("arbitrary",)),
    )(x2, target, final_gain.reshape(1, d))


N_SLOTS = 10


def _slot(seg):
    t = seg - 6
    return jnp.where(seg < 6, seg, 6 + 2 * (t % 2) + t // 2)


def _merge_bwd(dxb, wo, a_val, b_val, proj, gb_full, s, d, a):
    sw = d // N_CHIPS
    tm, tn, tk = _div(s, 512), _div(sw, 1024), _div(d, 2048)
    per = sw // tn
    per_slot = a // tn
    ga0, gp0 = (4 * a + 2 * a) // tn, (4 * a + 2 * a + d) // tn

    def epilogue(dm, ex, outs):
        a_ref, b_ref, ga_ref, gp_ref, gb_ref = ex
        da_ref, db_ref, dg_ref, dgb_ref = outs
        sa = _sigmoid(ga_ref[...] + gb_ref[0:1, :])
        sp = _sigmoid(gp_ref[...] + gb_ref[1:2, :])
        da_ref[...] = (dm * sa).astype(BF16)
        db_ref[...] = (dm * sp).astype(BF16)
        dga = dm * a_ref[...] * sa * (1.0 - sa)
        dgp = dm * b_ref[...] * sp * (1.0 - sp)
        dg_ref[0] = dga.astype(BF16)
        dg_ref[1] = dgp.astype(BF16)

        @pl.when(pl.program_id(1) == 0)
        def _():
            dgb_ref[...] = jnp.zeros_like(dgb_ref)

        dgb_ref[0:1, :] += jnp.sum(dga, axis=0, keepdims=True)
        dgb_ref[1:2, :] += jnp.sum(dgp, axis=0, keepdims=True)

    tile = pl.BlockSpec((tm, tn), lambda j, i, k: (i, j))
    sd = jax.ShapeDtypeStruct((s, d), BF16)
    return _matmul(
        "merge_bwd", NT, (d // tn, s // tm, d // tk),
        dxb, pl.BlockSpec((tm, tk), lambda j, i, k: (i, k)),
        wo, pl.BlockSpec((tn, tk), lambda j, i, k: (j, k)),
        (tm, tn), [sd, sd, jax.ShapeDtypeStruct((N_SLOTS, s, a), BF16), jax.ShapeDtypeStruct((2, d), F32)],
        [tile, tile, pl.BlockSpec((2, tm, tn), lambda j, i, k: (3 + j // per_slot, i, j % per_slot)),
         pl.BlockSpec((2, tn), lambda j, i, k: (0, j))],
        extra=(a_val, b_val, proj, proj, gb_full),
        extra_specs=(tile, tile, pl.BlockSpec((tm, tn), lambda j, i, k: (i, ga0 + j)),
                     pl.BlockSpec((tm, tn), lambda j, i, k: (i, gp0 + j)),
                     pl.BlockSpec((None, 2, tn), lambda j, i, k: (j // per, 0, j % per))),
        epilogue=epilogue, accumulate_outs=True)[0]


def _weight_grad(name, act, dout, shard_cols, slots=False, piece=None, sides=()):
    s, kdim = act.shape
    n = dout.shape[0] * dout.shape[2] if slots else dout.shape[1]
    row_tile = lambda i: i
    if shard_cols:
        sw = n // N_CHIPS
        tm, tn = _div(kdim, 1024), _div(math.gcd(sw, dout.shape[2]) if slots else sw, 1024)
        per = sw // tn
        if piece is not None:
            tm = kdim // (2 * piece[1])
            kdim = 2 * tm
            row_tile = lambda i: i * piece[1] + piece[0]
        shape = (N_CHIPS, kdim, sw)
        out = pl.BlockSpec((None, tm, tn), lambda i, j, k: (j // per, i, j % per))
    else:
        sh = kdim // N_CHIPS
        tm, tn = _div(sh, 1024), _div(n, 1024)
        per = sh // tm
        shape = (N_CHIPS, sh, n)
        out = pl.BlockSpec((None, tm, tn), lambda i, j, k: (i // per, i % per, j))
    tk = _div(s, 4096)
    if slots:
        per_slot = dout.shape[2] // tn
        dout_spec = pl.BlockSpec((None, tk, tn), lambda i, j, k: (_slot(j // per_slot), k, j % per_slot))
    else:
        dout_spec = pl.BlockSpec((tk, tn), lambda i, j, k: (k, j))
    return _matmul(
        name, TN, (kdim // tm, n // tn, s // tk),
        act, pl.BlockSpec((tk, tm), lambda i, j, k: (k, row_tile(i))), dout, dout_spec,
        (tm, tn), [jax.ShapeDtypeStruct(shape, F32), jax.ShapeDtypeStruct(shape, BF16)], [out, out], sides=sides)


def _norm_in_bwd(x, dh, dx2, gain):
    s, d = x.shape
    tr = _div(s, 256)

    def body(x_ref, dh_ref, dx2_ref, g_ref, gx_ref, dg_ref):
        @pl.when(pl.program_id(0) == 0)
        def _():
            dg_ref[...] = jnp.zeros_like(dg_ref)

        xv = x_ref[...]
        r = lax.rsqrt(jnp.mean(xv * xv, axis=-1, keepdims=True) + EPS)
        xn = xv * r
        dhv = dh_ref[...]
        dg_ref[...] += jnp.sum(dhv * xn, axis=0, keepdims=True)
        dxn = dhv * g_ref[...]
        gx_ref[...] = r * (dxn - xn * jnp.mean(dxn * xn, axis=-1, keepdims=True)) + dx2_ref[...]

    rows = pl.BlockSpec((tr, d), lambda i: (i, 0))
    vec = pl.BlockSpec((1, d), lambda i: (0, 0))
    return pl.pallas_call(
        body, name="norm_in_bwd", grid=(s // tr,),
        in_specs=[rows, rows, rows, vec], out_specs=[rows, vec],
        out_shape=[jax.ShapeDtypeStruct((s, d), F32), jax.ShapeDtypeStruct((1, d), F32)],
        compiler_params=_params(("arbitrary",)),
    )(x, dh, dx2, gain.reshape(1, d))


def _pack(vectors):
    flat = jnp.concatenate([v.reshape(-1).astype(F32) for v in vectors])
    rows = -(-flat.shape[0] // 1024) * 8
    return jnp.pad(flat, (0, rows * 128 - flat.shape[0])).reshape(rows, 128)


def _unpack(packed, like):
    flat, out, at = packed.reshape(-1), [], 0
    for v in like:
        out.append(flat[at:at + v.size].reshape(v.shape))
        at += v.size
    return out


def _small_adamw(g, w, m, v):
    def body(g_ref, w_ref, m_ref, v_ref, d_ref, mo_ref, vo_ref):
        delta, m2, v2 = _adamw_math(w_ref[...], g_ref[...], m_ref[...], v_ref[...])
        d_ref[...] = delta
        mo_ref[...] = m2
        vo_ref[...] = v2

    return pl.pallas_call(body, name="small_adamw", out_shape=[jax.ShapeDtypeStruct(g.shape, F32)] * 3)(g, w, m, v)


def kernel(x, norm_gain, w_in, rel_bias, pool_w, pool_scale, w_out_attn, w_out_pool, gate_bias, w_out, final_gain, loss_target, m_norm_gain, m_w_in, m_rel_bias, m_pool_w, m_pool_scale, m_w_out_attn, m_w_out_pool, m_gate_bias, m_w_out, m_final_gain, v_norm_gain, v_w_in, v_rel_bias, v_pool_w, v_pool_scale, v_w_out_attn, v_w_out_pool, v_gate_bias, v_w_out, v_final_gain):
    _, s, d = x.shape
    a = p = d // 2
    n_in = w_in.shape[1] * N_CHIPS
    sw_in = w_in.shape[1]
    pg = p // N_GROUPS
    xs = x.reshape(s, d)
    target = loss_target.reshape(s, d)
    c_arr = lax.axis_index("c").astype(jnp.int32).reshape(1)
    chip = 2 * lax.axis_index("x") + lax.axis_index("y")

    hb = _norm_in(xs, norm_gain)
    tm, tn, tk = _div(s, 1024), _div(sw_in, 1024), _div(d, 4096)
    per_in = sw_in // tn
    cx, cy = lax.axis_index("x"), lax.axis_index("y")
    order = jnp.stack([2 * cx + cy, 2 * (1 - cx) + cy, 2 * cx + (1 - cy), 2 * (1 - cx) + (1 - cy)]).astype(jnp.int32)

    def in_proj(name, first, count, weights, sides, carry=None):
        if weights.ndim == 3:
            w_spec = pl.BlockSpec((None, tk, tn), lambda i, j, k, o: (o[first + j // per_in], k, j % per_in))
        else:
            w_spec = pl.BlockSpec((tk, tn), lambda i, j, k, o: (k, j))
        return _matmul(
            name, NN, (s // tm, count * per_in, d // tk),
            hb, pl.BlockSpec((tm, tk), lambda i, j, k, o: (i, k)), weights, w_spec,
            (tm, tn), [jax.ShapeDtypeStruct((s, n_in), F32)],
            [pl.BlockSpec((tm, tn), lambda i, j, k, o: (i, o[first + j // per_in] * per_in + j % per_in))],
            sides=sides, prefetch=order, carry=carry)

    w_in_b = w_in.astype(BF16)
    (proj,), ((win_near,),) = in_proj("in_proj_own", 0, 1, w_in_b, [_gather_side([w_in_b], [True], peers=(0, 1))])
    (proj,), ((win_far,),) = in_proj("in_proj_near", 1, 2, win_near, [_relay_far_side(win_near)], carry=proj)
    pw_b, woa_b, wop_b, wo_b = (w.astype(BF16) for w in (pool_w, w_out_attn, w_out_pool, w_out))
    (proj,), ((pw_full, gb_full),) = in_proj(
        "in_proj_far", 3, 1, win_far, [_gather_side([pw_b, gate_bias], [False, False])], carry=proj)
    win_full = _with_own(_with_own(win_near, win_far, order[3]), w_in_b, chip)
    pw_full, gb_full = _with_own(pw_full, pw_b, chip), _with_own(gb_full, gate_bias, chip)
    table = _bias_table(rel_bias)
    (o_attn, ya), ((woa_full, wop_full),) = _attention_fwd(
        proj, table, s, a, [_gather_side([woa_b, wop_b], [True, True])])
    woa_full, wop_full = _with_own(woa_full, woa_b, chip), _with_own(wop_full, wop_b, chip)
    d_pool, t_pool, yp = _pool_fwd(proj, pw_full, pool_scale, s, a, p)
    (a_val, b_val, mb), ((wo_full,),) = _merge_fwd(ya, yp, woa_full, wop_full, proj, gb_full, s, d, a,
                                                  [_gather_side([wo_b], [True])])
    wo_mat = _with_own(wo_full, wo_b, chip).reshape(d, d)
    loss_rows, dx2, dx2b, g_final = _loss_head(_out_proj(mb, wo_mat, xs, s, d), target, final_gain)
    loss = lax.psum(jnp.sum(loss_rows), ("x", "y", "c"))

    da, db, dproj, g_gate_full = _merge_bwd(dx2b, wo_mat, a_val, b_val, proj, gb_full, s, d, a)
    (gwo, gwo_b), _ = _weight_grad("grad_w_out", mb, dx2b, shard_cols=False)
    (gwoa, gwoa_b), _ = _weight_grad("grad_w_out_attn", ya, da, shard_cols=True)
    (gwop, gwop_b), _ = _weight_grad("grad_w_out_pool", yp, db, shard_cols=True)
    early = ["w_out_attn", "w_out_pool", "w_out"]

    sw = d // N_CHIPS
    tm, tn, tk = _div(s, 1024), _div(a, 1024), _div(sw, 1024)
    per_o = sw // tk

    def back_through(name, dout, w_full, sides=()):
        return _matmul(
            name, NT, (s // tm, a // tn, d // tk),
            dout, pl.BlockSpec((tm, tk), lambda i, j, k: (i, k)),
            w_full, pl.BlockSpec((None, tn, tk), lambda i, j, k: (k // per_o, j, k % per_o)),
            (tm, tn), [jax.ShapeDtypeStruct((s, a), F32)], [pl.BlockSpec((tm, tn), lambda i, j, k: (i, j))],
            sides=sides)

    (dya,), (early_sib,) = back_through("grad_y_attn", da, woa_full, [_swap_side([gwoa_b, gwop_b, gwo_b])])
    early_pair = [_pair_sum("pair_sum_" + n, g, r, c_arr) for n, g, r in zip(early, [gwoa, gwop, gwo], early_sib)]
    (dyp,), _ = back_through("grad_y_pool", db, wop_full)
    dproj, gpw, g_pscale = _pool_bwd(proj, dyp, t_pool, d_pool, pw_full, pool_scale, dproj, s, a, p)
    dproj, dtable = _attention_bwd(proj, o_attn, dya, table, dproj, s, a)
    g_rel = jax.vjp(_bias_table, rel_bias)[1](dtable)[0]
    gpw3 = gpw.reshape(N_CHIPS, pg, pg)

    (gw0, gw0_b), (early_chips,) = _weight_grad("grad_w_in_0", hb, dproj, shard_cols=True, slots=True, piece=(0, 2),
                                                sides=[_scatter_side(early_pair)])
    (gw1, gw1_b), (sib0,) = _weight_grad("grad_w_in_1", hb, dproj, shard_cols=True, slots=True, piece=(1, 2),
                                         sides=[_swap_side([gw0_b, gpw3.astype(BF16)])])
    def with_own_sum(from_chips, pair):
        return _with_own(from_chips, lax.dynamic_index_in_dim(pair, chip, 0, keepdims=False), chip)

    early_halves = [_chip_sum("chip_sum_" + n, with_own_sum(r, q), c_arr)
                    for n, r, q in zip(early, early_chips, early_pair)]
    pair0 = [_pair_sum("pair_sum_w_in_0", gw0, sib0[0], c_arr), _pair_sum("pair_sum_pool_w", gpw3, sib0[1], c_arr)]

    tm, tn, tk = _div(s, 1024), _div(d, 1024), _div(a // 2, 1024)
    per_k, per_slot = sw_in // tk, a // tk
    nk_half = n_in // tk // 2
    steps = nk_half // 2

    def grad_h(name, k0, sides, plus=None):
        def body(a0, a1, b0, b1, *rest):
            o_ref, acc = rest[-2], rest[-1]
            k = pl.program_id(2)

            @pl.when(k == 0)
            def _():
                acc[...] = jnp.zeros_like(acc) if plus is None else rest[0][...]

            acc[...] += (lax.dot_general(a0[...], b0[...], NT, preferred_element_type=F32)
                         + lax.dot_general(a1[...], b1[...], NT, preferred_element_type=F32))

            @pl.when(k == steps - 1)
            def _():
                o_ref[...] = acc[...]

        def act(which):
            return pl.BlockSpec((None, tm, tk), lambda i, j, k: (
                _slot((2 * k + which + k0) // per_slot), i, (2 * k + which + k0) % per_slot))

        def wgt(which):
            return pl.BlockSpec((None, tn, tk), lambda i, j, k: (
                (2 * k + which + k0) // per_k, j, (2 * k + which + k0) % per_k))

        tile = pl.BlockSpec((tm, tn), lambda i, j, k: (i, j))
        return _run(name, list(sides), dict(
            body=body, grid=(s // tm, d // tn, steps),
            in_specs=[act(0), act(1), wgt(0), wgt(1)] + ([] if plus is None else [tile]), out_specs=[tile],
            out_shape=[jax.ShapeDtypeStruct((s, d), F32)], scratch_shapes=[pltpu.VMEM((tm, tn), F32)],
            operands=[dproj, dproj, win_full, win_full] + ([] if plus is None else [plus]),
            sem=("parallel", "parallel", "arbitrary")))

    (dh_a,), (chips0, (sib1,), early_grads) = grad_h(
        "grad_h_a", 0, [_scatter_side(pair0), _swap_side([gw1_b]), _share_side(early_halves)])
    pair1 = _pair_sum("pair_sum_w_in_1", gw1, sib1, c_arr)
    (dh,), ((chips1,),) = grad_h("grad_h_b", nk_half, [_scatter_side([pair1])], plus=dh_a)
    gwin_half = _chip_sum("chip_sum_w_in_0", with_own_sum(chips0[0], pair0[0]), c_arr, piece=0, pieces=2)
    gwin_half = _chip_sum("chip_sum_w_in_1", with_own_sum(chips1, pair1), c_arr, piece=1, pieces=2, into=gwin_half)
    gpw_half = _chip_sum("chip_sum_pool_w", with_own_sum(chips0[1], pair0[1]), c_arr)
    grad_x, g_norm = _norm_in_bwd(xs, dh, dx2, norm_gain)
    _, (late_grads,) = _run("reduce_share_halves", [_share_side([gwin_half, gpw_half])])

    names = ["w_in", "w_out_attn", "w_out_pool", "w_out", "pool_w"]
    grads = [late_grads[0], *early_grads, late_grads[1]]
    big = {}
    weights = [w_in, w_out_attn, w_out_pool, w_out, pool_w.reshape(pg, pg)]
    ms = [m_w_in, m_w_out_attn, m_w_out_pool, m_w_out, m_pool_w.reshape(pg, pg)]
    vs = [v_w_in, v_w_out_attn, v_w_out_pool, v_w_out, v_pool_w.reshape(pg, pg)]
    for n, g, w, m, v in zip(names, grads, weights, ms, vs):
        big[n] = [r.reshape(pool_w.shape) if n == "pool_w" else r for r in _adamw("adamw_" + n, g, w, m, v)]

    small_like = [norm_gain, final_gain, pool_scale, rel_bias, jnp.zeros((2, d), F32)]
    summed = _sum_slots("small_grads_sum", _all_to_all_small(_pack([g_norm, g_final, g_pscale, g_rel, g_gate_full])))
    g_norm_t, g_final_t, g_pscale_t, g_rel_t, g_gate_t = _unpack(summed, small_like)
    g_gate_t = lax.dynamic_slice_in_dim(g_gate_t, chip * sw, sw, axis=1)
    small_g = [g_norm_t, g_final_t, g_pscale_t, g_rel_t, g_gate_t]
    small_w = [norm_gain, final_gain, pool_scale, rel_bias, gate_bias]
    small_m = [m_norm_gain, m_final_gain, m_pool_scale, m_rel_bias, m_gate_bias]
    small_v = [v_norm_gain, v_final_gain, v_pool_scale, v_rel_bias, v_gate_bias]
    packed = _small_adamw(_pack(small_g), _pack(small_w), _pack(small_m), _pack(small_v))
    sd, sm, sv = [_unpack(t, small_w) for t in packed]
    small = {n: [small_g[i], sd[i], sm[i], sv[i]]
             for i, n in enumerate(["norm_gain", "final_gain", "pool_scale", "rel_bias", "gate_bias"])}

    every = {**big, **small}
    order = ["norm_gain", "w_in", "rel_bias", "pool_w", "pool_scale", "w_out_attn", "w_out_pool", "gate_bias",
             "w_out", "final_gain"]
    return (loss, grad_x.reshape(x.shape), *[every[n][0] for n in order], *[every[n][1] for n in order],
            *[every[n][2] for n in order], *[every[n][3] for n in order])
```

```python
import math

import jax
import jax.numpy as jnp
from jax import lax
from jax.experimental import pallas as pl
from jax.experimental.pallas import tpu as pltpu

F32 = jnp.float32
BF16 = jnp.bfloat16
MESH = pl.DeviceIdType.MESH
ANY = pl.BlockSpec(memory_space=pl.ANY)

N_CHIPS = 4
N_DEV = 8
CHUNK = 64
N_LEFT_CHUNKS = 8
PAD = N_LEFT_CHUNKS * CHUNK
HEAD_DIM = 128
MAX_REL = 128
POOL_WINDOWS = (2, 4, 8, 16)
N_GROUPS = len(POOL_WINDOWS)
HALO = 16
Q_GROUP = 4 * CHUNK
K_GROUP = Q_GROUP + PAD
NEG = -1e30
EPS = 1e-6
ADAM_LR, ADAM_B1, ADAM_B2, ADAM_EPS, ADAM_WD, ADAM_STEP = 0.001, 0.9, 0.999, 1e-08, 0.01, 10
VMEM_LIMIT = 56 * 1024 * 1024

NN = (((1,), (0,)), ((), ()))
NT = (((1,), (1,)), ((), ()))
TN = (((0,), (0,)), ((), ()))


def _div(n, pref):
    if n <= pref:
        return n
    for t in range(pref - pref % 128, 0, -128):
        if n % t == 0:
            return t
    raise ValueError((n, pref))


def _params(sem, **kw):
    return pltpu.CompilerParams(dimension_semantics=sem, vmem_limit_bytes=VMEM_LIMIT, **kw)


def _sigmoid(z):
    return jax.nn.sigmoid(z)


def _silu_and_grad(z):
    sg = _sigmoid(z)
    return z * sg, sg * (1.0 + z * (1.0 - sg))


def _matmul(name, dn, grid, a, a_spec, b, b_spec, acc_shape, outs, out_specs, extra=(), extra_specs=(),
            epilogue=None, accumulate_outs=False, sides=(), prefetch=None, carry=None):
    nk = grid[2]
    aliases = {}
    if carry is not None:
        aliases = {2 + len(extra): 0}
        extra, extra_specs = (*extra, carry), (*extra_specs, ANY)
    ne, no = len(extra), len(outs)

    def finish(res, ex, out_refs):
        if epilogue is None:
            for o in out_refs:
                o[...] = res.astype(o.dtype)
        else:
            epilogue(res, ex, out_refs)

    def body(*refs):
        a_ref, b_ref = refs[0], refs[1]
        ex = refs[2:2 + ne]
        out_refs = refs[2 + ne:2 + ne + no]
        if nk == 1:
            finish(lax.dot_general(a_ref[...], b_ref[...], dn, preferred_element_type=F32), ex, out_refs)
            return
        acc = refs[-1]
        k = pl.program_id(2)

        @pl.when(k == 0)
        def _():
            acc[...] = jnp.zeros_like(acc)

        acc[...] += lax.dot_general(a_ref[...], b_ref[...], dn, preferred_element_type=F32)

        @pl.when(k == nk - 1)
        def _():
            finish(acc[...], ex, out_refs)

    sem = ("arbitrary",) * 3 if accumulate_outs else ("parallel", "parallel", "arbitrary")
    return _run(name, list(sides), dict(
        body=body, grid=grid, in_specs=[a_spec, b_spec, *extra_specs], out_specs=list(out_specs),
        out_shape=list(outs), scratch_shapes=[] if nk == 1 else [pltpu.VMEM(acc_shape, F32)],
        operands=[a, b, *extra], sem=sem, aliases=aliases, prefetch=prefetch))


def _place():
    x, y, c = lax.axis_index("x"), lax.axis_index("y"), lax.axis_index("c")
    chips = [(1 - x, y), (x, 1 - y), (1 - x, 1 - y)]
    return x, y, c, chips


N_STREAMS = 1


class _Copies:
    def __init__(self, cps):
        self.cps = cps

    def start(self):
        for cp in self.cps:
            cp.start()

    def wait_send(self):
        for cp in self.cps:
            cp.wait_send()

    def wait_recv(self):
        for cp in self.cps:
            cp.wait_recv()

    def wait(self):
        for cp in self.cps:
            cp.wait()


def _remote(src, dst, send_sems, recv_sems, k, dev):
    lead = src.shape[0]
    n = N_STREAMS
    while n > 1 and (lead % n or (len(src.shape) == 2 and (lead // n) % 16)):
        n //= 2
    step = lead // n
    return _Copies([pltpu.make_async_remote_copy(
        src_ref=src.at[pl.ds(i * step, step)], dst_ref=dst.at[pl.ds(i * step, step)],
        send_sem=send_sems.at[k * N_STREAMS + i], recv_sem=recv_sems.at[k * N_STREAMS + i],
        device_id=dev, device_id_type=MESH) for i in range(n)])


class _Side:
    def __init__(self, ins, out_shapes, n_remote, n_local, start, finish, aliases=None):
        self.ins, self.out_shapes = list(ins), list(out_shapes)
        self.n_remote, self.n_local = max(n_remote, 1), max(n_local, 1)
        self.start, self.finish, self.aliases = start, finish, aliases or {}


def _run(name, sides, compute=None):
    cm = compute or dict(body=None, grid=(), in_specs=[], out_specs=[], out_shape=[], scratch_shapes=[], operands=[])
    grid = tuple(cm["grid"])
    ni, no, ns = len(cm["operands"]), len(cm["out_shape"]), len(cm["scratch_shapes"])
    n_in = [len(sd.ins) for sd in sides]
    n_out = [len(sd.out_shapes) for sd in sides]
    prefetch = cm.get("prefetch")
    shift = 0 if prefetch is None else 1

    def body(*refs):
        refs = refs[shift:]
        at = ni
        side_ins = []
        for n in n_in:
            side_ins.append(refs[at:at + n])
            at += n
        outs = refs[at:at + no]
        at += no
        side_outs = []
        for n in n_out:
            side_outs.append(refs[at:at + n])
            at += n
        scratch = refs[at:at + ns]
        at += ns
        sems = [refs[at + 3 * q:at + 3 * q + 3] for q in range(len(sides))]

        def each(step):
            for sd, i_, o_, m_ in zip(sides, side_ins, side_outs, sems):
                getattr(sd, step)(i_, o_, *m_)

        if not grid:
            each("start")
            each("finish")
            return
        first = last = None
        for ax, g in enumerate(grid):
            f, l = pl.program_id(ax) == 0, pl.program_id(ax) == g - 1
            first = f if first is None else first & f
            last = l if last is None else last & l
        if sides:
            pl.when(first)(lambda: each("start"))
        cm["body"](*refs[:ni], *outs, *scratch)
        if sides:
            pl.when(last)(lambda: each("finish"))

    aliases = {shift + i_: o_ for i_, o_ in (cm.get("aliases") or {}).items()}
    in_at, out_at = shift + ni, no
    for sd, a, b in zip(sides, n_in, n_out):
        for i_, o_ in sd.aliases.items():
            aliases[in_at + i_] = out_at + o_
        in_at, out_at = in_at + a, out_at + b
    scratch_shapes = list(cm["scratch_shapes"])
    for sd in sides:
        scratch_shapes += [pltpu.SemaphoreType.DMA((sd.n_remote * N_STREAMS,)),
                           pltpu.SemaphoreType.DMA((sd.n_remote * N_STREAMS,)), pltpu.SemaphoreType.DMA((sd.n_local,))]
    in_specs = list(cm["in_specs"]) + [ANY] * sum(n_in)
    out_specs = list(cm["out_specs"]) + [ANY] * sum(n_out)
    kw = dict(in_specs=in_specs, out_specs=out_specs, scratch_shapes=scratch_shapes)
    if grid:
        kw["grid"] = grid
    if prefetch is not None:
        kw = dict(grid_spec=pltpu.PrefetchScalarGridSpec(num_scalar_prefetch=1, **kw))
    if grid:
        kw["compiler_params"] = _params(("arbitrary",) * len(grid) if sides else cm["sem"])
    res = pl.pallas_call(
        body, name=name, out_shape=list(cm["out_shape"]) + [s for sd in sides for s in sd.out_shapes],
        input_output_aliases=aliases, **kw,
    )(*([] if prefetch is None else [prefetch]), *cm["operands"], *[a for sd in sides for a in sd.ins])
    res = list(res)
    side_res, at = [], no
    for n in n_out:
        side_res.append(res[at:at + n])
        at += n
    return res[:no], side_res


def _gather_side(shards, split, peers=(0, 1, 2)):
    n = len(shards)

    def plan(ins, outs, send_sems, recv_sems, _):
        x, y, c, chips = _place()
        me = 2 * x + y
        sibling = (x, y, 1 - c)
        direct, relays, arrivals = [], [], []
        for t in range(n):
            quarter = ins[t].shape[0] // 4
            for j in peers:
                cx, cy = chips[j]
                src_chip = 2 * cx + cy
                k = 12 * t + 4 * j
                if not split[t]:
                    direct.append(_remote(ins[t], outs[t].at[me], send_sems, recv_sems, k, (cx, cy, c)))
                    got = outs[t].at[src_chip]
                    arrivals.append(_remote(got, got, send_sems, recv_sems, k, (cx, cy, c)))
                    continue
                for r in range(2):
                    e = c ^ r
                    out_q = pl.ds((2 * c + e) * quarter, quarter)
                    direct.append(_remote(ins[t].at[out_q], outs[t].at[me, out_q], send_sems, recv_sems, k + r,
                                          (cx, cy, e)))
                    got = outs[t].at[src_chip, pl.ds((2 * e + c) * quarter, quarter)]
                    relays.append((_remote(got, got, send_sems, recv_sems, k + r, (cx, cy, e)),
                                   _remote(got, got, send_sems, recv_sems, k + 2 + r, sibling)))
                    theirs = outs[t].at[src_chip, pl.ds((2 * e + 1 - c) * quarter, quarter)]
                    arrivals.append(_remote(theirs, theirs, send_sems, recv_sems, k + 2 + (1 - r), sibling))
        return direct, relays, arrivals

    def start(*refs):
        for cp in plan(*refs)[0]:
            cp.start()

    def finish(*refs):
        direct, relays, arrivals = plan(*refs)
        for landed, onward in relays:
            landed.wait_recv()
            onward.start()
        for cp in arrivals:
            cp.wait_recv()
        for cp in direct + [onward for _, onward in relays]:
            cp.wait_send()

    return _Side(shards, [jax.ShapeDtypeStruct((N_CHIPS,) + s.shape, s.dtype) for s in shards], 12 * n, 0,
                 start, finish)


def _with_own(slots, block, chip):
    return lax.dynamic_update_slice(slots, block[None], (chip,) + (0,) * block.ndim)


def _relay_far_side(full):
    def plan(ins, outs, send_sems, recv_sems, __):
        x, y, c, _ = _place()
        src, far = ins[0], outs[0]
        eighth = far.shape[0] // 8
        x_id, y_id = 2 * (1 - x) + y, 2 * x + (1 - y)
        sends, landed, onward, from_sibling = [], [], [], []
        for side, (nb_x, nb_y, block) in enumerate([(1 - x, y, y_id), (x, 1 - y, x_id)]):
            for r in range(2):
                e = c ^ r
                out_rows = pl.ds((4 * side + 2 * c + e) * eighth, eighth)
                sends.append(_remote(src.at[block, out_rows], far.at[out_rows], send_sems, recv_sems, 2 * side + r,
                                     (nb_x, nb_y, e)))
                got = far.at[pl.ds((4 * side + 2 * e + c) * eighth, eighth)]
                landed.append(_remote(got, got, send_sems, recv_sems, 2 * side + r, (nb_x, nb_y, e)))
                onward.append(_remote(got, got, send_sems, recv_sems, 4 + 2 * side + r, (x, y, 1 - c)))
                theirs = far.at[pl.ds((4 * side + 2 * e + 1 - c) * eighth, eighth)]
                from_sibling.append(_remote(theirs, theirs, send_sems, recv_sems, 4 + 2 * side + (1 - r),
                                            (x, y, 1 - c)))
        return sends, landed, onward, from_sibling

    def start(*refs):
        for cp in plan(*refs)[0]:
            cp.start()

    def finish(*refs):
        sends, landed, onward, from_sibling = plan(*refs)
        for got, fwd in zip(landed, onward):
            got.wait_recv()
            fwd.start()
        for cp in from_sibling:
            cp.wait_recv()
        for cp in sends + onward:
            cp.wait_send()

    return _Side([full], [jax.ShapeDtypeStruct(full.shape[1:], full.dtype)], 8, 0, start, finish)


def _swap_side(parts):
    n = len(parts)

    def plan(ins, outs, send_sems, recv_sems, _):
        x, y, c, _ = _place()
        cps = []
        for t in range(n):
            half = ins[t].shape[1] // 2
            cps.append(_remote(ins[t].at[:, pl.ds((1 - c) * half, half)], outs[t], send_sems, recv_sems, t,
                               (x, y, 1 - c)))
        return cps

    def start(*refs):
        for cp in plan(*refs):
            cp.start()

    def finish(*refs):
        for cp in plan(*refs):
            cp.wait()

    return _Side(parts, [jax.ShapeDtypeStruct((p.shape[0], p.shape[1] // 2, p.shape[2]), p.dtype) for p in parts],
                 n, 0, start, finish)


def _scatter_side(parts):
    n = len(parts)

    def plan(ins, outs, send_sems, recv_sems, _):
        x, y, c, chips = _place()
        me = 2 * x + y
        sends, arrivals = [], []
        for t in range(n):
            for j, (cx, cy) in enumerate(chips):
                sends.append(_remote(ins[t].at[2 * cx + cy], outs[t].at[me], send_sems, recv_sems, 3 * t + j,
                                     (cx, cy, c)))
                got = outs[t].at[2 * cx + cy]
                arrivals.append(_remote(got, got, send_sems, recv_sems, 3 * t + j, (cx, cy, c)))
        return sends, arrivals

    def start(*refs):
        for cp in plan(*refs)[0]:
            cp.start()

    def finish(*refs):
        sends, arrivals = plan(*refs)
        for cp in arrivals:
            cp.wait_recv()
        for cp in sends:
            cp.wait_send()

    return _Side(parts, [jax.ShapeDtypeStruct(p.shape, p.dtype) for p in parts], 3 * n, 0, start, finish)


def _share_side(fulls):
    n = len(fulls)

    def plan(_, outs, send_sems, recv_sems, __):
        x, y, c, _ = _place()
        cps = []
        for t in range(n):
            half = outs[t].shape[0] // 2
            mine = outs[t].at[pl.ds(c * half, half)]
            theirs = outs[t].at[pl.ds((1 - c) * half, half)]
            cps.append((_remote(mine, mine, send_sems, recv_sems, t, (x, y, 1 - c)),
                        _remote(theirs, theirs, send_sems, recv_sems, t, (x, y, 1 - c))))
        return cps

    def start(*refs):
        for cp, _ in plan(*refs):
            cp.start()

    def finish(*refs):
        for cp, rv in plan(*refs):
            rv.wait_recv()
            cp.wait_send()

    return _Side(fulls, [jax.ShapeDtypeStruct(f.shape, f.dtype) for f in fulls], n, 0, start, finish,
                 aliases={t: t for t in range(n)})


def _all_to_all_small(packed):
    def body(in_ref, out_ref, send_sems, recv_sems, local_sem):
        x, y, c, _ = _place()
        me = 4 * x + 2 * y + c
        own = pltpu.make_async_copy(in_ref, out_ref.at[me], local_sem)
        own.start()
        cps, rvs = [], []
        for k in range(1, N_DEV):
            fx, fy, fc = (k >> 2) & 1, (k >> 1) & 1, k & 1
            px, py, pc = x ^ fx, y ^ fy, c ^ fc
            cp = _remote(in_ref, out_ref.at[me], send_sems, recv_sems, k - 1, (px, py, pc))
            cp.start()
            cps.append(cp)
            got = out_ref.at[4 * px + 2 * py + pc]
            rvs.append(_remote(got, got, send_sems, recv_sems, k - 1, (px, py, pc)))
        for rv in rvs:
            rv.wait_recv()
        for cp in cps:
            cp.wait_send()
        own.wait()

    return pl.pallas_call(
        body, name="small_grads_exchange",
        in_specs=[ANY], out_specs=ANY,
        out_shape=jax.ShapeDtypeStruct((N_DEV,) + packed.shape, packed.dtype),
        scratch_shapes=[pltpu.SemaphoreType.DMA(((N_DEV - 1) * N_STREAMS,)),
                        pltpu.SemaphoreType.DMA(((N_DEV - 1) * N_STREAMS,)), pltpu.SemaphoreType.DMA],
    )(packed)


def _pair_sum(name, g, recv, c_arr):
    _, rows, cols = g.shape
    half = rows // 2
    tr, tc = _div(half, 512), _div(cols, 1024)
    nrb = half // tr

    def body(c_ref, g_ref, r_ref, o_ref):
        o_ref[...] = (g_ref[...] + r_ref[...].astype(F32)).astype(BF16)

    return pl.pallas_call(
        body, name=name,
        grid_spec=pltpu.PrefetchScalarGridSpec(
            num_scalar_prefetch=1, grid=(N_CHIPS, nrb, cols // tc),
            in_specs=[pl.BlockSpec((None, tr, tc), lambda s, i, j, c: (s, c[0] * nrb + i, j)),
                      pl.BlockSpec((None, tr, tc), lambda s, i, j, c: (s, i, j))],
            out_specs=pl.BlockSpec((None, tr, tc), lambda s, i, j, c: (s, i, j))),
        out_shape=jax.ShapeDtypeStruct(recv.shape, BF16),
        compiler_params=_params(("parallel", "parallel", "parallel")),
    )(c_arr, g, recv)


def _chip_sum(name, recv, c_arr, piece=0, pieces=1, into=None):
    _, half, cols = recv.shape
    tr, tc = _div(half, 512), _div(cols, 1024)
    nrb = half // tr

    def body(c_ref, r_ref, *rest):
        o_ref = rest[-1]
        acc = r_ref[0].astype(F32)
        for s in range(1, N_CHIPS):
            acc = acc + r_ref[s].astype(F32)
        o_ref[...] = acc

    return pl.pallas_call(
        body, name=name,
        grid_spec=pltpu.PrefetchScalarGridSpec(
            num_scalar_prefetch=1, grid=(nrb, cols // tc),
            in_specs=[pl.BlockSpec((N_CHIPS, tr, tc), lambda i, j, c: (0, i, j))] + ([] if into is None else [ANY]),
            out_specs=pl.BlockSpec((tr, tc), lambda i, j, c: ((pieces * c[0] + piece) * nrb + i, j))),
        out_shape=jax.ShapeDtypeStruct((2 * pieces * half, cols), F32),
        input_output_aliases={} if into is None else {2: 0},
        compiler_params=_params(("parallel", "parallel")),
    )(c_arr, recv, *([] if into is None else [into]))


def _adamw_math(w, g, m, v):
    m2 = ADAM_B1 * m + (1.0 - ADAM_B1) * g
    v2 = ADAM_B2 * v + (1.0 - ADAM_B2) * (g * g)
    m_hat = m2 / (1.0 - ADAM_B1 ** ADAM_STEP)
    v_hat = v2 / (1.0 - ADAM_B2 ** ADAM_STEP)
    delta = -ADAM_LR * (m_hat / (jnp.sqrt(v_hat) + ADAM_EPS) + ADAM_WD * w)
    return delta, m2, v2


def _adamw(name, g, w, m, v):
    rows, cols = g.shape
    tr, tc = _div(rows, 256), _div(cols, 1024)
    spec = pl.BlockSpec((tr, tc), lambda i, j: (i, j))

    def body(g_ref, w_ref, m_ref, v_ref, go_ref, d_ref, mo_ref, vo_ref):
        gg = g_ref[...]
        delta, m2, v2 = _adamw_math(w_ref[...], gg, m_ref[...], v_ref[...])
        go_ref[...] = gg
        d_ref[...] = delta
        mo_ref[...] = m2
        vo_ref[...] = v2

    return pl.pallas_call(
        body, name=name, grid=(rows // tr, cols // tc),
        in_specs=[spec] * 4, out_specs=[spec] * 4,
        out_shape=[jax.ShapeDtypeStruct(g.shape, F32)] * 4,
        compiler_params=_params(("parallel", "parallel")),
    )(g, w, m, v)


def _sum_slots(name, slots):
    def body(s_ref, o_ref):
        acc = s_ref[0]
        for d in range(1, N_DEV):
            acc = acc + s_ref[d]
        o_ref[...] = acc

    return pl.pallas_call(body, name=name, out_shape=jax.ShapeDtypeStruct(slots.shape[1:], F32))(slots)


def _norm_in(x, gain):
    s, d = x.shape
    tr = _div(s, 256)

    def body(x_ref, g_ref, h_ref):
        xv = x_ref[...]
        r = lax.rsqrt(jnp.mean(xv * xv, axis=-1, keepdims=True) + EPS)
        h_ref[...] = (xv * r * g_ref[...]).astype(BF16)

    return pl.pallas_call(
        body, name="norm_in", grid=(s // tr,),
        in_specs=[pl.BlockSpec((tr, d), lambda i: (i, 0)), pl.BlockSpec((1, d), lambda i: (0, 0))],
        out_specs=pl.BlockSpec((tr, d), lambda i: (i, 0)),
        out_shape=jax.ShapeDtypeStruct((s, d), BF16),
        compiler_params=_params(("parallel",)),
    )(x, gain.reshape(1, d))


def _bias_table(rel_bias):
    h = rel_bias.shape[0]
    span = 2 * CHUNK - 1
    lo, hi = -(CHUNK - 1), (N_LEFT_CHUNKS + 1) * CHUNK - 1
    by_rel = jnp.concatenate([rel_bias[:, MAX_REL + lo:],
                              jnp.broadcast_to(rel_bias[:, -1:], (h, hi - MAX_REL))], axis=1)
    vec = jnp.stack([by_rel[:, m * CHUNK:m * CHUNK + span] for m in range(N_LEFT_CHUNKS + 1)], axis=1)
    rev = jnp.concatenate([vec[..., ::-1], jnp.zeros(vec.shape[:2] + (1,), vec.dtype)], axis=-1)
    skew = jnp.tile(rev, (1, 1, CHUNK))[..., :CHUNK * span].reshape(h, N_LEFT_CHUNKS + 1, CHUNK, span)
    blocks = skew[..., CHUNK - 1:]
    off = jnp.full((h, CHUNK, CHUNK), NEG, rel_bias.dtype)
    rows = []
    for qi in range(Q_GROUP // CHUNK):
        dist = [N_LEFT_CHUNKS + qi - kj for kj in range(K_GROUP // CHUNK)]
        rows.append(jnp.concatenate([blocks[:, m] if 0 <= m <= N_LEFT_CHUNKS else off for m in dist], axis=-1))
    return jnp.concatenate(rows, axis=-2)


def _softmax(scores, tab_ref, r0, scale):
    sc = scores * scale + tab_ref[...]
    col = lax.broadcasted_iota(jnp.int32, sc.shape, 1)
    sc = jnp.where(col >= PAD - r0, sc, NEG)
    e = jnp.exp(sc - jnp.max(sc, axis=-1, keepdims=True))
    return e * (1.0 / jnp.sum(e, axis=-1, keepdims=True))


def _attention_fwd(proj, tables, s, a, sides=()):
    heads = a // HEAD_DIM
    scale = HEAD_DIM ** -0.5
    groups = s // Q_GROUP

    def body(q_ref, k_ref, v_ref, z_ref, tab_ref, o_ref, ya_ref, kp, vp):
        kp[0:PAD, :] = jnp.zeros((PAD, HEAD_DIM), BF16)
        vp[0:PAD, :] = jnp.zeros((PAD, HEAD_DIM), BF16)
        kp[PAD:, :] = k_ref[...].astype(BF16)
        vp[PAD:, :] = v_ref[...].astype(BF16)

        def group(g, carry):
            r0 = pl.multiple_of(g * Q_GROUP, Q_GROUP)
            q = q_ref[pl.ds(r0, Q_GROUP), :].astype(BF16)
            p = _softmax(lax.dot_general(q, kp[pl.ds(r0, K_GROUP), :], NT, preferred_element_type=F32),
                         tab_ref, r0, scale)
            o = jnp.dot(p.astype(BF16), vp[pl.ds(r0, K_GROUP), :], preferred_element_type=F32)
            o_ref[pl.ds(r0, Q_GROUP), :] = o
            z = z_ref[pl.ds(r0, Q_GROUP), :]
            ya_ref[pl.ds(r0, Q_GROUP), :] = (o * (z * _sigmoid(z))).astype(BF16)
            return carry

        lax.fori_loop(0, groups, group, 0)

    col = lambda seg: (lambda h: (0, seg * heads + h))
    blk = lambda seg: pl.BlockSpec((s, HEAD_DIM), col(seg))
    return _run("attention_fwd", list(sides), dict(
        body=body, grid=(heads,),
        in_specs=[blk(0), blk(1), blk(2), blk(3),
                  pl.BlockSpec((None, Q_GROUP, K_GROUP), lambda h: (h, 0, 0))],
        out_specs=[blk(0), blk(0)],
        out_shape=[jax.ShapeDtypeStruct((s, a), F32), jax.ShapeDtypeStruct((s, a), BF16)],
        scratch_shapes=[pltpu.VMEM((PAD + s, HEAD_DIM), BF16), pltpu.VMEM((PAD + s, HEAD_DIM), BF16)],
        operands=[proj, proj, proj, proj, tables], sem=("parallel",)))


def _attention_bwd(proj, o, dya, tables, dproj, s, a):
    heads = a // HEAD_DIM
    scale = HEAD_DIM ** -0.5
    groups = s // Q_GROUP

    def body(q_ref, k_ref, v_ref, z_ref, o_ref, dy_ref, tab_ref, _, dp_ref, dtab_ref, kp, vp, dkp, dvp):
        kp[0:PAD, :] = jnp.zeros((PAD, HEAD_DIM), BF16)
        vp[0:PAD, :] = jnp.zeros((PAD, HEAD_DIM), BF16)
        kp[PAD:, :] = k_ref[...].astype(BF16)
        vp[PAD:, :] = v_ref[...].astype(BF16)
        dkp[...] = jnp.zeros_like(dkp)
        dvp[...] = jnp.zeros_like(dvp)
        dtab_ref[...] = jnp.zeros_like(dtab_ref)

        def group(g, carry):
            r0 = pl.multiple_of(g * Q_GROUP, Q_GROUP)
            rows = pl.ds(r0, Q_GROUP)
            band = pl.ds(r0, K_GROUP)
            q = q_ref[rows, :].astype(BF16)
            z = z_ref[rows, :]
            dy = dy_ref[rows, :]
            si, dsi = _silu_and_grad(z)
            dp_ref[3, rows, :] = (dy * o_ref[rows, :] * dsi).astype(BF16)
            dob = (dy * si).astype(BF16)
            p = _softmax(lax.dot_general(q, kp[band, :], NT, preferred_element_type=F32), tab_ref, r0, scale)
            dp = lax.dot_general(dob, vp[band, :], NT, preferred_element_type=F32)
            ds = p * (dp - jnp.sum(p * dp, axis=-1, keepdims=True))
            dtab_ref[...] += ds
            dsb = (ds * scale).astype(BF16)
            dp_ref[0, rows, :] = jnp.dot(dsb, kp[band, :], preferred_element_type=F32).astype(BF16)
            dkp[band, :] += lax.dot_general(dsb, q, TN, preferred_element_type=F32)
            dvp[band, :] += lax.dot_general(p.astype(BF16), dob, TN, preferred_element_type=F32)
            return carry

        lax.fori_loop(0, groups, group, 0)
        dp_ref[1] = dkp[PAD:, :].astype(BF16)
        dp_ref[2] = dvp[PAD:, :].astype(BF16)

    col = lambda seg: (lambda h: (0, seg * heads + h))
    blk = lambda seg: pl.BlockSpec((s, HEAD_DIM), col(seg))
    return pl.pallas_call(
        body, name="attention_bwd", grid=(heads,),
        in_specs=[blk(0), blk(1), blk(2), blk(3), blk(0), blk(0),
                  pl.BlockSpec((None, Q_GROUP, K_GROUP), lambda h: (h, 0, 0)), ANY],
        out_specs=[pl.BlockSpec((4, s, HEAD_DIM), lambda h: (0, 0, h)),
                   pl.BlockSpec((None, Q_GROUP, K_GROUP), lambda h: (h, 0, 0))],
        out_shape=[jax.ShapeDtypeStruct(dproj.shape, BF16), jax.ShapeDtypeStruct(tables.shape, F32)],
        input_output_aliases={7: 0},
        scratch_shapes=[pltpu.VMEM((PAD + s, HEAD_DIM), BF16), pltpu.VMEM((PAD + s, HEAD_DIM), BF16),
                        pltpu.VMEM((PAD + s, HEAD_DIM), F32), pltpu.VMEM((PAD + s, HEAD_DIM), F32)],
        compiler_params=_params(("parallel",)),
    )(proj, proj, proj, proj, o, dya, tables, dproj)


def _pick_window(gi, by_window):
    out = by_window[-1]
    for n in range(N_GROUPS - 2, -1, -1):
        out = jnp.where(gi == n, by_window[n], out)
    return out


def _inv_count(gi, first_row, rows):
    t = first_row + lax.broadcasted_iota(jnp.int32, (rows, 1), 0)
    w = jnp.left_shift(2, gi)
    return 1.0 / jnp.minimum(t + 1, w).astype(F32)


def _pool_fwd(proj, pw_full, pool_scale, s, a, p):
    pg = p // N_GROUPS
    ts = _div(s, 512)
    u0, z0 = 4 * a // pg, (4 * a + p) // pg
    hb = ts // HALO

    def body(u_ref, uh_ref, z_ref, pw_ref, ps_ref, d_ref, t_ref, y_ref, ext):
        gi, i = pl.program_id(0), pl.program_id(1)
        u = u_ref[...]
        ext[0:HALO, :] = jnp.where(i > 0, uh_ref[...], 0.0)
        ext[HALO:, :] = u
        e = ext[...]
        sums, shift = [], 1
        for _ in POOL_WINDOWS:
            e = e + pltpu.roll(e, shift, 0)
            sums.append(e)
            shift *= 2
        win = _pick_window(gi, sums)[HALO:, :]
        d = (win * _inv_count(gi, i * ts, ts) - u).astype(BF16)
        d_ref[...] = d
        t = jnp.dot(d, pw_ref[...].reshape(pg, pg), preferred_element_type=F32)
        t_ref[...] = t
        z = z_ref[...]
        y_ref[...] = (t * ps_ref[...] * (z * _sigmoid(z))).astype(BF16)

    out_spec = pl.BlockSpec((ts, pg), lambda g, i: (i, g))
    return pl.pallas_call(
        body, name="pool_fwd", grid=(N_GROUPS, s // ts),
        in_specs=[pl.BlockSpec((ts, pg), lambda g, i: (i, u0 + g)),
                  pl.BlockSpec((HALO, pg), lambda g, i: (jnp.maximum(i * hb - 1, 0), u0 + g)),
                  pl.BlockSpec((ts, pg), lambda g, i: (i, z0 + g)),
                  pl.BlockSpec((N_CHIPS, None, pg // N_CHIPS, pg), lambda g, i: (0, g, 0, 0)),
                  pl.BlockSpec((1, pg), lambda g, i: (0, g))],
        out_specs=[out_spec] * 3,
        out_shape=[jax.ShapeDtypeStruct((s, p), BF16), jax.ShapeDtypeStruct((s, p), F32),
                   jax.ShapeDtypeStruct((s, p), BF16)],
        scratch_shapes=[pltpu.VMEM((ts + HALO, pg), F32)],
        compiler_params=_params(("parallel", "parallel")),
    )(proj, proj, proj, pw_full, pool_scale.reshape(1, p))


def _pool_bwd(proj, dyp, t, d, pw_full, pool_scale, dproj, s, a, p):
    pg = p // N_GROUPS
    ts = _div(s, 512)
    nt = s // ts
    z0 = (4 * a + p) // pg
    hb = ts // HALO
    last_halo = s // HALO - 1

    def body(dy_ref, dyh_ref, z_ref, zh_ref, t_ref, th_ref, d_ref, pw_ref, ps_ref, _,
             dp_ref, dpw_ref, dps_ref, ext):
        gi, i = pl.program_id(0), pl.program_id(1)
        ps = ps_ref[...]
        pw = pw_ref[...].reshape(pg, pg)

        @pl.when(i == 0)
        def _():
            dpw_ref[...] = jnp.zeros_like(dpw_ref)
            dps_ref[...] = jnp.zeros_like(dps_ref)

        def through_gate(dy, z, tt):
            si, dsi = _silu_and_grad(z)
            return dy * si, dy * (tt * ps) * dsi

        tt = t_ref[...]
        dyl, dz = through_gate(dy_ref[...], z_ref[...], tt)
        dp_ref[1] = dz.astype(BF16)
        dps_ref[...] += jnp.sum(dyl * tt, axis=0, keepdims=True)
        dtb = (dyl * ps).astype(BF16)
        dpw_ref[...] += lax.dot_general(d_ref[...], dtb, TN, preferred_element_type=F32).reshape(dpw_ref.shape)
        dd = lax.dot_general(dtb, pw, NT, preferred_element_type=F32)
        dylh, _ = through_gate(dyh_ref[...], zh_ref[...], th_ref[...])
        ddh = lax.dot_general((dylh * ps).astype(BF16), pw, NT, preferred_element_type=F32)
        ddh = jnp.where(i < nt - 1, ddh, 0.0)
        ext[0:ts, :] = dd * _inv_count(gi, i * ts, ts)
        ext[ts:, :] = ddh * _inv_count(gi, (i + 1) * ts, HALO)
        e = ext[...]
        rows = ts + HALO
        sums, shift = [], 1
        for _ in POOL_WINDOWS:
            e = e + pltpu.roll(e, rows - shift, 0)
            sums.append(e)
            shift *= 2
        dp_ref[0] = (_pick_window(gi, sums)[:ts, :] - dd).astype(BF16)

    tile = lambda c0: pl.BlockSpec((ts, pg), lambda g, i: (i, c0 + g))
    halo = lambda c0: pl.BlockSpec((HALO, pg), lambda g, i: (jnp.minimum((i + 1) * hb, last_halo), c0 + g))
    pw_spec = pl.BlockSpec((N_CHIPS, None, pg // N_CHIPS, pg), lambda g, i: (0, g, 0, 0))
    return pl.pallas_call(
        body, name="pool_bwd", grid=(N_GROUPS, nt),
        in_specs=[tile(0), halo(0), tile(z0), halo(z0), tile(0), halo(0), tile(0), pw_spec,
                  pl.BlockSpec((1, pg), lambda g, i: (0, g)), ANY],
        out_specs=[pl.BlockSpec((2, ts, pg), lambda g, i: (2, i, g)), pw_spec,
                   pl.BlockSpec((1, pg), lambda g, i: (0, g))],
        out_shape=[jax.ShapeDtypeStruct(dproj.shape, BF16),
                   jax.ShapeDtypeStruct(pw_full.shape, F32), jax.ShapeDtypeStruct((1, p), F32)],
        input_output_aliases={9: 0},
        scratch_shapes=[pltpu.VMEM((ts + HALO, pg), F32)],
        compiler_params=_params(("parallel", "arbitrary")),
    )(dyp, dyp, proj, proj, t, t, d, pw_full, pool_scale.reshape(1, p), dproj)


def _merge_fwd(ya, yp, woa_full, wop_full, proj, gb_full, s, d, a, sides=()):
    sw = d // N_CHIPS
    tm, tn = _div(s, 512), _div(sw, 1024)
    per = sw // tn
    ga0, gp0 = (4 * a + 2 * a) // tn, (4 * a + 2 * a + d) // tn

    def body(ya_ref, yp_ref, wa_ref, wp_ref, ga_ref, gp_ref, gb_ref, a_out, b_out, m_out):
        av = jnp.dot(ya_ref[...], wa_ref[...], preferred_element_type=F32)
        bv = jnp.dot(yp_ref[...], wp_ref[...], preferred_element_type=F32)
        a_out[...] = av
        b_out[...] = bv
        sa = _sigmoid(ga_ref[...] + gb_ref[0:1, :])
        sp = _sigmoid(gp_ref[...] + gb_ref[1:2, :])
        m_out[...] = (sa * av + sp * bv).astype(BF16)

    act = pl.BlockSpec((tm, a), lambda j, i: (i, 0))
    wgt = pl.BlockSpec((None, a, tn), lambda j, i: (j // per, 0, j % per))
    out = pl.BlockSpec((tm, tn), lambda j, i: (i, j))
    return _run("merge_fwd", list(sides), dict(
        body=body, grid=(d // tn, s // tm),
        in_specs=[act, act, wgt, wgt,
                  pl.BlockSpec((tm, tn), lambda j, i: (i, ga0 + j)),
                  pl.BlockSpec((tm, tn), lambda j, i: (i, gp0 + j)),
                  pl.BlockSpec((None, 2, tn), lambda j, i: (j // per, 0, j % per))],
        out_specs=[out, out, out],
        out_shape=[jax.ShapeDtypeStruct((s, d), F32), jax.ShapeDtypeStruct((s, d), F32),
                   jax.ShapeDtypeStruct((s, d), BF16)],
        scratch_shapes=[], operands=[ya, yp, woa_full, wop_full, proj, proj, gb_full], sem=("parallel", "parallel")))


def _out_proj(mb, wo, x, s, d):
    tm, tn, tk = _div(s, 1024), _div(d, 1024), _div(d, 2048)

    def epilogue(res, ex, outs):
        outs[0][...] = res + ex[0][...]

    tile = pl.BlockSpec((tm, tn), lambda i, j, k: (i, j))
    return _matmul(
        "out_proj", NN, (s // tm, d // tn, d // tk),
        mb, pl.BlockSpec((tm, tk), lambda i, j, k: (i, k)),
        wo, pl.BlockSpec((tk, tn), lambda i, j, k: (k, j)),
        (tm, tn), [jax.ShapeDtypeStruct((s, d), F32)], [tile], extra=(x,), extra_specs=(tile,),
        epilogue=epilogue)[0][0]


def _loss_head(x2, target, final_gain):
    s, d = x2.shape
    tr = _div(s, 256)

    def body(x_ref, t_ref, g_ref, loss_ref, dx_ref, dxb_ref, dg_ref):
        @pl.when(pl.program_id(0) == 0)
        def _():
            dg_ref[...] = jnp.zeros_like(dg_ref)

        xv = x_ref[...]
        g = g_ref[...]
        r = lax.rsqrt(jnp.mean(xv * xv, axis=-1, keepdims=True) + EPS)
        xn = xv * r
        e = xn * g - t_ref[...]
        loss_ref[...] = 0.5 * jnp.mean(e * e, axis=-1, keepdims=True)
        dy = e / d
        dg_ref[...] += jnp.sum(dy * xn, axis=0, keepdims=True)
        dxn = dy * g
        dx = r * (dxn - xn * jnp.mean(dxn * xn, axis=-1, keepdims=True))
        dx_ref[...] = dx
        dxb_ref[...] = dx.astype(BF16)

    rows = pl.BlockSpec((tr, d), lambda i: (i, 0))
    vec = pl.BlockSpec((1, d), lambda i: (0, 0))
    return pl.pallas_call(
        body, name="loss_head", grid=(s // tr,),
        in_specs=[rows, rows, vec], out_specs=[pl.BlockSpec((tr, 1), lambda i: (i, 0)), rows, rows, vec],
        out_shape=[jax.ShapeDtypeStruct((s, 1), F32), jax.ShapeDtypeStruct((s, d), F32),
                   jax.ShapeDtypeStruct((s, d), BF16), jax.ShapeDtypeStruct((1, d), F32)],
        compiler_params=_params(("arbitrary",)),
    )(x2, target, final_gain.reshape(1, d))


N_SLOTS = 10


def _slot(seg):
    t = seg - 6
    return jnp.where(seg < 6, seg, 6 + 2 * (t % 2) + t // 2)


def _merge_bwd(dxb, wo, a_val, b_val, proj, gb_full, s, d, a):
    sw = d // N_CHIPS
    tm, tn, tk = _div(s, 256), _div(sw, 1024), d
    per = sw // tn
    per_slot = a // tn
    ga0, gp0 = (4 * a + 2 * a) // tn, (4 * a + 2 * a + d) // tn

    def epilogue(dm, ex, outs):
        a_ref, b_ref, ga_ref, gp_ref, gb_ref = ex
        da_ref, db_ref, dg_ref, dgb_ref = outs
        sa = _sigmoid(ga_ref[...] + gb_ref[0:1, :])
        sp = _sigmoid(gp_ref[...] + gb_ref[1:2, :])
        da_ref[...] = (dm * sa).astype(BF16)
        db_ref[...] = (dm * sp).astype(BF16)
        dga = dm * a_ref[...] * sa * (1.0 - sa)
        dgp = dm * b_ref[...] * sp * (1.0 - sp)
        dg_ref[0] = dga.astype(BF16)
        dg_ref[1] = dgp.astype(BF16)

        @pl.when(pl.program_id(1) == 0)
        def _():
            dgb_ref[...] = jnp.zeros_like(dgb_ref)

        dgb_ref[0:1, :] += jnp.sum(dga, axis=0, keepdims=True)
        dgb_ref[1:2, :] += jnp.sum(dgp, axis=0, keepdims=True)

    tile = pl.BlockSpec((tm, tn), lambda j, i, k: (i, j))
    sd = jax.ShapeDtypeStruct((s, d), BF16)
    return _matmul(
        "merge_bwd", NT, (d // tn, s // tm, d // tk),
        dxb, pl.BlockSpec((tm, tk), lambda j, i, k: (i, k)),
        wo, pl.BlockSpec((tn, tk), lambda j, i, k: (j, k)),
        (tm, tn), [sd, sd, jax.ShapeDtypeStruct((N_SLOTS, s, a), BF16), jax.ShapeDtypeStruct((2, d), F32)],
        [tile, tile, pl.BlockSpec((2, tm, tn), lambda j, i, k: (3 + j // per_slot, i, j % per_slot)),
         pl.BlockSpec((2, tn), lambda j, i, k: (0, j))],
        extra=(a_val, b_val, proj, proj, gb_full),
        extra_specs=(tile, tile, pl.BlockSpec((tm, tn), lambda j, i, k: (i, ga0 + j)),
                     pl.BlockSpec((tm, tn), lambda j, i, k: (i, gp0 + j)),
                     pl.BlockSpec((None, 2, tn), lambda j, i, k: (j // per, 0, j % per))),
        epilogue=epilogue, accumulate_outs=True)[0]


def _weight_grad(name, act, dout, shard_cols, slots=False, piece=None, sides=()):
    s, kdim = act.shape
    n = dout.shape[0] * dout.shape[2] if slots else dout.shape[1]
    row_tile = lambda i: i
    if shard_cols:
        sw = n // N_CHIPS
        tm, tn = _div(kdim, 1024), _div(math.gcd(sw, dout.shape[2]) if slots else sw, 1024)
        per = sw // tn
        if piece is not None:
            tm = kdim // (2 * piece[1])
            kdim = 2 * tm
            row_tile = lambda i: i * piece[1] + piece[0]
        shape = (N_CHIPS, kdim, sw)
        out = pl.BlockSpec((None, tm, tn), lambda i, j, k: (j // per, i, j % per))
    else:
        sh = kdim // N_CHIPS
        tm, tn = _div(sh, 1024), _div(n, 1024)
        per = sh // tm
        shape = (N_CHIPS, sh, n)
        out = pl.BlockSpec((None, tm, tn), lambda i, j, k: (i // per, i % per, j))
    tk = _div(s, 4096)
    if slots:
        per_slot = dout.shape[2] // tn
        dout_spec = pl.BlockSpec((None, tk, tn), lambda i, j, k: (_slot(j // per_slot), k, j % per_slot))
    else:
        dout_spec = pl.BlockSpec((tk, tn), lambda i, j, k: (k, j))
    return _matmul(
        name, TN, (kdim // tm, n // tn, s // tk),
        act, pl.BlockSpec((tk, tm), lambda i, j, k: (k, row_tile(i))), dout, dout_spec,
        (tm, tn), [jax.ShapeDtypeStruct(shape, F32), jax.ShapeDtypeStruct(shape, BF16)], [out, out], sides=sides)


def _norm_in_bwd(x, dh, dx2, gain):
    s, d = x.shape
    tr = _div(s, 256)

    def body(x_ref, dh_ref, dx2_ref, g_ref, gx_ref, dg_ref):
        @pl.when(pl.program_id(0) == 0)
        def _():
            dg_ref[...] = jnp.zeros_like(dg_ref)

        xv = x_ref[...]
        r = lax.rsqrt(jnp.mean(xv * xv, axis=-1, keepdims=True) + EPS)
        xn = xv * r
        dhv = dh_ref[...]
        dg_ref[...] += jnp.sum(dhv * xn, axis=0, keepdims=True)
        dxn = dhv * g_ref[...]
        gx_ref[...] = r * (dxn - xn * jnp.mean(dxn * xn, axis=-1, keepdims=True)) + dx2_ref[...]

    rows = pl.BlockSpec((tr, d), lambda i: (i, 0))
    vec = pl.BlockSpec((1, d), lambda i: (0, 0))
    return pl.pallas_call(
        body, name="norm_in_bwd", grid=(s // tr,),
        in_specs=[rows, rows, rows, vec], out_specs=[rows, vec],
        out_shape=[jax.ShapeDtypeStruct((s, d), F32), jax.ShapeDtypeStruct((1, d), F32)],
        compiler_params=_params(("arbitrary",)),
    )(x, dh, dx2, gain.reshape(1, d))


def _pack(vectors):
    flat = jnp.concatenate([v.reshape(-1).astype(F32) for v in vectors])
    rows = -(-flat.shape[0] // 1024) * 8
    return jnp.pad(flat, (0, rows * 128 - flat.shape[0])).reshape(rows, 128)


def _unpack(packed, like):
    flat, out, at = packed.reshape(-1), [], 0
    for v in like:
        out.append(flat[at:at + v.size].reshape(v.shape))
        at += v.size
    return out


def _small_adamw(g, w, m, v):
    def body(g_ref, w_ref, m_ref, v_ref, d_ref, mo_ref, vo_ref):
        delta, m2, v2 = _adamw_math(w_ref[...], g_ref[...], m_ref[...], v_ref[...])
        d_ref[...] = delta
        mo_ref[...] = m2
        vo_ref[...] = v2

    return pl.pallas_call(body, name="small_adamw", out_shape=[jax.ShapeDtypeStruct(g.shape, F32)] * 3)(g, w, m, v)


def kernel(x, norm_gain, w_in, rel_bias, pool_w, pool_scale, w_out_attn, w_out_pool, gate_bias, w_out, final_gain, loss_target, m_norm_gain, m_w_in, m_rel_bias, m_pool_w, m_pool_scale, m_w_out_attn, m_w_out_pool, m_gate_bias, m_w_out, m_final_gain, v_norm_gain, v_w_in, v_rel_bias, v_pool_w, v_pool_scale, v_w_out_attn, v_w_out_pool, v_gate_bias, v_w_out, v_final_gain):
    _, s, d = x.shape
    a = p = d // 2
    n_in = w_in.shape[1] * N_CHIPS
    sw_in = w_in.shape[1]
    pg = p // N_GROUPS
    xs = x.reshape(s, d)
    target = loss_target.reshape(s, d)
    c_arr = lax.axis_index("c").astype(jnp.int32).reshape(1)
    chip = 2 * lax.axis_index("x") + lax.axis_index("y")

    hb = _norm_in(xs, norm_gain)
    tm, tn, tk = _div(s, 1024), _div(sw_in, 1024), _div(d, 4096)
    per_in = sw_in // tn
    cx, cy = lax.axis_index("x"), lax.axis_index("y")
    order = jnp.stack([2 * cx + cy, 2 * (1 - cx) + cy, 2 * cx + (1 - cy), 2 * (1 - cx) + (1 - cy)]).astype(jnp.int32)

    def in_proj(name, first, count, weights, sides, carry=None):
        if weights.ndim == 3:
            w_spec = pl.BlockSpec((None, tk, tn), lambda i, j, k, o: (o[first + j // per_in], k, j % per_in))
        else:
            w_spec = pl.BlockSpec((tk, tn), lambda i, j, k, o: (k, j))
        return _matmul(
            name, NN, (s // tm, count * per_in, d // tk),
            hb, pl.BlockSpec((tm, tk), lambda i, j, k, o: (i, k)), weights, w_spec,
            (tm, tn), [jax.ShapeDtypeStruct((s, n_in), F32)],
            [pl.BlockSpec((tm, tn), lambda i, j, k, o: (i, o[first + j // per_in] * per_in + j % per_in))],
            sides=sides, prefetch=order, carry=carry)

    w_in_b = w_in.astype(BF16)
    (proj,), ((win_near,),) = in_proj("in_proj_own", 0, 1, w_in_b, [_gather_side([w_in_b], [True], peers=(0, 1))])
    (proj,), ((win_far,),) = in_proj("in_proj_near", 1, 2, win_near, [_relay_far_side(win_near)], carry=proj)
    pw_b, woa_b, wop_b, wo_b = (w.astype(BF16) for w in (pool_w, w_out_attn, w_out_pool, w_out))
    (proj,), ((pw_full, gb_full),) = in_proj(
        "in_proj_far", 3, 1, win_far, [_gather_side([pw_b, gate_bias], [False, False])], carry=proj)
    win_full = _with_own(_with_own(win_near, win_far, order[3]), w_in_b, chip)
    pw_full, gb_full = _with_own(pw_full, pw_b, chip), _with_own(gb_full, gate_bias, chip)
    table = _bias_table(rel_bias)
    (o_attn, ya), ((woa_full, wop_full),) = _attention_fwd(
        proj, table, s, a, [_gather_side([woa_b, wop_b], [True, True])])
    woa_full, wop_full = _with_own(woa_full, woa_b, chip), _with_own(wop_full, wop_b, chip)
    d_pool, t_pool, yp = _pool_fwd(proj, pw_full, pool_scale, s, a, p)
    (a_val, b_val, mb), ((wo_full,),) = _merge_fwd(ya, yp, woa_full, wop_full, proj, gb_full, s, d, a,
                                                  [_gather_side([wo_b], [True])])
    wo_mat = _with_own(wo_full, wo_b, chip).reshape(d, d)
    loss_rows, dx2, dx2b, g_final = _loss_head(_out_proj(mb, wo_mat, xs, s, d), target, final_gain)
    loss = lax.psum(jnp.sum(loss_rows), ("x", "y", "c"))

    da, db, dproj, g_gate_full = _merge_bwd(dx2b, wo_mat, a_val, b_val, proj, gb_full, s, d, a)
    (gwo, gwo_b), _ = _weight_grad("grad_w_out", mb, dx2b, shard_cols=False)
    (gwoa, gwoa_b), _ = _weight_grad("grad_w_out_attn", ya, da, shard_cols=True)
    (gwop, gwop_b), _ = _weight_grad("grad_w_out_pool", yp, db, shard_cols=True)
    early = ["w_out_attn", "w_out_pool", "w_out"]

    sw = d // N_CHIPS
    tm, tn, tk = _div(s, 1024), _div(a, 1024), _div(sw, 1024)
    per_o = sw // tk

    def back_through(name, dout, w_full, sides=()):
        return _matmul(
            name, NT, (s // tm, a // tn, d // tk),
            dout, pl.BlockSpec((tm, tk), lambda i, j, k: (i, k)),
            w_full, pl.BlockSpec((None, tn, tk), lambda i, j, k: (k // per_o, j, k % per_o)),
            (tm, tn), [jax.ShapeDtypeStruct((s, a), F32)], [pl.BlockSpec((tm, tn), lambda i, j, k: (i, j))],
            sides=sides)

    (dya,), (early_sib,) = back_through("grad_y_attn", da, woa_full, [_swap_side([gwoa_b, gwop_b, gwo_b])])
    early_pair = [_pair_sum("pair_sum_" + n, g, r, c_arr) for n, g, r in zip(early, [gwoa, gwop, gwo], early_sib)]
    (dyp,), _ = back_through("grad_y_pool", db, wop_full)
    dproj, gpw, g_pscale = _pool_bwd(proj, dyp, t_pool, d_pool, pw_full, pool_scale, dproj, s, a, p)
    dproj, dtable = _attention_bwd(proj, o_attn, dya, table, dproj, s, a)
    g_rel = jax.vjp(_bias_table, rel_bias)[1](dtable)[0]
    gpw3 = gpw.reshape(N_CHIPS, pg, pg)

    (gw0, gw0_b), (early_chips,) = _weight_grad("grad_w_in_0", hb, dproj, shard_cols=True, slots=True, piece=(0, 2),
                                                sides=[_scatter_side(early_pair)])
    (gw1, gw1_b), (sib0,) = _weight_grad("grad_w_in_1", hb, dproj, shard_cols=True, slots=True, piece=(1, 2),
                                         sides=[_swap_side([gw0_b, gpw3.astype(BF16)])])
    def with_own_sum(from_chips, pair):
        return _with_own(from_chips, lax.dynamic_index_in_dim(pair, chip, 0, keepdims=False), chip)

    early_halves = [_chip_sum("chip_sum_" + n, with_own_sum(r, q), c_arr)
                    for n, r, q in zip(early, early_chips, early_pair)]
    pair0 = [_pair_sum("pair_sum_w_in_0", gw0, sib0[0], c_arr), _pair_sum("pair_sum_pool_w", gpw3, sib0[1], c_arr)]

    tm, tn, tk = _div(s, 1024), _div(d, 1024), _div(a // 2, 1024)
    per_k, per_slot = sw_in // tk, a // tk
    nk_half = n_in // tk // 2
    steps = nk_half // 2

    def grad_h(name, k0, sides, plus=None):
        def body(a0, a1, b0, b1, *rest):
            o_ref, acc = rest[-2], rest[-1]
            k = pl.program_id(2)

            @pl.when(k == 0)
            def _():
                acc[...] = jnp.zeros_like(acc) if plus is None else rest[0][...]

            acc[...] += (lax.dot_general(a0[...], b0[...], NT, preferred_element_type=F32)
                         + lax.dot_general(a1[...], b1[...], NT, preferred_element_type=F32))

            @pl.when(k == steps - 1)
            def _():
                o_ref[...] = acc[...]

        def act(which):
            return pl.BlockSpec((None, tm, tk), lambda i, j, k: (
                _slot((2 * k + which + k0) // per_slot), i, (2 * k + which + k0) % per_slot))

        def wgt(which):
            return pl.BlockSpec((None, tn, tk), lambda i, j, k: (
                (2 * k + which + k0) // per_k, j, (2 * k + which + k0) % per_k))

        tile = pl.BlockSpec((tm, tn), lambda i, j, k: (i, j))
        return _run(name, list(sides), dict(
            body=body, grid=(s // tm, d // tn, steps),
            in_specs=[act(0), act(1), wgt(0), wgt(1)] + ([] if plus is None else [tile]), out_specs=[tile],
            out_shape=[jax.ShapeDtypeStruct((s, d), F32)], scratch_shapes=[pltpu.VMEM((tm, tn), F32)],
            operands=[dproj, dproj, win_full, win_full] + ([] if plus is None else [plus]),
            sem=("parallel", "parallel", "arbitrary")))

    (dh_a,), (chips0, (sib1,), early_grads) = grad_h(
        "grad_h_a", 0, [_scatter_side(pair0), _swap_side([gw1_b]), _share_side(early_halves)])
    pair1 = _pair_sum("pair_sum_w_in_1", gw1, sib1, c_arr)
    (dh,), ((chips1,),) = grad_h("grad_h_b", nk_half, [_scatter_side([pair1])], plus=dh_a)
    gwin_half = _chip_sum("chip_sum_w_in_0", with_own_sum(chips0[0], pair0[0]), c_arr, piece=0, pieces=2)
    gwin_half = _chip_sum("chip_sum_w_in_1", with_own_sum(chips1, pair1), c_arr, piece=1, pieces=2, into=gwin_half)
    gpw_half = _chip_sum("chip_sum_pool_w", with_own_sum(chips0[1], pair0[1]), c_arr)
    grad_x, g_norm = _norm_in_bwd(xs, dh, dx2, norm_gain)
    _, (late_grads,) = _run("reduce_share_halves", [_share_side([gwin_half, gpw_half])])

    names = ["w_in", "w_out_attn", "w_out_pool", "w_out", "pool_w"]
    grads = [late_grads[0], *early_grads, late_grads[1]]
    big = {}
    weights = [w_in, w_out_attn, w_out_pool, w_out, pool_w.reshape(pg, pg)]
    ms = [m_w_in, m_w_out_attn, m_w_out_pool, m_w_out, m_pool_w.reshape(pg, pg)]
    vs = [v_w_in, v_w_out_attn, v_w_out_pool, v_w_out, v_pool_w.reshape(pg, pg)]
    for n, g, w, m, v in zip(names, grads, weights, ms, vs):
        big[n] = [r.reshape(pool_w.shape) if n == "pool_w" else r for r in _adamw("adamw_" + n, g, w, m, v)]

    small_like = [norm_gain, final_gain, pool_scale, rel_bias, jnp.zeros((2, d), F32)]
    summed = _sum_slots("small_grads_sum", _all_to_all_small(_pack([g_norm, g_final, g_pscale, g_rel, g_gate_full])))
    g_norm_t, g_final_t, g_pscale_t, g_rel_t, g_gate_t = _unpack(summed, small_like)
    g_gate_t = lax.dynamic_slice_in_dim(g_gate_t, chip * sw, sw, axis=1)
    small_g = [g_norm_t, g_final_t, g_pscale_t, g_rel_t, g_gate_t]
    small_w = [norm_gain, final_gain, pool_scale, rel_bias, gate_bias]
    small_m = [m_norm_gain, m_final_gain, m_pool_scale, m_rel_bias, m_gate_bias]
    small_v = [v_norm_gain, v_final_gain, v_pool_scale, v_rel_bias, v_gate_bias]
    packed = _small_adamw(_pack(small_g), _pack(small_w), _pack(small_m), _pack(small_v))
    sd, sm, sv = [_unpack(t, small_w) for t in packed]
    small = {n: [small_g[i], sd[i], sm[i], sv[i]]
             for i, n in enumerate(["norm_gain", "final_gain", "pool_scale", "rel_bias", "gate_bias"])}

    every = {**big, **small}
    order = ["norm_gain", "w_in", "rel_bias", "pool_w", "pool_scale", "w_out_attn", "w_out_pool", "gate_bias",
             "w_out", "final_gain"]
    return (loss, grad_x.reshape(x.shape), *[every[n][0] for n in order], *[every[n][1] for n in order],
            *[every[n][2] for n in order], *[every[n][3] for n in order])
```

```python
import math

import jax
import jax.numpy as jnp
from jax import lax
from jax.experimental import pallas as pl
from jax.experimental.pallas import tpu as pltpu

F32 = jnp.float32
BF16 = jnp.bfloat16
MESH = pl.DeviceIdType.MESH
ANY = pl.BlockSpec(memory_space=pl.ANY)

N_CHIPS = 4
N_DEV = 8
CHUNK = 64
N_LEFT_CHUNKS = 8
PAD = N_LEFT_CHUNKS * CHUNK
HEAD_DIM = 128
MAX_REL = 128
POOL_WINDOWS = (2, 4, 8, 16)
N_GROUPS = len(POOL_WINDOWS)
HALO = 16
Q_GROUP = 4 * CHUNK
K_GROUP = Q_GROUP + PAD
NEG = -1e30
EPS = 1e-6
ADAM_LR, ADAM_B1, ADAM_B2, ADAM_EPS, ADAM_WD, ADAM_STEP = 0.001, 0.9, 0.999, 1e-08, 0.01, 10
VMEM_LIMIT = 56 * 1024 * 1024

NN = (((1,), (0,)), ((), ()))
NT = (((1,), (1,)), ((), ()))
TN = (((0,), (0,)), ((), ()))


def _div(n, pref):
    if n <= pref:
        return n
    for t in range(pref - pref % 128, 0, -128):
        if n % t == 0:
            return t
    raise ValueError((n, pref))


def _params(sem, **kw):
    return pltpu.CompilerParams(dimension_semantics=sem, vmem_limit_bytes=VMEM_LIMIT, **kw)


def _sigmoid(z):
    return jax.nn.sigmoid(z)


def _silu_and_grad(z):
    sg = _sigmoid(z)
    return z * sg, sg * (1.0 + z * (1.0 - sg))


def _matmul(name, dn, grid, a, a_spec, b, b_spec, acc_shape, outs, out_specs, extra=(), extra_specs=(),
            epilogue=None, accumulate_outs=False, sides=(), prefetch=None, carry=None):
    nk = grid[2]
    aliases = {}
    if carry is not None:
        aliases = {2 + len(extra): 0}
        extra, extra_specs = (*extra, carry), (*extra_specs, ANY)
    ne, no = len(extra), len(outs)

    def finish(res, ex, out_refs):
        if epilogue is None:
            for o in out_refs:
                o[...] = res.astype(o.dtype)
        else:
            epilogue(res, ex, out_refs)

    def body(*refs):
        a_ref, b_ref = refs[0], refs[1]
        ex = refs[2:2 + ne]
        out_refs = refs[2 + ne:2 + ne + no]
        if nk == 1:
            finish(lax.dot_general(a_ref[...], b_ref[...], dn, preferred_element_type=F32), ex, out_refs)
            return
        acc = refs[-1]
        k = pl.program_id(2)

        @pl.when(k == 0)
        def _():
            acc[...] = jnp.zeros_like(acc)

        acc[...] += lax.dot_general(a_ref[...], b_ref[...], dn, preferred_element_type=F32)

        @pl.when(k == nk - 1)
        def _():
            finish(acc[...], ex, out_refs)

    sem = ("arbitrary",) * 3 if accumulate_outs else ("parallel", "parallel", "arbitrary")
    return _run(name, list(sides), dict(
        body=body, grid=grid, in_specs=[a_spec, b_spec, *extra_specs], out_specs=list(out_specs),
        out_shape=list(outs), scratch_shapes=[] if nk == 1 else [pltpu.VMEM(acc_shape, F32)],
        operands=[a, b, *extra], sem=sem, aliases=aliases, prefetch=prefetch))


def _place():
    x, y, c = lax.axis_index("x"), lax.axis_index("y"), lax.axis_index("c")
    chips = [(1 - x, y), (x, 1 - y), (1 - x, 1 - y)]
    return x, y, c, chips


N_STREAMS = 1


class _Copies:
    def __init__(self, cps):
        self.cps = cps

    def start(self):
        for cp in self.cps:
            cp.start()

    def wait_send(self):
        for cp in self.cps:
            cp.wait_send()

    def wait_recv(self):
        for cp in self.cps:
            cp.wait_recv()

    def wait(self):
        for cp in self.cps:
            cp.wait()


def _remote(src, dst, send_sems, recv_sems, k, dev):
    lead = src.shape[0]
    n = N_STREAMS
    while n > 1 and (lead % n or (len(src.shape) == 2 and (lead // n) % 16)):
        n //= 2
    step = lead // n
    return _Copies([pltpu.make_async_remote_copy(
        src_ref=src.at[pl.ds(i * step, step)], dst_ref=dst.at[pl.ds(i * step, step)],
        send_sem=send_sems.at[k * N_STREAMS + i], recv_sem=recv_sems.at[k * N_STREAMS + i],
        device_id=dev, device_id_type=MESH) for i in range(n)])


class _Side:
    def __init__(self, ins, out_shapes, n_remote, n_local, start, finish, aliases=None):
        self.ins, self.out_shapes = list(ins), list(out_shapes)
        self.n_remote, self.n_local = max(n_remote, 1), max(n_local, 1)
        self.start, self.finish, self.aliases = start, finish, aliases or {}


def _run(name, sides, compute=None):
    cm = compute or dict(body=None, grid=(), in_specs=[], out_specs=[], out_shape=[], scratch_shapes=[], operands=[])
    grid = tuple(cm["grid"])
    ni, no, ns = len(cm["operands"]), len(cm["out_shape"]), len(cm["scratch_shapes"])
    n_in = [len(sd.ins) for sd in sides]
    n_out = [len(sd.out_shapes) for sd in sides]
    prefetch = cm.get("prefetch")
    shift = 0 if prefetch is None else 1

    def body(*refs):
        refs = refs[shift:]
        at = ni
        side_ins = []
        for n in n_in:
            side_ins.append(refs[at:at + n])
            at += n
        outs = refs[at:at + no]
        at += no
        side_outs = []
        for n in n_out:
            side_outs.append(refs[at:at + n])
            at += n
        scratch = refs[at:at + ns]
        at += ns
        sems = [refs[at + 3 * q:at + 3 * q + 3] for q in range(len(sides))]

        def each(step):
            for sd, i_, o_, m_ in zip(sides, side_ins, side_outs, sems):
                getattr(sd, step)(i_, o_, *m_)

        if not grid:
            each("start")
            each("finish")
            return
        first = last = None
        for ax, g in enumerate(grid):
            f, l = pl.program_id(ax) == 0, pl.program_id(ax) == g - 1
            first = f if first is None else first & f
            last = l if last is None else last & l
        if sides:
            pl.when(first)(lambda: each("start"))
        cm["body"](*refs[:ni], *outs, *scratch)
        if sides:
            pl.when(last)(lambda: each("finish"))

    aliases = {shift + i_: o_ for i_, o_ in (cm.get("aliases") or {}).items()}
    in_at, out_at = shift + ni, no
    for sd, a, b in zip(sides, n_in, n_out):
        for i_, o_ in sd.aliases.items():
            aliases[in_at + i_] = out_at + o_
        in_at, out_at = in_at + a, out_at + b
    scratch_shapes = list(cm["scratch_shapes"])
    for sd in sides:
        scratch_shapes += [pltpu.SemaphoreType.DMA((sd.n_remote * N_STREAMS,)),
                           pltpu.SemaphoreType.DMA((sd.n_remote * N_STREAMS,)), pltpu.SemaphoreType.DMA((sd.n_local,))]
    in_specs = list(cm["in_specs"]) + [ANY] * sum(n_in)
    out_specs = list(cm["out_specs"]) + [ANY] * sum(n_out)
    kw = dict(in_specs=in_specs, out_specs=out_specs, scratch_shapes=scratch_shapes)
    if grid:
        kw["grid"] = grid
    if prefetch is not None:
        kw = dict(grid_spec=pltpu.PrefetchScalarGridSpec(num_scalar_prefetch=1, **kw))
    if grid:
        kw["compiler_params"] = _params(("arbitrary",) * len(grid) if sides else cm["sem"])
    res = pl.pallas_call(
        body, name=name, out_shape=list(cm["out_shape"]) + [s for sd in sides for s in sd.out_shapes],
        input_output_aliases=aliases, **kw,
    )(*([] if prefetch is None else [prefetch]), *cm["operands"], *[a for sd in sides for a in sd.ins])
    res = list(res)
    side_res, at = [], no
    for n in n_out:
        side_res.append(res[at:at + n])
        at += n
    return res[:no], side_res


def _gather_side(shards, split, peers=(0, 1, 2)):
    n = len(shards)

    def plan(ins, outs, send_sems, recv_sems, _):
        x, y, c, chips = _place()
        me = 2 * x + y
        sibling = (x, y, 1 - c)
        direct, relays, arrivals = [], [], []
        for t in range(n):
            quarter = ins[t].shape[0] // 4
            for j in peers:
                cx, cy = chips[j]
                src_chip = 2 * cx + cy
                k = 12 * t + 4 * j
                if not split[t]:
                    direct.append(_remote(ins[t], outs[t].at[me], send_sems, recv_sems, k, (cx, cy, c)))
                    got = outs[t].at[src_chip]
                    arrivals.append(_remote(got, got, send_sems, recv_sems, k, (cx, cy, c)))
                    continue
                for r in range(2):
                    e = c ^ r
                    out_q = pl.ds((2 * c + e) * quarter, quarter)
                    direct.append(_remote(ins[t].at[out_q], outs[t].at[me, out_q], send_sems, recv_sems, k + r,
                                          (cx, cy, e)))
                    got = outs[t].at[src_chip, pl.ds((2 * e + c) * quarter, quarter)]
                    relays.append((_remote(got, got, send_sems, recv_sems, k + r, (cx, cy, e)),
                                   _remote(got, got, send_sems, recv_sems, k + 2 + r, sibling)))
                    theirs = outs[t].at[src_chip, pl.ds((2 * e + 1 - c) * quarter, quarter)]
                    arrivals.append(_remote(theirs, theirs, send_sems, recv_sems, k + 2 + (1 - r), sibling))
        return direct, relays, arrivals

    def start(*refs):
        for cp in plan(*refs)[0]:
            cp.start()

    def finish(*refs):
        direct, relays, arrivals = plan(*refs)
        for landed, onward in relays:
            landed.wait_recv()
            onward.start()
        for cp in arrivals:
            cp.wait_recv()
        for cp in direct + [onward for _, onward in relays]:
            cp.wait_send()

    return _Side(shards, [jax.ShapeDtypeStruct((N_CHIPS,) + s.shape, s.dtype) for s in shards], 12 * n, 0,
                 start, finish)


def _with_own(slots, block, chip):
    return lax.dynamic_update_slice(slots, block[None], (chip,) + (0,) * block.ndim)


def _relay_far_side(full):
    def plan(ins, outs, send_sems, recv_sems, __):
        x, y, c, _ = _place()
        src, far = ins[0], outs[0]
        eighth = far.shape[0] // 8
        x_id, y_id = 2 * (1 - x) + y, 2 * x + (1 - y)
        sends, landed, onward, from_sibling = [], [], [], []
        for side, (nb_x, nb_y, block) in enumerate([(1 - x, y, y_id), (x, 1 - y, x_id)]):
            for r in range(2):
                e = c ^ r
                out_rows = pl.ds((4 * side + 2 * c + e) * eighth, eighth)
                sends.append(_remote(src.at[block, out_rows], far.at[out_rows], send_sems, recv_sems, 2 * side + r,
                                     (nb_x, nb_y, e)))
                got = far.at[pl.ds((4 * side + 2 * e + c) * eighth, eighth)]
                landed.append(_remote(got, got, send_sems, recv_sems, 2 * side + r, (nb_x, nb_y, e)))
                onward.append(_remote(got, got, send_sems, recv_sems, 4 + 2 * side + r, (x, y, 1 - c)))
                theirs = far.at[pl.ds((4 * side + 2 * e + 1 - c) * eighth, eighth)]
                from_sibling.append(_remote(theirs, theirs, send_sems, recv_sems, 4 + 2 * side + (1 - r),
                                            (x, y, 1 - c)))
        return sends, landed, onward, from_sibling

    def start(*refs):
        for cp in plan(*refs)[0]:
            cp.start()

    def finish(*refs):
        sends, landed, onward, from_sibling = plan(*refs)
        for got, fwd in zip(landed, onward):
            got.wait_recv()
            fwd.start()
        for cp in from_sibling:
            cp.wait_recv()
        for cp in sends + onward:
            cp.wait_send()

    return _Side([full], [jax.ShapeDtypeStruct(full.shape[1:], full.dtype)], 8, 0, start, finish)


def _swap_side(parts):
    n = len(parts)

    def plan(ins, outs, send_sems, recv_sems, _):
        x, y, c, _ = _place()
        cps = []
        for t in range(n):
            half = ins[t].shape[1] // 2
            cps.append(_remote(ins[t].at[:, pl.ds((1 - c) * half, half)], outs[t], send_sems, recv_sems, t,
                               (x, y, 1 - c)))
        return cps

    def start(*refs):
        for cp in plan(*refs):
            cp.start()

    def finish(*refs):
        for cp in plan(*refs):
            cp.wait()

    return _Side(parts, [jax.ShapeDtypeStruct((p.shape[0], p.shape[1] // 2, p.shape[2]), p.dtype) for p in parts],
                 n, 0, start, finish)


def _scatter_side(parts):
    n = len(parts)

    def plan(ins, outs, send_sems, recv_sems, _):
        x, y, c, chips = _place()
        me = 2 * x + y
        sends, arrivals = [], []
        for t in range(n):
            for j, (cx, cy) in enumerate(chips):
                sends.append(_remote(ins[t].at[2 * cx + cy], outs[t].at[me], send_sems, recv_sems, 3 * t + j,
                                     (cx, cy, c)))
                got = outs[t].at[2 * cx + cy]
                arrivals.append(_remote(got, got, send_sems, recv_sems, 3 * t + j, (cx, cy, c)))
        return sends, arrivals

    def start(*refs):
        for cp in plan(*refs)[0]:
            cp.start()

    def finish(*refs):
        sends, arrivals = plan(*refs)
        for cp in arrivals:
            cp.wait_recv()
        for cp in sends:
            cp.wait_send()

    return _Side(parts, [jax.ShapeDtypeStruct(p.shape, p.dtype) for p in parts], 3 * n, 0, start, finish)


def _share_side(fulls):
    n = len(fulls)

    def plan(_, outs, send_sems, recv_sems, __):
        x, y, c, _ = _place()
        cps = []
        for t in range(n):
            half = outs[t].shape[0] // 2
            mine = outs[t].at[pl.ds(c * half, half)]
            theirs = outs[t].at[pl.ds((1 - c) * half, half)]
            cps.append((_remote(mine, mine, send_sems, recv_sems, t, (x, y, 1 - c)),
                        _remote(theirs, theirs, send_sems, recv_sems, t, (x, y, 1 - c))))
        return cps

    def start(*refs):
        for cp, _ in plan(*refs):
            cp.start()

    def finish(*refs):
        for cp, rv in plan(*refs):
            rv.wait_recv()
            cp.wait_send()

    return _Side(fulls, [jax.ShapeDtypeStruct(f.shape, f.dtype) for f in fulls], n, 0, start, finish,
                 aliases={t: t for t in range(n)})


def _all_to_all_small(packed):
    def body(in_ref, out_ref, send_sems, recv_sems, local_sem):
        x, y, c, _ = _place()
        me = 4 * x + 2 * y + c
        own = pltpu.make_async_copy(in_ref, out_ref.at[me], local_sem)
        own.start()
        cps, rvs = [], []
        for k in range(1, N_DEV):
            fx, fy, fc = (k >> 2) & 1, (k >> 1) & 1, k & 1
            px, py, pc = x ^ fx, y ^ fy, c ^ fc
            cp = _remote(in_ref, out_ref.at[me], send_sems, recv_sems, k - 1, (px, py, pc))
            cp.start()
            cps.append(cp)
            got = out_ref.at[4 * px + 2 * py + pc]
            rvs.append(_remote(got, got, send_sems, recv_sems, k - 1, (px, py, pc)))
        for rv in rvs:
            rv.wait_recv()
        for cp in cps:
            cp.wait_send()
        own.wait()

    return pl.pallas_call(
        body, name="small_grads_exchange",
        in_specs=[ANY], out_specs=ANY,
        out_shape=jax.ShapeDtypeStruct((N_DEV,) + packed.shape, packed.dtype),
        scratch_shapes=[pltpu.SemaphoreType.DMA(((N_DEV - 1) * N_STREAMS,)),
                        pltpu.SemaphoreType.DMA(((N_DEV - 1) * N_STREAMS,)), pltpu.SemaphoreType.DMA],
    )(packed)


def _pair_sum(name, g, recv, c_arr):
    _, rows, cols = g.shape
    half = rows // 2
    tr, tc = _div(half, 512), _div(cols, 1024)
    nrb = half // tr

    def body(c_ref, g_ref, r_ref, o_ref):
        o_ref[...] = (g_ref[...] + r_ref[...].astype(F32)).astype(BF16)

    return pl.pallas_call(
        body, name=name,
        grid_spec=pltpu.PrefetchScalarGridSpec(
            num_scalar_prefetch=1, grid=(N_CHIPS, nrb, cols // tc),
            in_specs=[pl.BlockSpec((None, tr, tc), lambda s, i, j, c: (s, c[0] * nrb + i, j)),
                      pl.BlockSpec((None, tr, tc), lambda s, i, j, c: (s, i, j))],
            out_specs=pl.BlockSpec((None, tr, tc), lambda s, i, j, c: (s, i, j))),
        out_shape=jax.ShapeDtypeStruct(recv.shape, BF16),
        compiler_params=_params(("parallel", "parallel", "parallel")),
    )(c_arr, g, recv)


def _chip_sum(name, recv, c_arr, piece=0, pieces=1, into=None):
    _, half, cols = recv.shape
    tr, tc = _div(half, 512), _div(cols, 1024)
    nrb = half // tr

    def body(c_ref, r_ref, *rest):
        o_ref = rest[-1]
        acc = r_ref[0].astype(F32)
        for s in range(1, N_CHIPS):
            acc = acc + r_ref[s].astype(F32)
        o_ref[...] = acc

    return pl.pallas_call(
        body, name=name,
        grid_spec=pltpu.PrefetchScalarGridSpec(
            num_scalar_prefetch=1, grid=(nrb, cols // tc),
            in_specs=[pl.BlockSpec((N_CHIPS, tr, tc), lambda i, j, c: (0, i, j))] + ([] if into is None else [ANY]),
            out_specs=pl.BlockSpec((tr, tc), lambda i, j, c: ((pieces * c[0] + piece) * nrb + i, j))),
        out_shape=jax.ShapeDtypeStruct((2 * pieces * half, cols), F32),
        input_output_aliases={} if into is None else {2: 0},
        compiler_params=_params(("parallel", "parallel")),
    )(c_arr, recv, *([] if into is None else [into]))


def _adamw_math(w, g, m, v):
    m2 = ADAM_B1 * m + (1.0 - ADAM_B1) * g
    v2 = ADAM_B2 * v + (1.0 - ADAM_B2) * (g * g)
    m_hat = m2 / (1.0 - ADAM_B1 ** ADAM_STEP)
    v_hat = v2 / (1.0 - ADAM_B2 ** ADAM_STEP)
    delta = -ADAM_LR * (m_hat / (jnp.sqrt(v_hat) + ADAM_EPS) + ADAM_WD * w)
    return delta, m2, v2


def _adamw(name, g, w, m, v):
    rows, cols = g.shape
    tr, tc = _div(rows, 512), _div(cols, 1024)
    spec = pl.BlockSpec((tr, tc), lambda i, j: (i, j))

    def body(g_ref, w_ref, m_ref, v_ref, go_ref, d_ref, mo_ref, vo_ref):
        gg = g_ref[...]
        delta, m2, v2 = _adamw_math(w_ref[...], gg, m_ref[...], v_ref[...])
        go_ref[...] = gg
        d_ref[...] = delta
        mo_ref[...] = m2
        vo_ref[...] = v2

    return pl.pallas_call(
        body, name=name, grid=(rows // tr, cols // tc),
        in_specs=[spec] * 4, out_specs=[spec] * 4,
        out_shape=[jax.ShapeDtypeStruct(g.shape, F32)] * 4,
        compiler_params=_params(("parallel", "parallel")),
    )(g, w, m, v)


def _sum_slots(name, slots):
    def body(s_ref, o_ref):
        acc = s_ref[0]
        for d in range(1, N_DEV):
            acc = acc + s_ref[d]
        o_ref[...] = acc

    return pl.pallas_call(body, name=name, out_shape=jax.ShapeDtypeStruct(slots.shape[1:], F32))(slots)


def _norm_in(x, gain):
    s, d = x.shape
    tr = _div(s, 256)

    def body(x_ref, g_ref, h_ref):
        xv = x_ref[...]
        r = lax.rsqrt(jnp.mean(xv * xv, axis=-1, keepdims=True) + EPS)
        h_ref[...] = (xv * r * g_ref[...]).astype(BF16)

    return pl.pallas_call(
        body, name="norm_in", grid=(s // tr,),
        in_specs=[pl.BlockSpec((tr, d), lambda i: (i, 0)), pl.BlockSpec((1, d), lambda i: (0, 0))],
        out_specs=pl.BlockSpec((tr, d), lambda i: (i, 0)),
        out_shape=jax.ShapeDtypeStruct((s, d), BF16),
        compiler_params=_params(("parallel",)),
    )(x, gain.reshape(1, d))


def _bias_table(rel_bias):
    h = rel_bias.shape[0]
    span = 2 * CHUNK - 1
    lo, hi = -(CHUNK - 1), (N_LEFT_CHUNKS + 1) * CHUNK - 1
    by_rel = jnp.concatenate([rel_bias[:, MAX_REL + lo:],
                              jnp.broadcast_to(rel_bias[:, -1:], (h, hi - MAX_REL))], axis=1)
    vec = jnp.stack([by_rel[:, m * CHUNK:m * CHUNK + span] for m in range(N_LEFT_CHUNKS + 1)], axis=1)
    rev = jnp.concatenate([vec[..., ::-1], jnp.zeros(vec.shape[:2] + (1,), vec.dtype)], axis=-1)
    skew = jnp.tile(rev, (1, 1, CHUNK))[..., :CHUNK * span].reshape(h, N_LEFT_CHUNKS + 1, CHUNK, span)
    blocks = skew[..., CHUNK - 1:]
    off = jnp.full((h, CHUNK, CHUNK), NEG, rel_bias.dtype)
    rows = []
    for qi in range(Q_GROUP // CHUNK):
        dist = [N_LEFT_CHUNKS + qi - kj for kj in range(K_GROUP // CHUNK)]
        rows.append(jnp.concatenate([blocks[:, m] if 0 <= m <= N_LEFT_CHUNKS else off for m in dist], axis=-1))
    return jnp.concatenate(rows, axis=-2)


def _softmax(scores, tab_ref, r0, scale):
    sc = scores * scale + tab_ref[...]
    col = lax.broadcasted_iota(jnp.int32, sc.shape, 1)
    sc = jnp.where(col >= PAD - r0, sc, NEG)
    e = jnp.exp(sc - jnp.max(sc, axis=-1, keepdims=True))
    return e * (1.0 / jnp.sum(e, axis=-1, keepdims=True))


def _attention_fwd(proj, tables, s, a, sides=()):
    heads = a // HEAD_DIM
    scale = HEAD_DIM ** -0.5
    groups = s // Q_GROUP

    def body(q_ref, k_ref, v_ref, z_ref, tab_ref, o_ref, ya_ref, kp, vp):
        kp[0:PAD, :] = jnp.zeros((PAD, HEAD_DIM), BF16)
        vp[0:PAD, :] = jnp.zeros((PAD, HEAD_DIM), BF16)
        kp[PAD:, :] = k_ref[...].astype(BF16)
        vp[PAD:, :] = v_ref[...].astype(BF16)

        def group(g, carry):
            r0 = pl.multiple_of(g * Q_GROUP, Q_GROUP)
            q = q_ref[pl.ds(r0, Q_GROUP), :].astype(BF16)
            p = _softmax(lax.dot_general(q, kp[pl.ds(r0, K_GROUP), :], NT, preferred_element_type=F32),
                         tab_ref, r0, scale)
            o = jnp.dot(p.astype(BF16), vp[pl.ds(r0, K_GROUP), :], preferred_element_type=F32)
            o_ref[pl.ds(r0, Q_GROUP), :] = o
            z = z_ref[pl.ds(r0, Q_GROUP), :]
            ya_ref[pl.ds(r0, Q_GROUP), :] = (o * (z * _sigmoid(z))).astype(BF16)
            return carry

        lax.fori_loop(0, groups, group, 0)

    col = lambda seg: (lambda h: (0, seg * heads + h))
    blk = lambda seg: pl.BlockSpec((s, HEAD_DIM), col(seg))
    return _run("attention_fwd", list(sides), dict(
        body=body, grid=(heads,),
        in_specs=[blk(0), blk(1), blk(2), blk(3),
                  pl.BlockSpec((None, Q_GROUP, K_GROUP), lambda h: (h, 0, 0))],
        out_specs=[blk(0), blk(0)],
        out_shape=[jax.ShapeDtypeStruct((s, a), F32), jax.ShapeDtypeStruct((s, a), BF16)],
        scratch_shapes=[pltpu.VMEM((PAD + s, HEAD_DIM), BF16), pltpu.VMEM((PAD + s, HEAD_DIM), BF16)],
        operands=[proj, proj, proj, proj, tables], sem=("parallel",)))


def _attention_bwd(proj, o, dya, tables, dproj, s, a):
    heads = a // HEAD_DIM
    scale = HEAD_DIM ** -0.5
    groups = s // Q_GROUP

    def body(q_ref, k_ref, v_ref, z_ref, o_ref, dy_ref, tab_ref, _, dp_ref, dtab_ref, kp, vp, dkp, dvp):
        kp[0:PAD, :] = jnp.zeros((PAD, HEAD_DIM), BF16)
        vp[0:PAD, :] = jnp.zeros((PAD, HEAD_DIM), BF16)
        kp[PAD:, :] = k_ref[...].astype(BF16)
        vp[PAD:, :] = v_ref[...].astype(BF16)
        dkp[...] = jnp.zeros_like(dkp)
        dvp[...] = jnp.zeros_like(dvp)
        dtab_ref[...] = jnp.zeros_like(dtab_ref)

        def group(g, carry):
            r0 = pl.multiple_of(g * Q_GROUP, Q_GROUP)
            rows = pl.ds(r0, Q_GROUP)
            band = pl.ds(r0, K_GROUP)
            q = q_ref[rows, :].astype(BF16)
            z = z_ref[rows, :]
            dy = dy_ref[rows, :]
            si, dsi = _silu_and_grad(z)
            dp_ref[3, rows, :] = (dy * o_ref[rows, :] * dsi).astype(BF16)
            dob = (dy * si).astype(BF16)
            p = _softmax(lax.dot_general(q, kp[band, :], NT, preferred_element_type=F32), tab_ref, r0, scale)
            dp = lax.dot_general(dob, vp[band, :], NT, preferred_element_type=F32)
            ds = p * (dp - jnp.sum(p * dp, axis=-1, keepdims=True))
            dtab_ref[...] += ds
            dsb = (ds * scale).astype(BF16)
            dp_ref[0, rows, :] = jnp.dot(dsb, kp[band, :], preferred_element_type=F32).astype(BF16)
            dkp[band, :] += lax.dot_general(dsb, q, TN, preferred_element_type=F32)
            dvp[band, :] += lax.dot_general(p.astype(BF16), dob, TN, preferred_element_type=F32)
            return carry

        lax.fori_loop(0, groups, group, 0)
        dp_ref[1] = dkp[PAD:, :].astype(BF16)
        dp_ref[2] = dvp[PAD:, :].astype(BF16)

    col = lambda seg: (lambda h: (0, seg * heads + h))
    blk = lambda seg: pl.BlockSpec((s, HEAD_DIM), col(seg))
    return pl.pallas_call(
        body, name="attention_bwd", grid=(heads,),
        in_specs=[blk(0), blk(1), blk(2), blk(3), blk(0), blk(0),
                  pl.BlockSpec((None, Q_GROUP, K_GROUP), lambda h: (h, 0, 0)), ANY],
        out_specs=[pl.BlockSpec((4, s, HEAD_DIM), lambda h: (0, 0, h)),
                   pl.BlockSpec((None, Q_GROUP, K_GROUP), lambda h: (h, 0, 0))],
        out_shape=[jax.ShapeDtypeStruct(dproj.shape, BF16), jax.ShapeDtypeStruct(tables.shape, F32)],
        input_output_aliases={7: 0},
        scratch_shapes=[pltpu.VMEM((PAD + s, HEAD_DIM), BF16), pltpu.VMEM((PAD + s, HEAD_DIM), BF16),
                        pltpu.VMEM((PAD + s, HEAD_DIM), F32), pltpu.VMEM((PAD + s, HEAD_DIM), F32)],
        compiler_params=_params(("parallel",)),
    )(proj, proj, proj, proj, o, dya, tables, dproj)


def _pick_window(gi, by_window):
    out = by_window[-1]
    for n in range(N_GROUPS - 2, -1, -1):
        out = jnp.where(gi == n, by_window[n], out)
    return out


def _inv_count(gi, first_row, rows):
    t = first_row + lax.broadcasted_iota(jnp.int32, (rows, 1), 0)
    w = jnp.left_shift(2, gi)
    return 1.0 / jnp.minimum(t + 1, w).astype(F32)


def _pool_fwd(proj, pw_full, pool_scale, s, a, p):
    pg = p // N_GROUPS
    ts = _div(s, 512)
    u0, z0 = 4 * a // pg, (4 * a + p) // pg
    hb = ts // HALO

    def body(u_ref, uh_ref, z_ref, pw_ref, ps_ref, d_ref, t_ref, y_ref, ext):
        gi, i = pl.program_id(0), pl.program_id(1)
        u = u_ref[...]
        ext[0:HALO, :] = jnp.where(i > 0, uh_ref[...], 0.0)
        ext[HALO:, :] = u
        e = ext[...]
        sums, shift = [], 1
        for _ in POOL_WINDOWS:
            e = e + pltpu.roll(e, shift, 0)
            sums.append(e)
            shift *= 2
        win = _pick_window(gi, sums)[HALO:, :]
        d = (win * _inv_count(gi, i * ts, ts) - u).astype(BF16)
        d_ref[...] = d
        t = jnp.dot(d, pw_ref[...].reshape(pg, pg), preferred_element_type=F32)
        t_ref[...] = t
        z = z_ref[...]
        y_ref[...] = (t * ps_ref[...] * (z * _sigmoid(z))).astype(BF16)

    out_spec = pl.BlockSpec((ts, pg), lambda g, i: (i, g))
    return pl.pallas_call(
        body, name="pool_fwd", grid=(N_GROUPS, s // ts),
        in_specs=[pl.BlockSpec((ts, pg), lambda g, i: (i, u0 + g)),
                  pl.BlockSpec((HALO, pg), lambda g, i: (jnp.maximum(i * hb - 1, 0), u0 + g)),
                  pl.BlockSpec((ts, pg), lambda g, i: (i, z0 + g)),
                  pl.BlockSpec((N_CHIPS, None, pg // N_CHIPS, pg), lambda g, i: (0, g, 0, 0)),
                  pl.BlockSpec((1, pg), lambda g, i: (0, g))],
        out_specs=[out_spec] * 3,
        out_shape=[jax.ShapeDtypeStruct((s, p), BF16), jax.ShapeDtypeStruct((s, p), F32),
                   jax.ShapeDtypeStruct((s, p), BF16)],
        scratch_shapes=[pltpu.VMEM((ts + HALO, pg), F32)],
        compiler_params=_params(("parallel", "parallel")),
    )(proj, proj, proj, pw_full, pool_scale.reshape(1, p))


def _pool_bwd(proj, dyp, t, d, pw_full, pool_scale, dproj, s, a, p):
    pg = p // N_GROUPS
    ts = _div(s, 512)
    nt = s // ts
    z0 = (4 * a + p) // pg
    hb = ts // HALO
    last_halo = s // HALO - 1

    def body(dy_ref, dyh_ref, z_ref, zh_ref, t_ref, th_ref, d_ref, pw_ref, ps_ref, _,
             dp_ref, dpw_ref, dps_ref, ext):
        gi, i = pl.program_id(0), pl.program_id(1)
        ps = ps_ref[...]
        pw = pw_ref[...].reshape(pg, pg)

        @pl.when(i == 0)
        def _():
            dpw_ref[...] = jnp.zeros_like(dpw_ref)
            dps_ref[...] = jnp.zeros_like(dps_ref)

        def through_gate(dy, z, tt):
            si, dsi = _silu_and_grad(z)
            return dy * si, dy * (tt * ps) * dsi

        tt = t_ref[...]
        dyl, dz = through_gate(dy_ref[...], z_ref[...], tt)
        dp_ref[1] = dz.astype(BF16)
        dps_ref[...] += jnp.sum(dyl * tt, axis=0, keepdims=True)
        dtb = (dyl * ps).astype(BF16)
        dpw_ref[...] += lax.dot_general(d_ref[...], dtb, TN, preferred_element_type=F32).reshape(dpw_ref.shape)
        dd = lax.dot_general(dtb, pw, NT, preferred_element_type=F32)
        dylh, _ = through_gate(dyh_ref[...], zh_ref[...], th_ref[...])
        ddh = lax.dot_general((dylh * ps).astype(BF16), pw, NT, preferred_element_type=F32)
        ddh = jnp.where(i < nt - 1, ddh, 0.0)
        ext[0:ts, :] = dd * _inv_count(gi, i * ts, ts)
        ext[ts:, :] = ddh * _inv_count(gi, (i + 1) * ts, HALO)
        e = ext[...]
        rows = ts + HALO
        sums, shift = [], 1
        for _ in POOL_WINDOWS:
            e = e + pltpu.roll(e, rows - shift, 0)
            sums.append(e)
            shift *= 2
        dp_ref[0] = (_pick_window(gi, sums)[:ts, :] - dd).astype(BF16)

    tile = lambda c0: pl.BlockSpec((ts, pg), lambda g, i: (i, c0 + g))
    halo = lambda c0: pl.BlockSpec((HALO, pg), lambda g, i: (jnp.minimum((i + 1) * hb, last_halo), c0 + g))
    pw_spec = pl.BlockSpec((N_CHIPS, None, pg // N_CHIPS, pg), lambda g, i: (0, g, 0, 0))
    return pl.pallas_call(
        body, name="pool_bwd", grid=(N_GROUPS, nt),
        in_specs=[tile(0), halo(0), tile(z0), halo(z0), tile(0), halo(0), tile(0), pw_spec,
                  pl.BlockSpec((1, pg), lambda g, i: (0, g)), ANY],
        out_specs=[pl.BlockSpec((2, ts, pg), lambda g, i: (2, i, g)), pw_spec,
                   pl.BlockSpec((1, pg), lambda g, i: (0, g))],
        out_shape=[jax.ShapeDtypeStruct(dproj.shape, BF16),
                   jax.ShapeDtypeStruct(pw_full.shape, F32), jax.ShapeDtypeStruct((1, p), F32)],
        input_output_aliases={9: 0},
        scratch_shapes=[pltpu.VMEM((ts + HALO, pg), F32)],
        compiler_params=_params(("parallel", "arbitrary")),
    )(dyp, dyp, proj, proj, t, t, d, pw_full, pool_scale.reshape(1, p), dproj)


def _merge_fwd(ya, yp, woa_full, wop_full, proj, gb_full, s, d, a, sides=()):
    sw = d // N_CHIPS
    tm, tn = _div(s, 512), _div(sw, 1024)
    per = sw // tn
    ga0, gp0 = (4 * a + 2 * a) // tn, (4 * a + 2 * a + d) // tn

    def body(ya_ref, yp_ref, wa_ref, wp_ref, ga_ref, gp_ref, gb_ref, a_out, b_out, m_out):
        av = jnp.dot(ya_ref[...], wa_ref[...], preferred_element_type=F32)
        bv = jnp.dot(yp_ref[...], wp_ref[...], preferred_element_type=F32)
        a_out[...] = av
        b_out[...] = bv
        sa = _sigmoid(ga_ref[...] + gb_ref[0:1, :])
        sp = _sigmoid(gp_ref[...] + gb_ref[1:2, :])
        m_out[...] = (sa * av + sp * bv).astype(BF16)

    act = pl.BlockSpec((tm, a), lambda j, i: (i, 0))
    wgt = pl.BlockSpec((None, a, tn), lambda j, i: (j // per, 0, j % per))
    out = pl.BlockSpec((tm, tn), lambda j, i: (i, j))
    return _run("merge_fwd", list(sides), dict(
        body=body, grid=(d // tn, s // tm),
        in_specs=[act, act, wgt, wgt,
                  pl.BlockSpec((tm, tn), lambda j, i: (i, ga0 + j)),
                  pl.BlockSpec((tm, tn), lambda j, i: (i, gp0 + j)),
                  pl.BlockSpec((None, 2, tn), lambda j, i: (j // per, 0, j % per))],
        out_specs=[out, out, out],
        out_shape=[jax.ShapeDtypeStruct((s, d), F32), jax.ShapeDtypeStruct((s, d), F32),
                   jax.ShapeDtypeStruct((s, d), BF16)],
        scratch_shapes=[], operands=[ya, yp, woa_full, wop_full, proj, proj, gb_full], sem=("parallel", "parallel")))


def _out_proj(mb, wo, x, s, d):
    tm, tn, tk = _div(s, 1024), _div(d, 1024), _div(d, 4096)

    def epilogue(res, ex, outs):
        outs[0][...] = res + ex[0][...]

    tile = pl.BlockSpec((tm, tn), lambda i, j, k: (i, j))
    return _matmul(
        "out_proj", NN, (s // tm, d // tn, d // tk),
        mb, pl.BlockSpec((tm, tk), lambda i, j, k: (i, k)),
        wo, pl.BlockSpec((tk, tn), lambda i, j, k: (k, j)),
        (tm, tn), [jax.ShapeDtypeStruct((s, d), F32)], [tile], extra=(x,), extra_specs=(tile,),
        epilogue=epilogue)[0][0]


def _loss_head(x2, target, final_gain):
    s, d = x2.shape
    tr = _div(s, 256)

    def body(x_ref, t_ref, g_ref, loss_ref, dx_ref, dxb_ref, dg_ref):
        @pl.when(pl.program_id(0) == 0)
        def _():
            dg_ref[...] = jnp.zeros_like(dg_ref)

        xv = x_ref[...]
        g = g_ref[...]
        r = lax.rsqrt(jnp.mean(xv * xv, axis=-1, keepdims=True) + EPS)
        xn = xv * r
        e = xn * g - t_ref[...]
        loss_ref[...] = 0.5 * jnp.mean(e * e, axis=-1, keepdims=True)
        dy = e / d
        dg_ref[...] += jnp.sum(dy * xn, axis=0, keepdims=True)
        dxn = dy * g
        dx = r * (dxn - xn * jnp.mean(dxn * xn, axis=-1, keepdims=True))
        dx_ref[...] = dx
        dxb_ref[...] = dx.astype(BF16)

    rows = pl.BlockSpec((tr, d), lambda i: (i, 0))
    vec = pl.BlockSpec((1, d), lambda i: (0, 0))
    return pl.pallas_call(
        body, name="loss_head", grid=(s // tr,),
        in_specs=[rows, rows, vec], out_specs=[pl.BlockSpec((tr, 1), lambda i: (i, 0)), rows, rows, vec],
        out_shape=[jax.ShapeDtypeStruct((s, 1), F32), jax.ShapeDtypeStruct((s, d), F32),
                   jax.ShapeDtypeStruct((s, d), BF16), jax.ShapeDtypeStruct((1, d), F32)],
        compiler_params=_params(("arbitrary",)),
    )(x2, target, final_gain.reshape(1, d))


N_SLOTS = 10


def _slot(seg):
    t = seg - 6
    return jnp.where(seg < 6, seg, 6 + 2 * (t % 2) + t // 2)


def _merge_bwd(dxb, wo, a_val, b_val, proj, gb_full, s, d, a):
    sw = d // N_CHIPS
    tm, tn, tk = _div(s, 256), _div(sw, 1024), d
    per = sw // tn
    per_slot = a // tn
    ga0, gp0 = (4 * a + 2 * a) // tn, (4 * a + 2 * a + d) // tn

    def epilogue(dm, ex, outs):
        a_ref, b_ref, ga_ref, gp_ref, gb_ref = ex
        da_ref, db_ref, dg_ref, dgb_ref = outs
        sa = _sigmoid(ga_ref[...] + gb_ref[0:1, :])
        sp = _sigmoid(gp_ref[...] + gb_ref[1:2, :])
        da_ref[...] = (dm * sa).astype(BF16)
        db_ref[...] = (dm * sp).astype(BF16)
        dga = dm * a_ref[...] * sa * (1.0 - sa)
        dgp = dm * b_ref[...] * sp * (1.0 - sp)
        dg_ref[0] = dga.astype(BF16)
        dg_ref[1] = dgp.astype(BF16)

        @pl.when(pl.program_id(1) == 0)
        def _():
            dgb_ref[...] = jnp.zeros_like(dgb_ref)

        dgb_ref[0:1, :] += jnp.sum(dga, axis=0, keepdims=True)
        dgb_ref[1:2, :] += jnp.sum(dgp, axis=0, keepdims=True)

    tile = pl.BlockSpec((tm, tn), lambda j, i, k: (i, j))
    sd = jax.ShapeDtypeStruct((s, d), BF16)
    return _matmul(
        "merge_bwd", NT, (d // tn, s // tm, d // tk),
        dxb, pl.BlockSpec((tm, tk), lambda j, i, k: (i, k)),
        wo, pl.BlockSpec((tn, tk), lambda j, i, k: (j, k)),
        (tm, tn), [sd, sd, jax.ShapeDtypeStruct((N_SLOTS, s, a), BF16), jax.ShapeDtypeStruct((2, d), F32)],
        [tile, tile, pl.BlockSpec((2, tm, tn), lambda j, i, k: (3 + j // per_slot, i, j % per_slot)),
         pl.BlockSpec((2, tn), lambda j, i, k: (0, j))],
        extra=(a_val, b_val, proj, proj, gb_full),
        extra_specs=(tile, tile, pl.BlockSpec((tm, tn), lambda j, i, k: (i, ga0 + j)),
                     pl.BlockSpec((tm, tn), lambda j, i, k: (i, gp0 + j)),
                     pl.BlockSpec((None, 2, tn), lambda j, i, k: (j // per, 0, j % per))),
        epilogue=epilogue, accumulate_outs=True)[0]


def _weight_grad(name, act, dout, shard_cols, slots=False, piece=None, sides=()):
    s, kdim = act.shape
    n = dout.shape[0] * dout.shape[2] if slots else dout.shape[1]
    row_tile = lambda i: i
    if shard_cols:
        sw = n // N_CHIPS
        tm, tn = _div(kdim, 1024), _div(math.gcd(sw, dout.shape[2]) if slots else sw, 1024)
        per = sw // tn
        if piece is not None:
            tm = kdim // (2 * piece[1])
            kdim = 2 * tm
            row_tile = lambda i: i * piece[1] + piece[0]
        shape = (N_CHIPS, kdim, sw)
        out = pl.BlockSpec((None, tm, tn), lambda i, j, k: (j // per, i, j % per))
    else:
        sh = kdim // N_CHIPS
        tm, tn = _div(sh, 1024), _div(n, 1024)
        per = sh // tm
        shape = (N_CHIPS, sh, n)
        out = pl.BlockSpec((None, tm, tn), lambda i, j, k: (i // per, i % per, j))
    tk = _div(s, 4096)
    if slots:
        per_slot = dout.shape[2] // tn
        dout_spec = pl.BlockSpec((None, tk, tn), lambda i, j, k: (_slot(j // per_slot), k, j % per_slot))
    else:
        dout_spec = pl.BlockSpec((tk, tn), lambda i, j, k: (k, j))
    return _matmul(
        name, TN, (kdim // tm, n // tn, s // tk),
        act, pl.BlockSpec((tk, tm), lambda i, j, k: (k, row_tile(i))), dout, dout_spec,
        (tm, tn), [jax.ShapeDtypeStruct(shape, F32), jax.ShapeDtypeStruct(shape, BF16)], [out, out], sides=sides)


def _norm_in_bwd(x, dh, dx2, gain):
    s, d = x.shape
    tr = _div(s, 256)

    def body(x_ref, dh_ref, dx2_ref, g_ref, gx_ref, dg_ref):
        @pl.when(pl.program_id(0) == 0)
        def _():
            dg_ref[...] = jnp.zeros_like(dg_ref)

        xv = x_ref[...]
        r = lax.rsqrt(jnp.mean(xv * xv, axis=-1, keepdims=True) + EPS)
        xn = xv * r
        dhv = dh_ref[...]
        dg_ref[...] += jnp.sum(dhv * xn, axis=0, keepdims=True)
        dxn = dhv * g_ref[...]
        gx_ref[...] = r * (dxn - xn * jnp.mean(dxn * xn, axis=-1, keepdims=True)) + dx2_ref[...]

    rows = pl.BlockSpec((tr, d), lambda i: (i, 0))
    vec = pl.BlockSpec((1, d), lambda i: (0, 0))
    return pl.pallas_call(
        body, name="norm_in_bwd", grid=(s // tr,),
        in_specs=[rows, rows, rows, vec], out_specs=[rows, vec],
        out_shape=[jax.ShapeDtypeStruct((s, d), F32), jax.ShapeDtypeStruct((1, d), F32)],
        compiler_params=_params(("arbitrary",)),
    )(x, dh, dx2, gain.reshape(1, d))


def _pack(vectors):
    flat = jnp.concatenate([v.reshape(-1).astype(F32) for v in vectors])
    rows = -(-flat.shape[0] // 1024) * 8
    return jnp.pad(flat, (0, rows * 128 - flat.shape[0])).reshape(rows, 128)


def _unpack(packed, like):
    flat, out, at = packed.reshape(-1), [], 0
    for v in like:
        out.append(flat[at:at + v.size].reshape(v.shape))
        at += v.size
    return out


def _small_adamw(g, w, m, v):
    def body(g_ref, w_ref, m_ref, v_ref, d_ref, mo_ref, vo_ref):
        delta, m2, v2 = _adamw_math(w_ref[...], g_ref[...], m_ref[...], v_ref[...])
        d_ref[...] = delta
        mo_ref[...] = m2
        vo_ref[...] = v2

    return pl.pallas_call(body, name="small_adamw", out_shape=[jax.ShapeDtypeStruct(g.shape, F32)] * 3)(g, w, m, v)


def kernel(x, norm_gain, w_in, rel_bias, pool_w, pool_scale, w_out_attn, w_out_pool, gate_bias, w_out, final_gain, loss_target, m_norm_gain, m_w_in, m_rel_bias, m_pool_w, m_pool_scale, m_w_out_attn, m_w_out_pool, m_gate_bias, m_w_out, m_final_gain, v_norm_gain, v_w_in, v_rel_bias, v_pool_w, v_pool_scale, v_w_out_attn, v_w_out_pool, v_gate_bias, v_w_out, v_final_gain):
    _, s, d = x.shape
    a = p = d // 2
    n_in = w_in.shape[1] * N_CHIPS
    sw_in = w_in.shape[1]
    pg = p // N_GROUPS
    xs = x.reshape(s, d)
    target = loss_target.reshape(s, d)
    c_arr = lax.axis_index("c").astype(jnp.int32).reshape(1)
    chip = 2 * lax.axis_index("x") + lax.axis_index("y")

    hb = _norm_in(xs, norm_gain)
    tm, tn, tk = _div(s, 1024), _div(sw_in, 1024), _div(d, 4096)
    per_in = sw_in // tn
    cx, cy = lax.axis_index("x"), lax.axis_index("y")
    order = jnp.stack([2 * cx + cy, 2 * (1 - cx) + cy, 2 * cx + (1 - cy), 2 * (1 - cx) + (1 - cy)]).astype(jnp.int32)

    def in_proj(name, first, count, weights, sides, carry=None):
        if weights.ndim == 3:
            w_spec = pl.BlockSpec((None, tk, tn), lambda i, j, k, o: (o[first + j // per_in], k, j % per_in))
        else:
            w_spec = pl.BlockSpec((tk, tn), lambda i, j, k, o: (k, j))
        return _matmul(
            name, NN, (s // tm, count * per_in, d // tk),
            hb, pl.BlockSpec((tm, tk), lambda i, j, k, o: (i, k)), weights, w_spec,
            (tm, tn), [jax.ShapeDtypeStruct((s, n_in), F32)],
            [pl.BlockSpec((tm, tn), lambda i, j, k, o: (i, o[first + j // per_in] * per_in + j % per_in))],
            sides=sides, prefetch=order, carry=carry)

    w_in_b = w_in.astype(BF16)
    (proj,), ((win_near,),) = in_proj("in_proj_own", 0, 1, w_in_b, [_gather_side([w_in_b], [True], peers=(0, 1))])
    (proj,), ((win_far,),) = in_proj("in_proj_near", 1, 2, win_near, [_relay_far_side(win_near)], carry=proj)
    pw_b, woa_b, wop_b, wo_b = (w.astype(BF16) for w in (pool_w, w_out_attn, w_out_pool, w_out))
    (proj,), ((pw_full, gb_full),) = in_proj(
        "in_proj_far", 3, 1, win_far, [_gather_side([pw_b, gate_bias], [False, False])], carry=proj)
    win_full = _with_own(_with_own(win_near, win_far, order[3]), w_in_b, chip)
    pw_full, gb_full = _with_own(pw_full, pw_b, chip), _with_own(gb_full, gate_bias, chip)
    table = _bias_table(rel_bias)
    (o_attn, ya), ((woa_full, wop_full),) = _attention_fwd(
        proj, table, s, a, [_gather_side([woa_b, wop_b], [True, True])])
    woa_full, wop_full = _with_own(woa_full, woa_b, chip), _with_own(wop_full, wop_b, chip)
    d_pool, t_pool, yp = _pool_fwd(proj, pw_full, pool_scale, s, a, p)
    (a_val, b_val, mb), ((wo_full,),) = _merge_fwd(ya, yp, woa_full, wop_full, proj, gb_full, s, d, a,
                                                  [_gather_side([wo_b], [True])])
    wo_mat = _with_own(wo_full, wo_b, chip).reshape(d, d)
    loss_rows, dx2, dx2b, g_final = _loss_head(_out_proj(mb, wo_mat, xs, s, d), target, final_gain)
    loss = lax.psum(jnp.sum(loss_rows), ("x", "y", "c"))

    da, db, dproj, g_gate_full = _merge_bwd(dx2b, wo_mat, a_val, b_val, proj, gb_full, s, d, a)
    (gwo, gwo_b), _ = _weight_grad("grad_w_out", mb, dx2b, shard_cols=False)
    (gwoa, gwoa_b), _ = _weight_grad("grad_w_out_attn", ya, da, shard_cols=True)
    (gwop, gwop_b), _ = _weight_grad("grad_w_out_pool", yp, db, shard_cols=True)
    early = ["w_out_attn", "w_out_pool", "w_out"]

    sw = d // N_CHIPS
    tm, tn, tk = _div(s, 1024), _div(a, 1024), _div(sw, 1024)
    per_o = sw // tk

    def back_through(name, dout, w_full, sides=()):
        nkb = d // tk

        def body(*refs):
            acts, wgts, o_ref = refs[:nkb], refs[nkb:2 * nkb], refs[2 * nkb]
            acc = lax.dot_general(acts[0][...], wgts[0][...], NT, preferred_element_type=F32)
            for q in range(1, nkb):
                acc = acc + lax.dot_general(acts[q][...], wgts[q][...], NT, preferred_element_type=F32)
            o_ref[...] = acc

        act = lambda q: pl.BlockSpec((tm, tk), lambda i, j: (i, q))
        wgt = lambda q: pl.BlockSpec((None, tn, tk), lambda i, j: (q // per_o, j, q % per_o))
        return _run(name, list(sides), dict(
            body=body, grid=(s // tm, a // tn),
            in_specs=[act(q) for q in range(nkb)] + [wgt(q) for q in range(nkb)],
            out_specs=[pl.BlockSpec((tm, tn), lambda i, j: (i, j))], out_shape=[jax.ShapeDtypeStruct((s, a), F32)],
            scratch_shapes=[], operands=[dout] * nkb + [w_full] * nkb, sem=("parallel", "parallel")))

    (dya,), (early_sib,) = back_through("grad_y_attn", da, woa_full, [_swap_side([gwoa_b, gwop_b, gwo_b])])
    early_pair = [_pair_sum("pair_sum_" + n, g, r, c_arr) for n, g, r in zip(early, [gwoa, gwop, gwo], early_sib)]
    (dyp,), _ = back_through("grad_y_pool", db, wop_full)
    dproj, gpw, g_pscale = _pool_bwd(proj, dyp, t_pool, d_pool, pw_full, pool_scale, dproj, s, a, p)
    dproj, dtable = _attention_bwd(proj, o_attn, dya, table, dproj, s, a)
    g_rel = jax.vjp(_bias_table, rel_bias)[1](dtable)[0]
    gpw3 = gpw.reshape(N_CHIPS, pg, pg)

    (gw0, gw0_b), (early_chips,) = _weight_grad("grad_w_in_0", hb, dproj, shard_cols=True, slots=True, piece=(0, 2),
                                                sides=[_scatter_side(early_pair)])
    (gw1, gw1_b), (sib0,) = _weight_grad("grad_w_in_1", hb, dproj, shard_cols=True, slots=True, piece=(1, 2),
                                         sides=[_swap_side([gw0_b, gpw3.astype(BF16)])])
    def with_own_sum(from_chips, pair):
        return _with_own(from_chips, lax.dynamic_index_in_dim(pair, chip, 0, keepdims=False), chip)

    early_halves = [_chip_sum("chip_sum_" + n, with_own_sum(r, q), c_arr)
                    for n, r, q in zip(early, early_chips, early_pair)]
    pair0 = [_pair_sum("pair_sum_w_in_0", gw0, sib0[0], c_arr), _pair_sum("pair_sum_pool_w", gpw3, sib0[1], c_arr)]

    tm, tn, tk = _div(s, 1024), _div(d, 1024), _div(a // 2, 1024)
    per_k, per_slot = sw_in // tk, a // tk
    nk_half = n_in // tk // 2
    steps = nk_half // 2

    def grad_h(name, k0, sides, plus=None):
        def body(a0, a1, b0, b1, *rest):
            o_ref, acc = rest[-2], rest[-1]
            k = pl.program_id(2)

            @pl.when(k == 0)
            def _():
                acc[...] = jnp.zeros_like(acc) if plus is None else rest[0][...]

            acc[...] += (lax.dot_general(a0[...], b0[...], NT, preferred_element_type=F32)
                         + lax.dot_general(a1[...], b1[...], NT, preferred_element_type=F32))

            @pl.when(k == steps - 1)
            def _():
                o_ref[...] = acc[...]

        def act(which):
            return pl.BlockSpec((None, tm, tk), lambda i, j, k: (
                _slot((2 * k + which + k0) // per_slot), i, (2 * k + which + k0) % per_slot))

        def wgt(which):
            return pl.BlockSpec((None, tn, tk), lambda i, j, k: (
                (2 * k + which + k0) // per_k, j, (2 * k + which + k0) % per_k))

        tile = pl.BlockSpec((tm, tn), lambda i, j, k: (i, j))
        return _run(name, list(sides), dict(
            body=body, grid=(s // tm, d // tn, steps),
            in_specs=[act(0), act(1), wgt(0), wgt(1)] + ([] if plus is None else [tile]), out_specs=[tile],
            out_shape=[jax.ShapeDtypeStruct((s, d), F32)], scratch_shapes=[pltpu.VMEM((tm, tn), F32)],
            operands=[dproj, dproj, win_full, win_full] + ([] if plus is None else [plus]),
            sem=("parallel", "parallel", "arbitrary")))

    (dh_a,), (chips0, (sib1,), early_grads) = grad_h(
        "grad_h_a", 0, [_scatter_side(pair0), _swap_side([gw1_b]), _share_side(early_halves)])
    pair1 = _pair_sum("pair_sum_w_in_1", gw1, sib1, c_arr)
    (dh,), ((chips1,),) = grad_h("grad_h_b", nk_half, [_scatter_side([pair1])], plus=dh_a)
    gwin_half = _chip_sum("chip_sum_w_in_0", with_own_sum(chips0[0], pair0[0]), c_arr, piece=0, pieces=2)
    gwin_half = _chip_sum("chip_sum_w_in_1", with_own_sum(chips1, pair1), c_arr, piece=1, pieces=2, into=gwin_half)
    gpw_half = _chip_sum("chip_sum_pool_w", with_own_sum(chips0[1], pair0[1]), c_arr)
    grad_x, g_norm = _norm_in_bwd(xs, dh, dx2, norm_gain)
    _, (late_grads,) = _run("reduce_share_halves", [_share_side([gwin_half, gpw_half])])

    names = ["w_in", "w_out_attn", "w_out_pool", "w_out", "pool_w"]
    grads = [late_grads[0], *early_grads, late_grads[1]]
    big = {}
    weights = [w_in, w_out_attn, w_out_pool, w_out, pool_w.reshape(pg, pg)]
    ms = [m_w_in, m_w_out_attn, m_w_out_pool, m_w_out, m_pool_w.reshape(pg, pg)]
    vs = [v_w_in, v_w_out_attn, v_w_out_pool, v_w_out, v_pool_w.reshape(pg, pg)]
    for n, g, w, m, v in zip(names, grads, weights, ms, vs):
        big[n] = [r.reshape(pool_w.shape) if n == "pool_w" else r for r in _adamw("adamw_" + n, g, w, m, v)]

    small_like = [norm_gain, final_gain, pool_scale, rel_bias, jnp.zeros((2, d), F32)]
    summed = _sum_slots("small_grads_sum", _all_to_all_small(_pack([g_norm, g_final, g_pscale, g_rel, g_gate_full])))
    g_norm_t, g_final_t, g_pscale_t, g_rel_t, g_gate_t = _unpack(summed, small_like)
    g_gate_t = lax.dynamic_slice_in_dim(g_gate_t, chip * sw, sw, axis=1)
    small_g = [g_norm_t, g_final_t, g_pscale_t, g_rel_t, g_gate_t]
    small_w = [norm_gain, final_gain, pool_scale, rel_bias, gate_bias]
    small_m = [m_norm_gain, m_final_gain, m_pool_scale, m_rel_bias, m_gate_bias]
    small_v = [v_norm_gain, v_final_gain, v_pool_scale, v_rel_bias, v_gate_bias]
    packed = _small_adamw(_pack(small_g), _pack(small_w), _pack(small_m), _pack(small_v))
    sd, sm, sv = [_unpack(t, small_w) for t in packed]
    small = {n: [small_g[i], sd[i], sm[i], sv[i]]
             for i, n in enumerate(["norm_gain", "final_gain", "pool_scale", "rel_bias", "gate_bias"])}

    every = {**big, **small}
    order = ["norm_gain", "w_in", "rel_bias", "pool_w", "pool_scale", "w_out_attn", "w_out_pool", "gate_bias",
             "w_out", "final_gain"]
    return (loss, grad_x.reshape(x.shape), *[every[n][0] for n in order], *[every[n][1] for n in order],
            *[every[n][2] for n in order], *[every[n][3] for n in order])
```

```python
import math

import jax
import jax.numpy as jnp
from jax import lax
from jax.experimental import pallas as pl
from jax.experimental.pallas import tpu as pltpu

F32 = jnp.float32
BF16 = jnp.bfloat16
MESH = pl.DeviceIdType.MESH
ANY = pl.BlockSpec(memory_space=pl.ANY)

N_CHIPS = 4
N_DEV = 8
CHUNK = 64
N_LEFT_CHUNKS = 8
PAD = N_LEFT_CHUNKS * CHUNK
HEAD_DIM = 128
MAX_REL = 128
POOL_WINDOWS = (2, 4, 8, 16)
N_GROUPS = len(POOL_WINDOWS)
HALO = 16
Q_GROUP = 4 * CHUNK
K_GROUP = Q_GROUP + PAD
NEG = -1e30
EPS = 1e-6
ADAM_LR, ADAM_B1, ADAM_B2, ADAM_EPS, ADAM_WD, ADAM_STEP = 0.001, 0.9, 0.999, 1e-08, 0.01, 10
VMEM_LIMIT = 56 * 1024 * 1024

NN = (((1,), (0,)), ((), ()))
NT = (((1,), (1,)), ((), ()))
TN = (((0,), (0,)), ((), ()))


def _div(n, pref):
    if n <= pref:
        return n
    for t in range(pref - pref % 128, 0, -128):
        if n % t == 0:
            return t
    raise ValueError((n, pref))


def _params(sem, **kw):
    return pltpu.CompilerParams(dimension_semantics=sem, vmem_limit_bytes=VMEM_LIMIT, **kw)


def _sigmoid(z):
    return jax.nn.sigmoid(z)


def _silu_and_grad(z):
    sg = _sigmoid(z)
    return z * sg, sg * (1.0 + z * (1.0 - sg))


def _matmul(name, dn, grid, a, a_spec, b, b_spec, acc_shape, outs, out_specs, extra=(), extra_specs=(),
            epilogue=None, accumulate_outs=False, sides=(), prefetch=None, carry=None):
    nk = grid[2]
    aliases = {}
    if carry is not None:
        aliases = {2 + len(extra): 0}
        extra, extra_specs = (*extra, carry), (*extra_specs, ANY)
    ne, no = len(extra), len(outs)

    def finish(res, ex, out_refs):
        if epilogue is None:
            for o in out_refs:
                o[...] = res.astype(o.dtype)
        else:
            epilogue(res, ex, out_refs)

    def body(*refs):
        a_ref, b_ref = refs[0], refs[1]
        ex = refs[2:2 + ne]
        out_refs = refs[2 + ne:2 + ne + no]
        if nk == 1:
            finish(lax.dot_general(a_ref[...], b_ref[...], dn, preferred_element_type=F32), ex, out_refs)
            return
        acc = refs[-1]
        k = pl.program_id(2)

        @pl.when(k == 0)
        def _():
            acc[...] = jnp.zeros_like(acc)

        acc[...] += lax.dot_general(a_ref[...], b_ref[...], dn, preferred_element_type=F32)

        @pl.when(k == nk - 1)
        def _():
            finish(acc[...], ex, out_refs)

    sem = ("arbitrary",) * 3 if accumulate_outs else ("parallel", "parallel", "arbitrary")
    return _run(name, list(sides), dict(
        body=body, grid=grid, in_specs=[a_spec, b_spec, *extra_specs], out_specs=list(out_specs),
        out_shape=list(outs), scratch_shapes=[] if nk == 1 else [pltpu.VMEM(acc_shape, F32)],
        operands=[a, b, *extra], sem=sem, aliases=aliases, prefetch=prefetch))


def _place():
    x, y, c = lax.axis_index("x"), lax.axis_index("y"), lax.axis_index("c")
    chips = [(1 - x, y), (x, 1 - y), (1 - x, 1 - y)]
    return x, y, c, chips


N_STREAMS = 1


class _Copies:
    def __init__(self, cps):
        self.cps = cps

    def start(self):
        for cp in self.cps:
            cp.start()

    def wait_send(self):
        for cp in self.cps:
            cp.wait_send()

    def wait_recv(self):
        for cp in self.cps:
            cp.wait_recv()

    def wait(self):
        for cp in self.cps:
            cp.wait()


def _remote(src, dst, send_sems, recv_sems, k, dev):
    lead = src.shape[0]
    n = N_STREAMS
    while n > 1 and (lead % n or (len(src.shape) == 2 and (lead // n) % 16)):
        n //= 2
    step = lead // n
    return _Copies([pltpu.make_async_remote_copy(
        src_ref=src.at[pl.ds(i * step, step)], dst_ref=dst.at[pl.ds(i * step, step)],
        send_sem=send_sems.at[k * N_STREAMS + i], recv_sem=recv_sems.at[k * N_STREAMS + i],
        device_id=dev, device_id_type=MESH) for i in range(n)])


class _Side:
    def __init__(self, ins, out_shapes, n_remote, n_local, start, finish, aliases=None):
        self.ins, self.out_shapes = list(ins), list(out_shapes)
        self.n_remote, self.n_local = max(n_remote, 1), max(n_local, 1)
        self.start, self.finish, self.aliases = start, finish, aliases or {}


def _run(name, sides, compute=None):
    cm = compute or dict(body=None, grid=(), in_specs=[], out_specs=[], out_shape=[], scratch_shapes=[], operands=[])
    grid = tuple(cm["grid"])
    ni, no, ns = len(cm["operands"]), len(cm["out_shape"]), len(cm["scratch_shapes"])
    n_in = [len(sd.ins) for sd in sides]
    n_out = [len(sd.out_shapes) for sd in sides]
    prefetch = cm.get("prefetch")
    shift = 0 if prefetch is None else 1

    def body(*refs):
        refs = refs[shift:]
        at = ni
        side_ins = []
        for n in n_in:
            side_ins.append(refs[at:at + n])
            at += n
        outs = refs[at:at + no]
        at += no
        side_outs = []
        for n in n_out:
            side_outs.append(refs[at:at + n])
            at += n
        scratch = refs[at:at + ns]
        at += ns
        sems = [refs[at + 3 * q:at + 3 * q + 3] for q in range(len(sides))]

        def each(step):
            for sd, i_, o_, m_ in zip(sides, side_ins, side_outs, sems):
                getattr(sd, step)(i_, o_, *m_)

        if not grid:
            each("start")
            each("finish")
            return
        first = last = None
        for ax, g in enumerate(grid):
            f, l = pl.program_id(ax) == 0, pl.program_id(ax) == g - 1
            first = f if first is None else first & f
            last = l if last is None else last & l
        if sides:
            pl.when(first)(lambda: each("start"))
        cm["body"](*refs[:ni], *outs, *scratch)
        if sides:
            pl.when(last)(lambda: each("finish"))

    aliases = {shift + i_: o_ for i_, o_ in (cm.get("aliases") or {}).items()}
    in_at, out_at = shift + ni, no
    for sd, a, b in zip(sides, n_in, n_out):
        for i_, o_ in sd.aliases.items():
            aliases[in_at + i_] = out_at + o_
        in_at, out_at = in_at + a, out_at + b
    scratch_shapes = list(cm["scratch_shapes"])
    for sd in sides:
        scratch_shapes += [pltpu.SemaphoreType.DMA((sd.n_remote * N_STREAMS,)),
                           pltpu.SemaphoreType.DMA((sd.n_remote * N_STREAMS,)), pltpu.SemaphoreType.DMA((sd.n_local,))]
    in_specs = list(cm["in_specs"]) + [ANY] * sum(n_in)
    out_specs = list(cm["out_specs"]) + [ANY] * sum(n_out)
    kw = dict(in_specs=in_specs, out_specs=out_specs, scratch_shapes=scratch_shapes)
    if grid:
        kw["grid"] = grid
    if prefetch is not None:
        kw = dict(grid_spec=pltpu.PrefetchScalarGridSpec(num_scalar_prefetch=1, **kw))
    if grid:
        kw["compiler_params"] = _params(("arbitrary",) * len(grid) if sides else cm["sem"])
    res = pl.pallas_call(
        body, name=name, out_shape=list(cm["out_shape"]) + [s for sd in sides for s in sd.out_shapes],
        input_output_aliases=aliases, **kw,
    )(*([] if prefetch is None else [prefetch]), *cm["operands"], *[a for sd in sides for a in sd.ins])
    res = list(res)
    side_res, at = [], no
    for n in n_out:
        side_res.append(res[at:at + n])
        at += n
    return res[:no], side_res


def _gather_side(shards, split, peers=(0, 1, 2)):
    n = len(shards)

    def plan(ins, outs, send_sems, recv_sems, _):
        x, y, c, chips = _place()
        me = 2 * x + y
        sibling = (x, y, 1 - c)
        direct, relays, arrivals = [], [], []
        for t in range(n):
            quarter = ins[t].shape[0] // 4
            for j in peers:
                cx, cy = chips[j]
                src_chip = 2 * cx + cy
                k = 12 * t + 4 * j
                if not split[t]:
                    direct.append(_remote(ins[t], outs[t].at[me], send_sems, recv_sems, k, (cx, cy, c)))
                    got = outs[t].at[src_chip]
                    arrivals.append(_remote(got, got, send_sems, recv_sems, k, (cx, cy, c)))
                    continue
                for r in range(2):
                    e = c ^ r
                    out_q = pl.ds((2 * c + e) * quarter, quarter)
                    direct.append(_remote(ins[t].at[out_q], outs[t].at[me, out_q], send_sems, recv_sems, k + r,
                                          (cx, cy, e)))
                    got = outs[t].at[src_chip, pl.ds((2 * e + c) * quarter, quarter)]
                    relays.append((_remote(got, got, send_sems, recv_sems, k + r, (cx, cy, e)),
                                   _remote(got, got, send_sems, recv_sems, k + 2 + r, sibling)))
                    theirs = outs[t].at[src_chip, pl.ds((2 * e + 1 - c) * quarter, quarter)]
                    arrivals.append(_remote(theirs, theirs, send_sems, recv_sems, k + 2 + (1 - r), sibling))
        return direct, relays, arrivals

    def start(*refs):
        for cp in plan(*refs)[0]:
            cp.start()

    def finish(*refs):
        direct, relays, arrivals = plan(*refs)
        for landed, onward in relays:
            landed.wait_recv()
            onward.start()
        for cp in arrivals:
            cp.wait_recv()
        for cp in direct + [onward for _, onward in relays]:
            cp.wait_send()

    return _Side(shards, [jax.ShapeDtypeStruct((N_CHIPS,) + s.shape, s.dtype) for s in shards], 12 * n, 0,
                 start, finish)


def _with_own(slots, block, chip):
    return lax.dynamic_update_slice(slots, block[None], (chip,) + (0,) * block.ndim)


def _relay_far_side(full):
    def plan(ins, outs, send_sems, recv_sems, __):
        x, y, c, _ = _place()
        src, far = ins[0], outs[0]
        eighth = far.shape[0] // 8
        x_id, y_id = 2 * (1 - x) + y, 2 * x + (1 - y)
        sends, landed, onward, from_sibling = [], [], [], []
        for side, (nb_x, nb_y, block) in enumerate([(1 - x, y, y_id), (x, 1 - y, x_id)]):
            for r in range(2):
                e = c ^ r
                out_rows = pl.ds((4 * side + 2 * c + e) * eighth, eighth)
                sends.append(_remote(src.at[block, out_rows], far.at[out_rows], send_sems, recv_sems, 2 * side + r,
                                     (nb_x, nb_y, e)))
                got = far.at[pl.ds((4 * side + 2 * e + c) * eighth, eighth)]
                landed.append(_remote(got, got, send_sems, recv_sems, 2 * side + r, (nb_x, nb_y, e)))
                onward.append(_remote(got, got, send_sems, recv_sems, 4 + 2 * side + r, (x, y, 1 - c)))
                theirs = far.at[pl.ds((4 * side + 2 * e + 1 - c) * eighth, eighth)]
                from_sibling.append(_remote(theirs, theirs, send_sems, recv_sems, 4 + 2 * side + (1 - r),
                                            (x, y, 1 - c)))
        return sends, landed, onward, from_sibling

    def start(*refs):
        for cp in plan(*refs)[0]:
            cp.start()

    def finish(*refs):
        sends, landed, onward, from_sibling = plan(*refs)
        for got, fwd in zip(landed, onward):
            got.wait_recv()
            fwd.start()
        for cp in from_sibling:
            cp.wait_recv()
        for cp in sends + onward:
            cp.wait_send()

    return _Side([full], [jax.ShapeDtypeStruct(full.shape[1:], full.dtype)], 8, 0, start, finish)


def _swap_side(parts):
    n = len(parts)

    def plan(ins, outs, send_sems, recv_sems, _):
        x, y, c, _ = _place()
        cps = []
        for t in range(n):
            half = ins[t].shape[1] // 2
            cps.append(_remote(ins[t].at[:, pl.ds((1 - c) * half, half)], outs[t], send_sems, recv_sems, t,
                               (x, y, 1 - c)))
        return cps

    def start(*refs):
        for cp in plan(*refs):
            cp.start()

    def finish(*refs):
        for cp in plan(*refs):
            cp.wait()

    return _Side(parts, [jax.ShapeDtypeStruct((p.shape[0], p.shape[1] // 2, p.shape[2]), p.dtype) for p in parts],
                 n, 0, start, finish)


def _scatter_side(parts):
    n = len(parts)

    def plan(ins, outs, send_sems, recv_sems, _):
        x, y, c, chips = _place()
        me = 2 * x + y
        sends, arrivals = [], []
        for t in range(n):
            for j, (cx, cy) in enumerate(chips):
                sends.append(_remote(ins[t].at[2 * cx + cy], outs[t].at[me], send_sems, recv_sems, 3 * t + j,
                                     (cx, cy, c)))
                got = outs[t].at[2 * cx + cy]
                arrivals.append(_remote(got, got, send_sems, recv_sems, 3 * t + j, (cx, cy, c)))
        return sends, arrivals

    def start(*refs):
        for cp in plan(*refs)[0]:
            cp.start()

    def finish(*refs):
        sends, arrivals = plan(*refs)
        for cp in arrivals:
            cp.wait_recv()
        for cp in sends:
            cp.wait_send()

    return _Side(parts, [jax.ShapeDtypeStruct(p.shape, p.dtype) for p in parts], 3 * n, 0, start, finish)


def _share_side(fulls, piece=0, pieces=1):
    n = len(fulls)

    def plan(_, outs, send_sems, recv_sems, __):
        x, y, c, _ = _place()
        cps = []
        for t in range(n):
            half = outs[t].shape[0] // (2 * pieces)
            mine = outs[t].at[pl.ds((pieces * c + piece) * half, half)]
            theirs = outs[t].at[pl.ds((pieces * (1 - c) + piece) * half, half)]
            cps.append((_remote(mine, mine, send_sems, recv_sems, t, (x, y, 1 - c)),
                        _remote(theirs, theirs, send_sems, recv_sems, t, (x, y, 1 - c))))
        return cps

    def start(*refs):
        for cp, _ in plan(*refs):
            cp.start()

    def finish(*refs):
        for cp, rv in plan(*refs):
            rv.wait_recv()
            cp.wait_send()

    return _Side(fulls, [jax.ShapeDtypeStruct(f.shape, f.dtype) for f in fulls], n, 0, start, finish,
                 aliases={t: t for t in range(n)})


def _all_to_all_small(packed):
    def body(in_ref, out_ref, send_sems, recv_sems, local_sem):
        x, y, c, _ = _place()
        me = 4 * x + 2 * y + c
        own = pltpu.make_async_copy(in_ref, out_ref.at[me], local_sem)
        own.start()
        cps, rvs = [], []
        for k in range(1, N_DEV):
            fx, fy, fc = (k >> 2) & 1, (k >> 1) & 1, k & 1
            px, py, pc = x ^ fx, y ^ fy, c ^ fc
            cp = _remote(in_ref, out_ref.at[me], send_sems, recv_sems, k - 1, (px, py, pc))
            cp.start()
            cps.append(cp)
            got = out_ref.at[4 * px + 2 * py + pc]
            rvs.append(_remote(got, got, send_sems, recv_sems, k - 1, (px, py, pc)))
        for rv in rvs:
            rv.wait_recv()
        for cp in cps:
            cp.wait_send()
        own.wait()

    return pl.pallas_call(
        body, name="small_grads_exchange",
        in_specs=[ANY], out_specs=ANY,
        out_shape=jax.ShapeDtypeStruct((N_DEV,) + packed.shape, packed.dtype),
        scratch_shapes=[pltpu.SemaphoreType.DMA(((N_DEV - 1) * N_STREAMS,)),
                        pltpu.SemaphoreType.DMA(((N_DEV - 1) * N_STREAMS,)), pltpu.SemaphoreType.DMA],
    )(packed)


def _pair_sum(name, g, recv, c_arr):
    _, rows, cols = g.shape
    half = rows // 2
    tr, tc = _div(half, 512), _div(cols, 1024)
    nrb = half // tr

    def body(c_ref, g_ref, r_ref, o_ref):
        o_ref[...] = (g_ref[...] + r_ref[...].astype(F32)).astype(BF16)

    return pl.pallas_call(
        body, name=name,
        grid_spec=pltpu.PrefetchScalarGridSpec(
            num_scalar_prefetch=1, grid=(N_CHIPS, nrb, cols // tc),
            in_specs=[pl.BlockSpec((None, tr, tc), lambda s, i, j, c: (s, c[0] * nrb + i, j)),
                      pl.BlockSpec((None, tr, tc), lambda s, i, j, c: (s, i, j))],
            out_specs=pl.BlockSpec((None, tr, tc), lambda s, i, j, c: (s, i, j))),
        out_shape=jax.ShapeDtypeStruct(recv.shape, BF16),
        compiler_params=_params(("parallel", "parallel", "parallel")),
    )(c_arr, g, recv)


def _chip_sum(name, recv, c_arr, piece=0, pieces=1, into=None):
    _, half, cols = recv.shape
    tr, tc = _div(half, 512), _div(cols, 1024)
    nrb = half // tr

    def body(c_ref, r_ref, *rest):
        o_ref = rest[-1]
        acc = r_ref[0].astype(F32)
        for s in range(1, N_CHIPS):
            acc = acc + r_ref[s].astype(F32)
        o_ref[...] = acc

    return pl.pallas_call(
        body, name=name,
        grid_spec=pltpu.PrefetchScalarGridSpec(
            num_scalar_prefetch=1, grid=(nrb, cols // tc),
            in_specs=[pl.BlockSpec((N_CHIPS, tr, tc), lambda i, j, c: (0, i, j))] + ([] if into is None else [ANY]),
            out_specs=pl.BlockSpec((tr, tc), lambda i, j, c: ((pieces * c[0] + piece) * nrb + i, j))),
        out_shape=jax.ShapeDtypeStruct((2 * pieces * half, cols), F32),
        input_output_aliases={} if into is None else {2: 0},
        compiler_params=_params(("parallel", "parallel")),
    )(c_arr, recv, *([] if into is None else [into]))


def _adamw_math(w, g, m, v):
    m2 = ADAM_B1 * m + (1.0 - ADAM_B1) * g
    v2 = ADAM_B2 * v + (1.0 - ADAM_B2) * (g * g)
    m_hat = m2 / (1.0 - ADAM_B1 ** ADAM_STEP)
    v_hat = v2 / (1.0 - ADAM_B2 ** ADAM_STEP)
    delta = -ADAM_LR * (m_hat / (jnp.sqrt(v_hat) + ADAM_EPS) + ADAM_WD * w)
    return delta, m2, v2


def _adamw(name, g, w, m, v):
    rows, cols = g.shape
    tr, tc = _div(rows, 512), _div(cols, 1024)
    spec = pl.BlockSpec((tr, tc), lambda i, j: (i, j))

    def body(g_ref, w_ref, m_ref, v_ref, go_ref, d_ref, mo_ref, vo_ref):
        gg = g_ref[...]
        delta, m2, v2 = _adamw_math(w_ref[...], gg, m_ref[...], v_ref[...])
        go_ref[...] = gg
        d_ref[...] = delta
        mo_ref[...] = m2
        vo_ref[...] = v2

    return pl.pallas_call(
        body, name=name, grid=(rows // tr, cols // tc),
        in_specs=[spec] * 4, out_specs=[spec] * 4,
        out_shape=[jax.ShapeDtypeStruct(g.shape, F32)] * 4,
        compiler_params=_params(("parallel", "parallel")),
    )(g, w, m, v)


def _sum_slots(name, slots):
    def body(s_ref, o_ref):
        acc = s_ref[0]
        for d in range(1, N_DEV):
            acc = acc + s_ref[d]
        o_ref[...] = acc

    return pl.pallas_call(body, name=name, out_shape=jax.ShapeDtypeStruct(slots.shape[1:], F32))(slots)


def _norm_in(x, gain):
    s, d = x.shape
    tr = _div(s, 256)

    def body(x_ref, g_ref, h_ref):
        xv = x_ref[...]
        r = lax.rsqrt(jnp.mean(xv * xv, axis=-1, keepdims=True) + EPS)
        h_ref[...] = (xv * r * g_ref[...]).astype(BF16)

    return pl.pallas_call(
        body, name="norm_in", grid=(s // tr,),
        in_specs=[pl.BlockSpec((tr, d), lambda i: (i, 0)), pl.BlockSpec((1, d), lambda i: (0, 0))],
        out_specs=pl.BlockSpec((tr, d), lambda i: (i, 0)),
        out_shape=jax.ShapeDtypeStruct((s, d), BF16),
        compiler_params=_params(("parallel",)),
    )(x, gain.reshape(1, d))


def _bias_table(rel_bias):
    h = rel_bias.shape[0]
    span = 2 * CHUNK - 1
    lo, hi = -(CHUNK - 1), (N_LEFT_CHUNKS + 1) * CHUNK - 1
    by_rel = jnp.concatenate([rel_bias[:, MAX_REL + lo:],
                              jnp.broadcast_to(rel_bias[:, -1:], (h, hi - MAX_REL))], axis=1)
    vec = jnp.stack([by_rel[:, m * CHUNK:m * CHUNK + span] for m in range(N_LEFT_CHUNKS + 1)], axis=1)
    rev = jnp.concatenate([vec[..., ::-1], jnp.zeros(vec.shape[:2] + (1,), vec.dtype)], axis=-1)
    skew = jnp.tile(rev, (1, 1, CHUNK))[..., :CHUNK * span].reshape(h, N_LEFT_CHUNKS + 1, CHUNK, span)
    blocks = skew[..., CHUNK - 1:]
    off = jnp.full((h, CHUNK, CHUNK), NEG, rel_bias.dtype)
    rows = []
    for qi in range(Q_GROUP // CHUNK):
        dist = [N_LEFT_CHUNKS + qi - kj for kj in range(K_GROUP // CHUNK)]
        rows.append(jnp.concatenate([blocks[:, m] if 0 <= m <= N_LEFT_CHUNKS else off for m in dist], axis=-1))
    return jnp.concatenate(rows, axis=-2)


def _softmax(scores, tab_ref, r0, scale):
    sc = scores * scale + tab_ref[...]
    col = lax.broadcasted_iota(jnp.int32, sc.shape, 1)
    sc = jnp.where(col >= PAD - r0, sc, NEG)
    e = jnp.exp(sc - jnp.max(sc, axis=-1, keepdims=True))
    return e * (1.0 / jnp.sum(e, axis=-1, keepdims=True))


def _attention_fwd(proj, tables, s, a, sides=()):
    heads = a // HEAD_DIM
    scale = HEAD_DIM ** -0.5
    groups = s // Q_GROUP

    def body(q_ref, k_ref, v_ref, z_ref, tab_ref, o_ref, ya_ref, kp, vp):
        kp[0:PAD, :] = jnp.zeros((PAD, HEAD_DIM), BF16)
        vp[0:PAD, :] = jnp.zeros((PAD, HEAD_DIM), BF16)
        kp[PAD:, :] = k_ref[...].astype(BF16)
        vp[PAD:, :] = v_ref[...].astype(BF16)

        def group(g, carry):
            r0 = pl.multiple_of(g * Q_GROUP, Q_GROUP)
            q = q_ref[pl.ds(r0, Q_GROUP), :].astype(BF16)
            p = _softmax(lax.dot_general(q, kp[pl.ds(r0, K_GROUP), :], NT, preferred_element_type=F32),
                         tab_ref, r0, scale)
            o = jnp.dot(p.astype(BF16), vp[pl.ds(r0, K_GROUP), :], preferred_element_type=F32)
            o_ref[pl.ds(r0, Q_GROUP), :] = o
            z = z_ref[pl.ds(r0, Q_GROUP), :]
            ya_ref[pl.ds(r0, Q_GROUP), :] = (o * (z * _sigmoid(z))).astype(BF16)
            return carry

        lax.fori_loop(0, groups, group, 0)

    col = lambda seg: (lambda h: (0, seg * heads + h))
    blk = lambda seg: pl.BlockSpec((s, HEAD_DIM), col(seg))
    return _run("attention_fwd", list(sides), dict(
        body=body, grid=(heads,),
        in_specs=[blk(0), blk(1), blk(2), blk(3),
                  pl.BlockSpec((None, Q_GROUP, K_GROUP), lambda h: (h, 0, 0))],
        out_specs=[blk(0), blk(0)],
        out_shape=[jax.ShapeDtypeStruct((s, a), F32), jax.ShapeDtypeStruct((s, a), BF16)],
        scratch_shapes=[pltpu.VMEM((PAD + s, HEAD_DIM), BF16), pltpu.VMEM((PAD + s, HEAD_DIM), BF16)],
        operands=[proj, proj, proj, proj, tables], sem=("parallel",)))


def _attention_bwd(proj, o, dya, tables, dproj, s, a):
    heads = a // HEAD_DIM
    scale = HEAD_DIM ** -0.5
    groups = s // Q_GROUP

    def body(q_ref, k_ref, v_ref, z_ref, o_ref, dy_ref, tab_ref, _, dp_ref, dtab_ref, kp, vp, dkp, dvp):
        kp[0:PAD, :] = jnp.zeros((PAD, HEAD_DIM), BF16)
        vp[0:PAD, :] = jnp.zeros((PAD, HEAD_DIM), BF16)
        kp[PAD:, :] = k_ref[...].astype(BF16)
        vp[PAD:, :] = v_ref[...].astype(BF16)
        dkp[...] = jnp.zeros_like(dkp)
        dvp[...] = jnp.zeros_like(dvp)
        dtab_ref[...] = jnp.zeros_like(dtab_ref)

        def group(g, carry):
            r0 = pl.multiple_of(g * Q_GROUP, Q_GROUP)
            rows = pl.ds(r0, Q_GROUP)
            band = pl.ds(r0, K_GROUP)
            q = q_ref[rows, :].astype(BF16)
            z = z_ref[rows, :]
            dy = dy_ref[rows, :]
            si, dsi = _silu_and_grad(z)
            dp_ref[3, rows, :] = (dy * o_ref[rows, :] * dsi).astype(BF16)
            dob = (dy * si).astype(BF16)
            p = _softmax(lax.dot_general(q, kp[band, :], NT, preferred_element_type=F32), tab_ref, r0, scale)
            dp = lax.dot_general(dob, vp[band, :], NT, preferred_element_type=F32)
            ds = p * (dp - jnp.sum(p * dp, axis=-1, keepdims=True))
            dtab_ref[...] += ds
            dsb = (ds * scale).astype(BF16)
            dp_ref[0, rows, :] = jnp.dot(dsb, kp[band, :], preferred_element_type=F32).astype(BF16)
            dkp[band, :] += lax.dot_general(dsb, q, TN, preferred_element_type=F32)
            dvp[band, :] += lax.dot_general(p.astype(BF16), dob, TN, preferred_element_type=F32)
            return carry

        lax.fori_loop(0, groups, group, 0)
        dp_ref[1] = dkp[PAD:, :].astype(BF16)
        dp_ref[2] = dvp[PAD:, :].astype(BF16)

    col = lambda seg: (lambda h: (0, seg * heads + h))
    blk = lambda seg: pl.BlockSpec((s, HEAD_DIM), col(seg))
    return pl.pallas_call(
        body, name="attention_bwd", grid=(heads,),
        in_specs=[blk(0), blk(1), blk(2), blk(3), blk(0), blk(0),
                  pl.BlockSpec((None, Q_GROUP, K_GROUP), lambda h: (h, 0, 0)), ANY],
        out_specs=[pl.BlockSpec((4, s, HEAD_DIM), lambda h: (0, 0, h)),
                   pl.BlockSpec((None, Q_GROUP, K_GROUP), lambda h: (h, 0, 0))],
        out_shape=[jax.ShapeDtypeStruct(dproj.shape, BF16), jax.ShapeDtypeStruct(tables.shape, F32)],
        input_output_aliases={7: 0},
        scratch_shapes=[pltpu.VMEM((PAD + s, HEAD_DIM), BF16), pltpu.VMEM((PAD + s, HEAD_DIM), BF16),
                        pltpu.VMEM((PAD + s, HEAD_DIM), F32), pltpu.VMEM((PAD + s, HEAD_DIM), F32)],
        compiler_params=_params(("parallel",)),
    )(proj, proj, proj, proj, o, dya, tables, dproj)


def _pick_window(gi, by_window):
    out = by_window[-1]
    for n in range(N_GROUPS - 2, -1, -1):
        out = jnp.where(gi == n, by_window[n], out)
    return out


def _inv_count(gi, first_row, rows):
    t = first_row + lax.broadcasted_iota(jnp.int32, (rows, 1), 0)
    w = jnp.left_shift(2, gi)
    return 1.0 / jnp.minimum(t + 1, w).astype(F32)


def _pool_fwd(proj, pw_full, pool_scale, s, a, p):
    pg = p // N_GROUPS
    ts = _div(s, 512)
    u0, z0 = 4 * a // pg, (4 * a + p) // pg
    hb = ts // HALO

    def body(u_ref, uh_ref, z_ref, pw_ref, ps_ref, d_ref, t_ref, y_ref, ext):
        gi, i = pl.program_id(0), pl.program_id(1)
        u = u_ref[...]
        ext[0:HALO, :] = jnp.where(i > 0, uh_ref[...], 0.0)
        ext[HALO:, :] = u
        e = ext[...]
        sums, shift = [], 1
        for _ in POOL_WINDOWS:
            e = e + pltpu.roll(e, shift, 0)
            sums.append(e)
            shift *= 2
        win = _pick_window(gi, sums)[HALO:, :]
        d = (win * _inv_count(gi, i * ts, ts) - u).astype(BF16)
        d_ref[...] = d
        t = jnp.dot(d, pw_ref[...].reshape(pg, pg), preferred_element_type=F32)
        t_ref[...] = t
        z = z_ref[...]
        y_ref[...] = (t * ps_ref[...] * (z * _sigmoid(z))).astype(BF16)

    out_spec = pl.BlockSpec((ts, pg), lambda g, i: (i, g))
    return pl.pallas_call(
        body, name="pool_fwd", grid=(N_GROUPS, s // ts),
        in_specs=[pl.BlockSpec((ts, pg), lambda g, i: (i, u0 + g)),
                  pl.BlockSpec((HALO, pg), lambda g, i: (jnp.maximum(i * hb - 1, 0), u0 + g)),
                  pl.BlockSpec((ts, pg), lambda g, i: (i, z0 + g)),
                  pl.BlockSpec((N_CHIPS, None, pg // N_CHIPS, pg), lambda g, i: (0, g, 0, 0)),
                  pl.BlockSpec((1, pg), lambda g, i: (0, g))],
        out_specs=[out_spec] * 3,
        out_shape=[jax.ShapeDtypeStruct((s, p), BF16), jax.ShapeDtypeStruct((s, p), F32),
                   jax.ShapeDtypeStruct((s, p), BF16)],
        scratch_shapes=[pltpu.VMEM((ts + HALO, pg), F32)],
        compiler_params=_params(("parallel", "parallel")),
    )(proj, proj, proj, pw_full, pool_scale.reshape(1, p))


def _pool_bwd(proj, dyp, t, d, pw_full, pool_scale, dproj, s, a, p):
    pg = p // N_GROUPS
    ts = _div(s, 512)
    nt = s // ts
    z0 = (4 * a + p) // pg
    hb = ts // HALO
    last_halo = s // HALO - 1

    def body(dy_ref, dyh_ref, z_ref, zh_ref, t_ref, th_ref, d_ref, pw_ref, ps_ref, _,
             dp_ref, dpw_ref, dps_ref, ext):
        gi, i = pl.program_id(0), pl.program_id(1)
        ps = ps_ref[...]
        pw = pw_ref[...].reshape(pg, pg)

        @pl.when(i == 0)
        def _():
            dpw_ref[...] = jnp.zeros_like(dpw_ref)
            dps_ref[...] = jnp.zeros_like(dps_ref)

        def through_gate(dy, z, tt):
            si, dsi = _silu_and_grad(z)
            return dy * si, dy * (tt * ps) * dsi

        tt = t_ref[...]
        dyl, dz = through_gate(dy_ref[...], z_ref[...], tt)
        dp_ref[1] = dz.astype(BF16)
        dps_ref[...] += jnp.sum(dyl * tt, axis=0, keepdims=True)
        dtb = (dyl * ps).astype(BF16)
        dpw_ref[...] += lax.dot_general(d_ref[...], dtb, TN, preferred_element_type=F32).reshape(dpw_ref.shape)
        dd = lax.dot_general(dtb, pw, NT, preferred_element_type=F32)
        dylh, _ = through_gate(dyh_ref[...], zh_ref[...], th_ref[...])
        ddh = lax.dot_general((dylh * ps).astype(BF16), pw, NT, preferred_element_type=F32)
        ddh = jnp.where(i < nt - 1, ddh, 0.0)
        ext[0:ts, :] = dd * _inv_count(gi, i * ts, ts)
        ext[ts:, :] = ddh * _inv_count(gi, (i + 1) * ts, HALO)
        e = ext[...]
        rows = ts + HALO
        sums, shift = [], 1
        for _ in POOL_WINDOWS:
            e = e + pltpu.roll(e, rows - shift, 0)
            sums.append(e)
            shift *= 2
        dp_ref[0] = (_pick_window(gi, sums)[:ts, :] - dd).astype(BF16)

    tile = lambda c0: pl.BlockSpec((ts, pg), lambda g, i: (i, c0 + g))
    halo = lambda c0: pl.BlockSpec((HALO, pg), lambda g, i: (jnp.minimum((i + 1) * hb, last_halo), c0 + g))
    pw_spec = pl.BlockSpec((N_CHIPS, None, pg // N_CHIPS, pg), lambda g, i: (0, g, 0, 0))
    return pl.pallas_call(
        body, name="pool_bwd", grid=(N_GROUPS, nt),
        in_specs=[tile(0), halo(0), tile(z0), halo(z0), tile(0), halo(0), tile(0), pw_spec,
                  pl.BlockSpec((1, pg), lambda g, i: (0, g)), ANY],
        out_specs=[pl.BlockSpec((2, ts, pg), lambda g, i: (2, i, g)), pw_spec,
                   pl.BlockSpec((1, pg), lambda g, i: (0, g))],
        out_shape=[jax.ShapeDtypeStruct(dproj.shape, BF16),
                   jax.ShapeDtypeStruct(pw_full.shape, F32), jax.ShapeDtypeStruct((1, p), F32)],
        input_output_aliases={9: 0},
        scratch_shapes=[pltpu.VMEM((ts + HALO, pg), F32)],
        compiler_params=_params(("parallel", "arbitrary")),
    )(dyp, dyp, proj, proj, t, t, d, pw_full, pool_scale.reshape(1, p), dproj)


def _merge_fwd(ya, yp, woa_full, wop_full, proj, gb_full, s, d, a, sides=()):
    sw = d // N_CHIPS
    tm, tn = _div(s, 512), _div(sw, 1024)
    per = sw // tn
    ga0, gp0 = (4 * a + 2 * a) // tn, (4 * a + 2 * a + d) // tn

    def body(ya_ref, yp_ref, wa_ref, wp_ref, ga_ref, gp_ref, gb_ref, a_out, b_out, m_out):
        av = jnp.dot(ya_ref[...], wa_ref[...], preferred_element_type=F32)
        bv = jnp.dot(yp_ref[...], wp_ref[...], preferred_element_type=F32)
        a_out[...] = av
        b_out[...] = bv
        sa = _sigmoid(ga_ref[...] + gb_ref[0:1, :])
        sp = _sigmoid(gp_ref[...] + gb_ref[1:2, :])
        m_out[...] = (sa * av + sp * bv).astype(BF16)

    act = pl.BlockSpec((tm, a), lambda j, i: (i, 0))
    wgt = pl.BlockSpec((None, a, tn), lambda j, i: (j // per, 0, j % per))
    out = pl.BlockSpec((tm, tn), lambda j, i: (i, j))
    return _run("merge_fwd", list(sides), dict(
        body=body, grid=(d // tn, s // tm),
        in_specs=[act, act, wgt, wgt,
                  pl.BlockSpec((tm, tn), lambda j, i: (i, ga0 + j)),
                  pl.BlockSpec((tm, tn), lambda j, i: (i, gp0 + j)),
                  pl.BlockSpec((None, 2, tn), lambda j, i: (j // per, 0, j % per))],
        out_specs=[out, out, out],
        out_shape=[jax.ShapeDtypeStruct((s, d), F32), jax.ShapeDtypeStruct((s, d), F32),
                   jax.ShapeDtypeStruct((s, d), BF16)],
        scratch_shapes=[], operands=[ya, yp, woa_full, wop_full, proj, proj, gb_full], sem=("parallel", "parallel")))


def _out_proj(mb, wo, x, s, d):
    tm, tn, tk = _div(s, 1024), _div(d, 1024), _div(d, 4096)

    def epilogue(res, ex, outs):
        outs[0][...] = res + ex[0][...]

    tile = pl.BlockSpec((tm, tn), lambda i, j, k: (i, j))
    return _matmul(
        "out_proj", NN, (s // tm, d // tn, d // tk),
        mb, pl.BlockSpec((tm, tk), lambda i, j, k: (i, k)),
        wo, pl.BlockSpec((tk, tn), lambda i, j, k: (k, j)),
        (tm, tn), [jax.ShapeDtypeStruct((s, d), F32)], [tile], extra=(x,), extra_specs=(tile,),
        epilogue=epilogue)[0][0]


def _loss_head(x2, target, final_gain):
    s, d = x2.shape
    tr = _div(s, 256)

    def body(x_ref, t_ref, g_ref, loss_ref, dx_ref, dxb_ref, dg_ref):
        @pl.when(pl.program_id(0) == 0)
        def _():
            dg_ref[...] = jnp.zeros_like(dg_ref)

        xv = x_ref[...]
        g = g_ref[...]
        r = lax.rsqrt(jnp.mean(xv * xv, axis=-1, keepdims=True) + EPS)
        xn = xv * r
        e = xn * g - t_ref[...]
        loss_ref[...] = 0.5 * jnp.mean(e * e, axis=-1, keepdims=True)
        dy = e / d
        dg_ref[...] += jnp.sum(dy * xn, axis=0, keepdims=True)
        dxn = dy * g
        dx = r * (dxn - xn * jnp.mean(dxn * xn, axis=-1, keepdims=True))
        dx_ref[...] = dx
        dxb_ref[...] = dx.astype(BF16)

    rows = pl.BlockSpec((tr, d), lambda i: (i, 0))
    vec = pl.BlockSpec((1, d), lambda i: (0, 0))
    return pl.pallas_call(
        body, name="loss_head", grid=(s // tr,),
        in_specs=[rows, rows, vec], out_specs=[pl.BlockSpec((tr, 1), lambda i: (i, 0)), rows, rows, vec],
        out_shape=[jax.ShapeDtypeStruct((s, 1), F32), jax.ShapeDtypeStruct((s, d), F32),
                   jax.ShapeDtypeStruct((s, d), BF16), jax.ShapeDtypeStruct((1, d), F32)],
        compiler_params=_params(("arbitrary",)),
    )(x2, target, final_gain.reshape(1, d))


N_SLOTS = 10


def _slot(seg):
    t = seg - 6
    return jnp.where(seg < 6, seg, 6 + 2 * (t % 2) + t // 2)


def _merge_bwd(dxb, wo, a_val, b_val, proj, gb_full, s, d, a):
    sw = d // N_CHIPS
    tm, tn, tk = _div(s, 256), _div(sw, 1024), d
    per = sw // tn
    per_slot = a // tn
    ga0, gp0 = (4 * a + 2 * a) // tn, (4 * a + 2 * a + d) // tn

    def epilogue(dm, ex, outs):
        a_ref, b_ref, ga_ref, gp_ref, gb_ref = ex
        da_ref, db_ref, dg_ref, dgb_ref = outs
        sa = _sigmoid(ga_ref[...] + gb_ref[0:1, :])
        sp = _sigmoid(gp_ref[...] + gb_ref[1:2, :])
        da_ref[...] = (dm * sa).astype(BF16)
        db_ref[...] = (dm * sp).astype(BF16)
        dga = dm * a_ref[...] * sa * (1.0 - sa)
        dgp = dm * b_ref[...] * sp * (1.0 - sp)
        dg_ref[0] = dga.astype(BF16)
        dg_ref[1] = dgp.astype(BF16)

        @pl.when(pl.program_id(1) == 0)
        def _():
            dgb_ref[...] = jnp.zeros_like(dgb_ref)

        dgb_ref[0:1, :] += jnp.sum(dga, axis=0, keepdims=True)
        dgb_ref[1:2, :] += jnp.sum(dgp, axis=0, keepdims=True)

    tile = pl.BlockSpec((tm, tn), lambda j, i, k: (i, j))
    sd = jax.ShapeDtypeStruct((s, d), BF16)
    return _matmul(
        "merge_bwd", NT, (d // tn, s // tm, d // tk),
        dxb, pl.BlockSpec((tm, tk), lambda j, i, k: (i, k)),
        wo, pl.BlockSpec((tn, tk), lambda j, i, k: (j, k)),
        (tm, tn), [sd, sd, jax.ShapeDtypeStruct((N_SLOTS, s, a), BF16), jax.ShapeDtypeStruct((2, d), F32)],
        [tile, tile, pl.BlockSpec((2, tm, tn), lambda j, i, k: (3 + j // per_slot, i, j % per_slot)),
         pl.BlockSpec((2, tn), lambda j, i, k: (0, j))],
        extra=(a_val, b_val, proj, proj, gb_full),
        extra_specs=(tile, tile, pl.BlockSpec((tm, tn), lambda j, i, k: (i, ga0 + j)),
                     pl.BlockSpec((tm, tn), lambda j, i, k: (i, gp0 + j)),
                     pl.BlockSpec((None, 2, tn), lambda j, i, k: (j // per, 0, j % per))),
        epilogue=epilogue, accumulate_outs=True)[0]


def _weight_grad(name, act, dout, shard_cols, slots=False, piece=None, sides=()):
    s, kdim = act.shape
    n = dout.shape[0] * dout.shape[2] if slots else dout.shape[1]
    row_tile = lambda i: i
    if shard_cols:
        sw = n // N_CHIPS
        tm, tn = _div(kdim, 1024), _div(math.gcd(sw, dout.shape[2]) if slots else sw, 1024)
        per = sw // tn
        if piece is not None:
            tm = kdim // (2 * piece[1])
            kdim = 2 * tm
            row_tile = lambda i: i * piece[1] + piece[0]
        shape = (N_CHIPS, kdim, sw)
        out = pl.BlockSpec((None, tm, tn), lambda i, j, k: (j // per, i, j % per))
    else:
        sh = kdim // N_CHIPS
        tm, tn = _div(sh, 1024), _div(n, 1024)
        per = sh // tm
        shape = (N_CHIPS, sh, n)
        out = pl.BlockSpec((None, tm, tn), lambda i, j, k: (i // per, i % per, j))
    tk = _div(s, 4096)
    if slots:
        per_slot = dout.shape[2] // tn
        dout_spec = pl.BlockSpec((None, tk, tn), lambda i, j, k: (_slot(j // per_slot), k, j % per_slot))
    else:
        dout_spec = pl.BlockSpec((tk, tn), lambda i, j, k: (k, j))
    return _matmul(
        name, TN, (kdim // tm, n // tn, s // tk),
        act, pl.BlockSpec((tk, tm), lambda i, j, k: (k, row_tile(i))), dout, dout_spec,
        (tm, tn), [jax.ShapeDtypeStruct(shape, F32), jax.ShapeDtypeStruct(shape, BF16)], [out, out], sides=sides)


def _norm_in_bwd(x, dh, dx2, gain):
    s, d = x.shape
    tr = _div(s, 256)

    def body(x_ref, dh_ref, dx2_ref, g_ref, gx_ref, dg_ref):
        @pl.when(pl.program_id(0) == 0)
        def _():
            dg_ref[...] = jnp.zeros_like(dg_ref)

        xv = x_ref[...]
        r = lax.rsqrt(jnp.mean(xv * xv, axis=-1, keepdims=True) + EPS)
        xn = xv * r
        dhv = dh_ref[...]
        dg_ref[...] += jnp.sum(dhv * xn, axis=0, keepdims=True)
        dxn = dhv * g_ref[...]
        gx_ref[...] = r * (dxn - xn * jnp.mean(dxn * xn, axis=-1, keepdims=True)) + dx2_ref[...]

    rows = pl.BlockSpec((tr, d), lambda i: (i, 0))
    vec = pl.BlockSpec((1, d), lambda i: (0, 0))
    return pl.pallas_call(
        body, name="norm_in_bwd", grid=(s // tr,),
        in_specs=[rows, rows, rows, vec], out_specs=[rows, vec],
        out_shape=[jax.ShapeDtypeStruct((s, d), F32), jax.ShapeDtypeStruct((1, d), F32)],
        compiler_params=_params(("arbitrary",)),
    )(x, dh, dx2, gain.reshape(1, d))


def _pack(vectors):
    flat = jnp.concatenate([v.reshape(-1).astype(F32) for v in vectors])
    rows = -(-flat.shape[0] // 1024) * 8
    return jnp.pad(flat, (0, rows * 128 - flat.shape[0])).reshape(rows, 128)


def _unpack(packed, like):
    flat, out, at = packed.reshape(-1), [], 0
    for v in like:
        out.append(flat[at:at + v.size].reshape(v.shape))
        at += v.size
    return out


def _small_adamw(g, w, m, v):
    def body(g_ref, w_ref, m_ref, v_ref, d_ref, mo_ref, vo_ref):
        delta, m2, v2 = _adamw_math(w_ref[...], g_ref[...], m_ref[...], v_ref[...])
        d_ref[...] = delta
        mo_ref[...] = m2
        vo_ref[...] = v2

    return pl.pallas_call(body, name="small_adamw", out_shape=[jax.ShapeDtypeStruct(g.shape, F32)] * 3)(g, w, m, v)


def kernel(x, norm_gain, w_in, rel_bias, pool_w, pool_scale, w_out_attn, w_out_pool, gate_bias, w_out, final_gain, loss_target, m_norm_gain, m_w_in, m_rel_bias, m_pool_w, m_pool_scale, m_w_out_attn, m_w_out_pool, m_gate_bias, m_w_out, m_final_gain, v_norm_gain, v_w_in, v_rel_bias, v_pool_w, v_pool_scale, v_w_out_attn, v_w_out_pool, v_gate_bias, v_w_out, v_final_gain):
    _, s, d = x.shape
    a = p = d // 2
    n_in = w_in.shape[1] * N_CHIPS
    sw_in = w_in.shape[1]
    pg = p // N_GROUPS
    xs = x.reshape(s, d)
    target = loss_target.reshape(s, d)
    c_arr = lax.axis_index("c").astype(jnp.int32).reshape(1)
    chip = 2 * lax.axis_index("x") + lax.axis_index("y")

    hb = _norm_in(xs, norm_gain)
    tm, tn, tk = _div(s, 1024), _div(sw_in, 1024), _div(d, 4096)
    per_in = sw_in // tn
    cx, cy = lax.axis_index("x"), lax.axis_index("y")
    order = jnp.stack([2 * cx + cy, 2 * (1 - cx) + cy, 2 * cx + (1 - cy), 2 * (1 - cx) + (1 - cy)]).astype(jnp.int32)

    def in_proj(name, first, count, weights, sides, carry=None):
        if weights.ndim == 3:
            w_spec = pl.BlockSpec((None, tk, tn), lambda i, j, k, o: (o[first + j // per_in], k, j % per_in))
        else:
            w_spec = pl.BlockSpec((tk, tn), lambda i, j, k, o: (k, j))
        return _matmul(
            name, NN, (s // tm, count * per_in, d // tk),
            hb, pl.BlockSpec((tm, tk), lambda i, j, k, o: (i, k)), weights, w_spec,
            (tm, tn), [jax.ShapeDtypeStruct((s, n_in), F32)],
            [pl.BlockSpec((tm, tn), lambda i, j, k, o: (i, o[first + j // per_in] * per_in + j % per_in))],
            sides=sides, prefetch=order, carry=carry)

    w_in_b = w_in.astype(BF16)
    (proj,), ((win_near,),) = in_proj("in_proj_own", 0, 1, w_in_b, [_gather_side([w_in_b], [True], peers=(0, 1))])
    (proj,), ((win_far,),) = in_proj("in_proj_near", 1, 2, win_near, [_relay_far_side(win_near)], carry=proj)
    pw_b, woa_b, wop_b, wo_b = (w.astype(BF16) for w in (pool_w, w_out_attn, w_out_pool, w_out))
    (proj,), ((pw_full, gb_full),) = in_proj(
        "in_proj_far", 3, 1, win_far, [_gather_side([pw_b, gate_bias], [False, False])], carry=proj)
    win_full = _with_own(_with_own(win_near, win_far, order[3]), w_in_b, chip)
    pw_full, gb_full = _with_own(pw_full, pw_b, chip), _with_own(gb_full, gate_bias, chip)
    table = _bias_table(rel_bias)
    (o_attn, ya), ((woa_full, wop_full),) = _attention_fwd(
        proj, table, s, a, [_gather_side([woa_b, wop_b], [True, True])])
    woa_full, wop_full = _with_own(woa_full, woa_b, chip), _with_own(wop_full, wop_b, chip)
    d_pool, t_pool, yp = _pool_fwd(proj, pw_full, pool_scale, s, a, p)
    (a_val, b_val, mb), ((wo_full,),) = _merge_fwd(ya, yp, woa_full, wop_full, proj, gb_full, s, d, a,
                                                  [_gather_side([wo_b], [True])])
    wo_mat = _with_own(wo_full, wo_b, chip).reshape(d, d)
    loss_rows, dx2, dx2b, g_final = _loss_head(_out_proj(mb, wo_mat, xs, s, d), target, final_gain)
    loss = lax.psum(jnp.sum(loss_rows), ("x", "y", "c"))

    da, db, dproj, g_gate_full = _merge_bwd(dx2b, wo_mat, a_val, b_val, proj, gb_full, s, d, a)
    (gwo, gwo_b), _ = _weight_grad("grad_w_out", mb, dx2b, shard_cols=False)
    (gwoa, gwoa_b), _ = _weight_grad("grad_w_out_attn", ya, da, shard_cols=True)
    (gwop, gwop_b), _ = _weight_grad("grad_w_out_pool", yp, db, shard_cols=True)
    early = ["w_out_attn", "w_out_pool", "w_out"]

    sw = d // N_CHIPS
    tm, tn, tk = _div(s, 1024), _div(a, 1024), _div(sw, 1024)
    per_o = sw // tk

    def back_through(name, dout, w_full, sides=()):
        nkb = d // tk

        def body(*refs):
            acts, wgts, o_ref = refs[:nkb], refs[nkb:2 * nkb], refs[2 * nkb]
            acc = lax.dot_general(acts[0][...], wgts[0][...], NT, preferred_element_type=F32)
            for q in range(1, nkb):
                acc = acc + lax.dot_general(acts[q][...], wgts[q][...], NT, preferred_element_type=F32)
            o_ref[...] = acc

        act = lambda q: pl.BlockSpec((tm, tk), lambda i, j: (i, q))
        wgt = lambda q: pl.BlockSpec((None, tn, tk), lambda i, j: (q // per_o, j, q % per_o))
        return _run(name, list(sides), dict(
            body=body, grid=(s // tm, a // tn),
            in_specs=[act(q) for q in range(nkb)] + [wgt(q) for q in range(nkb)],
            out_specs=[pl.BlockSpec((tm, tn), lambda i, j: (i, j))], out_shape=[jax.ShapeDtypeStruct((s, a), F32)],
            scratch_shapes=[], operands=[dout] * nkb + [w_full] * nkb, sem=("parallel", "parallel")))

    (dya,), (early_sib,) = back_through("grad_y_attn", da, woa_full, [_swap_side([gwoa_b, gwop_b, gwo_b])])
    early_pair = [_pair_sum("pair_sum_" + n, g, r, c_arr) for n, g, r in zip(early, [gwoa, gwop, gwo], early_sib)]
    (dyp,), _ = back_through("grad_y_pool", db, wop_full)
    dproj, gpw, g_pscale = _pool_bwd(proj, dyp, t_pool, d_pool, pw_full, pool_scale, dproj, s, a, p)
    dproj, dtable = _attention_bwd(proj, o_attn, dya, table, dproj, s, a)
    g_rel = jax.vjp(_bias_table, rel_bias)[1](dtable)[0]
    gpw3 = gpw.reshape(N_CHIPS, pg, pg)

    (gw0, gw0_b), (early_chips,) = _weight_grad("grad_w_in_0", hb, dproj, shard_cols=True, slots=True, piece=(0, 2),
                                                sides=[_scatter_side(early_pair)])
    (gw1, gw1_b), (sib0,) = _weight_grad("grad_w_in_1", hb, dproj, shard_cols=True, slots=True, piece=(1, 2),
                                         sides=[_swap_side([gw0_b, gpw3.astype(BF16)])])
    def with_own_sum(from_chips, pair):
        return _with_own(from_chips, lax.dynamic_index_in_dim(pair, chip, 0, keepdims=False), chip)

    early_halves = [_chip_sum("chip_sum_" + n, with_own_sum(r, q), c_arr)
                    for n, r, q in zip(early, early_chips, early_pair)]
    pair0 = [_pair_sum("pair_sum_w_in_0", gw0, sib0[0], c_arr), _pair_sum("pair_sum_pool_w", gpw3, sib0[1], c_arr)]

    tm, tn, tk = _div(s, 1024), _div(d, 1024), _div(a // 2, 1024)
    per_k, per_slot = sw_in // tk, a // tk
    nk_half = n_in // tk // 2
    steps = nk_half // 2

    def grad_h(name, k0, sides, plus=None):
        def body(a0, a1, b0, b1, *rest):
            o_ref, acc = rest[-2], rest[-1]
            k = pl.program_id(2)

            @pl.when(k == 0)
            def _():
                acc[...] = jnp.zeros_like(acc) if plus is None else rest[0][...]

            acc[...] += (lax.dot_general(a0[...], b0[...], NT, preferred_element_type=F32)
                         + lax.dot_general(a1[...], b1[...], NT, preferred_element_type=F32))

            @pl.when(k == steps - 1)
            def _():
                o_ref[...] = acc[...]

        def act(which):
            return pl.BlockSpec((None, tm, tk), lambda i, j, k: (
                _slot((2 * k + which + k0) // per_slot), i, (2 * k + which + k0) % per_slot))

        def wgt(which):
            return pl.BlockSpec((None, tn, tk), lambda i, j, k: (
                (2 * k + which + k0) // per_k, j, (2 * k + which + k0) % per_k))

        tile = pl.BlockSpec((tm, tn), lambda i, j, k: (i, j))
        return _run(name, list(sides), dict(
            body=body, grid=(s // tm, d // tn, steps),
            in_specs=[act(0), act(1), wgt(0), wgt(1)] + ([] if plus is None else [tile]), out_specs=[tile],
            out_shape=[jax.ShapeDtypeStruct((s, d), F32)], scratch_shapes=[pltpu.VMEM((tm, tn), F32)],
            operands=[dproj, dproj, win_full, win_full] + ([] if plus is None else [plus]),
            sem=("parallel", "parallel", "arbitrary")))

    (dh_a,), (chips0, (sib1,), early_grads) = grad_h(
        "grad_h_a", 0, [_scatter_side(pair0), _swap_side([gw1_b]), _share_side(early_halves)])
    pair1 = _pair_sum("pair_sum_w_in_1", gw1, sib1, c_arr)
    gwin_half = _chip_sum("chip_sum_w_in_0", with_own_sum(chips0[0], pair0[0]), c_arr, piece=0, pieces=2)
    gpw_half = _chip_sum("chip_sum_pool_w", with_own_sum(chips0[1], pair0[1]), c_arr)
    (dh,), ((chips1,), (gwin_half,), (gpw_full,)) = grad_h(
        "grad_h_b", nk_half,
        [_scatter_side([pair1]), _share_side([gwin_half], piece=0, pieces=2), _share_side([gpw_half])], plus=dh_a)
    gwin_half = _chip_sum("chip_sum_w_in_1", with_own_sum(chips1, pair1), c_arr, piece=1, pieces=2, into=gwin_half)
    grad_x, g_norm = _norm_in_bwd(xs, dh, dx2, norm_gain)
    _, ((gwin_full,),) = _run("reduce_share_halves", [_share_side([gwin_half], piece=1, pieces=2)])
    late_grads = [gwin_full, gpw_full]

    names = ["w_in", "w_out_attn", "w_out_pool", "w_out", "pool_w"]
    grads = [late_grads[0], *early_grads, late_grads[1]]
    big = {}
    weights = [w_in, w_out_attn, w_out_pool, w_out, pool_w.reshape(pg, pg)]
    ms = [m_w_in, m_w_out_attn, m_w_out_pool, m_w_out, m_pool_w.reshape(pg, pg)]
    vs = [v_w_in, v_w_out_attn, v_w_out_pool, v_w_out, v_pool_w.reshape(pg, pg)]
    for n, g, w, m, v in zip(names, grads, weights, ms, vs):
        big[n] = [r.reshape(pool_w.shape) if n == "pool_w" else r for r in _adamw("adamw_" + n, g, w, m, v)]

    small_like = [norm_gain, final_gain, pool_scale, rel_bias, jnp.zeros((2, d), F32)]
    summed = _sum_slots("small_grads_sum", _all_to_all_small(_pack([g_norm, g_final, g_pscale, g_rel, g_gate_full])))
    g_norm_t, g_final_t, g_pscale_t, g_rel_t, g_gate_t = _unpack(summed, small_like)
    g_gate_t = lax.dynamic_slice_in_dim(g_gate_t, chip * sw, sw, axis=1)
    small_g = [g_norm_t, g_final_t, g_pscale_t, g_rel_t, g_gate_t]
    small_w = [norm_gain, final_gain, pool_scale, rel_bias, gate_bias]
    small_m = [m_norm_gain, m_final_gain, m_pool_scale, m_rel_bias, m_gate_bias]
    small_v = [v_norm_gain, v_final_gain, v_pool_scale, v_rel_bias, v_gate_bias]
    packed = _small_adamw(_pack(small_g), _pack(small_w), _pack(small_m), _pack(small_v))
    sd, sm, sv = [_unpack(t, small_w) for t in packed]
    small = {n: [small_g[i], sd[i], sm[i], sv[i]]
             for i, n in enumerate(["norm_gain", "final_gain", "pool_scale", "rel_bias", "gate_bias"])}

    every = {**big, **small}
    order = ["norm_gain", "w_in", "rel_bias", "pool_w", "pool_scale", "w_out_attn", "w_out_pool", "gate_bias",
             "w_out", "final_gain"]
    return (loss, grad_x.reshape(x.shape), *[every[n][0] for n in order], *[every[n][1] for n in order],
            *[every[n][2] for n in order], *[every[n][3] for n in order])
```

```python
import math

import jax
import jax.numpy as jnp
from jax import lax
from jax.experimental import pallas as pl
from jax.experimental.pallas import tpu as pltpu

F32 = jnp.float32
BF16 = jnp.bfloat16
MESH = pl.DeviceIdType.MESH
ANY = pl.BlockSpec(memory_space=pl.ANY)

N_CHIPS = 4
N_DEV = 8
CHUNK = 64
N_LEFT_CHUNKS = 8
PAD = N_LEFT_CHUNKS * CHUNK
HEAD_DIM = 128
MAX_REL = 128
POOL_WINDOWS = (2, 4, 8, 16)
N_GROUPS = len(POOL_WINDOWS)
HALO = 16
Q_GROUP = 4 * CHUNK
K_GROUP = Q_GROUP + PAD
NEG = -1e30
EPS = 1e-6
ADAM_LR, ADAM_B1, ADAM_B2, ADAM_EPS, ADAM_WD, ADAM_STEP = 0.001, 0.9, 0.999, 1e-08, 0.01, 10
VMEM_LIMIT = 56 * 1024 * 1024

NN = (((1,), (0,)), ((), ()))
NT = (((1,), (1,)), ((), ()))
TN = (((0,), (0,)), ((), ()))


def _div(n, pref):
    if n <= pref:
        return n
    for t in range(pref - pref % 128, 0, -128):
        if n % t == 0:
            return t
    raise ValueError((n, pref))


def _params(sem, **kw):
    return pltpu.CompilerParams(dimension_semantics=sem, vmem_limit_bytes=VMEM_LIMIT, **kw)


def _sigmoid(z):
    return jax.nn.sigmoid(z)


def _silu_and_grad(z):
    sg = _sigmoid(z)
    return z * sg, sg * (1.0 + z * (1.0 - sg))


def _matmul(name, dn, grid, a, a_spec, b, b_spec, acc_shape, outs, out_specs, extra=(), extra_specs=(),
            epilogue=None, accumulate_outs=False, sides=(), prefetch=None, carry=None):
    nk = grid[2]
    aliases = {}
    if carry is not None:
        aliases = {2 + len(extra): 0}
        extra, extra_specs = (*extra, carry), (*extra_specs, ANY)
    ne, no = len(extra), len(outs)

    def finish(res, ex, out_refs):
        if epilogue is None:
            for o in out_refs:
                o[...] = res.astype(o.dtype)
        else:
            epilogue(res, ex, out_refs)

    def body(*refs):
        a_ref, b_ref = refs[0], refs[1]
        ex = refs[2:2 + ne]
        out_refs = refs[2 + ne:2 + ne + no]
        if nk == 1:
            finish(lax.dot_general(a_ref[...], b_ref[...], dn, preferred_element_type=F32), ex, out_refs)
            return
        acc = refs[-1]
        k = pl.program_id(2)

        @pl.when(k == 0)
        def _():
            acc[...] = jnp.zeros_like(acc)

        acc[...] += lax.dot_general(a_ref[...], b_ref[...], dn, preferred_element_type=F32)

        @pl.when(k == nk - 1)
        def _():
            finish(acc[...], ex, out_refs)

    sem = ("arbitrary",) * 3 if accumulate_outs else ("parallel", "parallel", "arbitrary")
    return _run(name, list(sides), dict(
        body=body, grid=grid, in_specs=[a_spec, b_spec, *extra_specs], out_specs=list(out_specs),
        out_shape=list(outs), scratch_shapes=[] if nk == 1 else [pltpu.VMEM(acc_shape, F32)],
        operands=[a, b, *extra], sem=sem, aliases=aliases, prefetch=prefetch))


def _place():
    x, y, c = lax.axis_index("x"), lax.axis_index("y"), lax.axis_index("c")
    chips = [(1 - x, y), (x, 1 - y), (1 - x, 1 - y)]
    return x, y, c, chips


N_STREAMS = 1


class _Copies:
    def __init__(self, cps):
        self.cps = cps

    def start(self):
        for cp in self.cps:
            cp.start()

    def wait_send(self):
        for cp in self.cps:
            cp.wait_send()

    def wait_recv(self):
        for cp in self.cps:
            cp.wait_recv()

    def wait(self):
        for cp in self.cps:
            cp.wait()


def _remote(src, dst, send_sems, recv_sems, k, dev):
    lead = src.shape[0]
    n = N_STREAMS
    while n > 1 and (lead % n or (len(src.shape) == 2 and (lead // n) % 16)):
        n //= 2
    step = lead // n
    return _Copies([pltpu.make_async_remote_copy(
        src_ref=src.at[pl.ds(i * step, step)], dst_ref=dst.at[pl.ds(i * step, step)],
        send_sem=send_sems.at[k * N_STREAMS + i], recv_sem=recv_sems.at[k * N_STREAMS + i],
        device_id=dev, device_id_type=MESH) for i in range(n)])


class _Side:
    def __init__(self, ins, out_shapes, n_remote, n_local, start, finish, aliases=None):
        self.ins, self.out_shapes = list(ins), list(out_shapes)
        self.n_remote, self.n_local = max(n_remote, 1), max(n_local, 1)
        self.start, self.finish, self.aliases = start, finish, aliases or {}


def _run(name, sides, compute=None):
    cm = compute or dict(body=None, grid=(), in_specs=[], out_specs=[], out_shape=[], scratch_shapes=[], operands=[])
    grid = tuple(cm["grid"])
    ni, no, ns = len(cm["operands"]), len(cm["out_shape"]), len(cm["scratch_shapes"])
    n_in = [len(sd.ins) for sd in sides]
    n_out = [len(sd.out_shapes) for sd in sides]
    prefetch = cm.get("prefetch")
    shift = 0 if prefetch is None else 1

    def body(*refs):
        refs = refs[shift:]
        at = ni
        side_ins = []
        for n in n_in:
            side_ins.append(refs[at:at + n])
            at += n
        outs = refs[at:at + no]
        at += no
        side_outs = []
        for n in n_out:
            side_outs.append(refs[at:at + n])
            at += n
        scratch = refs[at:at + ns]
        at += ns
        sems = [refs[at + 3 * q:at + 3 * q + 3] for q in range(len(sides))]

        def each(step):
            for sd, i_, o_, m_ in zip(sides, side_ins, side_outs, sems):
                getattr(sd, step)(i_, o_, *m_)

        if not grid:
            each("start")
            each("finish")
            return
        first = last = None
        for ax, g in enumerate(grid):
            f, l = pl.program_id(ax) == 0, pl.program_id(ax) == g - 1
            first = f if first is None else first & f
            last = l if last is None else last & l
        if sides:
            pl.when(first)(lambda: each("start"))
        cm["body"](*refs[:ni], *outs, *scratch)
        if sides:
            pl.when(last)(lambda: each("finish"))

    aliases = {shift + i_: o_ for i_, o_ in (cm.get("aliases") or {}).items()}
    in_at, out_at = shift + ni, no
    for sd, a, b in zip(sides, n_in, n_out):
        for i_, o_ in sd.aliases.items():
            aliases[in_at + i_] = out_at + o_
        in_at, out_at = in_at + a, out_at + b
    scratch_shapes = list(cm["scratch_shapes"])
    for sd in sides:
        scratch_shapes += [pltpu.SemaphoreType.DMA((sd.n_remote * N_STREAMS,)),
                           pltpu.SemaphoreType.DMA((sd.n_remote * N_STREAMS,)), pltpu.SemaphoreType.DMA((sd.n_local,))]
    in_specs = list(cm["in_specs"]) + [ANY] * sum(n_in)
    out_specs = list(cm["out_specs"]) + [ANY] * sum(n_out)
    kw = dict(in_specs=in_specs, out_specs=out_specs, scratch_shapes=scratch_shapes)
    if grid:
        kw["grid"] = grid
    if prefetch is not None:
        kw = dict(grid_spec=pltpu.PrefetchScalarGridSpec(num_scalar_prefetch=1, **kw))
    if grid:
        kw["compiler_params"] = _params(("arbitrary",) * len(grid) if sides else cm["sem"])
    res = pl.pallas_call(
        body, name=name, out_shape=list(cm["out_shape"]) + [s for sd in sides for s in sd.out_shapes],
        input_output_aliases=aliases, **kw,
    )(*([] if prefetch is None else [prefetch]), *cm["operands"], *[a for sd in sides for a in sd.ins])
    res = list(res)
    side_res, at = [], no
    for n in n_out:
        side_res.append(res[at:at + n])
        at += n
    return res[:no], side_res


def _gather_side(shards, split, peers=(0, 1, 2)):
    n = len(shards)

    def plan(ins, outs, send_sems, recv_sems, _):
        x, y, c, chips = _place()
        me = 2 * x + y
        sibling = (x, y, 1 - c)
        direct, relays, arrivals = [], [], []
        for t in range(n):
            quarter = ins[t].shape[0] // 4
            for j in peers:
                cx, cy = chips[j]
                src_chip = 2 * cx + cy
                k = 12 * t + 4 * j
                if not split[t]:
                    direct.append(_remote(ins[t], outs[t].at[me], send_sems, recv_sems, k, (cx, cy, c)))
                    got = outs[t].at[src_chip]
                    arrivals.append(_remote(got, got, send_sems, recv_sems, k, (cx, cy, c)))
                    continue
                for r in range(2):
                    e = c ^ r
                    out_q = pl.ds((2 * c + e) * quarter, quarter)
                    direct.append(_remote(ins[t].at[out_q], outs[t].at[me, out_q], send_sems, recv_sems, k + r,
                                          (cx, cy, e)))
                    got = outs[t].at[src_chip, pl.ds((2 * e + c) * quarter, quarter)]
                    relays.append((_remote(got, got, send_sems, recv_sems, k + r, (cx, cy, e)),
                                   _remote(got, got, send_sems, recv_sems, k + 2 + r, sibling)))
                    theirs = outs[t].at[src_chip, pl.ds((2 * e + 1 - c) * quarter, quarter)]
                    arrivals.append(_remote(theirs, theirs, send_sems, recv_sems, k + 2 + (1 - r), sibling))
        return direct, relays, arrivals

    def start(*refs):
        for cp in plan(*refs)[0]:
            cp.start()

    def finish(*refs):
        direct, relays, arrivals = plan(*refs)
        for landed, onward in relays:
            landed.wait_recv()
            onward.start()
        for cp in arrivals:
            cp.wait_recv()
        for cp in direct + [onward for _, onward in relays]:
            cp.wait_send()

    return _Side(shards, [jax.ShapeDtypeStruct((N_CHIPS,) + s.shape, s.dtype) for s in shards], 12 * n, 0,
                 start, finish)


def _with_own(slots, block, chip):
    return lax.dynamic_update_slice(slots, block[None], (chip,) + (0,) * block.ndim)


def _relay_far_side(full):
    def plan(ins, outs, send_sems, recv_sems, __):
        x, y, c, _ = _place()
        src, far = ins[0], outs[0]
        eighth = far.shape[0] // 8
        x_id, y_id = 2 * (1 - x) + y, 2 * x + (1 - y)
        sends, landed, onward, from_sibling = [], [], [], []
        for side, (nb_x, nb_y, block) in enumerate([(1 - x, y, y_id), (x, 1 - y, x_id)]):
            for r in range(2):
                e = c ^ r
                out_rows = pl.ds((4 * side + 2 * c + e) * eighth, eighth)
                sends.append(_remote(src.at[block, out_rows], far.at[out_rows], send_sems, recv_sems, 2 * side + r,
                                     (nb_x, nb_y, e)))
                got = far.at[pl.ds((4 * side + 2 * e + c) * eighth, eighth)]
                landed.append(_remote(got, got, send_sems, recv_sems, 2 * side + r, (nb_x, nb_y, e)))
                onward.append(_remote(got, got, send_sems, recv_sems, 4 + 2 * side + r, (x, y, 1 - c)))
                theirs = far.at[pl.ds((4 * side + 2 * e + 1 - c) * eighth, eighth)]
                from_sibling.append(_remote(theirs, theirs, send_sems, recv_sems, 4 + 2 * side + (1 - r),
                                            (x, y, 1 - c)))
        return sends, landed, onward, from_sibling

    def start(*refs):
        for cp in plan(*refs)[0]:
            cp.start()

    def finish(*refs):
        sends, landed, onward, from_sibling = plan(*refs)
        for got, fwd in zip(landed, onward):
            got.wait_recv()
            fwd.start()
        for cp in from_sibling:
            cp.wait_recv()
        for cp in sends + onward:
            cp.wait_send()

    return _Side([full], [jax.ShapeDtypeStruct(full.shape[1:], full.dtype)], 8, 0, start, finish)


def _swap_side(parts):
    n = len(parts)

    def plan(ins, outs, send_sems, recv_sems, _):
        x, y, c, _ = _place()
        cps = []
        for t in range(n):
            half = ins[t].shape[1] // 2
            cps.append(_remote(ins[t].at[:, pl.ds((1 - c) * half, half)], outs[t], send_sems, recv_sems, t,
                               (x, y, 1 - c)))
        return cps

    def start(*refs):
        for cp in plan(*refs):
            cp.start()

    def finish(*refs):
        for cp in plan(*refs):
            cp.wait()

    return _Side(parts, [jax.ShapeDtypeStruct((p.shape[0], p.shape[1] // 2, p.shape[2]), p.dtype) for p in parts],
                 n, 0, start, finish)


def _scatter_side(parts):
    n = len(parts)

    def plan(ins, outs, send_sems, recv_sems, _):
        x, y, c, chips = _place()
        me = 2 * x + y
        sends, arrivals = [], []
        for t in range(n):
            for j, (cx, cy) in enumerate(chips):
                sends.append(_remote(ins[t].at[2 * cx + cy], outs[t].at[me], send_sems, recv_sems, 3 * t + j,
                                     (cx, cy, c)))
                got = outs[t].at[2 * cx + cy]
                arrivals.append(_remote(got, got, send_sems, recv_sems, 3 * t + j, (cx, cy, c)))
        return sends, arrivals

    def start(*refs):
        for cp in plan(*refs)[0]:
            cp.start()

    def finish(*refs):
        sends, arrivals = plan(*refs)
        for cp in arrivals:
            cp.wait_recv()
        for cp in sends:
            cp.wait_send()

    return _Side(parts, [jax.ShapeDtypeStruct(p.shape, p.dtype) for p in parts], 3 * n, 0, start, finish)


def _share_side(fulls, piece=0, pieces=1):
    n = len(fulls)

    def plan(_, outs, send_sems, recv_sems, __):
        x, y, c, _ = _place()
        cps = []
        for t in range(n):
            half = outs[t].shape[0] // (2 * pieces)
            mine = outs[t].at[pl.ds((pieces * c + piece) * half, half)]
            theirs = outs[t].at[pl.ds((pieces * (1 - c) + piece) * half, half)]
            cps.append((_remote(mine, mine, send_sems, recv_sems, t, (x, y, 1 - c)),
                        _remote(theirs, theirs, send_sems, recv_sems, t, (x, y, 1 - c))))
        return cps

    def start(*refs):
        for cp, _ in plan(*refs):
            cp.start()

    def finish(*refs):
        for cp, rv in plan(*refs):
            rv.wait_recv()
            cp.wait_send()

    return _Side(fulls, [jax.ShapeDtypeStruct(f.shape, f.dtype) for f in fulls], n, 0, start, finish,
                 aliases={t: t for t in range(n)})


def _all_to_all_small(packed):
    def body(in_ref, out_ref, send_sems, recv_sems, local_sem):
        x, y, c, _ = _place()
        me = 4 * x + 2 * y + c
        own = pltpu.make_async_copy(in_ref, out_ref.at[me], local_sem)
        own.start()
        cps, rvs = [], []
        for k in range(1, N_DEV):
            fx, fy, fc = (k >> 2) & 1, (k >> 1) & 1, k & 1
            px, py, pc = x ^ fx, y ^ fy, c ^ fc
            cp = _remote(in_ref, out_ref.at[me], send_sems, recv_sems, k - 1, (px, py, pc))
            cp.start()
            cps.append(cp)
            got = out_ref.at[4 * px + 2 * py + pc]
            rvs.append(_remote(got, got, send_sems, recv_sems, k - 1, (px, py, pc)))
        for rv in rvs:
            rv.wait_recv()
        for cp in cps:
            cp.wait_send()
        own.wait()

    return pl.pallas_call(
        body, name="small_grads_exchange",
        in_specs=[ANY], out_specs=ANY,
        out_shape=jax.ShapeDtypeStruct((N_DEV,) + packed.shape, packed.dtype),
        scratch_shapes=[pltpu.SemaphoreType.DMA(((N_DEV - 1) * N_STREAMS,)),
                        pltpu.SemaphoreType.DMA(((N_DEV - 1) * N_STREAMS,)), pltpu.SemaphoreType.DMA],
    )(packed)


def _pair_sum(name, g, recv, c_arr):
    _, rows, cols = g.shape
    half = rows // 2
    tr, tc = _div(half, 512), _div(cols, 1024)
    nrb = half // tr

    def body(c_ref, g_ref, r_ref, o_ref):
        o_ref[...] = (g_ref[...].astype(F32) + r_ref[...].astype(F32)).astype(BF16)

    return pl.pallas_call(
        body, name=name,
        grid_spec=pltpu.PrefetchScalarGridSpec(
            num_scalar_prefetch=1, grid=(N_CHIPS, nrb, cols // tc),
            in_specs=[pl.BlockSpec((None, tr, tc), lambda s, i, j, c: (s, c[0] * nrb + i, j)),
                      pl.BlockSpec((None, tr, tc), lambda s, i, j, c: (s, i, j))],
            out_specs=pl.BlockSpec((None, tr, tc), lambda s, i, j, c: (s, i, j))),
        out_shape=jax.ShapeDtypeStruct(recv.shape, BF16),
        compiler_params=_params(("parallel", "parallel", "parallel")),
    )(c_arr, g, recv)


def _chip_sum(name, recv, c_arr, piece=0, pieces=1, into=None):
    _, half, cols = recv.shape
    tr, tc = _div(half, 512), _div(cols, 1024)
    nrb = half // tr

    def body(c_ref, r_ref, *rest):
        o_ref = rest[-1]
        acc = r_ref[0].astype(F32)
        for s in range(1, N_CHIPS):
            acc = acc + r_ref[s].astype(F32)
        o_ref[...] = acc

    return pl.pallas_call(
        body, name=name,
        grid_spec=pltpu.PrefetchScalarGridSpec(
            num_scalar_prefetch=1, grid=(nrb, cols // tc),
            in_specs=[pl.BlockSpec((N_CHIPS, tr, tc), lambda i, j, c: (0, i, j))] + ([] if into is None else [ANY]),
            out_specs=pl.BlockSpec((tr, tc), lambda i, j, c: ((pieces * c[0] + piece) * nrb + i, j))),
        out_shape=jax.ShapeDtypeStruct((2 * pieces * half, cols), F32),
        input_output_aliases={} if into is None else {2: 0},
        compiler_params=_params(("parallel", "parallel")),
    )(c_arr, recv, *([] if into is None else [into]))


def _adamw_math(w, g, m, v):
    m2 = ADAM_B1 * m + (1.0 - ADAM_B1) * g
    v2 = ADAM_B2 * v + (1.0 - ADAM_B2) * (g * g)
    m_hat = m2 / (1.0 - ADAM_B1 ** ADAM_STEP)
    v_hat = v2 / (1.0 - ADAM_B2 ** ADAM_STEP)
    delta = -ADAM_LR * (m_hat / (jnp.sqrt(v_hat) + ADAM_EPS) + ADAM_WD * w)
    return delta, m2, v2


def _adamw(name, g, w, m, v):
    rows, cols = g.shape
    tr, tc = _div(rows, 512), _div(cols, 1024)
    spec = pl.BlockSpec((tr, tc), lambda i, j: (i, j))

    def body(g_ref, w_ref, m_ref, v_ref, go_ref, d_ref, mo_ref, vo_ref):
        gg = g_ref[...]
        delta, m2, v2 = _adamw_math(w_ref[...], gg, m_ref[...], v_ref[...])
        go_ref[...] = gg
        d_ref[...] = delta
        mo_ref[...] = m2
        vo_ref[...] = v2

    return pl.pallas_call(
        body, name=name, grid=(rows // tr, cols // tc),
        in_specs=[spec] * 4, out_specs=[spec] * 4,
        out_shape=[jax.ShapeDtypeStruct(g.shape, F32)] * 4,
        compiler_params=_params(("parallel", "parallel")),
    )(g, w, m, v)


def _sum_slots(name, slots):
    def body(s_ref, o_ref):
        acc = s_ref[0]
        for d in range(1, N_DEV):
            acc = acc + s_ref[d]
        o_ref[...] = acc

    return pl.pallas_call(body, name=name, out_shape=jax.ShapeDtypeStruct(slots.shape[1:], F32))(slots)


def _norm_in(x, gain):
    s, d = x.shape
    tr = _div(s, 256)

    def body(x_ref, g_ref, h_ref):
        xv = x_ref[...]
        r = lax.rsqrt(jnp.mean(xv * xv, axis=-1, keepdims=True) + EPS)
        h_ref[...] = (xv * r * g_ref[...]).astype(BF16)

    return pl.pallas_call(
        body, name="norm_in", grid=(s // tr,),
        in_specs=[pl.BlockSpec((tr, d), lambda i: (i, 0)), pl.BlockSpec((1, d), lambda i: (0, 0))],
        out_specs=pl.BlockSpec((tr, d), lambda i: (i, 0)),
        out_shape=jax.ShapeDtypeStruct((s, d), BF16),
        compiler_params=_params(("parallel",)),
    )(x, gain.reshape(1, d))


def _bias_table(rel_bias):
    h = rel_bias.shape[0]
    span = 2 * CHUNK - 1
    lo, hi = -(CHUNK - 1), (N_LEFT_CHUNKS + 1) * CHUNK - 1
    by_rel = jnp.concatenate([rel_bias[:, MAX_REL + lo:],
                              jnp.broadcast_to(rel_bias[:, -1:], (h, hi - MAX_REL))], axis=1)
    vec = jnp.stack([by_rel[:, m * CHUNK:m * CHUNK + span] for m in range(N_LEFT_CHUNKS + 1)], axis=1)
    rev = jnp.concatenate([vec[..., ::-1], jnp.zeros(vec.shape[:2] + (1,), vec.dtype)], axis=-1)
    skew = jnp.tile(rev, (1, 1, CHUNK))[..., :CHUNK * span].reshape(h, N_LEFT_CHUNKS + 1, CHUNK, span)
    blocks = skew[..., CHUNK - 1:]
    off = jnp.full((h, CHUNK, CHUNK), NEG, rel_bias.dtype)
    rows = []
    for qi in range(Q_GROUP // CHUNK):
        dist = [N_LEFT_CHUNKS + qi - kj for kj in range(K_GROUP // CHUNK)]
        rows.append(jnp.concatenate([blocks[:, m] if 0 <= m <= N_LEFT_CHUNKS else off for m in dist], axis=-1))
    return jnp.concatenate(rows, axis=-2)


def _softmax(scores, tab_ref, r0, scale):
    sc = scores * scale + tab_ref[...]
    col = lax.broadcasted_iota(jnp.int32, sc.shape, 1)
    sc = jnp.where(col >= PAD - r0, sc, NEG)
    e = jnp.exp(sc - jnp.max(sc, axis=-1, keepdims=True))
    return e * (1.0 / jnp.sum(e, axis=-1, keepdims=True))


def _attention_fwd(proj, tables, s, a, sides=()):
    heads = a // HEAD_DIM
    scale = HEAD_DIM ** -0.5
    groups = s // Q_GROUP

    def body(q_ref, k_ref, v_ref, z_ref, tab_ref, o_ref, ya_ref, kp, vp):
        kp[0:PAD, :] = jnp.zeros((PAD, HEAD_DIM), BF16)
        vp[0:PAD, :] = jnp.zeros((PAD, HEAD_DIM), BF16)
        kp[PAD:, :] = k_ref[...].astype(BF16)
        vp[PAD:, :] = v_ref[...].astype(BF16)

        def group(g, carry):
            r0 = pl.multiple_of(g * Q_GROUP, Q_GROUP)
            q = q_ref[pl.ds(r0, Q_GROUP), :].astype(BF16)
            p = _softmax(lax.dot_general(q, kp[pl.ds(r0, K_GROUP), :], NT, preferred_element_type=F32),
                         tab_ref, r0, scale)
            o = jnp.dot(p.astype(BF16), vp[pl.ds(r0, K_GROUP), :], preferred_element_type=F32)
            o_ref[pl.ds(r0, Q_GROUP), :] = o
            z = z_ref[pl.ds(r0, Q_GROUP), :]
            ya_ref[pl.ds(r0, Q_GROUP), :] = (o * (z * _sigmoid(z))).astype(BF16)
            return carry

        lax.fori_loop(0, groups, group, 0)

    col = lambda seg: (lambda h: (0, seg * heads + h))
    blk = lambda seg: pl.BlockSpec((s, HEAD_DIM), col(seg))
    return _run("attention_fwd", list(sides), dict(
        body=body, grid=(heads,),
        in_specs=[blk(0), blk(1), blk(2), blk(3),
                  pl.BlockSpec((None, Q_GROUP, K_GROUP), lambda h: (h, 0, 0))],
        out_specs=[blk(0), blk(0)],
        out_shape=[jax.ShapeDtypeStruct((s, a), F32), jax.ShapeDtypeStruct((s, a), BF16)],
        scratch_shapes=[pltpu.VMEM((PAD + s, HEAD_DIM), BF16), pltpu.VMEM((PAD + s, HEAD_DIM), BF16)],
        operands=[proj, proj, proj, proj, tables], sem=("parallel",)))


def _attention_bwd(proj, o, dya, tables, dproj, s, a):
    heads = a // HEAD_DIM
    scale = HEAD_DIM ** -0.5
    groups = s // Q_GROUP

    def body(q_ref, k_ref, v_ref, z_ref, o_ref, dy_ref, tab_ref, _, dp_ref, dtab_ref, kp, vp, dkp, dvp):
        kp[0:PAD, :] = jnp.zeros((PAD, HEAD_DIM), BF16)
        vp[0:PAD, :] = jnp.zeros((PAD, HEAD_DIM), BF16)
        kp[PAD:, :] = k_ref[...].astype(BF16)
        vp[PAD:, :] = v_ref[...].astype(BF16)
        dkp[...] = jnp.zeros_like(dkp)
        dvp[...] = jnp.zeros_like(dvp)
        dtab_ref[...] = jnp.zeros_like(dtab_ref)

        def group(g, carry):
            r0 = pl.multiple_of(g * Q_GROUP, Q_GROUP)
            rows = pl.ds(r0, Q_GROUP)
            band = pl.ds(r0, K_GROUP)
            q = q_ref[rows, :].astype(BF16)
            z = z_ref[rows, :]
            dy = dy_ref[rows, :]
            si, dsi = _silu_and_grad(z)
            dp_ref[3, rows, :] = (dy * o_ref[rows, :] * dsi).astype(BF16)
            dob = (dy * si).astype(BF16)
            p = _softmax(lax.dot_general(q, kp[band, :], NT, preferred_element_type=F32), tab_ref, r0, scale)
            dp = lax.dot_general(dob, vp[band, :], NT, preferred_element_type=F32)
            ds = p * (dp - jnp.sum(p * dp, axis=-1, keepdims=True))
            dtab_ref[...] += ds
            dsb = (ds * scale).astype(BF16)
            dp_ref[0, rows, :] = jnp.dot(dsb, kp[band, :], preferred_element_type=F32).astype(BF16)
            dkp[band, :] += lax.dot_general(dsb, q, TN, preferred_element_type=F32)
            dvp[band, :] += lax.dot_general(p.astype(BF16), dob, TN, preferred_element_type=F32)
            return carry

        lax.fori_loop(0, groups, group, 0)
        dp_ref[1] = dkp[PAD:, :].astype(BF16)
        dp_ref[2] = dvp[PAD:, :].astype(BF16)

    col = lambda seg: (lambda h: (0, seg * heads + h))
    blk = lambda seg: pl.BlockSpec((s, HEAD_DIM), col(seg))
    return pl.pallas_call(
        body, name="attention_bwd", grid=(heads,),
        in_specs=[blk(0), blk(1), blk(2), blk(3), blk(0), blk(0),
                  pl.BlockSpec((None, Q_GROUP, K_GROUP), lambda h: (h, 0, 0)), ANY],
        out_specs=[pl.BlockSpec((4, s, HEAD_DIM), lambda h: (0, 0, h)),
                   pl.BlockSpec((None, Q_GROUP, K_GROUP), lambda h: (h, 0, 0))],
        out_shape=[jax.ShapeDtypeStruct(dproj.shape, BF16), jax.ShapeDtypeStruct(tables.shape, F32)],
        input_output_aliases={7: 0},
        scratch_shapes=[pltpu.VMEM((PAD + s, HEAD_DIM), BF16), pltpu.VMEM((PAD + s, HEAD_DIM), BF16),
                        pltpu.VMEM((PAD + s, HEAD_DIM), F32), pltpu.VMEM((PAD + s, HEAD_DIM), F32)],
        compiler_params=_params(("parallel",)),
    )(proj, proj, proj, proj, o, dya, tables, dproj)


def _pick_window(gi, by_window):
    out = by_window[-1]
    for n in range(N_GROUPS - 2, -1, -1):
        out = jnp.where(gi == n, by_window[n], out)
    return out


def _inv_count(gi, first_row, rows):
    t = first_row + lax.broadcasted_iota(jnp.int32, (rows, 1), 0)
    w = jnp.left_shift(2, gi)
    return 1.0 / jnp.minimum(t + 1, w).astype(F32)


def _pool_fwd(proj, pw_full, pool_scale, s, a, p):
    pg = p // N_GROUPS
    ts = _div(s, 512)
    u0, z0 = 4 * a // pg, (4 * a + p) // pg
    hb = ts // HALO

    def body(u_ref, uh_ref, z_ref, pw_ref, ps_ref, d_ref, t_ref, y_ref, ext):
        gi, i = pl.program_id(0), pl.program_id(1)
        u = u_ref[...]
        ext[0:HALO, :] = jnp.where(i > 0, uh_ref[...], 0.0)
        ext[HALO:, :] = u
        e = ext[...]
        sums, shift = [], 1
        for _ in POOL_WINDOWS:
            e = e + pltpu.roll(e, shift, 0)
            sums.append(e)
            shift *= 2
        win = _pick_window(gi, sums)[HALO:, :]
        d = (win * _inv_count(gi, i * ts, ts) - u).astype(BF16)
        d_ref[...] = d
        t = jnp.dot(d, pw_ref[...].reshape(pg, pg), preferred_element_type=F32)
        t_ref[...] = t
        z = z_ref[...]
        y_ref[...] = (t * ps_ref[...] * (z * _sigmoid(z))).astype(BF16)

    out_spec = pl.BlockSpec((ts, pg), lambda g, i: (i, g))
    return pl.pallas_call(
        body, name="pool_fwd", grid=(N_GROUPS, s // ts),
        in_specs=[pl.BlockSpec((ts, pg), lambda g, i: (i, u0 + g)),
                  pl.BlockSpec((HALO, pg), lambda g, i: (jnp.maximum(i * hb - 1, 0), u0 + g)),
                  pl.BlockSpec((ts, pg), lambda g, i: (i, z0 + g)),
                  pl.BlockSpec((N_CHIPS, None, pg // N_CHIPS, pg), lambda g, i: (0, g, 0, 0)),
                  pl.BlockSpec((1, pg), lambda g, i: (0, g))],
        out_specs=[out_spec] * 3,
        out_shape=[jax.ShapeDtypeStruct((s, p), BF16), jax.ShapeDtypeStruct((s, p), F32),
                   jax.ShapeDtypeStruct((s, p), BF16)],
        scratch_shapes=[pltpu.VMEM((ts + HALO, pg), F32)],
        compiler_params=_params(("parallel", "parallel")),
    )(proj, proj, proj, pw_full, pool_scale.reshape(1, p))


def _pool_bwd(proj, dyp, t, d, pw_full, pool_scale, dproj, s, a, p):
    pg = p // N_GROUPS
    ts = _div(s, 512)
    nt = s // ts
    z0 = (4 * a + p) // pg
    hb = ts // HALO
    last_halo = s // HALO - 1

    def body(dy_ref, dyh_ref, z_ref, zh_ref, t_ref, th_ref, d_ref, pw_ref, ps_ref, _,
             dp_ref, dpw_ref, dps_ref, ext):
        gi, i = pl.program_id(0), pl.program_id(1)
        ps = ps_ref[...]
        pw = pw_ref[...].reshape(pg, pg)

        @pl.when(i == 0)
        def _():
            dpw_ref[...] = jnp.zeros_like(dpw_ref)
            dps_ref[...] = jnp.zeros_like(dps_ref)

        def through_gate(dy, z, tt):
            si, dsi = _silu_and_grad(z)
            return dy * si, dy * (tt * ps) * dsi

        tt = t_ref[...]
        dyl, dz = through_gate(dy_ref[...], z_ref[...], tt)
        dp_ref[1] = dz.astype(BF16)
        dps_ref[...] += jnp.sum(dyl * tt, axis=0, keepdims=True)
        dtb = (dyl * ps).astype(BF16)
        dpw_ref[...] += lax.dot_general(d_ref[...], dtb, TN, preferred_element_type=F32).reshape(dpw_ref.shape)
        dd = lax.dot_general(dtb, pw, NT, preferred_element_type=F32)
        dylh, _ = through_gate(dyh_ref[...], zh_ref[...], th_ref[...])
        ddh = lax.dot_general((dylh * ps).astype(BF16), pw, NT, preferred_element_type=F32)
        ddh = jnp.where(i < nt - 1, ddh, 0.0)
        ext[0:ts, :] = dd * _inv_count(gi, i * ts, ts)
        ext[ts:, :] = ddh * _inv_count(gi, (i + 1) * ts, HALO)
        e = ext[...]
        rows = ts + HALO
        sums, shift = [], 1
        for _ in POOL_WINDOWS:
            e = e + pltpu.roll(e, rows - shift, 0)
            sums.append(e)
            shift *= 2
        dp_ref[0] = (_pick_window(gi, sums)[:ts, :] - dd).astype(BF16)

    tile = lambda c0: pl.BlockSpec((ts, pg), lambda g, i: (i, c0 + g))
    halo = lambda c0: pl.BlockSpec((HALO, pg), lambda g, i: (jnp.minimum((i + 1) * hb, last_halo), c0 + g))
    pw_spec = pl.BlockSpec((N_CHIPS, None, pg // N_CHIPS, pg), lambda g, i: (0, g, 0, 0))
    return pl.pallas_call(
        body, name="pool_bwd", grid=(N_GROUPS, nt),
        in_specs=[tile(0), halo(0), tile(z0), halo(z0), tile(0), halo(0), tile(0), pw_spec,
                  pl.BlockSpec((1, pg), lambda g, i: (0, g)), ANY],
        out_specs=[pl.BlockSpec((2, ts, pg), lambda g, i: (2, i, g)), pw_spec,
                   pl.BlockSpec((1, pg), lambda g, i: (0, g))],
        out_shape=[jax.ShapeDtypeStruct(dproj.shape, BF16),
                   jax.ShapeDtypeStruct(pw_full.shape, F32), jax.ShapeDtypeStruct((1, p), F32)],
        input_output_aliases={9: 0},
        scratch_shapes=[pltpu.VMEM((ts + HALO, pg), F32)],
        compiler_params=_params(("parallel", "arbitrary")),
    )(dyp, dyp, proj, proj, t, t, d, pw_full, pool_scale.reshape(1, p), dproj)


def _merge_fwd(ya, yp, woa_full, wop_full, proj, gb_full, s, d, a, sides=()):
    sw = d // N_CHIPS
    tm, tn = _div(s, 512), _div(sw, 1024)
    per = sw // tn
    ga0, gp0 = (4 * a + 2 * a) // tn, (4 * a + 2 * a + d) // tn

    def body(ya_ref, yp_ref, wa_ref, wp_ref, ga_ref, gp_ref, gb_ref, a_out, b_out, m_out):
        av = jnp.dot(ya_ref[...], wa_ref[...], preferred_element_type=F32)
        bv = jnp.dot(yp_ref[...], wp_ref[...], preferred_element_type=F32)
        a_out[...] = av
        b_out[...] = bv
        sa = _sigmoid(ga_ref[...] + gb_ref[0:1, :])
        sp = _sigmoid(gp_ref[...] + gb_ref[1:2, :])
        m_out[...] = (sa * av + sp * bv).astype(BF16)

    act = pl.BlockSpec((tm, a), lambda j, i: (i, 0))
    wgt = pl.BlockSpec((None, a, tn), lambda j, i: (j // per, 0, j % per))
    out = pl.BlockSpec((tm, tn), lambda j, i: (i, j))
    return _run("merge_fwd", list(sides), dict(
        body=body, grid=(d // tn, s // tm),
        in_specs=[act, act, wgt, wgt,
                  pl.BlockSpec((tm, tn), lambda j, i: (i, ga0 + j)),
                  pl.BlockSpec((tm, tn), lambda j, i: (i, gp0 + j)),
                  pl.BlockSpec((None, 2, tn), lambda j, i: (j // per, 0, j % per))],
        out_specs=[out, out, out],
        out_shape=[jax.ShapeDtypeStruct((s, d), F32), jax.ShapeDtypeStruct((s, d), F32),
                   jax.ShapeDtypeStruct((s, d), BF16)],
        scratch_shapes=[], operands=[ya, yp, woa_full, wop_full, proj, proj, gb_full], sem=("parallel", "parallel")))


def _out_proj(mb, wo, x, s, d):
    tm, tn, tk = _div(s, 1024), _div(d, 1024), _div(d, 4096)

    def epilogue(res, ex, outs):
        outs[0][...] = res + ex[0][...]

    tile = pl.BlockSpec((tm, tn), lambda i, j, k: (i, j))
    return _matmul(
        "out_proj", NN, (s // tm, d // tn, d // tk),
        mb, pl.BlockSpec((tm, tk), lambda i, j, k: (i, k)),
        wo, pl.BlockSpec((tk, tn), lambda i, j, k: (k, j)),
        (tm, tn), [jax.ShapeDtypeStruct((s, d), F32)], [tile], extra=(x,), extra_specs=(tile,),
        epilogue=epilogue)[0][0]


def _loss_head(x2, target, final_gain):
    s, d = x2.shape
    tr = _div(s, 256)

    def body(x_ref, t_ref, g_ref, loss_ref, dx_ref, dxb_ref, dg_ref):
        @pl.when(pl.program_id(0) == 0)
        def _():
            dg_ref[...] = jnp.zeros_like(dg_ref)

        xv = x_ref[...]
        g = g_ref[...]
        r = lax.rsqrt(jnp.mean(xv * xv, axis=-1, keepdims=True) + EPS)
        xn = xv * r
        e = xn * g - t_ref[...]
        loss_ref[...] = 0.5 * jnp.mean(e * e, axis=-1, keepdims=True)
        dy = e / d
        dg_ref[...] += jnp.sum(dy * xn, axis=0, keepdims=True)
        dxn = dy * g
        dx = r * (dxn - xn * jnp.mean(dxn * xn, axis=-1, keepdims=True))
        dx_ref[...] = dx
        dxb_ref[...] = dx.astype(BF16)

    rows = pl.BlockSpec((tr, d), lambda i: (i, 0))
    vec = pl.BlockSpec((1, d), lambda i: (0, 0))
    return pl.pallas_call(
        body, name="loss_head", grid=(s // tr,),
        in_specs=[rows, rows, vec], out_specs=[pl.BlockSpec((tr, 1), lambda i: (i, 0)), rows, rows, vec],
        out_shape=[jax.ShapeDtypeStruct((s, 1), F32), jax.ShapeDtypeStruct((s, d), F32),
                   jax.ShapeDtypeStruct((s, d), BF16), jax.ShapeDtypeStruct((1, d), F32)],
        compiler_params=_params(("arbitrary",)),
    )(x2, target, final_gain.reshape(1, d))


N_SLOTS = 10


def _slot(seg):
    t = seg - 6
    return jnp.where(seg < 6, seg, 6 + 2 * (t % 2) + t // 2)


def _merge_bwd(dxb, wo, a_val, b_val, proj, gb_full, s, d, a):
    sw = d // N_CHIPS
    tm, tn, tk = _div(s, 256), _div(sw, 1024), d
    per = sw // tn
    per_slot = a // tn
    ga0, gp0 = (4 * a + 2 * a) // tn, (4 * a + 2 * a + d) // tn

    def epilogue(dm, ex, outs):
        a_ref, b_ref, ga_ref, gp_ref, gb_ref = ex
        da_ref, db_ref, dg_ref, dgb_ref = outs
        sa = _sigmoid(ga_ref[...] + gb_ref[0:1, :])
        sp = _sigmoid(gp_ref[...] + gb_ref[1:2, :])
        da_ref[...] = (dm * sa).astype(BF16)
        db_ref[...] = (dm * sp).astype(BF16)
        dga = dm * a_ref[...] * sa * (1.0 - sa)
        dgp = dm * b_ref[...] * sp * (1.0 - sp)
        dg_ref[0] = dga.astype(BF16)
        dg_ref[1] = dgp.astype(BF16)

        @pl.when(pl.program_id(1) == 0)
        def _():
            dgb_ref[...] = jnp.zeros_like(dgb_ref)

        dgb_ref[0:1, :] += jnp.sum(dga, axis=0, keepdims=True)
        dgb_ref[1:2, :] += jnp.sum(dgp, axis=0, keepdims=True)

    tile = pl.BlockSpec((tm, tn), lambda j, i, k: (i, j))
    sd = jax.ShapeDtypeStruct((s, d), BF16)
    return _matmul(
        "merge_bwd", NT, (d // tn, s // tm, d // tk),
        dxb, pl.BlockSpec((tm, tk), lambda j, i, k: (i, k)),
        wo, pl.BlockSpec((tn, tk), lambda j, i, k: (j, k)),
        (tm, tn), [sd, sd, jax.ShapeDtypeStruct((N_SLOTS, s, a), BF16), jax.ShapeDtypeStruct((2, d), F32)],
        [tile, tile, pl.BlockSpec((2, tm, tn), lambda j, i, k: (3 + j // per_slot, i, j % per_slot)),
         pl.BlockSpec((2, tn), lambda j, i, k: (0, j))],
        extra=(a_val, b_val, proj, proj, gb_full),
        extra_specs=(tile, tile, pl.BlockSpec((tm, tn), lambda j, i, k: (i, ga0 + j)),
                     pl.BlockSpec((tm, tn), lambda j, i, k: (i, gp0 + j)),
                     pl.BlockSpec((None, 2, tn), lambda j, i, k: (j // per, 0, j % per))),
        epilogue=epilogue, accumulate_outs=True)[0]


def _weight_grad(name, act, dout, shard_cols, slots=False, piece=None, sides=()):
    s, kdim = act.shape
    n = dout.shape[0] * dout.shape[2] if slots else dout.shape[1]
    row_tile = lambda i: i
    if shard_cols:
        sw = n // N_CHIPS
        tm, tn = _div(kdim, 1024), _div(math.gcd(sw, dout.shape[2]) if slots else sw, 1024)
        per = sw // tn
        if piece is not None:
            tm = kdim // (2 * piece[1])
            kdim = 2 * tm
            row_tile = lambda i: i * piece[1] + piece[0]
        shape = (N_CHIPS, kdim, sw)
        out = pl.BlockSpec((None, tm, tn), lambda i, j, k: (j // per, i, j % per))
    else:
        sh = kdim // N_CHIPS
        tm, tn = _div(sh, 1024), _div(n, 1024)
        per = sh // tm
        shape = (N_CHIPS, sh, n)
        out = pl.BlockSpec((None, tm, tn), lambda i, j, k: (i // per, i % per, j))
    tk = _div(s, 4096)
    if slots:
        per_slot = dout.shape[2] // tn
        dout_spec = pl.BlockSpec((None, tk, tn), lambda i, j, k: (_slot(j // per_slot), k, j % per_slot))
    else:
        dout_spec = pl.BlockSpec((tk, tn), lambda i, j, k: (k, j))
    return _matmul(
        name, TN, (kdim // tm, n // tn, s // tk),
        act, pl.BlockSpec((tk, tm), lambda i, j, k: (k, row_tile(i))), dout, dout_spec,
        (tm, tn), [jax.ShapeDtypeStruct(shape, BF16)], [out], sides=sides)


def _norm_in_bwd(x, dh, dx2, gain):
    s, d = x.shape
    tr = _div(s, 256)

    def body(x_ref, dh_ref, dx2_ref, g_ref, gx_ref, dg_ref):
        @pl.when(pl.program_id(0) == 0)
        def _():
            dg_ref[...] = jnp.zeros_like(dg_ref)

        xv = x_ref[...]
        r = lax.rsqrt(jnp.mean(xv * xv, axis=-1, keepdims=True) + EPS)
        xn = xv * r
        dhv = dh_ref[...]
        dg_ref[...] += jnp.sum(dhv * xn, axis=0, keepdims=True)
        dxn = dhv * g_ref[...]
        gx_ref[...] = r * (dxn - xn * jnp.mean(dxn * xn, axis=-1, keepdims=True)) + dx2_ref[...]

    rows = pl.BlockSpec((tr, d), lambda i: (i, 0))
    vec = pl.BlockSpec((1, d), lambda i: (0, 0))
    return pl.pallas_call(
        body, name="norm_in_bwd", grid=(s // tr,),
        in_specs=[rows, rows, rows, vec], out_specs=[rows, vec],
        out_shape=[jax.ShapeDtypeStruct((s, d), F32), jax.ShapeDtypeStruct((1, d), F32)],
        compiler_params=_params(("arbitrary",)),
    )(x, dh, dx2, gain.reshape(1, d))


def _pack(vectors):
    flat = jnp.concatenate([v.reshape(-1).astype(F32) for v in vectors])
    rows = -(-flat.shape[0] // 1024) * 8
    return jnp.pad(flat, (0, rows * 128 - flat.shape[0])).reshape(rows, 128)


def _unpack(packed, like):
    flat, out, at = packed.reshape(-1), [], 0
    for v in like:
        out.append(flat[at:at + v.size].reshape(v.shape))
        at += v.size
    return out


def _small_adamw(g, w, m, v):
    def body(g_ref, w_ref, m_ref, v_ref, d_ref, mo_ref, vo_ref):
        delta, m2, v2 = _adamw_math(w_ref[...], g_ref[...], m_ref[...], v_ref[...])
        d_ref[...] = delta
        mo_ref[...] = m2
        vo_ref[...] = v2

    return pl.pallas_call(body, name="small_adamw", out_shape=[jax.ShapeDtypeStruct(g.shape, F32)] * 3)(g, w, m, v)


def kernel(x, norm_gain, w_in, rel_bias, pool_w, pool_scale, w_out_attn, w_out_pool, gate_bias, w_out, final_gain, loss_target, m_norm_gain, m_w_in, m_rel_bias, m_pool_w, m_pool_scale, m_w_out_attn, m_w_out_pool, m_gate_bias, m_w_out, m_final_gain, v_norm_gain, v_w_in, v_rel_bias, v_pool_w, v_pool_scale, v_w_out_attn, v_w_out_pool, v_gate_bias, v_w_out, v_final_gain):
    _, s, d = x.shape
    a = p = d // 2
    n_in = w_in.shape[1] * N_CHIPS
    sw_in = w_in.shape[1]
    pg = p // N_GROUPS
    xs = x.reshape(s, d)
    target = loss_target.reshape(s, d)
    c_arr = lax.axis_index("c").astype(jnp.int32).reshape(1)
    chip = 2 * lax.axis_index("x") + lax.axis_index("y")

    hb = _norm_in(xs, norm_gain)
    tm, tn, tk = _div(s, 1024), _div(sw_in, 1024), _div(d, 4096)
    per_in = sw_in // tn
    cx, cy = lax.axis_index("x"), lax.axis_index("y")
    order = jnp.stack([2 * cx + cy, 2 * (1 - cx) + cy, 2 * cx + (1 - cy), 2 * (1 - cx) + (1 - cy)]).astype(jnp.int32)

    def in_proj(name, first, count, weights, sides, carry=None):
        if weights.ndim == 3:
            w_spec = pl.BlockSpec((None, tk, tn), lambda i, j, k, o: (o[first + j // per_in], k, j % per_in))
        else:
            w_spec = pl.BlockSpec((tk, tn), lambda i, j, k, o: (k, j))
        return _matmul(
            name, NN, (s // tm, count * per_in, d // tk),
            hb, pl.BlockSpec((tm, tk), lambda i, j, k, o: (i, k)), weights, w_spec,
            (tm, tn), [jax.ShapeDtypeStruct((s, n_in), F32)],
            [pl.BlockSpec((tm, tn), lambda i, j, k, o: (i, o[first + j // per_in] * per_in + j % per_in))],
            sides=sides, prefetch=order, carry=carry)

    w_in_b = w_in.astype(BF16)
    (proj,), ((win_near,),) = in_proj("in_proj_own", 0, 1, w_in_b, [_gather_side([w_in_b], [True], peers=(0, 1))])
    (proj,), ((win_far,),) = in_proj("in_proj_near", 1, 2, win_near, [_relay_far_side(win_near)], carry=proj)
    pw_b, woa_b, wop_b, wo_b = (w.astype(BF16) for w in (pool_w, w_out_attn, w_out_pool, w_out))
    (proj,), ((pw_full, gb_full),) = in_proj(
        "in_proj_far", 3, 1, win_far, [_gather_side([pw_b, gate_bias], [False, False])], carry=proj)
    win_full = _with_own(_with_own(win_near, win_far, order[3]), w_in_b, chip)
    pw_full, gb_full = _with_own(pw_full, pw_b, chip), _with_own(gb_full, gate_bias, chip)
    table = _bias_table(rel_bias)
    (o_attn, ya), ((woa_full, wop_full),) = _attention_fwd(
        proj, table, s, a, [_gather_side([woa_b, wop_b], [True, True])])
    woa_full, wop_full = _with_own(woa_full, woa_b, chip), _with_own(wop_full, wop_b, chip)
    d_pool, t_pool, yp = _pool_fwd(proj, pw_full, pool_scale, s, a, p)
    (a_val, b_val, mb), ((wo_full,),) = _merge_fwd(ya, yp, woa_full, wop_full, proj, gb_full, s, d, a,
                                                  [_gather_side([wo_b], [True])])
    wo_mat = _with_own(wo_full, wo_b, chip).reshape(d, d)
    loss_rows, dx2, dx2b, g_final = _loss_head(_out_proj(mb, wo_mat, xs, s, d), target, final_gain)
    loss = lax.psum(jnp.sum(loss_rows), ("x", "y", "c"))

    da, db, dproj, g_gate_full = _merge_bwd(dx2b, wo_mat, a_val, b_val, proj, gb_full, s, d, a)
    (gwo_b,), _ = _weight_grad("grad_w_out", mb, dx2b, shard_cols=False)
    (gwoa_b,), _ = _weight_grad("grad_w_out_attn", ya, da, shard_cols=True)
    (gwop_b,), _ = _weight_grad("grad_w_out_pool", yp, db, shard_cols=True)
    early = ["w_out_attn", "w_out_pool", "w_out"]

    sw = d // N_CHIPS
    tm, tn, tk = _div(s, 1024), _div(a, 1024), _div(sw, 1024)
    per_o = sw // tk

    def back_through(name, dout, w_full, sides=()):
        nkb = d // tk

        def body(*refs):
            acts, wgts, o_ref = refs[:nkb], refs[nkb:2 * nkb], refs[2 * nkb]
            acc = lax.dot_general(acts[0][...], wgts[0][...], NT, preferred_element_type=F32)
            for q in range(1, nkb):
                acc = acc + lax.dot_general(acts[q][...], wgts[q][...], NT, preferred_element_type=F32)
            o_ref[...] = acc

        act = lambda q: pl.BlockSpec((tm, tk), lambda i, j: (i, q))
        wgt = lambda q: pl.BlockSpec((None, tn, tk), lambda i, j: (q // per_o, j, q % per_o))
        return _run(name, list(sides), dict(
            body=body, grid=(s // tm, a // tn),
            in_specs=[act(q) for q in range(nkb)] + [wgt(q) for q in range(nkb)],
            out_specs=[pl.BlockSpec((tm, tn), lambda i, j: (i, j))], out_shape=[jax.ShapeDtypeStruct((s, a), F32)],
            scratch_shapes=[], operands=[dout] * nkb + [w_full] * nkb, sem=("parallel", "parallel")))

    (dya,), (early_sib,) = back_through("grad_y_attn", da, woa_full, [_swap_side([gwoa_b, gwop_b, gwo_b])])
    early_pair = [_pair_sum("pair_sum_" + n, g, r, c_arr) for n, g, r in zip(early, [gwoa_b, gwop_b, gwo_b], early_sib)]
    (dyp,), _ = back_through("grad_y_pool", db, wop_full)
    dproj, gpw, g_pscale = _pool_bwd(proj, dyp, t_pool, d_pool, pw_full, pool_scale, dproj, s, a, p)
    dproj, dtable = _attention_bwd(proj, o_attn, dya, table, dproj, s, a)
    g_rel = jax.vjp(_bias_table, rel_bias)[1](dtable)[0]
    gpw3 = gpw.reshape(N_CHIPS, pg, pg)

    (gw0_b,), (early_chips,) = _weight_grad("grad_w_in_0", hb, dproj, shard_cols=True, slots=True, piece=(0, 2),
                                            sides=[_scatter_side(early_pair)])
    (gw1_b,), (sib0,) = _weight_grad("grad_w_in_1", hb, dproj, shard_cols=True, slots=True, piece=(1, 2),
                                     sides=[_swap_side([gw0_b, gpw3.astype(BF16)])])
    def with_own_sum(from_chips, pair):
        return _with_own(from_chips, lax.dynamic_index_in_dim(pair, chip, 0, keepdims=False), chip)

    early_halves = [_chip_sum("chip_sum_" + n, with_own_sum(r, q), c_arr)
                    for n, r, q in zip(early, early_chips, early_pair)]
    pair0 = [_pair_sum("pair_sum_w_in_0", gw0_b, sib0[0], c_arr), _pair_sum("pair_sum_pool_w", gpw3, sib0[1], c_arr)]

    tm, tn, tk = _div(s, 1024), _div(d, 1024), _div(a // 2, 1024)
    per_k, per_slot = sw_in // tk, a // tk
    nk_half = n_in // tk // 2
    steps = nk_half // 2

    def grad_h(name, k0, sides, plus=None):
        def body(a0, a1, b0, b1, *rest):
            o_ref, acc = rest[-2], rest[-1]
            k = pl.program_id(2)

            @pl.when(k == 0)
            def _():
                acc[...] = jnp.zeros_like(acc) if plus is None else rest[0][...]

            acc[...] += (lax.dot_general(a0[...], b0[...], NT, preferred_element_type=F32)
                         + lax.dot_general(a1[...], b1[...], NT, preferred_element_type=F32))

            @pl.when(k == steps - 1)
            def _():
                o_ref[...] = acc[...]

        def act(which):
            return pl.BlockSpec((None, tm, tk), lambda i, j, k: (
                _slot((2 * k + which + k0) // per_slot), i, (2 * k + which + k0) % per_slot))

        def wgt(which):
            return pl.BlockSpec((None, tn, tk), lambda i, j, k: (
                (2 * k + which + k0) // per_k, j, (2 * k + which + k0) % per_k))

        tile = pl.BlockSpec((tm, tn), lambda i, j, k: (i, j))
        return _run(name, list(sides), dict(
            body=body, grid=(s // tm, d // tn, steps),
            in_specs=[act(0), act(1), wgt(0), wgt(1)] + ([] if plus is None else [tile]), out_specs=[tile],
            out_shape=[jax.ShapeDtypeStruct((s, d), F32)], scratch_shapes=[pltpu.VMEM((tm, tn), F32)],
            operands=[dproj, dproj, win_full, win_full] + ([] if plus is None else [plus]),
            sem=("parallel", "parallel", "arbitrary")))

    (dh_a,), (chips0, (sib1,), early_grads) = grad_h(
        "grad_h_a", 0, [_scatter_side(pair0), _swap_side([gw1_b]), _share_side(early_halves)])
    pair1 = _pair_sum("pair_sum_w_in_1", gw1_b, sib1, c_arr)
    gwin_half = _chip_sum("chip_sum_w_in_0", with_own_sum(chips0[0], pair0[0]), c_arr, piece=0, pieces=2)
    gpw_half = _chip_sum("chip_sum_pool_w", with_own_sum(chips0[1], pair0[1]), c_arr)
    (dh,), ((chips1,), (gwin_half,), (gpw_full,)) = grad_h(
        "grad_h_b", nk_half,
        [_scatter_side([pair1]), _share_side([gwin_half], piece=0, pieces=2), _share_side([gpw_half])], plus=dh_a)
    gwin_half = _chip_sum("chip_sum_w_in_1", with_own_sum(chips1, pair1), c_arr, piece=1, pieces=2, into=gwin_half)
    grad_x, g_norm = _norm_in_bwd(xs, dh, dx2, norm_gain)
    _, ((gwin_full,),) = _run("reduce_share_halves", [_share_side([gwin_half], piece=1, pieces=2)])
    late_grads = [gwin_full, gpw_full]

    names = ["w_in", "w_out_attn", "w_out_pool", "w_out", "pool_w"]
    grads = [late_grads[0], *early_grads, late_grads[1]]
    big = {}
    weights = [w_in, w_out_attn, w_out_pool, w_out, pool_w.reshape(pg, pg)]
    ms = [m_w_in, m_w_out_attn, m_w_out_pool, m_w_out, m_pool_w.reshape(pg, pg)]
    vs = [v_w_in, v_w_out_attn, v_w_out_pool, v_w_out, v_pool_w.reshape(pg, pg)]
    for n, g, w, m, v in zip(names, grads, weights, ms, vs):
        big[n] = [r.reshape(pool_w.shape) if n == "pool_w" else r for r in _adamw("adamw_" + n, g, w, m, v)]

    small_like = [norm_gain, final_gain, pool_scale, rel_bias, jnp.zeros((2, d), F32)]
    summed = _sum_slots("small_grads_sum", _all_to_all_small(_pack([g_norm, g_final, g_pscale, g_rel, g_gate_full])))
    g_norm_t, g_final_t, g_pscale_t, g_rel_t, g_gate_t = _unpack(summed, small_like)
    g_gate_t = lax.dynamic_slice_in_dim(g_gate_t, chip * sw, sw, axis=1)
    small_g = [g_norm_t, g_final_t, g_pscale_t, g_rel_t, g_gate_t]
    small_w = [norm_gain, final_gain, pool_scale, rel_bias, gate_bias]
    small_m = [m_norm_gain, m_final_gain, m_pool_scale, m_rel_bias, m_gate_bias]
    small_v = [v_norm_gain, v_final_gain, v_pool_scale, v_rel_bias, v_gate_bias]
    packed = _small_adamw(_pack(small_g), _pack(small_w), _pack(small_m), _pack(small_v))
    sd, sm, sv = [_unpack(t, small_w) for t in packed]
    small = {n: [small_g[i], sd[i], sm[i], sv[i]]
             for i, n in enumerate(["norm_gain", "final_gain", "pool_scale", "rel_bias", "gate_bias"])}

    every = {**big, **small}
    order = ["norm_gain", "w_in", "rel_bias", "pool_w", "pool_scale", "w_out_attn", "w_out_pool", "gate_bias",
             "w_out", "final_gain"]
    return (loss, grad_x.reshape(x.shape), *[every[n][0] for n in order], *[every[n][1] for n in order],
            *[every[n][2] for n in order], *[every[n][3] for n in order])
```

```python
import math

import jax
import jax.numpy as jnp
from jax import lax
from jax.experimental import pallas as pl
from jax.experimental.pallas import tpu as pltpu

F32 = jnp.float32
BF16 = jnp.bfloat16
MESH = pl.DeviceIdType.MESH
ANY = pl.BlockSpec(memory_space=pl.ANY)

N_CHIPS = 4
N_DEV = 8
CHUNK = 64
N_LEFT_CHUNKS = 8
PAD = N_LEFT_CHUNKS * CHUNK
HEAD_DIM = 128
MAX_REL = 128
POOL_WINDOWS = (2, 4, 8, 16)
N_GROUPS = len(POOL_WINDOWS)
HALO = 16
Q_GROUP = 4 * CHUNK
K_GROUP = Q_GROUP + PAD
NEG = -1e30
EPS = 1e-6
ADAM_LR, ADAM_B1, ADAM_B2, ADAM_EPS, ADAM_WD, ADAM_STEP = 0.001, 0.9, 0.999, 1e-08, 0.01, 10
VMEM_LIMIT = 56 * 1024 * 1024

NN = (((1,), (0,)), ((), ()))
NT = (((1,), (1,)), ((), ()))
TN = (((0,), (0,)), ((), ()))


def _div(n, pref):
    if n <= pref:
        return n
    for t in range(pref - pref % 128, 0, -128):
        if n % t == 0:
            return t
    raise ValueError((n, pref))


def _params(sem, **kw):
    return pltpu.CompilerParams(dimension_semantics=sem, vmem_limit_bytes=VMEM_LIMIT, **kw)


def _sigmoid(z):
    return jax.nn.sigmoid(z)


def _silu_and_grad(z):
    sg = _sigmoid(z)
    return z * sg, sg * (1.0 + z * (1.0 - sg))


def _matmul(name, dn, grid, a, a_spec, b, b_spec, acc_shape, outs, out_specs, extra=(), extra_specs=(),
            epilogue=None, accumulate_outs=False, sides=(), prefetch=None, carry=None):
    nk = grid[2]
    aliases = {}
    if carry is not None:
        aliases = {2 + len(extra): 0}
        extra, extra_specs = (*extra, carry), (*extra_specs, ANY)
    ne, no = len(extra), len(outs)

    def finish(res, ex, out_refs):
        if epilogue is None:
            for o in out_refs:
                o[...] = res.astype(o.dtype)
        else:
            epilogue(res, ex, out_refs)

    def body(*refs):
        a_ref, b_ref = refs[0], refs[1]
        ex = refs[2:2 + ne]
        out_refs = refs[2 + ne:2 + ne + no]
        if nk == 1:
            finish(lax.dot_general(a_ref[...], b_ref[...], dn, preferred_element_type=F32), ex, out_refs)
            return
        acc = refs[-1]
        k = pl.program_id(2)

        @pl.when(k == 0)
        def _():
            acc[...] = jnp.zeros_like(acc)

        acc[...] += lax.dot_general(a_ref[...], b_ref[...], dn, preferred_element_type=F32)

        @pl.when(k == nk - 1)
        def _():
            finish(acc[...], ex, out_refs)

    sem = ("arbitrary",) * 3 if accumulate_outs else ("parallel", "parallel", "arbitrary")
    return _run(name, list(sides), dict(
        body=body, grid=grid, in_specs=[a_spec, b_spec, *extra_specs], out_specs=list(out_specs),
        out_shape=list(outs), scratch_shapes=[] if nk == 1 else [pltpu.VMEM(acc_shape, F32)],
        operands=[a, b, *extra], sem=sem, aliases=aliases, prefetch=prefetch))


def _place():
    x, y, c = lax.axis_index("x"), lax.axis_index("y"), lax.axis_index("c")
    chips = [(1 - x, y), (x, 1 - y), (1 - x, 1 - y)]
    return x, y, c, chips


N_STREAMS = 1


class _Copies:
    def __init__(self, cps):
        self.cps = cps

    def start(self):
        for cp in self.cps:
            cp.start()

    def wait_send(self):
        for cp in self.cps:
            cp.wait_send()

    def wait_recv(self):
        for cp in self.cps:
            cp.wait_recv()

    def wait(self):
        for cp in self.cps:
            cp.wait()


def _remote(src, dst, send_sems, recv_sems, k, dev):
    lead = src.shape[0]
    n = N_STREAMS
    while n > 1 and (lead % n or (len(src.shape) == 2 and (lead // n) % 16)):
        n //= 2
    step = lead // n
    return _Copies([pltpu.make_async_remote_copy(
        src_ref=src.at[pl.ds(i * step, step)], dst_ref=dst.at[pl.ds(i * step, step)],
        send_sem=send_sems.at[k * N_STREAMS + i], recv_sem=recv_sems.at[k * N_STREAMS + i],
        device_id=dev, device_id_type=MESH) for i in range(n)])


class _Side:
    def __init__(self, ins, out_shapes, n_remote, n_local, start, finish, aliases=None):
        self.ins, self.out_shapes = list(ins), list(out_shapes)
        self.n_remote, self.n_local = max(n_remote, 1), max(n_local, 1)
        self.start, self.finish, self.aliases = start, finish, aliases or {}


def _run(name, sides, compute=None):
    cm = compute or dict(body=None, grid=(), in_specs=[], out_specs=[], out_shape=[], scratch_shapes=[], operands=[])
    grid = tuple(cm["grid"])
    ni, no, ns = len(cm["operands"]), len(cm["out_shape"]), len(cm["scratch_shapes"])
    n_in = [len(sd.ins) for sd in sides]
    n_out = [len(sd.out_shapes) for sd in sides]
    prefetch = cm.get("prefetch")
    shift = 0 if prefetch is None else 1

    def body(*refs):
        refs = refs[shift:]
        at = ni
        side_ins = []
        for n in n_in:
            side_ins.append(refs[at:at + n])
            at += n
        outs = refs[at:at + no]
        at += no
        side_outs = []
        for n in n_out:
            side_outs.append(refs[at:at + n])
            at += n
        scratch = refs[at:at + ns]
        at += ns
        sems = [refs[at + 3 * q:at + 3 * q + 3] for q in range(len(sides))]

        def each(step):
            for sd, i_, o_, m_ in zip(sides, side_ins, side_outs, sems):
                getattr(sd, step)(i_, o_, *m_)

        if not grid:
            each("start")
            each("finish")
            return
        first = last = None
        for ax, g in enumerate(grid):
            f, l = pl.program_id(ax) == 0, pl.program_id(ax) == g - 1
            first = f if first is None else first & f
            last = l if last is None else last & l
        if sides:
            pl.when(first)(lambda: each("start"))
        cm["body"](*refs[:ni], *outs, *scratch)
        if sides:
            pl.when(last)(lambda: each("finish"))

    aliases = {shift + i_: o_ for i_, o_ in (cm.get("aliases") or {}).items()}
    in_at, out_at = shift + ni, no
    for sd, a, b in zip(sides, n_in, n_out):
        for i_, o_ in sd.aliases.items():
            aliases[in_at + i_] = out_at + o_
        in_at, out_at = in_at + a, out_at + b
    scratch_shapes = list(cm["scratch_shapes"])
    for sd in sides:
        scratch_shapes += [pltpu.SemaphoreType.DMA((sd.n_remote * N_STREAMS,)),
                           pltpu.SemaphoreType.DMA((sd.n_remote * N_STREAMS,)), pltpu.SemaphoreType.DMA((sd.n_local,))]
    in_specs = list(cm["in_specs"]) + [ANY] * sum(n_in)
    out_specs = list(cm["out_specs"]) + [ANY] * sum(n_out)
    kw = dict(in_specs=in_specs, out_specs=out_specs, scratch_shapes=scratch_shapes)
    if grid:
        kw["grid"] = grid
    if prefetch is not None:
        kw = dict(grid_spec=pltpu.PrefetchScalarGridSpec(num_scalar_prefetch=1, **kw))
    if grid:
        kw["compiler_params"] = _params(("arbitrary",) * len(grid) if sides else cm["sem"])
    res = pl.pallas_call(
        body, name=name, out_shape=list(cm["out_shape"]) + [s for sd in sides for s in sd.out_shapes],
        input_output_aliases=aliases, **kw,
    )(*([] if prefetch is None else [prefetch]), *cm["operands"], *[a for sd in sides for a in sd.ins])
    res = list(res)
    side_res, at = [], no
    for n in n_out:
        side_res.append(res[at:at + n])
        at += n
    return res[:no], side_res


def _gather_side(shards, split, peers=(0, 1, 2)):
    n = len(shards)

    def plan(ins, outs, send_sems, recv_sems, _):
        x, y, c, chips = _place()
        me = 2 * x + y
        sibling = (x, y, 1 - c)
        direct, relays, arrivals = [], [], []
        for t in range(n):
            quarter = ins[t].shape[0] // 4
            for j in peers:
                cx, cy = chips[j]
                src_chip = 2 * cx + cy
                k = 12 * t + 4 * j
                if not split[t]:
                    direct.append(_remote(ins[t], outs[t].at[me], send_sems, recv_sems, k, (cx, cy, c)))
                    got = outs[t].at[src_chip]
                    arrivals.append(_remote(got, got, send_sems, recv_sems, k, (cx, cy, c)))
                    continue
                for r in range(2):
                    e = c ^ r
                    out_q = pl.ds((2 * c + e) * quarter, quarter)
                    direct.append(_remote(ins[t].at[out_q], outs[t].at[me, out_q], send_sems, recv_sems, k + r,
                                          (cx, cy, e)))
                    got = outs[t].at[src_chip, pl.ds((2 * e + c) * quarter, quarter)]
                    relays.append((_remote(got, got, send_sems, recv_sems, k + r, (cx, cy, e)),
                                   _remote(got, got, send_sems, recv_sems, k + 2 + r, sibling)))
                    theirs = outs[t].at[src_chip, pl.ds((2 * e + 1 - c) * quarter, quarter)]
                    arrivals.append(_remote(theirs, theirs, send_sems, recv_sems, k + 2 + (1 - r), sibling))
        return direct, relays, arrivals

    def start(*refs):
        for cp in plan(*refs)[0]:
            cp.start()

    def finish(*refs):
        direct, relays, arrivals = plan(*refs)
        for landed, onward in relays:
            landed.wait_recv()
            onward.start()
        for cp in arrivals:
            cp.wait_recv()
        for cp in direct + [onward for _, onward in relays]:
            cp.wait_send()

    return _Side(shards, [jax.ShapeDtypeStruct((N_CHIPS,) + s.shape, s.dtype) for s in shards], 12 * n, 0,
                 start, finish)


def _with_own(slots, block, chip):
    return lax.dynamic_update_slice(slots, block[None], (chip,) + (0,) * block.ndim)


def _relay_far_side(full):
    def plan(ins, outs, send_sems, recv_sems, __):
        x, y, c, _ = _place()
        src, far = ins[0], outs[0]
        eighth = far.shape[0] // 8
        x_id, y_id = 2 * (1 - x) + y, 2 * x + (1 - y)
        sends, landed, onward, from_sibling = [], [], [], []
        for side, (nb_x, nb_y, block) in enumerate([(1 - x, y, y_id), (x, 1 - y, x_id)]):
            for r in range(2):
                e = c ^ r
                out_rows = pl.ds((4 * side + 2 * c + e) * eighth, eighth)
                sends.append(_remote(src.at[block, out_rows], far.at[out_rows], send_sems, recv_sems, 2 * side + r,
                                     (nb_x, nb_y, e)))
                got = far.at[pl.ds((4 * side + 2 * e + c) * eighth, eighth)]
                landed.append(_remote(got, got, send_sems, recv_sems, 2 * side + r, (nb_x, nb_y, e)))
                onward.append(_remote(got, got, send_sems, recv_sems, 4 + 2 * side + r, (x, y, 1 - c)))
                theirs = far.at[pl.ds((4 * side + 2 * e + 1 - c) * eighth, eighth)]
                from_sibling.append(_remote(theirs, theirs, send_sems, recv_sems, 4 + 2 * side + (1 - r),
                                            (x, y, 1 - c)))
        return sends, landed, onward, from_sibling

    def start(*refs):
        for cp in plan(*refs)[0]:
            cp.start()

    def finish(*refs):
        sends, landed, onward, from_sibling = plan(*refs)
        for got, fwd in zip(landed, onward):
            got.wait_recv()
            fwd.start()
        for cp in from_sibling:
            cp.wait_recv()
        for cp in sends + onward:
            cp.wait_send()

    return _Side([full], [jax.ShapeDtypeStruct(full.shape[1:], full.dtype)], 8, 0, start, finish)


def _swap_side(parts):
    n = len(parts)

    def plan(ins, outs, send_sems, recv_sems, _):
        x, y, c, _ = _place()
        cps = []
        for t in range(n):
            half = ins[t].shape[1] // 2
            cps.append(_remote(ins[t].at[:, pl.ds((1 - c) * half, half)], outs[t], send_sems, recv_sems, t,
                               (x, y, 1 - c)))
        return cps

    def start(*refs):
        for cp in plan(*refs):
            cp.start()

    def finish(*refs):
        for cp in plan(*refs):
            cp.wait()

    return _Side(parts, [jax.ShapeDtypeStruct((p.shape[0], p.shape[1] // 2, p.shape[2]), p.dtype) for p in parts],
                 n, 0, start, finish)


def _scatter_side(parts):
    n = len(parts)

    def plan(ins, outs, send_sems, recv_sems, _):
        x, y, c, chips = _place()
        me = 2 * x + y
        sends, arrivals = [], []
        for t in range(n):
            for j, (cx, cy) in enumerate(chips):
                sends.append(_remote(ins[t].at[2 * cx + cy], outs[t].at[me], send_sems, recv_sems, 3 * t + j,
                                     (cx, cy, c)))
                got = outs[t].at[2 * cx + cy]
                arrivals.append(_remote(got, got, send_sems, recv_sems, 3 * t + j, (cx, cy, c)))
        return sends, arrivals

    def start(*refs):
        for cp in plan(*refs)[0]:
            cp.start()

    def finish(*refs):
        sends, arrivals = plan(*refs)
        for cp in arrivals:
            cp.wait_recv()
        for cp in sends:
            cp.wait_send()

    return _Side(parts, [jax.ShapeDtypeStruct(p.shape, p.dtype) for p in parts], 3 * n, 0, start, finish)


def _share_side(fulls, piece=0, pieces=1):
    n = len(fulls)

    def plan(_, outs, send_sems, recv_sems, __):
        x, y, c, _ = _place()
        cps = []
        for t in range(n):
            half = outs[t].shape[0] // (2 * pieces)
            mine = outs[t].at[pl.ds((pieces * c + piece) * half, half)]
            theirs = outs[t].at[pl.ds((pieces * (1 - c) + piece) * half, half)]
            cps.append((_remote(mine, mine, send_sems, recv_sems, t, (x, y, 1 - c)),
                        _remote(theirs, theirs, send_sems, recv_sems, t, (x, y, 1 - c))))
        return cps

    def start(*refs):
        for cp, _ in plan(*refs):
            cp.start()

    def finish(*refs):
        for cp, rv in plan(*refs):
            rv.wait_recv()
            cp.wait_send()

    return _Side(fulls, [jax.ShapeDtypeStruct(f.shape, f.dtype) for f in fulls], n, 0, start, finish,
                 aliases={t: t for t in range(n)})


def _all_to_all_small(packed):
    def body(in_ref, out_ref, send_sems, recv_sems, local_sem):
        x, y, c, _ = _place()
        me = 4 * x + 2 * y + c
        own = pltpu.make_async_copy(in_ref, out_ref.at[me], local_sem)
        own.start()
        cps, rvs = [], []
        for k in range(1, N_DEV):
            fx, fy, fc = (k >> 2) & 1, (k >> 1) & 1, k & 1
            px, py, pc = x ^ fx, y ^ fy, c ^ fc
            cp = _remote(in_ref, out_ref.at[me], send_sems, recv_sems, k - 1, (px, py, pc))
            cp.start()
            cps.append(cp)
            got = out_ref.at[4 * px + 2 * py + pc]
            rvs.append(_remote(got, got, send_sems, recv_sems, k - 1, (px, py, pc)))
        for rv in rvs:
            rv.wait_recv()
        for cp in cps:
            cp.wait_send()
        own.wait()

    return pl.pallas_call(
        body, name="small_grads_exchange",
        in_specs=[ANY], out_specs=ANY,
        out_shape=jax.ShapeDtypeStruct((N_DEV,) + packed.shape, packed.dtype),
        scratch_shapes=[pltpu.SemaphoreType.DMA(((N_DEV - 1) * N_STREAMS,)),
                        pltpu.SemaphoreType.DMA(((N_DEV - 1) * N_STREAMS,)), pltpu.SemaphoreType.DMA],
    )(packed)


def _pair_sum(name, g, recv, c_arr):
    _, rows, cols = g.shape
    half = rows // 2
    tr, tc = _div(half, 512), _div(cols, 8192)
    nrb = half // tr

    def body(c_ref, g_ref, r_ref, o_ref):
        o_ref[...] = (g_ref[...].astype(F32) + r_ref[...].astype(F32)).astype(BF16)

    return pl.pallas_call(
        body, name=name,
        grid_spec=pltpu.PrefetchScalarGridSpec(
            num_scalar_prefetch=1, grid=(N_CHIPS, nrb, cols // tc),
            in_specs=[pl.BlockSpec((None, tr, tc), lambda s, i, j, c: (s, c[0] * nrb + i, j)),
                      pl.BlockSpec((None, tr, tc), lambda s, i, j, c: (s, i, j))],
            out_specs=pl.BlockSpec((None, tr, tc), lambda s, i, j, c: (s, i, j))),
        out_shape=jax.ShapeDtypeStruct(recv.shape, BF16),
        compiler_params=_params(("parallel", "parallel", "parallel")),
    )(c_arr, g, recv)


def _chip_sum(name, recv, c_arr, piece=0, pieces=1, into=None):
    _, half, cols = recv.shape
    tr, tc = _div(half, 512), _div(cols, 1024)
    nrb = half // tr

    def body(c_ref, r_ref, *rest):
        o_ref = rest[-1]
        acc = r_ref[0].astype(F32)
        for s in range(1, N_CHIPS):
            acc = acc + r_ref[s].astype(F32)
        o_ref[...] = acc

    return pl.pallas_call(
        body, name=name,
        grid_spec=pltpu.PrefetchScalarGridSpec(
            num_scalar_prefetch=1, grid=(nrb, cols // tc),
            in_specs=[pl.BlockSpec((N_CHIPS, tr, tc), lambda i, j, c: (0, i, j))] + ([] if into is None else [ANY]),
            out_specs=pl.BlockSpec((tr, tc), lambda i, j, c: ((pieces * c[0] + piece) * nrb + i, j))),
        out_shape=jax.ShapeDtypeStruct((2 * pieces * half, cols), F32),
        input_output_aliases={} if into is None else {2: 0},
        compiler_params=_params(("parallel", "parallel")),
    )(c_arr, recv, *([] if into is None else [into]))


def _adamw_math(w, g, m, v):
    m2 = ADAM_B1 * m + (1.0 - ADAM_B1) * g
    v2 = ADAM_B2 * v + (1.0 - ADAM_B2) * (g * g)
    m_hat = m2 / (1.0 - ADAM_B1 ** ADAM_STEP)
    v_hat = v2 / (1.0 - ADAM_B2 ** ADAM_STEP)
    delta = -ADAM_LR * (m_hat / (jnp.sqrt(v_hat) + ADAM_EPS) + ADAM_WD * w)
    return delta, m2, v2


def _adamw(name, g, w, m, v):
    rows, cols = g.shape
    tr, tc = _div(rows, 512), _div(cols, 1024)
    spec = pl.BlockSpec((tr, tc), lambda i, j: (i, j))

    def body(g_ref, w_ref, m_ref, v_ref, go_ref, d_ref, mo_ref, vo_ref):
        gg = g_ref[...]
        delta, m2, v2 = _adamw_math(w_ref[...], gg, m_ref[...], v_ref[...])
        go_ref[...] = gg
        d_ref[...] = delta
        mo_ref[...] = m2
        vo_ref[...] = v2

    return pl.pallas_call(
        body, name=name, grid=(rows // tr, cols // tc),
        in_specs=[spec] * 4, out_specs=[spec] * 4,
        out_shape=[jax.ShapeDtypeStruct(g.shape, F32)] * 4,
        compiler_params=_params(("parallel", "parallel")),
    )(g, w, m, v)


def _sum_slots(name, slots):
    def body(s_ref, o_ref):
        acc = s_ref[0]
        for d in range(1, N_DEV):
            acc = acc + s_ref[d]
        o_ref[...] = acc

    return pl.pallas_call(body, name=name, out_shape=jax.ShapeDtypeStruct(slots.shape[1:], F32))(slots)


def _norm_in(x, gain):
    s, d = x.shape
    tr = _div(s, 256)

    def body(x_ref, g_ref, h_ref):
        xv = x_ref[...]
        r = lax.rsqrt(jnp.mean(xv * xv, axis=-1, keepdims=True) + EPS)
        h_ref[...] = (xv * r * g_ref[...]).astype(BF16)

    return pl.pallas_call(
        body, name="norm_in", grid=(s // tr,),
        in_specs=[pl.BlockSpec((tr, d), lambda i: (i, 0)), pl.BlockSpec((1, d), lambda i: (0, 0))],
        out_specs=pl.BlockSpec((tr, d), lambda i: (i, 0)),
        out_shape=jax.ShapeDtypeStruct((s, d), BF16),
        compiler_params=_params(("parallel",)),
    )(x, gain.reshape(1, d))


def _bias_table(rel_bias):
    h = rel_bias.shape[0]
    span = 2 * CHUNK - 1
    lo, hi = -(CHUNK - 1), (N_LEFT_CHUNKS + 1) * CHUNK - 1
    by_rel = jnp.concatenate([rel_bias[:, MAX_REL + lo:],
                              jnp.broadcast_to(rel_bias[:, -1:], (h, hi - MAX_REL))], axis=1)
    vec = jnp.stack([by_rel[:, m * CHUNK:m * CHUNK + span] for m in range(N_LEFT_CHUNKS + 1)], axis=1)
    rev = jnp.concatenate([vec[..., ::-1], jnp.zeros(vec.shape[:2] + (1,), vec.dtype)], axis=-1)
    skew = jnp.tile(rev, (1, 1, CHUNK))[..., :CHUNK * span].reshape(h, N_LEFT_CHUNKS + 1, CHUNK, span)
    blocks = skew[..., CHUNK - 1:]
    off = jnp.full((h, CHUNK, CHUNK), NEG, rel_bias.dtype)
    rows = []
    for qi in range(Q_GROUP // CHUNK):
        dist = [N_LEFT_CHUNKS + qi - kj for kj in range(K_GROUP // CHUNK)]
        rows.append(jnp.concatenate([blocks[:, m] if 0 <= m <= N_LEFT_CHUNKS else off for m in dist], axis=-1))
    return jnp.concatenate(rows, axis=-2)


def _softmax(scores, tab_ref, r0, scale):
    sc = scores * scale + tab_ref[...]
    col = lax.broadcasted_iota(jnp.int32, sc.shape, 1)
    sc = jnp.where(col >= PAD - r0, sc, NEG)
    e = jnp.exp(sc - jnp.max(sc, axis=-1, keepdims=True))
    return e * (1.0 / jnp.sum(e, axis=-1, keepdims=True))


def _attention_fwd(proj, tables, s, a, sides=()):
    heads = a // HEAD_DIM
    scale = HEAD_DIM ** -0.5
    groups = s // Q_GROUP

    def body(q_ref, k_ref, v_ref, z_ref, tab_ref, o_ref, ya_ref, kp, vp):
        kp[0:PAD, :] = jnp.zeros((PAD, HEAD_DIM), BF16)
        vp[0:PAD, :] = jnp.zeros((PAD, HEAD_DIM), BF16)
        kp[PAD:, :] = k_ref[...].astype(BF16)
        vp[PAD:, :] = v_ref[...].astype(BF16)

        def group(g, carry):
            r0 = pl.multiple_of(g * Q_GROUP, Q_GROUP)
            q = q_ref[pl.ds(r0, Q_GROUP), :].astype(BF16)
            p = _softmax(lax.dot_general(q, kp[pl.ds(r0, K_GROUP), :], NT, preferred_element_type=F32),
                         tab_ref, r0, scale)
            o = jnp.dot(p.astype(BF16), vp[pl.ds(r0, K_GROUP), :], preferred_element_type=F32)
            o_ref[pl.ds(r0, Q_GROUP), :] = o
            z = z_ref[pl.ds(r0, Q_GROUP), :]
            ya_ref[pl.ds(r0, Q_GROUP), :] = (o * (z * _sigmoid(z))).astype(BF16)
            return carry

        lax.fori_loop(0, groups, group, 0)

    col = lambda seg: (lambda h: (0, seg * heads + h))
    blk = lambda seg: pl.BlockSpec((s, HEAD_DIM), col(seg))
    return _run("attention_fwd", list(sides), dict(
        body=body, grid=(heads,),
        in_specs=[blk(0), blk(1), blk(2), blk(3),
                  pl.BlockSpec((None, Q_GROUP, K_GROUP), lambda h: (h, 0, 0))],
        out_specs=[blk(0), blk(0)],
        out_shape=[jax.ShapeDtypeStruct((s, a), F32), jax.ShapeDtypeStruct((s, a), BF16)],
        scratch_shapes=[pltpu.VMEM((PAD + s, HEAD_DIM), BF16), pltpu.VMEM((PAD + s, HEAD_DIM), BF16)],
        operands=[proj, proj, proj, proj, tables], sem=("parallel",)))


def _attention_bwd(proj, o, dya, tables, dproj, s, a):
    heads = a // HEAD_DIM
    scale = HEAD_DIM ** -0.5
    groups = s // Q_GROUP

    def body(q_ref, k_ref, v_ref, z_ref, o_ref, dy_ref, tab_ref, _, dp_ref, dtab_ref, kp, vp, dkp, dvp):
        kp[0:PAD, :] = jnp.zeros((PAD, HEAD_DIM), BF16)
        vp[0:PAD, :] = jnp.zeros((PAD, HEAD_DIM), BF16)
        kp[PAD:, :] = k_ref[...].astype(BF16)
        vp[PAD:, :] = v_ref[...].astype(BF16)
        dkp[...] = jnp.zeros_like(dkp)
        dvp[...] = jnp.zeros_like(dvp)
        dtab_ref[...] = jnp.zeros_like(dtab_ref)

        def group(g, carry):
            r0 = pl.multiple_of(g * Q_GROUP, Q_GROUP)
            rows = pl.ds(r0, Q_GROUP)
            band = pl.ds(r0, K_GROUP)
            q = q_ref[rows, :].astype(BF16)
            z = z_ref[rows, :]
            dy = dy_ref[rows, :]
            si, dsi = _silu_and_grad(z)
            dp_ref[3, rows, :] = (dy * o_ref[rows, :] * dsi).astype(BF16)
            dob = (dy * si).astype(BF16)
            p = _softmax(lax.dot_general(q, kp[band, :], NT, preferred_element_type=F32), tab_ref, r0, scale)
            dp = lax.dot_general(dob, vp[band, :], NT, preferred_element_type=F32)
            ds = p * (dp - jnp.sum(p * dp, axis=-1, keepdims=True))
            dtab_ref[...] += ds
            dsb = (ds * scale).astype(BF16)
            dp_ref[0, rows, :] = jnp.dot(dsb, kp[band, :], preferred_element_type=F32).astype(BF16)
            dkp[band, :] += lax.dot_general(dsb, q, TN, preferred_element_type=F32)
            dvp[band, :] += lax.dot_general(p.astype(BF16), dob, TN, preferred_element_type=F32)
            return carry

        lax.fori_loop(0, groups, group, 0)
        dp_ref[1] = dkp[PAD:, :].astype(BF16)
        dp_ref[2] = dvp[PAD:, :].astype(BF16)

    col = lambda seg: (lambda h: (0, seg * heads + h))
    blk = lambda seg: pl.BlockSpec((s, HEAD_DIM), col(seg))
    return pl.pallas_call(
        body, name="attention_bwd", grid=(heads,),
        in_specs=[blk(0), blk(1), blk(2), blk(3), blk(0), blk(0),
                  pl.BlockSpec((None, Q_GROUP, K_GROUP), lambda h: (h, 0, 0)), ANY],
        out_specs=[pl.BlockSpec((4, s, HEAD_DIM), lambda h: (0, 0, h)),
                   pl.BlockSpec((None, Q_GROUP, K_GROUP), lambda h: (h, 0, 0))],
        out_shape=[jax.ShapeDtypeStruct(dproj.shape, BF16), jax.ShapeDtypeStruct(tables.shape, F32)],
        input_output_aliases={7: 0},
        scratch_shapes=[pltpu.VMEM((PAD + s, HEAD_DIM), BF16), pltpu.VMEM((PAD + s, HEAD_DIM), BF16),
                        pltpu.VMEM((PAD + s, HEAD_DIM), F32), pltpu.VMEM((PAD + s, HEAD_DIM), F32)],
        compiler_params=_params(("parallel",)),
    )(proj, proj, proj, proj, o, dya, tables, dproj)


def _pick_window(gi, by_window):
    out = by_window[-1]
    for n in range(N_GROUPS - 2, -1, -1):
        out = jnp.where(gi == n, by_window[n], out)
    return out


def _inv_count(gi, first_row, rows):
    t = first_row + lax.broadcasted_iota(jnp.int32, (rows, 1), 0)
    w = jnp.left_shift(2, gi)
    return 1.0 / jnp.minimum(t + 1, w).astype(F32)


def _pool_fwd(proj, pw_full, pool_scale, s, a, p):
    pg = p // N_GROUPS
    ts = _div(s, 512)
    u0, z0 = 4 * a // pg, (4 * a + p) // pg
    hb = ts // HALO

    def body(u_ref, uh_ref, z_ref, pw_ref, ps_ref, d_ref, t_ref, y_ref, ext):
        gi, i = pl.program_id(0), pl.program_id(1)
        u = u_ref[...]
        ext[0:HALO, :] = jnp.where(i > 0, uh_ref[...], 0.0)
        ext[HALO:, :] = u
        e = ext[...]
        sums, shift = [], 1
        for _ in POOL_WINDOWS:
            e = e + pltpu.roll(e, shift, 0)
            sums.append(e)
            shift *= 2
        win = _pick_window(gi, sums)[HALO:, :]
        d = (win * _inv_count(gi, i * ts, ts) - u).astype(BF16)
        d_ref[...] = d
        t = jnp.dot(d, pw_ref[...].reshape(pg, pg), preferred_element_type=F32)
        t_ref[...] = t
        z = z_ref[...]
        y_ref[...] = (t * ps_ref[...] * (z * _sigmoid(z))).astype(BF16)

    out_spec = pl.BlockSpec((ts, pg), lambda g, i: (i, g))
    return pl.pallas_call(
        body, name="pool_fwd", grid=(N_GROUPS, s // ts),
        in_specs=[pl.BlockSpec((ts, pg), lambda g, i: (i, u0 + g)),
                  pl.BlockSpec((HALO, pg), lambda g, i: (jnp.maximum(i * hb - 1, 0), u0 + g)),
                  pl.BlockSpec((ts, pg), lambda g, i: (i, z0 + g)),
                  pl.BlockSpec((N_CHIPS, None, pg // N_CHIPS, pg), lambda g, i: (0, g, 0, 0)),
                  pl.BlockSpec((1, pg), lambda g, i: (0, g))],
        out_specs=[out_spec] * 3,
        out_shape=[jax.ShapeDtypeStruct((s, p), BF16), jax.ShapeDtypeStruct((s, p), F32),
                   jax.ShapeDtypeStruct((s, p), BF16)],
        scratch_shapes=[pltpu.VMEM((ts + HALO, pg), F32)],
        compiler_params=_params(("parallel", "parallel")),
    )(proj, proj, proj, pw_full, pool_scale.reshape(1, p))


def _pool_bwd(proj, dyp, t, d, pw_full, pool_scale, dproj, s, a, p):
    pg = p // N_GROUPS
    ts = _div(s, 512)
    nt = s // ts
    z0 = (4 * a + p) // pg
    hb = ts // HALO
    last_halo = s // HALO - 1

    def body(dy_ref, dyh_ref, z_ref, zh_ref, t_ref, th_ref, d_ref, pw_ref, ps_ref, _,
             dp_ref, dpw_ref, dps_ref, ext):
        gi, i = pl.program_id(0), pl.program_id(1)
        ps = ps_ref[...]
        pw = pw_ref[...].reshape(pg, pg)

        @pl.when(i == 0)
        def _():
            dpw_ref[...] = jnp.zeros_like(dpw_ref)
            dps_ref[...] = jnp.zeros_like(dps_ref)

        def through_gate(dy, z, tt):
            si, dsi = _silu_and_grad(z)
            return dy * si, dy * (tt * ps) * dsi

        tt = t_ref[...]
        dyl, dz = through_gate(dy_ref[...], z_ref[...], tt)
        dp_ref[1] = dz.astype(BF16)
        dps_ref[...] += jnp.sum(dyl * tt, axis=0, keepdims=True)
        dtb = (dyl * ps).astype(BF16)
        dpw_ref[...] += lax.dot_general(d_ref[...], dtb, TN, preferred_element_type=F32).reshape(dpw_ref.shape)
        dd = lax.dot_general(dtb, pw, NT, preferred_element_type=F32)
        dylh, _ = through_gate(dyh_ref[...], zh_ref[...], th_ref[...])
        ddh = lax.dot_general((dylh * ps).astype(BF16), pw, NT, preferred_element_type=F32)
        ddh = jnp.where(i < nt - 1, ddh, 0.0)
        ext[0:ts, :] = dd * _inv_count(gi, i * ts, ts)
        ext[ts:, :] = ddh * _inv_count(gi, (i + 1) * ts, HALO)
        e = ext[...]
        rows = ts + HALO
        sums, shift = [], 1
        for _ in POOL_WINDOWS:
            e = e + pltpu.roll(e, rows - shift, 0)
            sums.append(e)
            shift *= 2
        dp_ref[0] = (_pick_window(gi, sums)[:ts, :] - dd).astype(BF16)

    tile = lambda c0: pl.BlockSpec((ts, pg), lambda g, i: (i, c0 + g))
    halo = lambda c0: pl.BlockSpec((HALO, pg), lambda g, i: (jnp.minimum((i + 1) * hb, last_halo), c0 + g))
    pw_spec = pl.BlockSpec((N_CHIPS, None, pg // N_CHIPS, pg), lambda g, i: (0, g, 0, 0))
    return pl.pallas_call(
        body, name="pool_bwd", grid=(N_GROUPS, nt),
        in_specs=[tile(0), halo(0), tile(z0), halo(z0), tile(0), halo(0), tile(0), pw_spec,
                  pl.BlockSpec((1, pg), lambda g, i: (0, g)), ANY],
        out_specs=[pl.BlockSpec((2, ts, pg), lambda g, i: (2, i, g)), pw_spec,
                   pl.BlockSpec((1, pg), lambda g, i: (0, g))],
        out_shape=[jax.ShapeDtypeStruct(dproj.shape, BF16),
                   jax.ShapeDtypeStruct(pw_full.shape, F32), jax.ShapeDtypeStruct((1, p), F32)],
        input_output_aliases={9: 0},
        scratch_shapes=[pltpu.VMEM((ts + HALO, pg), F32)],
        compiler_params=_params(("parallel", "arbitrary")),
    )(dyp, dyp, proj, proj, t, t, d, pw_full, pool_scale.reshape(1, p), dproj)


def _merge_fwd(ya, yp, woa_full, wop_full, proj, gb_full, s, d, a, sides=()):
    sw = d // N_CHIPS
    tm, tn = _div(s, 512), _div(sw, 1024)
    per = sw // tn
    ga0, gp0 = (4 * a + 2 * a) // tn, (4 * a + 2 * a + d) // tn

    def body(ya_ref, yp_ref, wa_ref, wp_ref, ga_ref, gp_ref, gb_ref, a_out, b_out, m_out):
        av = jnp.dot(ya_ref[...], wa_ref[...], preferred_element_type=F32)
        bv = jnp.dot(yp_ref[...], wp_ref[...], preferred_element_type=F32)
        a_out[...] = av
        b_out[...] = bv
        sa = _sigmoid(ga_ref[...] + gb_ref[0:1, :])
        sp = _sigmoid(gp_ref[...] + gb_ref[1:2, :])
        m_out[...] = (sa * av + sp * bv).astype(BF16)

    act = pl.BlockSpec((tm, a), lambda j, i: (i, 0))
    wgt = pl.BlockSpec((None, a, tn), lambda j, i: (j // per, 0, j % per))
    out = pl.BlockSpec((tm, tn), lambda j, i: (i, j))
    return _run("merge_fwd", list(sides), dict(
        body=body, grid=(d // tn, s // tm),
        in_specs=[act, act, wgt, wgt,
                  pl.BlockSpec((tm, tn), lambda j, i: (i, ga0 + j)),
                  pl.BlockSpec((tm, tn), lambda j, i: (i, gp0 + j)),
                  pl.BlockSpec((None, 2, tn), lambda j, i: (j // per, 0, j % per))],
        out_specs=[out, out, out],
        out_shape=[jax.ShapeDtypeStruct((s, d), F32), jax.ShapeDtypeStruct((s, d), F32),
                   jax.ShapeDtypeStruct((s, d), BF16)],
        scratch_shapes=[], operands=[ya, yp, woa_full, wop_full, proj, proj, gb_full], sem=("parallel", "parallel")))


def _out_proj(mb, wo, x, s, d):
    tm, tn, tk = _div(s, 1024), _div(d, 1024), _div(d, 4096)

    def epilogue(res, ex, outs):
        outs[0][...] = res + ex[0][...]

    tile = pl.BlockSpec((tm, tn), lambda i, j, k: (i, j))
    return _matmul(
        "out_proj", NN, (s // tm, d // tn, d // tk),
        mb, pl.BlockSpec((tm, tk), lambda i, j, k: (i, k)),
        wo, pl.BlockSpec((tk, tn), lambda i, j, k: (k, j)),
        (tm, tn), [jax.ShapeDtypeStruct((s, d), F32)], [tile], extra=(x,), extra_specs=(tile,),
        epilogue=epilogue)[0][0]


def _loss_head(x2, target, final_gain):
    s, d = x2.shape
    tr = _div(s, 256)

    def body(x_ref, t_ref, g_ref, loss_ref, dx_ref, dxb_ref, dg_ref):
        @pl.when(pl.program_id(0) == 0)
        def _():
            dg_ref[...] = jnp.zeros_like(dg_ref)

        xv = x_ref[...]
        g = g_ref[...]
        r = lax.rsqrt(jnp.mean(xv * xv, axis=-1, keepdims=True) + EPS)
        xn = xv * r
        e = xn * g - t_ref[...]
        loss_ref[...] = 0.5 * jnp.mean(e * e, axis=-1, keepdims=True)
        dy = e / d
        dg_ref[...] += jnp.sum(dy * xn, axis=0, keepdims=True)
        dxn = dy * g
        dx = r * (dxn - xn * jnp.mean(dxn * xn, axis=-1, keepdims=True))
        dx_ref[...] = dx
        dxb_ref[...] = dx.astype(BF16)

    rows = pl.BlockSpec((tr, d), lambda i: (i, 0))
    vec = pl.BlockSpec((1, d), lambda i: (0, 0))
    return pl.pallas_call(
        body, name="loss_head", grid=(s // tr,),
        in_specs=[rows, rows, vec], out_specs=[pl.BlockSpec((tr, 1), lambda i: (i, 0)), rows, rows, vec],
        out_shape=[jax.ShapeDtypeStruct((s, 1), F32), jax.ShapeDtypeStruct((s, d), F32),
                   jax.ShapeDtypeStruct((s, d), BF16), jax.ShapeDtypeStruct((1, d), F32)],
        compiler_params=_params(("arbitrary",)),
    )(x2, target, final_gain.reshape(1, d))


N_SLOTS = 10


def _slot(seg):
    t = seg - 6
    return jnp.where(seg < 6, seg, 6 + 2 * (t % 2) + t // 2)


def _merge_bwd(dxb, wo, a_val, b_val, proj, gb_full, s, d, a):
    sw = d // N_CHIPS
    tm, tn, tk = _div(s, 256), _div(sw, 1024), d
    per = sw // tn
    per_slot = a // tn
    ga0, gp0 = (4 * a + 2 * a) // tn, (4 * a + 2 * a + d) // tn

    def epilogue(dm, ex, outs):
        a_ref, b_ref, ga_ref, gp_ref, gb_ref = ex
        da_ref, db_ref, dg_ref, dgb_ref = outs
        sa = _sigmoid(ga_ref[...] + gb_ref[0:1, :])
        sp = _sigmoid(gp_ref[...] + gb_ref[1:2, :])
        da_ref[...] = (dm * sa).astype(BF16)
        db_ref[...] = (dm * sp).astype(BF16)
        dga = dm * a_ref[...] * sa * (1.0 - sa)
        dgp = dm * b_ref[...] * sp * (1.0 - sp)
        dg_ref[0] = dga.astype(BF16)
        dg_ref[1] = dgp.astype(BF16)

        @pl.when(pl.program_id(1) == 0)
        def _():
            dgb_ref[...] = jnp.zeros_like(dgb_ref)

        dgb_ref[0:1, :] += jnp.sum(dga, axis=0, keepdims=True)
        dgb_ref[1:2, :] += jnp.sum(dgp, axis=0, keepdims=True)

    tile = pl.BlockSpec((tm, tn), lambda j, i, k: (i, j))
    sd = jax.ShapeDtypeStruct((s, d), BF16)
    return _matmul(
        "merge_bwd", NT, (d // tn, s // tm, d // tk),
        dxb, pl.BlockSpec((tm, tk), lambda j, i, k: (i, k)),
        wo, pl.BlockSpec((tn, tk), lambda j, i, k: (j, k)),
        (tm, tn), [sd, sd, jax.ShapeDtypeStruct((N_SLOTS, s, a), BF16), jax.ShapeDtypeStruct((2, d), F32)],
        [tile, tile, pl.BlockSpec((2, tm, tn), lambda j, i, k: (3 + j // per_slot, i, j % per_slot)),
         pl.BlockSpec((2, tn), lambda j, i, k: (0, j))],
        extra=(a_val, b_val, proj, proj, gb_full),
        extra_specs=(tile, tile, pl.BlockSpec((tm, tn), lambda j, i, k: (i, ga0 + j)),
                     pl.BlockSpec((tm, tn), lambda j, i, k: (i, gp0 + j)),
                     pl.BlockSpec((None, 2, tn), lambda j, i, k: (j // per, 0, j % per))),
        epilogue=epilogue, accumulate_outs=True)[0]


def _weight_grad(name, act, dout, shard_cols, slots=False, piece=None, sides=()):
    s, kdim = act.shape
    n = dout.shape[0] * dout.shape[2] if slots else dout.shape[1]
    row_tile = lambda i: i
    if shard_cols:
        sw = n // N_CHIPS
        tm, tn = _div(kdim, 1024), _div(math.gcd(sw, dout.shape[2]) if slots else sw, 1024)
        per = sw // tn
        if piece is not None:
            tm = kdim // (2 * piece[1])
            kdim = 2 * tm
            row_tile = lambda i: i * piece[1] + piece[0]
        shape = (N_CHIPS, kdim, sw)
        out = pl.BlockSpec((None, tm, tn), lambda i, j, k: (j // per, i, j % per))
    else:
        sh = kdim // N_CHIPS
        tm, tn = _div(sh, 1024), _div(n, 1024)
        per = sh // tm
        shape = (N_CHIPS, sh, n)
        out = pl.BlockSpec((None, tm, tn), lambda i, j, k: (i // per, i % per, j))
    tk = _div(s, 4096)
    if slots:
        per_slot = dout.shape[2] // tn
        dout_spec = pl.BlockSpec((None, tk, tn), lambda i, j, k: (_slot(j // per_slot), k, j % per_slot))
    else:
        dout_spec = pl.BlockSpec((tk, tn), lambda i, j, k: (k, j))
    return _matmul(
        name, TN, (kdim // tm, n // tn, s // tk),
        act, pl.BlockSpec((tk, tm), lambda i, j, k: (k, row_tile(i))), dout, dout_spec,
        (tm, tn), [jax.ShapeDtypeStruct(shape, BF16)], [out], sides=sides)


def _norm_in_bwd(x, dh, dx2, gain):
    s, d = x.shape
    tr = _div(s, 256)

    def body(x_ref, dh_ref, dx2_ref, g_ref, gx_ref, dg_ref):
        @pl.when(pl.program_id(0) == 0)
        def _():
            dg_ref[...] = jnp.zeros_like(dg_ref)

        xv = x_ref[...]
        r = lax.rsqrt(jnp.mean(xv * xv, axis=-1, keepdims=True) + EPS)
        xn = xv * r
        dhv = dh_ref[...]
        dg_ref[...] += jnp.sum(dhv * xn, axis=0, keepdims=True)
        dxn = dhv * g_ref[...]
        gx_ref[...] = r * (dxn - xn * jnp.mean(dxn * xn, axis=-1, keepdims=True)) + dx2_ref[...]

    rows = pl.BlockSpec((tr, d), lambda i: (i, 0))
    vec = pl.BlockSpec((1, d), lambda i: (0, 0))
    return pl.pallas_call(
        body, name="norm_in_bwd", grid=(s // tr,),
        in_specs=[rows, rows, rows, vec], out_specs=[rows, vec],
        out_shape=[jax.ShapeDtypeStruct((s, d), F32), jax.ShapeDtypeStruct((1, d), F32)],
        compiler_params=_params(("arbitrary",)),
    )(x, dh, dx2, gain.reshape(1, d))


def _pack(vectors):
    flat = jnp.concatenate([v.reshape(-1).astype(F32) for v in vectors])
    rows = -(-flat.shape[0] // 1024) * 8
    return jnp.pad(flat, (0, rows * 128 - flat.shape[0])).reshape(rows, 128)


def _unpack(packed, like):
    flat, out, at = packed.reshape(-1), [], 0
    for v in like:
        out.append(flat[at:at + v.size].reshape(v.shape))
        at += v.size
    return out


def _small_adamw(g, w, m, v):
    def body(g_ref, w_ref, m_ref, v_ref, d_ref, mo_ref, vo_ref):
        delta, m2, v2 = _adamw_math(w_ref[...], g_ref[...], m_ref[...], v_ref[...])
        d_ref[...] = delta
        mo_ref[...] = m2
        vo_ref[...] = v2

    return pl.pallas_call(body, name="small_adamw", out_shape=[jax.ShapeDtypeStruct(g.shape, F32)] * 3)(g, w, m, v)


def kernel(x, norm_gain, w_in, rel_bias, pool_w, pool_scale, w_out_attn, w_out_pool, gate_bias, w_out, final_gain, loss_target, m_norm_gain, m_w_in, m_rel_bias, m_pool_w, m_pool_scale, m_w_out_attn, m_w_out_pool, m_gate_bias, m_w_out, m_final_gain, v_norm_gain, v_w_in, v_rel_bias, v_pool_w, v_pool_scale, v_w_out_attn, v_w_out_pool, v_gate_bias, v_w_out, v_final_gain):
    _, s, d = x.shape
    a = p = d // 2
    n_in = w_in.shape[1] * N_CHIPS
    sw_in = w_in.shape[1]
    pg = p // N_GROUPS
    xs = x.reshape(s, d)
    target = loss_target.reshape(s, d)
    c_arr = lax.axis_index("c").astype(jnp.int32).reshape(1)
    chip = 2 * lax.axis_index("x") + lax.axis_index("y")

    hb = _norm_in(xs, norm_gain)
    tm, tn, tk = _div(s, 1024), _div(sw_in, 1024), _div(d, 4096)
    per_in = sw_in // tn
    cx, cy = lax.axis_index("x"), lax.axis_index("y")
    order = jnp.stack([2 * cx + cy, 2 * (1 - cx) + cy, 2 * cx + (1 - cy), 2 * (1 - cx) + (1 - cy)]).astype(jnp.int32)

    def in_proj(name, first, count, weights, sides, carry=None):
        if weights.ndim == 3:
            w_spec = pl.BlockSpec((None, tk, tn), lambda i, j, k, o: (o[first + j // per_in], k, j % per_in))
        else:
            w_spec = pl.BlockSpec((tk, tn), lambda i, j, k, o: (k, j))
        return _matmul(
            name, NN, (s // tm, count * per_in, d // tk),
            hb, pl.BlockSpec((tm, tk), lambda i, j, k, o: (i, k)), weights, w_spec,
            (tm, tn), [jax.ShapeDtypeStruct((s, n_in), F32)],
            [pl.BlockSpec((tm, tn), lambda i, j, k, o: (i, o[first + j // per_in] * per_in + j % per_in))],
            sides=sides, prefetch=order, carry=carry)

    w_in_b = w_in.astype(BF16)
    (proj,), ((win_near,),) = in_proj("in_proj_own", 0, 1, w_in_b, [_gather_side([w_in_b], [True], peers=(0, 1))])
    (proj,), ((win_far,),) = in_proj("in_proj_near", 1, 2, win_near, [_relay_far_side(win_near)], carry=proj)
    pw_b, woa_b, wop_b, wo_b = (w.astype(BF16) for w in (pool_w, w_out_attn, w_out_pool, w_out))
    (proj,), ((pw_full, gb_full),) = in_proj(
        "in_proj_far", 3, 1, win_far, [_gather_side([pw_b, gate_bias], [False, False])], carry=proj)
    win_full = _with_own(_with_own(win_near, win_far, order[3]), w_in_b, chip)
    pw_full, gb_full = _with_own(pw_full, pw_b, chip), _with_own(gb_full, gate_bias, chip)
    table = _bias_table(rel_bias)
    (o_attn, ya), ((woa_full, wop_full),) = _attention_fwd(
        proj, table, s, a, [_gather_side([woa_b, wop_b], [True, True])])
    woa_full, wop_full = _with_own(woa_full, woa_b, chip), _with_own(wop_full, wop_b, chip)
    d_pool, t_pool, yp = _pool_fwd(proj, pw_full, pool_scale, s, a, p)
    (a_val, b_val, mb), ((wo_full,),) = _merge_fwd(ya, yp, woa_full, wop_full, proj, gb_full, s, d, a,
                                                  [_gather_side([wo_b], [True])])
    wo_mat = _with_own(wo_full, wo_b, chip).reshape(d, d)
    loss_rows, dx2, dx2b, g_final = _loss_head(_out_proj(mb, wo_mat, xs, s, d), target, final_gain)
    loss = lax.psum(jnp.sum(loss_rows), ("x", "y", "c"))

    da, db, dproj, g_gate_full = _merge_bwd(dx2b, wo_mat, a_val, b_val, proj, gb_full, s, d, a)
    (gwo_b,), _ = _weight_grad("grad_w_out", mb, dx2b, shard_cols=False)
    (gwoa_b,), _ = _weight_grad("grad_w_out_attn", ya, da, shard_cols=True)
    (gwop_b,), _ = _weight_grad("grad_w_out_pool", yp, db, shard_cols=True)
    early = ["w_out_attn", "w_out_pool", "w_out"]

    sw = d // N_CHIPS
    tm, tn, tk = _div(s, 1024), _div(a, 1024), _div(sw, 1024)
    per_o = sw // tk

    def back_through(name, dout, w_full, sides=()):
        nkb = d // tk

        def body(*refs):
            acts, wgts, o_ref = refs[:nkb], refs[nkb:2 * nkb], refs[2 * nkb]
            acc = lax.dot_general(acts[0][...], wgts[0][...], NT, preferred_element_type=F32)
            for q in range(1, nkb):
                acc = acc + lax.dot_general(acts[q][...], wgts[q][...], NT, preferred_element_type=F32)
            o_ref[...] = acc

        act = lambda q: pl.BlockSpec((tm, tk), lambda i, j: (i, q))
        wgt = lambda q: pl.BlockSpec((None, tn, tk), lambda i, j: (q // per_o, j, q % per_o))
        return _run(name, list(sides), dict(
            body=body, grid=(s // tm, a // tn),
            in_specs=[act(q) for q in range(nkb)] + [wgt(q) for q in range(nkb)],
            out_specs=[pl.BlockSpec((tm, tn), lambda i, j: (i, j))], out_shape=[jax.ShapeDtypeStruct((s, a), F32)],
            scratch_shapes=[], operands=[dout] * nkb + [w_full] * nkb, sem=("parallel", "parallel")))

    (dya,), (early_sib,) = back_through("grad_y_attn", da, woa_full, [_swap_side([gwoa_b, gwop_b, gwo_b])])
    early_pair = [_pair_sum("pair_sum_" + n, g, r, c_arr) for n, g, r in zip(early, [gwoa_b, gwop_b, gwo_b], early_sib)]
    (dyp,), _ = back_through("grad_y_pool", db, wop_full)
    dproj, gpw, g_pscale = _pool_bwd(proj, dyp, t_pool, d_pool, pw_full, pool_scale, dproj, s, a, p)
    dproj, dtable = _attention_bwd(proj, o_attn, dya, table, dproj, s, a)
    g_rel = jax.vjp(_bias_table, rel_bias)[1](dtable)[0]
    gpw3 = gpw.reshape(N_CHIPS, pg, pg)

    (gw0_b,), (early_chips,) = _weight_grad("grad_w_in_0", hb, dproj, shard_cols=True, slots=True, piece=(0, 2),
                                            sides=[_scatter_side(early_pair)])
    (gw1_b,), (sib0,) = _weight_grad("grad_w_in_1", hb, dproj, shard_cols=True, slots=True, piece=(1, 2),
                                     sides=[_swap_side([gw0_b, gpw3.astype(BF16)])])
    def with_own_sum(from_chips, pair):
        return _with_own(from_chips, lax.dynamic_index_in_dim(pair, chip, 0, keepdims=False), chip)

    early_halves = [_chip_sum("chip_sum_" + n, with_own_sum(r, q), c_arr)
                    for n, r, q in zip(early, early_chips, early_pair)]
    pair0 = [_pair_sum("pair_sum_w_in_0", gw0_b, sib0[0], c_arr), _pair_sum("pair_sum_pool_w", gpw3, sib0[1], c_arr)]

    tm, tn, tk = _div(s, 1024), _div(d, 1024), _div(a // 2, 1024)
    per_k, per_slot = sw_in // tk, a // tk
    nk_half = n_in // tk // 2
    steps = nk_half // 2

    def grad_h(name, k0, sides, plus=None):
        def body(a0, a1, b0, b1, *rest):
            o_ref, acc = rest[-2], rest[-1]
            k = pl.program_id(2)

            @pl.when(k == 0)
            def _():
                acc[...] = jnp.zeros_like(acc) if plus is None else rest[0][...]

            acc[...] += (lax.dot_general(a0[...], b0[...], NT, preferred_element_type=F32)
                         + lax.dot_general(a1[...], b1[...], NT, preferred_element_type=F32))

            @pl.when(k == steps - 1)
            def _():
                o_ref[...] = acc[...]

        def act(which):
            return pl.BlockSpec((None, tm, tk), lambda i, j, k: (
                _slot((2 * k + which + k0) // per_slot), i, (2 * k + which + k0) % per_slot))

        def wgt(which):
            return pl.BlockSpec((None, tn, tk), lambda i, j, k: (
                (2 * k + which + k0) // per_k, j, (2 * k + which + k0) % per_k))

        tile = pl.BlockSpec((tm, tn), lambda i, j, k: (i, j))
        return _run(name, list(sides), dict(
            body=body, grid=(s // tm, d // tn, steps),
            in_specs=[act(0), act(1), wgt(0), wgt(1)] + ([] if plus is None else [tile]), out_specs=[tile],
            out_shape=[jax.ShapeDtypeStruct((s, d), F32)], scratch_shapes=[pltpu.VMEM((tm, tn), F32)],
            operands=[dproj, dproj, win_full, win_full] + ([] if plus is None else [plus]),
            sem=("parallel", "parallel", "arbitrary")))

    (dh_a,), (chips0, (sib1,), early_grads) = grad_h(
        "grad_h_a", 0, [_scatter_side(pair0), _swap_side([gw1_b]), _share_side(early_halves)])
    pair1 = _pair_sum("pair_sum_w_in_1", gw1_b, sib1, c_arr)
    gwin_half = _chip_sum("chip_sum_w_in_0", with_own_sum(chips0[0], pair0[0]), c_arr, piece=0, pieces=2)
    gpw_half = _chip_sum("chip_sum_pool_w", with_own_sum(chips0[1], pair0[1]), c_arr)
    (dh,), ((chips1,), (gwin_half,), (gpw_full,)) = grad_h(
        "grad_h_b", nk_half,
        [_scatter_side([pair1]), _share_side([gwin_half], piece=0, pieces=2), _share_side([gpw_half])], plus=dh_a)
    gwin_half = _chip_sum("chip_sum_w_in_1", with_own_sum(chips1, pair1), c_arr, piece=1, pieces=2, into=gwin_half)
    grad_x, g_norm = _norm_in_bwd(xs, dh, dx2, norm_gain)
    _, ((gwin_full,),) = _run("reduce_share_halves", [_share_side([gwin_half], piece=1, pieces=2)])
    late_grads = [gwin_full, gpw_full]

    names = ["w_in", "w_out_attn", "w_out_pool", "w_out", "pool_w"]
    grads = [late_grads[0], *early_grads, late_grads[1]]
    big = {}
    weights = [w_in, w_out_attn, w_out_pool, w_out, pool_w.reshape(pg, pg)]
    ms = [m_w_in, m_w_out_attn, m_w_out_pool, m_w_out, m_pool_w.reshape(pg, pg)]
    vs = [v_w_in, v_w_out_attn, v_w_out_pool, v_w_out, v_pool_w.reshape(pg, pg)]
    for n, g, w, m, v in zip(names, grads, weights, ms, vs):
        big[n] = [r.reshape(pool_w.shape) if n == "pool_w" else r for r in _adamw("adamw_" + n, g, w, m, v)]

    small_like = [norm_gain, final_gain, pool_scale, rel_bias, jnp.zeros((2, d), F32)]
    summed = _sum_slots("small_grads_sum", _all_to_all_small(_pack([g_norm, g_final, g_pscale, g_rel, g_gate_full])))
    g_norm_t, g_final_t, g_pscale_t, g_rel_t, g_gate_t = _unpack(summed, small_like)
    g_gate_t = lax.dynamic_slice_in_dim(g_gate_t, chip * sw, sw, axis=1)
    small_g = [g_norm_t, g_final_t, g_pscale_t, g_rel_t, g_gate_t]
    small_w = [norm_gain, final_gain, pool_scale, rel_bias, gate_bias]
    small_m = [m_norm_gain, m_final_gain, m_pool_scale, m_rel_bias, m_gate_bias]
    small_v = [v_norm_gain, v_final_gain, v_pool_scale, v_rel_bias, v_gate_bias]
    packed = _small_adamw(_pack(small_g), _pack(small_w), _pack(small_m), _pack(small_v))
    sd, sm, sv = [_unpack(t, small_w) for t in packed]
    small = {n: [small_g[i], sd[i], sm[i], sv[i]]
             for i, n in enumerate(["norm_gain", "final_gain", "pool_scale", "rel_bias", "gate_bias"])}

    every = {**big, **small}
    order = ["norm_gain", "w_in", "rel_bias", "pool_w", "pool_scale", "w_out_attn", "w_out_pool", "gate_bias",
             "w_out", "final_gain"]
    return (loss, grad_x.reshape(x.shape), *[every[n][0] for n in order], *[every[n][1] for n in order],
            *[every[n][2] for n in order], *[every[n][3] for n in order])
```
